```python
import jax, jax.numpy as jnp
from jax import lax
import numpy as np

D_MODEL = 1024
BATCH = 8
SEQ = 4096
DEPTH = 4

D_MIX = D_MODEL
CONV_CH = D_MIX // 2
CONV_K = 31
SB_HEADS = 8
SB_HEAD_DIM = (D_MIX - CONV_CH) // SB_HEADS
SB_WIDTH = SB_HEADS * SB_HEAD_DIM
BLOCK_Q = 128
MEM_LEN = 256
MEM_HEADS = 4
MEM_HEAD_DIM = D_MODEL // MEM_HEADS
D_FF = ((8 * D_MODEL // 3 + 127) // 128) * 128
FFN_K = 3
IN_WIDTH = 2 * CONV_CH + 3 * SB_WIDTH
DEEPNORM_ALPHA = (2.0 * DEPTH) ** 0.25
DEEPNORM_BETA = (8.0 * DEPTH) ** -0.25
LN_EPS = 1e-5

kernel_name = "hybrid_conformer_stickbreak_deepnorm_trunk"


def layer_norm(x, g, b):
    xf = x.astype(jnp.float32)
    mu = jnp.mean(xf, axis=-1, keepdims=True)
    var = jnp.mean(jnp.square(xf - mu), axis=-1, keepdims=True)
    y = (xf - mu) * lax.rsqrt(var + LN_EPS)
    return (y * g.astype(jnp.float32) + b.astype(jnp.float32)).astype(x.dtype)


def causal_dwconv(x, w, b):
    k = w.shape[0]
    y = lax.conv_general_dilated(
        x, w[:, None, :].astype(x.dtype), window_strides=(1,), padding=[(k - 1, 0)],
        dimension_numbers=('NWC', 'WIO', 'NWC'), feature_group_count=x.shape[-1])
    return y + b


def stick_breaking_attention(q, k, v):
    bsz, seq, nh, dh = q.shape
    nb = seq // BLOCK_Q
    scale = dh ** -0.5
    qf = q.astype(jnp.float32).transpose(0, 2, 1, 3)
    kf = k.astype(jnp.float32).transpose(0, 2, 1, 3)
    vf = v.astype(jnp.float32).transpose(0, 2, 1, 3)
    q_blocks = qf.reshape(bsz, nh, nb, BLOCK_Q, dh).transpose(2, 0, 1, 3, 4)
    key_pos = jnp.arange(seq)

    def one_block(args):
        q_blk, blk = args
        q_pos = blk * BLOCK_Q + jnp.arange(BLOCK_Q)
        visible = key_pos[None, :] < q_pos[:, None]
        z = jnp.einsum('bhqd,bhkd->bhqk', q_blk, kf) * scale
        log_keep = jnp.where(visible, jax.nn.log_sigmoid(-z), 0.0)
        later = lax.cumsum(log_keep, axis=3, reverse=True) - log_keep
        a = jnp.where(visible, jnp.exp(jax.nn.log_sigmoid(z) + later), 0.0)
        return jnp.einsum('bhqk,bhkd->bhqd', a, vf)

    out = lax.map(one_block, (q_blocks, jnp.arange(nb)))
    out = out.transpose(1, 0, 3, 2, 4).reshape(bsz, seq, nh * dh)
    return out.astype(q.dtype)


def hybrid_mixer(h, w_in, conv_w, conv_b, conv_ln_g, conv_ln_b, w_out):
    bsz, seq, _ = h.shape
    proj = h @ w_in
    glu_a, glu_g, q, k, v = jnp.split(
        proj, [CONV_CH, 2 * CONV_CH, 2 * CONV_CH + SB_WIDTH, 2 * CONV_CH + 2 * SB_WIDTH], axis=-1)
    u = glu_a * jax.nn.sigmoid(glu_g)
    u = causal_dwconv(u, conv_w, conv_b)
    u = jax.nn.silu(layer_norm(u, conv_ln_g, conv_ln_b))
    heads = lambda t: t.reshape(bsz, seq, SB_HEADS, SB_HEAD_DIM)
    a = stick_breaking_attention(heads(q), heads(k), heads(v))
    return jnp.concatenate([u, a], axis=-1) @ w_out


def memory_cross_attention(h, mem, wq, wk, wv, wo):
    bsz, seq, _ = h.shape
    q = (h @ wq).reshape(bsz, seq, MEM_HEADS, MEM_HEAD_DIM).astype(jnp.float32)
    k = (mem @ wk).reshape(bsz, -1, MEM_HEADS, MEM_HEAD_DIM).astype(jnp.float32)
    v = (mem @ wv).reshape(bsz, -1, MEM_HEADS, MEM_HEAD_DIM).astype(jnp.float32)
    s = jnp.einsum('bqhd,bmhd->bhqm', q, k) * (MEM_HEAD_DIM ** -0.5)
    p = jax.nn.softmax(s, axis=-1)
    o = jnp.einsum('bhqm,bmhd->bqhd', p, v).reshape(bsz, seq, D_MODEL).astype(h.dtype)
    return o @ wo


def conv_gated_ffn(h, w_up, conv_w, conv_b, w_down):
    up = causal_dwconv(h @ w_up, conv_w, conv_b)
    val, gate = jnp.split(up, 2, axis=-1)
    return (jax.nn.silu(gate) * val) @ w_down


def _fwd_setup_inputs(seed: int = 0) -> dict:
    key = jax.random.key(seed)
    ks = jax.random.split(key, 32)
    nrm = lambda k, shape, s: jax.random.normal(k, shape, jnp.float32) * s
    gain = lambda k, n: 1.0 + 0.05 * jax.random.normal(k, (DEPTH, n), jnp.float32)
    bias = lambda k, n: 0.01 * jax.random.normal(k, (DEPTH, n), jnp.float32)
    col = jnp.arange(IN_WIDTH)
    v_scale = jnp.where(col >= 2 * CONV_CH + 2 * SB_WIDTH, DEEPNORM_BETA, 1.0).astype(jnp.float32)
    w_in = nrm(ks[2], (DEPTH, D_MODEL, IN_WIDTH), D_MODEL ** -0.5) * v_scale
    return {
        "x": nrm(ks[0], (BATCH, SEQ, D_MODEL), 1.0),
        "mem": nrm(ks[1], (BATCH, MEM_LEN, D_MODEL), 1.0),
        "w_in": w_in,
        "conv_w": nrm(ks[3], (DEPTH, CONV_K, CONV_CH), CONV_K ** -0.5),
        "conv_b": bias(ks[4], CONV_CH),
        "conv_ln_g": gain(ks[5], CONV_CH),
        "conv_ln_b": bias(ks[6], CONV_CH),
        "w_out": nrm(ks[7], (DEPTH, D_MIX, D_MODEL), D_MIX ** -0.5 * DEEPNORM_BETA),
        "ln1_g": gain(ks[8], D_MODEL),
        "ln1_b": bias(ks[9], D_MODEL),
        "mem_wq": nrm(ks[10], (DEPTH, D_MODEL, D_MODEL), D_MODEL ** -0.5),
        "mem_wk": nrm(ks[11], (DEPTH, D_MODEL, D_MODEL), D_MODEL ** -0.5),
        "mem_wv": nrm(ks[12], (DEPTH, D_MODEL, D_MODEL), D_MODEL ** -0.5 * DEEPNORM_BETA),
        "mem_wo": nrm(ks[13], (DEPTH, D_MODEL, D_MODEL), D_MODEL ** -0.5 * DEEPNORM_BETA),
        "ln2_g": gain(ks[14], D_MODEL),
        "ln2_b": bias(ks[15], D_MODEL),
        "ffn_up": nrm(ks[16], (DEPTH, D_MODEL, 2 * D_FF), D_MODEL ** -0.5),
        "ffn_conv_w": nrm(ks[17], (DEPTH, FFN_K, 2 * D_FF), FFN_K ** -0.5),
        "ffn_conv_b": bias(ks[18], 2 * D_FF),
        "ffn_down": nrm(ks[19], (DEPTH, D_FF, D_MODEL), D_FF ** -0.5 * DEEPNORM_BETA),
        "ln3_g": gain(ks[20], D_MODEL),
        "ln3_b": bias(ks[21], D_MODEL),
    }


def _fwd_reference(x, mem, w_in, conv_w, conv_b, conv_ln_g, conv_ln_b, w_out, ln1_g, ln1_b,
              mem_wq, mem_wk, mem_wv, mem_wo, ln2_g, ln2_b,
              ffn_up, ffn_conv_w, ffn_conv_b, ffn_down, ln3_g, ln3_b):
    for l in range(DEPTH):
        mix = hybrid_mixer(x, w_in[l], conv_w[l], conv_b[l], conv_ln_g[l], conv_ln_b[l], w_out[l])
        x = layer_norm(DEEPNORM_ALPHA * x + mix, ln1_g[l], ln1_b[l])
        cross = memory_cross_attention(x, mem, mem_wq[l], mem_wk[l], mem_wv[l], mem_wo[l])
        x = layer_norm(DEEPNORM_ALPHA * x + cross, ln2_g[l], ln2_b[l])
        ffn = conv_gated_ffn(x, ffn_up[l], ffn_conv_w[l], ffn_conv_b[l], ffn_down[l])
        x = layer_norm(DEEPNORM_ALPHA * x + ffn, ln3_g[l], ln3_b[l])
    return x


import jax as _jax
import jax.numpy as _jnp

TWIN_FORMAT = 'train_step'
FWD_PARAMS = ['x', 'mem', 'w_in', 'conv_w', 'conv_b', 'conv_ln_g', 'conv_ln_b', 'w_out', 'ln1_g', 'ln1_b', 'mem_wq', 'mem_wk', 'mem_wv', 'mem_wo', 'ln2_g', 'ln2_b', 'ffn_up', 'ffn_conv_w', 'ffn_conv_b', 'ffn_down', 'ln3_g', 'ln3_b']
TWIN_WEIGHTS = ['w_in', 'conv_w', 'conv_b', 'conv_ln_g', 'conv_ln_b', 'w_out', 'ln1_g', 'ln1_b', 'mem_wq', 'mem_wk', 'mem_wv', 'mem_wo', 'ln2_g', 'ln2_b', 'ffn_up', 'ffn_conv_w', 'ffn_conv_b', 'ffn_down', 'ln3_g', 'ln3_b']
TWIN_DIFF_INPUT = 'x'
TWIN_INPUTS = ['x', 'mem', 'w_in', 'conv_w', 'conv_b', 'conv_ln_g', 'conv_ln_b', 'w_out', 'ln1_g', 'ln1_b', 'mem_wq', 'mem_wk', 'mem_wv', 'mem_wo', 'ln2_g', 'ln2_b', 'ffn_up', 'ffn_conv_w', 'ffn_conv_b', 'ffn_down', 'ln3_g', 'ln3_b', 'loss_target', 'm_w_in', 'm_conv_w', 'm_conv_b', 'm_conv_ln_g', 'm_conv_ln_b', 'm_w_out', 'm_ln1_g', 'm_ln1_b', 'm_mem_wq', 'm_mem_wk', 'm_mem_wv', 'm_mem_wo', 'm_ln2_g', 'm_ln2_b', 'm_ffn_up', 'm_ffn_conv_w', 'm_ffn_conv_b', 'm_ffn_down', 'm_ln3_g', 'm_ln3_b', 'v_w_in', 'v_conv_w', 'v_conv_b', 'v_conv_ln_g', 'v_conv_ln_b', 'v_w_out', 'v_ln1_g', 'v_ln1_b', 'v_mem_wq', 'v_mem_wk', 'v_mem_wv', 'v_mem_wo', 'v_ln2_g', 'v_ln2_b', 'v_ffn_up', 'v_ffn_conv_w', 'v_ffn_conv_b', 'v_ffn_down', 'v_ln3_g', 'v_ln3_b']
TWIN_OUTPUTS = ['loss', 'grad_x', 'grad_w_in', 'grad_conv_w', 'grad_conv_b', 'grad_conv_ln_g', 'grad_conv_ln_b', 'grad_w_out', 'grad_ln1_g', 'grad_ln1_b', 'grad_mem_wq', 'grad_mem_wk', 'grad_mem_wv', 'grad_mem_wo', 'grad_ln2_g', 'grad_ln2_b', 'grad_ffn_up', 'grad_ffn_conv_w', 'grad_ffn_conv_b', 'grad_ffn_down', 'grad_ln3_g', 'grad_ln3_b', 'delta_w_in', 'delta_conv_w', 'delta_conv_b', 'delta_conv_ln_g', 'delta_conv_ln_b', 'delta_w_out', 'delta_ln1_g', 'delta_ln1_b', 'delta_mem_wq', 'delta_mem_wk', 'delta_mem_wv', 'delta_mem_wo', 'delta_ln2_g', 'delta_ln2_b', 'delta_ffn_up', 'delta_ffn_conv_w', 'delta_ffn_conv_b', 'delta_ffn_down', 'delta_ln3_g', 'delta_ln3_b', 'new_m_w_in', 'new_m_conv_w', 'new_m_conv_b', 'new_m_conv_ln_g', 'new_m_conv_ln_b', 'new_m_w_out', 'new_m_ln1_g', 'new_m_ln1_b', 'new_m_mem_wq', 'new_m_mem_wk', 'new_m_mem_wv', 'new_m_mem_wo', 'new_m_ln2_g', 'new_m_ln2_b', 'new_m_ffn_up', 'new_m_ffn_conv_w', 'new_m_ffn_conv_b', 'new_m_ffn_down', 'new_m_ln3_g', 'new_m_ln3_b', 'new_v_w_in', 'new_v_conv_w', 'new_v_conv_b', 'new_v_conv_ln_g', 'new_v_conv_ln_b', 'new_v_w_out', 'new_v_ln1_g', 'new_v_ln1_b', 'new_v_mem_wq', 'new_v_mem_wk', 'new_v_mem_wv', 'new_v_mem_wo', 'new_v_ln2_g', 'new_v_ln2_b', 'new_v_ffn_up', 'new_v_ffn_conv_w', 'new_v_ffn_conv_b', 'new_v_ffn_down', 'new_v_ln3_g', 'new_v_ln3_b']
TWIN_LEAF_KINDS = {'loss': 'loss', 'grad_x': 'grad_x', 'grad_w_in': 'grad_w', 'grad_conv_w': 'grad_w', 'grad_conv_b': 'grad_w', 'grad_conv_ln_g': 'grad_w', 'grad_conv_ln_b': 'grad_w', 'grad_w_out': 'grad_w', 'grad_ln1_g': 'grad_w', 'grad_ln1_b': 'grad_w', 'grad_mem_wq': 'grad_w', 'grad_mem_wk': 'grad_w', 'grad_mem_wv': 'grad_w', 'grad_mem_wo': 'grad_w', 'grad_ln2_g': 'grad_w', 'grad_ln2_b': 'grad_w', 'grad_ffn_up': 'grad_w', 'grad_ffn_conv_w': 'grad_w', 'grad_ffn_conv_b': 'grad_w', 'grad_ffn_down': 'grad_w', 'grad_ln3_g': 'grad_w', 'grad_ln3_b': 'grad_w', 'delta_w_in': 'delta_w', 'delta_conv_w': 'delta_w', 'delta_conv_b': 'delta_w', 'delta_conv_ln_g': 'delta_w', 'delta_conv_ln_b': 'delta_w', 'delta_w_out': 'delta_w', 'delta_ln1_g': 'delta_w', 'delta_ln1_b': 'delta_w', 'delta_mem_wq': 'delta_w', 'delta_mem_wk': 'delta_w', 'delta_mem_wv': 'delta_w', 'delta_mem_wo': 'delta_w', 'delta_ln2_g': 'delta_w', 'delta_ln2_b': 'delta_w', 'delta_ffn_up': 'delta_w', 'delta_ffn_conv_w': 'delta_w', 'delta_ffn_conv_b': 'delta_w', 'delta_ffn_down': 'delta_w', 'delta_ln3_g': 'delta_w', 'delta_ln3_b': 'delta_w', 'new_m_w_in': 'new_m', 'new_m_conv_w': 'new_m', 'new_m_conv_b': 'new_m', 'new_m_conv_ln_g': 'new_m', 'new_m_conv_ln_b': 'new_m', 'new_m_w_out': 'new_m', 'new_m_ln1_g': 'new_m', 'new_m_ln1_b': 'new_m', 'new_m_mem_wq': 'new_m', 'new_m_mem_wk': 'new_m', 'new_m_mem_wv': 'new_m', 'new_m_mem_wo': 'new_m', 'new_m_ln2_g': 'new_m', 'new_m_ln2_b': 'new_m', 'new_m_ffn_up': 'new_m', 'new_m_ffn_conv_w': 'new_m', 'new_m_ffn_conv_b': 'new_m', 'new_m_ffn_down': 'new_m', 'new_m_ln3_g': 'new_m', 'new_m_ln3_b': 'new_m', 'new_v_w_in': 'new_v', 'new_v_conv_w': 'new_v', 'new_v_conv_b': 'new_v', 'new_v_conv_ln_g': 'new_v', 'new_v_conv_ln_b': 'new_v', 'new_v_w_out': 'new_v', 'new_v_ln1_g': 'new_v', 'new_v_ln1_b': 'new_v', 'new_v_mem_wq': 'new_v', 'new_v_mem_wk': 'new_v', 'new_v_mem_wv': 'new_v', 'new_v_mem_wo': 'new_v', 'new_v_ln2_g': 'new_v', 'new_v_ln2_b': 'new_v', 'new_v_ffn_up': 'new_v', 'new_v_ffn_conv_w': 'new_v', 'new_v_ffn_conv_b': 'new_v', 'new_v_ffn_down': 'new_v', 'new_v_ln3_g': 'new_v', 'new_v_ln3_b': 'new_v'}


def _forward(args):
    return _fwd_reference(*[args[k] for k in FWD_PARAMS])


def _output_shape():
    out = _jax.eval_shape(lambda: _forward(_fwd_setup_inputs(0)))
    return out.shape, out.dtype

N_MICROBATCH = 1
ADAM_LR = 0.001
ADAM_B1 = 0.9
ADAM_B2 = 0.999
ADAM_EPS = 1e-08
ADAM_WD = 0.01
ADAM_STEP = 10
PER_EXAMPLE_BATCH_AXIS = {'x': 0, 'mem': 0, 'loss_target': 0}
SHARED_INPUTS = []
_WEIGHT_DTYPES = {'w_in': _jnp.float32, 'conv_w': _jnp.float32, 'conv_b': _jnp.float32, 'conv_ln_g': _jnp.float32, 'conv_ln_b': _jnp.float32, 'w_out': _jnp.float32, 'ln1_g': _jnp.float32, 'ln1_b': _jnp.float32, 'mem_wq': _jnp.float32, 'mem_wk': _jnp.float32, 'mem_wv': _jnp.float32, 'mem_wo': _jnp.float32, 'ln2_g': _jnp.float32, 'ln2_b': _jnp.float32, 'ffn_up': _jnp.float32, 'ffn_conv_w': _jnp.float32, 'ffn_conv_b': _jnp.float32, 'ffn_down': _jnp.float32, 'ln3_g': _jnp.float32, 'ln3_b': _jnp.float32}
MOMENT_SCALE = {'w_in': 1.944145e-02, 'conv_w': 2.867126e-02, 'conv_b': 6.510270e-02, 'conv_ln_g': 3.860102e-02, 'conv_ln_b': 4.077651e-02, 'w_out': 5.394840e-02, 'ln1_g': 3.019493e+00, 'ln1_b': 2.904680e-01, 'mem_wq': 2.038972e-03, 'mem_wk': 2.050136e-03, 'mem_wv': 5.172252e-03, 'mem_wo': 5.107257e-03, 'ln2_g': 3.021711e+00, 'ln2_b': 2.918995e-01, 'ffn_up': 1.678884e-02, 'ffn_conv_w': 1.695149e-02, 'ffn_conv_b': 1.695185e-02, 'ffn_down': 6.538639e-02, 'ln3_g': 1.691123e+01, 'ln3_b': 1.082814e+00}


def _to_microbatches(a, axis):
    t = _jnp.moveaxis(a, axis, 0)
    t = t.reshape((N_MICROBATCH, t.shape[0] // N_MICROBATCH) + t.shape[1:])
    return _jnp.moveaxis(t, 1, axis + 1)


def setup_inputs(seed: int = 0) -> dict:
    inp = _fwd_setup_inputs(seed)
    key = _jax.random.fold_in(_jax.random.key(seed), 7919)
    shape, _ = _output_shape()
    out = dict(inp)
    out["loss_target"] = _jax.random.normal(_jax.random.fold_in(key, 0), shape, _jnp.float32)
    for i, name in enumerate(TWIN_WEIGHTS):
        w = inp[name].astype(_jnp.float32)
        if MOMENT_SCALE is None:
            s = _jnp.sqrt(_jnp.mean(_jnp.square(w)) + 1e-30)
        else:
            s = MOMENT_SCALE[name]
        km, kv = _jax.random.split(_jax.random.fold_in(key, i + 1))
        out[name] = w
        out["m_" + name] = s * _jax.random.normal(km, w.shape, _jnp.float32)
        out["v_" + name] = (s * s) * _jax.random.uniform(kv, w.shape, _jnp.float32, 0.5, 1.5)
    if N_MICROBATCH > 1:
        for name, axis in PER_EXAMPLE_BATCH_AXIS.items():
            out[name] = _to_microbatches(out[name], axis)
    return {'x': out['x'], 'mem': out['mem'], 'w_in': out['w_in'], 'conv_w': out['conv_w'], 'conv_b': out['conv_b'], 'conv_ln_g': out['conv_ln_g'], 'conv_ln_b': out['conv_ln_b'], 'w_out': out['w_out'], 'ln1_g': out['ln1_g'], 'ln1_b': out['ln1_b'], 'mem_wq': out['mem_wq'], 'mem_wk': out['mem_wk'], 'mem_wv': out['mem_wv'], 'mem_wo': out['mem_wo'], 'ln2_g': out['ln2_g'], 'ln2_b': out['ln2_b'], 'ffn_up': out['ffn_up'], 'ffn_conv_w': out['ffn_conv_w'], 'ffn_conv_b': out['ffn_conv_b'], 'ffn_down': out['ffn_down'], 'ln3_g': out['ln3_g'], 'ln3_b': out['ln3_b'], 'loss_target': out['loss_target'], 'm_w_in': out['m_w_in'], 'm_conv_w': out['m_conv_w'], 'm_conv_b': out['m_conv_b'], 'm_conv_ln_g': out['m_conv_ln_g'], 'm_conv_ln_b': out['m_conv_ln_b'], 'm_w_out': out['m_w_out'], 'm_ln1_g': out['m_ln1_g'], 'm_ln1_b': out['m_ln1_b'], 'm_mem_wq': out['m_mem_wq'], 'm_mem_wk': out['m_mem_wk'], 'm_mem_wv': out['m_mem_wv'], 'm_mem_wo': out['m_mem_wo'], 'm_ln2_g': out['m_ln2_g'], 'm_ln2_b': out['m_ln2_b'], 'm_ffn_up': out['m_ffn_up'], 'm_ffn_conv_w': out['m_ffn_conv_w'], 'm_ffn_conv_b': out['m_ffn_conv_b'], 'm_ffn_down': out['m_ffn_down'], 'm_ln3_g': out['m_ln3_g'], 'm_ln3_b': out['m_ln3_b'], 'v_w_in': out['v_w_in'], 'v_conv_w': out['v_conv_w'], 'v_conv_b': out['v_conv_b'], 'v_conv_ln_g': out['v_conv_ln_g'], 'v_conv_ln_b': out['v_conv_ln_b'], 'v_w_out': out['v_w_out'], 'v_ln1_g': out['v_ln1_g'], 'v_ln1_b': out['v_ln1_b'], 'v_mem_wq': out['v_mem_wq'], 'v_mem_wk': out['v_mem_wk'], 'v_mem_wv': out['v_mem_wv'], 'v_mem_wo': out['v_mem_wo'], 'v_ln2_g': out['v_ln2_g'], 'v_ln2_b': out['v_ln2_b'], 'v_ffn_up': out['v_ffn_up'], 'v_ffn_conv_w': out['v_ffn_conv_w'], 'v_ffn_conv_b': out['v_ffn_conv_b'], 'v_ffn_down': out['v_ffn_down'], 'v_ln3_g': out['v_ln3_g'], 'v_ln3_b': out['v_ln3_b']}


def _loss(weights, diff, rest, loss_target):
    with _jax.named_scope("forward"):
        args = {**rest, TWIN_DIFF_INPUT: diff, **{k: w.astype(_WEIGHT_DTYPES[k]) for k, w in weights.items()}}
        y = _forward(args)
    with _jax.named_scope("loss_head"):
        err = _jnp.square(y.astype(_jnp.float32) - loss_target)
        return 0.5 * _jnp.sum(_jnp.mean(err, axis=-1)) if err.ndim else 0.5 * err


def _adamw(w, g, m, v):
    m = ADAM_B1 * m + (1.0 - ADAM_B1) * g
    v = ADAM_B2 * v + (1.0 - ADAM_B2) * _jnp.square(g)
    m_hat = m / (1.0 - ADAM_B1 ** ADAM_STEP)
    v_hat = v / (1.0 - ADAM_B2 ** ADAM_STEP)
    delta = -ADAM_LR * (m_hat / (_jnp.sqrt(v_hat) + ADAM_EPS) + ADAM_WD * w)
    return delta, m, v


def reference(x, mem, w_in, conv_w, conv_b, conv_ln_g, conv_ln_b, w_out, ln1_g, ln1_b, mem_wq, mem_wk, mem_wv, mem_wo, ln2_g, ln2_b, ffn_up, ffn_conv_w, ffn_conv_b, ffn_down, ln3_g, ln3_b, loss_target, m_w_in, m_conv_w, m_conv_b, m_conv_ln_g, m_conv_ln_b, m_w_out, m_ln1_g, m_ln1_b, m_mem_wq, m_mem_wk, m_mem_wv, m_mem_wo, m_ln2_g, m_ln2_b, m_ffn_up, m_ffn_conv_w, m_ffn_conv_b, m_ffn_down, m_ln3_g, m_ln3_b, v_w_in, v_conv_w, v_conv_b, v_conv_ln_g, v_conv_ln_b, v_w_out, v_ln1_g, v_ln1_b, v_mem_wq, v_mem_wk, v_mem_wv, v_mem_wo, v_ln2_g, v_ln2_b, v_ffn_up, v_ffn_conv_w, v_ffn_conv_b, v_ffn_down, v_ln3_g, v_ln3_b):
    given = dict(x=x, mem=mem, w_in=w_in, conv_w=conv_w, conv_b=conv_b, conv_ln_g=conv_ln_g, conv_ln_b=conv_ln_b, w_out=w_out, ln1_g=ln1_g, ln1_b=ln1_b, mem_wq=mem_wq, mem_wk=mem_wk, mem_wv=mem_wv, mem_wo=mem_wo, ln2_g=ln2_g, ln2_b=ln2_b, ffn_up=ffn_up, ffn_conv_w=ffn_conv_w, ffn_conv_b=ffn_conv_b, ffn_down=ffn_down, ln3_g=ln3_g, ln3_b=ln3_b, loss_target=loss_target, m_w_in=m_w_in, m_conv_w=m_conv_w, m_conv_b=m_conv_b, m_conv_ln_g=m_conv_ln_g, m_conv_ln_b=m_conv_ln_b, m_w_out=m_w_out, m_ln1_g=m_ln1_g, m_ln1_b=m_ln1_b, m_mem_wq=m_mem_wq, m_mem_wk=m_mem_wk, m_mem_wv=m_mem_wv, m_mem_wo=m_mem_wo, m_ln2_g=m_ln2_g, m_ln2_b=m_ln2_b, m_ffn_up=m_ffn_up, m_ffn_conv_w=m_ffn_conv_w, m_ffn_conv_b=m_ffn_conv_b, m_ffn_down=m_ffn_down, m_ln3_g=m_ln3_g, m_ln3_b=m_ln3_b, v_w_in=v_w_in, v_conv_w=v_conv_w, v_conv_b=v_conv_b, v_conv_ln_g=v_conv_ln_g, v_conv_ln_b=v_conv_ln_b, v_w_out=v_w_out, v_ln1_g=v_ln1_g, v_ln1_b=v_ln1_b, v_mem_wq=v_mem_wq, v_mem_wk=v_mem_wk, v_mem_wv=v_mem_wv, v_mem_wo=v_mem_wo, v_ln2_g=v_ln2_g, v_ln2_b=v_ln2_b, v_ffn_up=v_ffn_up, v_ffn_conv_w=v_ffn_conv_w, v_ffn_conv_b=v_ffn_conv_b, v_ffn_down=v_ffn_down, v_ln3_g=v_ln3_g, v_ln3_b=v_ln3_b)
    weights = {n: given[n] for n in TWIN_WEIGHTS}
    shared = {n: given[n] for n in SHARED_INPUTS}
    per_example = {n: given[n] for n in ['x', 'mem']}
    grad_fn = _jax.value_and_grad(_loss, argnums=(0, 1))

    def one_microbatch(ex, loss_target):
        ex = dict(ex)
        diff = ex.pop(TWIN_DIFF_INPUT)
        return grad_fn(weights, diff, {**shared, **ex}, loss_target)

    if N_MICROBATCH == 1:
        loss, (grad_w, grad_x) = one_microbatch(per_example, given["loss_target"])
    else:
        def body(carry, xs):
            loss_sum, grad_sum = carry
            l_k, (gw_k, gx_k) = one_microbatch(xs[0], xs[1])
            with _jax.named_scope("update"):
                return (loss_sum + l_k, _jax.tree.map(_jnp.add, grad_sum, gw_k)), gx_k

        init = (_jnp.zeros((), _jnp.float32), _jax.tree.map(_jnp.zeros_like, weights))
        (loss, grad_w), grad_x = _jax.lax.scan(body, init, (per_example, given["loss_target"]))
    with _jax.named_scope("update"):
        delta_w, new_m, new_v = {}, {}, {}
        for n in TWIN_WEIGHTS:
            delta_w[n], new_m[n], new_v[n] = _adamw(weights[n], grad_w[n], given["m_" + n], given["v_" + n])
    return (loss, grad_x, *[grad_w[n] for n in TWIN_WEIGHTS], *[delta_w[n] for n in TWIN_WEIGHTS],
            *[new_m[n] for n in TWIN_WEIGHTS], *[new_v[n] for n in TWIN_WEIGHTS])
```

```python
import functools

import jax
import jax.numpy as jnp
from jax import lax
from jax.experimental import pallas as pl
from jax.experimental.pallas import tpu as pltpu

F32 = jnp.float32
BF16 = jnp.bfloat16

N_DEV = 8
LANES = 128
PACK_W = 1024
PACK_ROW_ALIGN = 256
SB_HEAD_DIM = 64
MEM_HEAD_DIM = 256
LN_EPS = 1e-5
VMEM_LIMIT = 56 * 1024 * 1024

ADAM_LR = 0.001
ADAM_B1 = 0.9
ADAM_B2 = 0.999
ADAM_EPS = 1e-08
ADAM_WD = 0.01
ADAM_STEP = 10

IN_NAMES = ['x', 'mem', 'w_in', 'conv_w', 'conv_b', 'conv_ln_g', 'conv_ln_b', 'w_out', 'ln1_g', 'ln1_b',
            'mem_wq', 'mem_wk', 'mem_wv', 'mem_wo', 'ln2_g', 'ln2_b', 'ffn_up', 'ffn_conv_w', 'ffn_conv_b',
            'ffn_down', 'ln3_g', 'ln3_b']
WEIGHTS = IN_NAMES[2:]
SHARDED = [('w_in', 1), ('conv_w', 1), ('w_out', 0), ('mem_wq', 0), ('mem_wk', 0), ('mem_wv', 0),
           ('mem_wo', 0), ('ffn_up', 1), ('ffn_conv_w', 1), ('ffn_down', 0)]
SHARDED_AXIS = dict(SHARDED)
REPLICATED = ['conv_b', 'conv_ln_g', 'conv_ln_b', 'ln1_g', 'ln1_b', 'ln2_g', 'ln2_b', 'ffn_conv_b', 'ln3_g', 'ln3_b']
F32_GATHERED = ['conv_w', 'ffn_conv_w']


def _pick(dim, pref, align=LANES):
    if dim <= pref:
        return dim
    t = (pref // align) * align
    while t >= align:
        if dim % t == 0:
            return t
        t -= align
    return dim


def _params(sem):
    return pltpu.CompilerParams(dimension_semantics=sem, vmem_limit_bytes=VMEM_LIMIT)


def _mm(a, b, *, ta=False, tb=False, out_dtype=F32, name, tm=512, tn=1024, tk=1024):
    if ta:
        kdim, m = a.shape
    else:
        m, kdim = a.shape
    if tb:
        n, kb = b.shape
    else:
        kb, n = b.shape
    assert kdim == kb, (a.shape, b.shape, ta, tb)
    tm, tn, tk = _pick(m, tm), _pick(n, tn), _pick(kdim, tk)
    nk = kdim // tk
    dims = (((0 if ta else 1,), (1 if tb else 0,)), ((), ()))

    def body(a_ref, b_ref, o_ref, *scratch):
        prod = lax.dot_general(a_ref[...].astype(BF16), b_ref[...].astype(BF16), dims,
                               preferred_element_type=F32)
        if nk == 1:
            o_ref[...] = prod.astype(out_dtype)
        else:
            acc_ref, = scratch
            k = pl.program_id(2)

            @pl.when(k == 0)
            def _():
                acc_ref[...] = prod

            @pl.when(k > 0)
            def _():
                acc_ref[...] += prod

            @pl.when(k == nk - 1)
            def _():
                o_ref[...] = acc_ref[...].astype(out_dtype)

    a_spec = pl.BlockSpec((tk, tm), lambda i, j, k: (k, i)) if ta else pl.BlockSpec((tm, tk), lambda i, j, k: (i, k))
    b_spec = pl.BlockSpec((tn, tk), lambda i, j, k: (j, k)) if tb else pl.BlockSpec((tk, tn), lambda i, j, k: (k, j))
    return pl.pallas_call(
        body, name=name,
        out_shape=jax.ShapeDtypeStruct((m, n), out_dtype),
        grid=(m // tm, n // tn, nk),
        in_specs=[a_spec, b_spec],
        out_specs=pl.BlockSpec((tm, tn), lambda i, j, k: (i, j)),
        scratch_shapes=[] if nk == 1 else [pltpu.VMEM((tm, tn), F32)],
        compiler_params=_params(("parallel", "parallel", "arbitrary")),
    )(a, b)


def _ln_stats(r):
    mu = jnp.mean(r, axis=-1, keepdims=True)
    xc = r - mu
    var = jnp.mean(xc * xc, axis=-1, keepdims=True)
    rstd = lax.rsqrt(var + LN_EPS)
    return xc * rstd, rstd


def _res_ln_fwd(x, f, g, b, *, alpha, name, ts=512):
    s, d = x.shape
    ts = _pick(s, ts)

    def body(x_ref, f_ref, g_ref, b_ref, r_ref, y_ref, yb_ref):
        r = alpha * x_ref[...] + f_ref[...]
        xhat, _ = _ln_stats(r)
        y = xhat * g_ref[...] + b_ref[...]
        r_ref[...] = r
        y_ref[...] = y
        yb_ref[...] = y.astype(BF16)

    tok = pl.BlockSpec((ts, d), lambda i: (i, 0))
    vec = pl.BlockSpec((1, d), lambda i: (0, 0))
    return pl.pallas_call(
        body, name=name,
        out_shape=(jax.ShapeDtypeStruct((s, d), F32), jax.ShapeDtypeStruct((s, d), F32),
                   jax.ShapeDtypeStruct((s, d), BF16)),
        grid=(s // ts,), in_specs=[tok, tok, vec, vec], out_specs=(tok, tok, tok),
        compiler_params=_params(("parallel",)),
    )(x, f, g.reshape(1, d), b.reshape(1, d))


def _ln_bwd(da, dres, r, g, *, alpha, name, ts=512):
    s, d = r.shape
    ts = _pick(s, ts)
    has_res = dres is not None

    def body(*refs):
        if has_res:
            da_ref, dres_ref, r_ref, g_ref, dr_ref, drb_ref, dg_ref, db_ref = refs
            dy = da_ref[...] + alpha * dres_ref[...]
        else:
            da_ref, r_ref, g_ref, dr_ref, drb_ref, dg_ref, db_ref = refs
            dy = da_ref[...]
        xhat, rstd = _ln_stats(r_ref[...])
        dxhat = dy * g_ref[...]
        m1 = jnp.mean(dxhat, axis=-1, keepdims=True)
        m2 = jnp.mean(dxhat * xhat, axis=-1, keepdims=True)
        dr = rstd * (dxhat - m1 - xhat * m2)
        dr_ref[...] = dr
        drb_ref[...] = dr.astype(BF16)

        @pl.when(pl.program_id(0) == 0)
        def _():
            dg_ref[...] = jnp.zeros_like(dg_ref)
            db_ref[...] = jnp.zeros_like(db_ref)

        dg_ref[...] += jnp.sum(dy * xhat, axis=0, keepdims=True)
        db_ref[...] += jnp.sum(dy, axis=0, keepdims=True)

    tok = pl.BlockSpec((ts, d), lambda i: (i, 0))
    vec = pl.BlockSpec((1, d), lambda i: (0, 0))
    ins = [da, dres, r, g.reshape(1, d)] if has_res else [da, r, g.reshape(1, d)]
    dr, drb, dg, db = pl.pallas_call(
        body, name=name,
        out_shape=(jax.ShapeDtypeStruct((s, d), F32), jax.ShapeDtypeStruct((s, d), BF16),
                   jax.ShapeDtypeStruct((1, d), F32), jax.ShapeDtypeStruct((1, d), F32)),
        grid=(s // ts,), in_specs=[tok] * (len(ins) - 1) + [vec], out_specs=(tok, tok, vec, vec),
        compiler_params=_params(("arbitrary",)),
    )(*ins)
    return dr, drb, dg.reshape(d), db.reshape(d)


def _axpy(a, b, *, alpha, name, ts=512):
    s, d = a.shape
    ts = _pick(s, ts)

    def body(a_ref, b_ref, o_ref):
        o_ref[...] = a_ref[...] + alpha * b_ref[...]

    tok = pl.BlockSpec((ts, d), lambda i: (i, 0))
    return pl.pallas_call(body, name=name, out_shape=jax.ShapeDtypeStruct((s, d), F32), grid=(s // ts,),
                          in_specs=[tok, tok], out_specs=tok, compiler_params=_params(("parallel",)))(a, b)


def _loss_and_grad(y, target, *, name, ts=512):
    s, d = y.shape
    ts = _pick(s, ts)
    inv_d = 1.0 / d

    def body(y_ref, t_ref, dy_ref, loss_ref):
        e = y_ref[...] - t_ref[...]
        dy_ref[...] = e * inv_d

        @pl.when(pl.program_id(0) == 0)
        def _():
            loss_ref[...] = jnp.zeros_like(loss_ref)

        loss_ref[...] += jnp.sum(e * e, axis=0, keepdims=True) * (0.5 * inv_d)

    tok = pl.BlockSpec((ts, d), lambda i: (i, 0))
    vec = pl.BlockSpec((1, d), lambda i: (0, 0))
    dy, part = pl.pallas_call(
        body, name=name,
        out_shape=(jax.ShapeDtypeStruct((s, d), F32), jax.ShapeDtypeStruct((1, d), F32)),
        grid=(s // ts,), in_specs=[tok, tok], out_specs=(tok, vec),
        compiler_params=_params(("arbitrary",)),
    )(y, target)
    return dy, part


def _row_sum(v, *, name):
    def body(v_ref, o_ref):
        o_ref[...] = jnp.sum(v_ref[...], axis=1, keepdims=True)

    return pl.pallas_call(body, name=name, out_shape=jax.ShapeDtypeStruct((1, 1), F32))(v)


def _sigmoid(x):
    return 1.0 / (1.0 + jnp.exp(-x))


def _row_chunks(s, pref=512):
    c = _pick(s, pref, 8)
    return [(i * c, c) for i in range(s // c)]


def _glu_conv_fwd(proj, w, b, *, cc, name):
    s = proj.shape[0]
    kw = w.shape[0]
    pad = 32
    assert kw - 1 <= pad
    ncb = cc // LANES
    chunks = _row_chunks(s)

    def body(a_ref, g_ref, w_ref, b_ref, o_ref, u0_ref):
        u0_ref[pl.ds(0, pad), :] = jnp.zeros((pad, LANES), F32)
        for r0, rc in chunks:
            u0_ref[pl.ds(pad + r0, rc), :] = a_ref[pl.ds(r0, rc), :] * _sigmoid(g_ref[pl.ds(r0, rc), :])
        for r0, rc in chunks:
            acc = jnp.zeros((rc, LANES), F32) + b_ref[...]
            for k in range(kw):
                acc = acc + w_ref[pl.ds(k, 1), :] * u0_ref[pl.ds(pad + r0 - (kw - 1) + k, rc), :]
            o_ref[pl.ds(r0, rc), :] = acc

    return pl.pallas_call(
        body, name=name,
        out_shape=jax.ShapeDtypeStruct((s, cc), F32),
        grid=(ncb,),
        in_specs=[pl.BlockSpec((s, LANES), lambda c: (0, c)), pl.BlockSpec((s, LANES), lambda c: (0, ncb + c)),
                  pl.BlockSpec((kw, LANES), lambda c: (0, c)), pl.BlockSpec((1, LANES), lambda c: (0, c))],
        out_specs=pl.BlockSpec((s, LANES), lambda c: (0, c)),
        scratch_shapes=[pltpu.VMEM((s + pad, LANES), F32)],
        compiler_params=_params(("parallel",)),
    )(proj, proj, w, b.reshape(1, cc))


def _glu_conv_bwd(du1, proj, w, *, cc, name):
    s = proj.shape[0]
    kw = w.shape[0]
    pad = 32
    ncb = cc // LANES
    chunks = _row_chunks(s)

    def body(d_ref, a_ref, g_ref, w_ref, da_ref, dg_ref, dw_ref, db_ref, u0_ref, dp_ref):
        u0_ref[pl.ds(0, pad), :] = jnp.zeros((pad, LANES), F32)
        dp_ref[pl.ds(s, pad), :] = jnp.zeros((pad, LANES), F32)
        for r0, rc in chunks:
            u0_ref[pl.ds(pad + r0, rc), :] = a_ref[pl.ds(r0, rc), :] * _sigmoid(g_ref[pl.ds(r0, rc), :])
            dp_ref[pl.ds(r0, rc), :] = d_ref[pl.ds(r0, rc), :]
        dws = [jnp.zeros((1, LANES), F32) for _ in range(kw)]
        dbs = jnp.zeros((1, LANES), F32)
        for r0, rc in chunks:
            d = dp_ref[pl.ds(r0, rc), :]
            dbs = dbs + jnp.sum(d, axis=0, keepdims=True)
            du0 = jnp.zeros((rc, LANES), F32)
            for k in range(kw):
                du0 = du0 + w_ref[pl.ds(k, 1), :] * dp_ref[pl.ds(r0 + (kw - 1) - k, rc), :]
                dws[k] = dws[k] + jnp.sum(d * u0_ref[pl.ds(pad + r0 - (kw - 1) + k, rc), :], axis=0, keepdims=True)
            sg = _sigmoid(g_ref[pl.ds(r0, rc), :])
            a = a_ref[pl.ds(r0, rc), :]
            da_ref[pl.ds(r0, rc), :] = (du0 * sg).astype(BF16)
            dg_ref[pl.ds(r0, rc), :] = (du0 * a * sg * (1.0 - sg)).astype(BF16)
        for k in range(kw):
            dw_ref[pl.ds(k, 1), :] = dws[k]
        db_ref[...] = dbs

    col = lambda off: pl.BlockSpec((s, LANES), lambda c: (0, off + c))
    da, dg, dw, db = pl.pallas_call(
        body, name=name,
        out_shape=(jax.ShapeDtypeStruct((s, cc), BF16), jax.ShapeDtypeStruct((s, cc), BF16),
                   jax.ShapeDtypeStruct((kw, cc), F32), jax.ShapeDtypeStruct((1, cc), F32)),
        grid=(ncb,),
        in_specs=[col(0), col(0), col(ncb), pl.BlockSpec((kw, LANES), lambda c: (0, c))],
        out_specs=(col(0), col(0), pl.BlockSpec((kw, LANES), lambda c: (0, c)), pl.BlockSpec((1, LANES), lambda c: (0, c))),
        scratch_shapes=[pltpu.VMEM((s + pad, LANES), F32), pltpu.VMEM((s + pad, LANES), F32)],
        compiler_params=_params(("parallel",)),
    )(du1, proj, proj, w)
    return da, dg, dw, db.reshape(cc)


def _cln_silu_fwd(u1, g, b, *, name, ts=512):
    s, cc = u1.shape
    ts = _pick(s, ts)

    def body(u_ref, g_ref, b_ref, o_ref):
        xhat, _ = _ln_stats(u_ref[...])
        y = xhat * g_ref[...] + b_ref[...]
        o_ref[...] = (y * _sigmoid(y)).astype(BF16)

    tok = pl.BlockSpec((ts, cc), lambda i: (i, 0))
    vec = pl.BlockSpec((1, cc), lambda i: (0, 0))
    return pl.pallas_call(body, name=name, out_shape=jax.ShapeDtypeStruct((s, cc), BF16), grid=(s // ts,),
                          in_specs=[tok, vec, vec], out_specs=tok,
                          compiler_params=_params(("parallel",)))(u1, g.reshape(1, cc), b.reshape(1, cc))


def _cln_silu_bwd(dua, u1, g, b, *, name, ts=512):
    s, cc = u1.shape
    ts = _pick(s, ts)

    def body(d_ref, u_ref, g_ref, b_ref, du_ref, dg_ref, db_ref):
        xhat, rstd = _ln_stats(u_ref[...])
        y = xhat * g_ref[...] + b_ref[...]
        sg = _sigmoid(y)
        dy = d_ref[...] * (sg * (1.0 + y * (1.0 - sg)))
        dxhat = dy * g_ref[...]
        m1 = jnp.mean(dxhat, axis=-1, keepdims=True)
        m2 = jnp.mean(dxhat * xhat, axis=-1, keepdims=True)
        du_ref[...] = rstd * (dxhat - m1 - xhat * m2)

        @pl.when(pl.program_id(0) == 0)
        def _():
            dg_ref[...] = jnp.zeros_like(dg_ref)
            db_ref[...] = jnp.zeros_like(db_ref)

        dg_ref[...] += jnp.sum(dy * xhat, axis=0, keepdims=True)
        db_ref[...] += jnp.sum(dy, axis=0, keepdims=True)

    tok = pl.BlockSpec((ts, cc), lambda i: (i, 0))
    vec = pl.BlockSpec((1, cc), lambda i: (0, 0))
    du1, dg, db = pl.pallas_call(
        body, name=name,
        out_shape=(jax.ShapeDtypeStruct((s, cc), F32), jax.ShapeDtypeStruct((1, cc), F32),
                   jax.ShapeDtypeStruct((1, cc), F32)),
        grid=(s // ts,), in_specs=[tok, tok, vec, vec], out_specs=(tok, vec, vec),
        compiler_params=_params(("arbitrary",)),
    )(dua, u1, g.reshape(1, cc), b.reshape(1, cc))
    return du1, dg.reshape(cc), db.reshape(cc)


def _softplus_parts(z):
    sp = jnp.log1p(jnp.exp(-jnp.abs(z)))
    return jnp.minimum(-z, 0.0) - sp, jnp.minimum(z, 0.0) - sp


def _split_dot(x, m):
    hi = x.astype(BF16)
    lo = (x - hi.astype(F32)).astype(BF16)
    return jnp.dot(hi, m, preferred_element_type=F32) + jnp.dot(lo, m, preferred_element_type=F32)


_NT = (((1,), (1,)), ((), ()))
_TN = (((0,), (0,)), ((), ()))


def _sb_fwd(proj, *, col0, n_pairs, name, tq=256):
    s = proj.shape[0]
    tq = _pick(s, tq)
    cb0 = col0 // LANES
    scale = SB_HEAD_DIM ** -0.5

    def body(q_ref, k_ref, v_ref, o_ref, t_ref):
        i = pl.program_id(1)
        lane = lax.broadcasted_iota(jnp.int32, (1, LANES), 1)
        row = lax.broadcasted_iota(jnp.int32, (tq, tq), 0)
        col = lax.broadcasted_iota(jnp.int32, (tq, tq), 1)
        vis = col < row
        m_after = (row > col).astype(BF16)
        q = q_ref[...] * scale
        acc_tot = jnp.zeros((tq, LANES), F32)
        t_tot = jnp.zeros((tq, LANES), F32)
        for h in range(2):
            hm = (lane >= SB_HEAD_DIM * h) & (lane < SB_HEAD_DIM * (h + 1))
            qh = jnp.where(hm, q, 0.0).astype(BF16)

            def tile(j, c, acc, masked, hm=hm, qh=qh):
                start = pl.multiple_of(j * tq, tq)
                kb = k_ref[pl.ds(start, tq), :].astype(BF16)
                vb = jnp.where(hm, v_ref[pl.ds(start, tq), :], 0.0).astype(BF16)
                z = lax.dot_general(qh, kb, _NT, preferred_element_type=F32)
                lk, lb = _softplus_parts(z)
                if masked:
                    lk = jnp.where(vis, lk, 0.0)
                later = _split_dot(lk, m_after)
                a = jnp.exp(lb + later + c)
                if masked:
                    a = jnp.where(vis, a, 0.0)
                acc = acc + jnp.dot(a.astype(BF16), vb, preferred_element_type=F32)
                return c + jnp.sum(lk, axis=1, keepdims=True), acc

            c, acc = tile(i, jnp.zeros((tq, 1), F32), jnp.zeros((tq, LANES), F32), True)
            c, acc = lax.fori_loop(0, i, lambda n, ca: tile(i - 1 - n, ca[0], ca[1], False), (c, acc))
            acc_tot = acc_tot + acc
            t_tot = jnp.where(hm, c, t_tot)
        o_ref[...] = acc_tot.astype(BF16)
        t_ref[...] = t_tot

    seq = lambda off: pl.BlockSpec((s, LANES), lambda p, i: (0, cb0 + off + p))
    return pl.pallas_call(
        body, name=name,
        out_shape=(jax.ShapeDtypeStruct((s, n_pairs * LANES), BF16), jax.ShapeDtypeStruct((n_pairs, s, LANES), F32)),
        grid=(n_pairs, s // tq),
        in_specs=[pl.BlockSpec((tq, LANES), lambda p, i: (i, cb0 + p)), seq(n_pairs), seq(2 * n_pairs)],
        out_specs=(pl.BlockSpec((tq, LANES), lambda p, i: (i, p)), pl.BlockSpec((None, tq, LANES), lambda p, i: (p, i, 0))),
        compiler_params=_params(("parallel", "arbitrary")),
    )(proj, proj, proj)


def _sb_bwd(proj, t_sum, dua, *, col0, n_pairs, do_col0, name, tq=256):
    s = proj.shape[0]
    tq = _pick(s, tq)
    cb0 = col0 // LANES
    dcb0 = do_col0 // LANES
    scale = SB_HEAD_DIM ** -0.5

    def body(q_ref, k_ref, v_ref, t_ref, do_ref, dq_ref, dk_ref, dv_ref):
        i = pl.program_id(1)

        @pl.when(i == 0)
        def _():
            dk_ref[...] = jnp.zeros_like(dk_ref)
            dv_ref[...] = jnp.zeros_like(dv_ref)

        lane = lax.broadcasted_iota(jnp.int32, (1, LANES), 1)
        row = lax.broadcasted_iota(jnp.int32, (tq, tq), 0)
        col = lax.broadcasted_iota(jnp.int32, (tq, tq), 1)
        vis = col < row
        m_upto = (row <= col).astype(BF16)
        m_before = (row < col).astype(BF16)
        q = q_ref[...] * scale
        do = do_ref[...]
        dq_tot = jnp.zeros((tq, LANES), F32)
        for h in range(2):
            hm = (lane >= SB_HEAD_DIM * h) & (lane < SB_HEAD_DIM * (h + 1))
            qh = jnp.where(hm, q, 0.0).astype(BF16)
            doh = jnp.where(hm, do, 0.0).astype(BF16)
            t_all = t_ref[:, SB_HEAD_DIM * h:SB_HEAD_DIM * h + 1]

            def tile(j, p_sum, g_sum, dq, masked, hm=hm, qh=qh, doh=doh, t_all=t_all):
                start = pl.multiple_of(j * tq, tq)
                kb = jnp.where(hm, k_ref[pl.ds(start, tq), :], 0.0).astype(BF16)
                vb = jnp.where(hm, v_ref[pl.ds(start, tq), :], 0.0).astype(BF16)
                z = lax.dot_general(qh, kb, _NT, preferred_element_type=F32)
                lk_raw, lb = _softplus_parts(z)
                lk = jnp.where(vis, lk_raw, 0.0) if masked else lk_raw
                upto = _split_dot(lk, m_upto)
                a = jnp.exp(lb + (t_all - p_sum) - upto)
                if masked:
                    a = jnp.where(vis, a, 0.0)
                ab = a.astype(BF16)
                da = lax.dot_general(doh, vb, _NT, preferred_element_type=F32)
                g = a * da
                dv_ref[pl.ds(start, tq), :] += lax.dot_general(ab, doh, _TN, preferred_element_type=F32)
                g_before = g_sum + _split_dot(g, m_before)
                dz = g * jnp.exp(lk_raw) - g_before * jnp.exp(lb)
                if masked:
                    dz = jnp.where(vis, dz, 0.0)
                dzb = dz.astype(BF16)
                dq = dq + jnp.dot(dzb, kb, preferred_element_type=F32)
                dk_ref[pl.ds(start, tq), :] += lax.dot_general(dzb, qh, _TN, preferred_element_type=F32)
                return (p_sum + jnp.sum(lk, axis=1, keepdims=True), g_sum + jnp.sum(g, axis=1, keepdims=True), dq)

            zero = jnp.zeros((tq, 1), F32)
            carry = lax.fori_loop(0, i, lambda j, c: tile(j, c[0], c[1], c[2], False),
                                  (zero, zero, jnp.zeros((tq, LANES), F32)))
            _, _, dq = tile(i, carry[0], carry[1], carry[2], True)
            dq_tot = dq_tot + dq
        dq_ref[...] = dq_tot * scale

    seq = lambda off: pl.BlockSpec((s, LANES), lambda p, i: (0, cb0 + off + p))
    out = jax.ShapeDtypeStruct((s, n_pairs * LANES), F32)
    res = pl.BlockSpec((s, LANES), lambda p, i: (0, p))
    return pl.pallas_call(
        body, name=name,
        out_shape=(out, out, out),
        grid=(n_pairs, s // tq),
        in_specs=[pl.BlockSpec((tq, LANES), lambda p, i: (i, cb0 + p)), seq(n_pairs), seq(2 * n_pairs),
                  pl.BlockSpec((None, tq, LANES), lambda p, i: (p, i, 0)),
                  pl.BlockSpec((tq, LANES), lambda p, i: (i, dcb0 + p))],
        out_specs=(pl.BlockSpec((tq, LANES), lambda p, i: (i, p)), res, res),
        compiler_params=_params(("arbitrary", "arbitrary")),
    )(proj, proj, proj, t_sum, dua)


def _mem_attn_fwd(qm, km, vm, *, name, tq=512):
    s, d = qm.shape
    heads = d // MEM_HEAD_DIM
    mlen = km.shape[0]
    tq = _pick(s, tq)
    scale = MEM_HEAD_DIM ** -0.5

    def body(q_ref, k_ref, v_ref, o_ref):
        for h in range(heads):
            sl = slice(h * MEM_HEAD_DIM, (h + 1) * MEM_HEAD_DIM)
            q = (q_ref[:, sl] * scale).astype(BF16)
            sc = lax.dot_general(q, k_ref[:, sl].astype(BF16), _NT, preferred_element_type=F32)
            e = jnp.exp(sc - jnp.max(sc, axis=1, keepdims=True))
            p = e / jnp.sum(e, axis=1, keepdims=True)
            o_ref[:, sl] = jnp.dot(p.astype(BF16), v_ref[:, sl].astype(BF16), preferred_element_type=F32).astype(BF16)

    tok = pl.BlockSpec((tq, d), lambda i: (i, 0))
    kv = pl.BlockSpec((mlen, d), lambda i: (0, 0))
    return pl.pallas_call(body, name=name, out_shape=jax.ShapeDtypeStruct((s, d), BF16), grid=(s // tq,),
                          in_specs=[tok, kv, kv], out_specs=tok, compiler_params=_params(("parallel",)))(qm, km, vm)


def _mem_attn_bwd(qm, km, vm, do, *, name, tq=512):
    s, d = qm.shape
    heads = d // MEM_HEAD_DIM
    mlen = km.shape[0]
    tq = _pick(s, tq)
    scale = MEM_HEAD_DIM ** -0.5

    def body(q_ref, k_ref, v_ref, do_ref, dq_ref, dk_ref, dv_ref):
        @pl.when(pl.program_id(0) == 0)
        def _():
            dk_ref[...] = jnp.zeros_like(dk_ref)
            dv_ref[...] = jnp.zeros_like(dv_ref)

        for h in range(heads):
            sl = slice(h * MEM_HEAD_DIM, (h + 1) * MEM_HEAD_DIM)
            q = (q_ref[:, sl] * scale).astype(BF16)
            k = k_ref[:, sl].astype(BF16)
            v = v_ref[:, sl].astype(BF16)
            sc = lax.dot_general(q, k, _NT, preferred_element_type=F32)
            e = jnp.exp(sc - jnp.max(sc, axis=1, keepdims=True))
            p = e / jnp.sum(e, axis=1, keepdims=True)
            dob = do_ref[:, sl].astype(BF16)
            dv_ref[:, sl] += lax.dot_general(p.astype(BF16), dob, _TN, preferred_element_type=F32)
            dp = lax.dot_general(dob, v, _NT, preferred_element_type=F32)
            ds = (p * (dp - jnp.sum(dp * p, axis=1, keepdims=True))).astype(BF16)
            dq_ref[:, sl] = (jnp.dot(ds, k, preferred_element_type=F32) * scale).astype(BF16)
            dk_ref[:, sl] += lax.dot_general(ds, q, _TN, preferred_element_type=F32)

    tok = pl.BlockSpec((tq, d), lambda i: (i, 0))
    kv = pl.BlockSpec((mlen, d), lambda i: (0, 0))
    return pl.pallas_call(
        body, name=name,
        out_shape=(jax.ShapeDtypeStruct((s, d), BF16), jax.ShapeDtypeStruct((mlen, d), F32),
                   jax.ShapeDtypeStruct((mlen, d), F32)),
        grid=(s // tq,), in_specs=[tok, kv, kv, tok], out_specs=(tok, kv, kv),
        compiler_params=_params(("arbitrary",)),
    )(qm, km, vm, do)


def _ffn_act_fwd(up, w, b, *, name):
    s, two_f = up.shape
    ff = two_f // 2
    nfb = ff // LANES
    kw = w.shape[0]
    pad = 8
    chunks = _row_chunks(s)

    def body(v_ref, g_ref, wv_ref, wg_ref, bv_ref, bg_ref, o_ref, vp_ref, gp_ref):
        vp_ref[pl.ds(0, pad), :] = jnp.zeros((pad, LANES), F32)
        gp_ref[pl.ds(0, pad), :] = jnp.zeros((pad, LANES), F32)
        for r0, rc in chunks:
            vp_ref[pl.ds(pad + r0, rc), :] = v_ref[pl.ds(r0, rc), :]
            gp_ref[pl.ds(pad + r0, rc), :] = g_ref[pl.ds(r0, rc), :]
        for r0, rc in chunks:
            vc = jnp.zeros((rc, LANES), F32) + bv_ref[...]
            gc = jnp.zeros((rc, LANES), F32) + bg_ref[...]
            for k in range(kw):
                off = pad + r0 - (kw - 1) + k
                vc = vc + wv_ref[pl.ds(k, 1), :] * vp_ref[pl.ds(off, rc), :]
                gc = gc + wg_ref[pl.ds(k, 1), :] * gp_ref[pl.ds(off, rc), :]
            o_ref[pl.ds(r0, rc), :] = (gc * _sigmoid(gc) * vc).astype(BF16)

    col = lambda off: pl.BlockSpec((s, LANES), lambda c: (0, off + c))
    tap = lambda off: pl.BlockSpec((kw, LANES), lambda c: (0, off + c))
    vec = lambda off: pl.BlockSpec((1, LANES), lambda c: (0, off + c))
    return pl.pallas_call(
        body, name=name, out_shape=jax.ShapeDtypeStruct((s, ff), BF16), grid=(nfb,),
        in_specs=[col(0), col(nfb), tap(0), tap(nfb), vec(0), vec(nfb)], out_specs=col(0),
        scratch_shapes=[pltpu.VMEM((s + pad, LANES), F32), pltpu.VMEM((s + pad, LANES), F32)],
        compiler_params=_params(("parallel",)),
    )(up, up, w, w, b.reshape(1, two_f), b.reshape(1, two_f))


def _ffn_act_bwd(up, dact, w, b, *, name):
    s, two_f = up.shape
    ff = two_f // 2
    nfb = ff // LANES
    kw = w.shape[0]
    pad = 8
    chunks = _row_chunks(s)

    def body(v_ref, g_ref, d_ref, wv_ref, wg_ref, bv_ref, bg_ref, dv_ref, dg_ref, dwv_ref, dwg_ref, dbv_ref, dbg_ref,
             vp_ref, gp_ref, dvc_ref, dgc_ref):
        vp_ref[pl.ds(0, pad), :] = jnp.zeros((pad, LANES), F32)
        gp_ref[pl.ds(0, pad), :] = jnp.zeros((pad, LANES), F32)
        dvc_ref[pl.ds(s, pad), :] = jnp.zeros((pad, LANES), F32)
        dgc_ref[pl.ds(s, pad), :] = jnp.zeros((pad, LANES), F32)
        for r0, rc in chunks:
            vp_ref[pl.ds(pad + r0, rc), :] = v_ref[pl.ds(r0, rc), :]
            gp_ref[pl.ds(pad + r0, rc), :] = g_ref[pl.ds(r0, rc), :]
        dwv = [jnp.zeros((1, LANES), F32) for _ in range(kw)]
        dwg = [jnp.zeros((1, LANES), F32) for _ in range(kw)]
        dbv = jnp.zeros((1, LANES), F32)
        dbg = jnp.zeros((1, LANES), F32)
        for r0, rc in chunks:
            vc = jnp.zeros((rc, LANES), F32) + bv_ref[...]
            gc = jnp.zeros((rc, LANES), F32) + bg_ref[...]
            for k in range(kw):
                off = pad + r0 - (kw - 1) + k
                vc = vc + wv_ref[pl.ds(k, 1), :] * vp_ref[pl.ds(off, rc), :]
                gc = gc + wg_ref[pl.ds(k, 1), :] * gp_ref[pl.ds(off, rc), :]
            sg = _sigmoid(gc)
            d = d_ref[pl.ds(r0, rc), :]
            dvc = d * (gc * sg)
            dgc = d * vc * (sg * (1.0 + gc * (1.0 - sg)))
            dvc_ref[pl.ds(r0, rc), :] = dvc
            dgc_ref[pl.ds(r0, rc), :] = dgc
            dbv = dbv + jnp.sum(dvc, axis=0, keepdims=True)
            dbg = dbg + jnp.sum(dgc, axis=0, keepdims=True)
            for k in range(kw):
                off = pad + r0 - (kw - 1) + k
                dwv[k] = dwv[k] + jnp.sum(dvc * vp_ref[pl.ds(off, rc), :], axis=0, keepdims=True)
                dwg[k] = dwg[k] + jnp.sum(dgc * gp_ref[pl.ds(off, rc), :], axis=0, keepdims=True)
        for r0, rc in chunks:
            dv = jnp.zeros((rc, LANES), F32)
            dg = jnp.zeros((rc, LANES), F32)
            for k in range(kw):
                off = r0 + (kw - 1) - k
                dv = dv + wv_ref[pl.ds(k, 1), :] * dvc_ref[pl.ds(off, rc), :]
                dg = dg + wg_ref[pl.ds(k, 1), :] * dgc_ref[pl.ds(off, rc), :]
            dv_ref[pl.ds(r0, rc), :] = dv.astype(BF16)
            dg_ref[pl.ds(r0, rc), :] = dg.astype(BF16)
        for k in range(kw):
            dwv_ref[pl.ds(k, 1), :] = dwv[k]
            dwg_ref[pl.ds(k, 1), :] = dwg[k]
        dbv_ref[...] = dbv
        dbg_ref[...] = dbg

    col = lambda off: pl.BlockSpec((s, LANES), lambda c: (0, off + c))
    tap = lambda off: pl.BlockSpec((kw, LANES), lambda c: (0, off + c))
    vec = lambda off: pl.BlockSpec((1, LANES), lambda c: (0, off + c))
    big = lambda: pltpu.VMEM((s + pad, LANES), F32)
    dv, dg, dwv, dwg, dbv, dbg = pl.pallas_call(
        body, name=name,
        out_shape=(jax.ShapeDtypeStruct((s, ff), BF16), jax.ShapeDtypeStruct((s, ff), BF16),
                   jax.ShapeDtypeStruct((kw, ff), F32), jax.ShapeDtypeStruct((kw, ff), F32),
                   jax.ShapeDtypeStruct((1, ff), F32), jax.ShapeDtypeStruct((1, ff), F32)),
        grid=(nfb,),
        in_specs=[col(0), col(nfb), col(0), tap(0), tap(nfb), vec(0), vec(nfb)],
        out_specs=(col(0), col(0), tap(0), tap(0), vec(0), vec(0)),
        scratch_shapes=[big(), big(), big(), big()],
        compiler_params=_params(("parallel",)),
    )(up, up, dact, w, w, b.reshape(1, two_f), b.reshape(1, two_f))
    return (jnp.concatenate([dv, dg], axis=1), jnp.concatenate([dwv, dwg], axis=1),
            jnp.concatenate([dbv, dbg], axis=1).reshape(two_f))


def _sum_adamw(parts, w, m, v, *, name, tr=256):
    rows = w.shape[0]
    tr = _pick(rows, tr, 8)
    c1 = 1.0 / (1.0 - ADAM_B1 ** ADAM_STEP)
    c2 = 1.0 / (1.0 - ADAM_B2 ** ADAM_STEP)

    def body(p_ref, w_ref, m_ref, v_ref, g_ref, d_ref, nm_ref, nv_ref):
        g = p_ref[0].astype(F32)
        for k in range(1, N_DEV):
            g = g + p_ref[k].astype(F32)
        nm = ADAM_B1 * m_ref[...] + (1.0 - ADAM_B1) * g
        nv = ADAM_B2 * v_ref[...] + (1.0 - ADAM_B2) * (g * g)
        g_ref[...] = g
        nm_ref[...] = nm
        nv_ref[...] = nv
        d_ref[...] = -ADAM_LR * ((nm * c1) / (jnp.sqrt(nv * c2) + ADAM_EPS) + ADAM_WD * w_ref[...])

    blk = pl.BlockSpec((tr, PACK_W), lambda i: (i, 0))
    out = jax.ShapeDtypeStruct((rows, PACK_W), F32)
    return pl.pallas_call(
        body, name=name, out_shape=(out, out, out, out), grid=(rows // tr,),
        in_specs=[pl.BlockSpec((N_DEV, tr, PACK_W), lambda i: (0, i, 0)), blk, blk, blk],
        out_specs=(blk, blk, blk, blk), compiler_params=_params(("parallel",)),
    )(parts, w, m, v)


def _mesh_pos():
    return lax.axis_index("x"), lax.axis_index("y"), lax.axis_index("c")


def _flip(pos, k):
    x, y, c = pos
    return (1 - x if k & 4 else x, 1 - y if k & 2 else y, 1 - c if k & 1 else c)


def _dev_index(pos):
    return 4 * pos[0] + 2 * pos[1] + pos[2]


def _all_gather(x, *, name):
    rows, cols = x.shape

    def body(x_ref, out_ref, send_sems, recv_sems, local_sem):
        me = _mesh_pos()
        sibling = _flip(me, 1)
        chips = [_flip(me, 4), _flip(me, 2), _flip(me, 6)]

        def copy(k, block, to, src=None):
            slot = out_ref.at[_dev_index(block)]
            return pltpu.make_async_remote_copy(
                src_ref=slot if src is None else src, dst_ref=slot,
                send_sem=send_sems.at[k], recv_sem=recv_sems.at[k],
                device_id=to, device_id_type=pl.DeviceIdType.MESH)

        mine = pltpu.make_async_copy(x_ref, out_ref.at[_dev_index(me)], local_sem)
        mine.start()
        first = [copy(0, me, sibling, src=x_ref)] + [copy(1 + j, me, chip, src=x_ref) for j, chip in enumerate(chips)]
        for cp in first:
            cp.start()
        passed = [copy(4 + j, chip, sibling) for j, chip in enumerate(chips)]
        for j, chip in enumerate(chips):
            copy(1 + j, chip, me).wait_recv()
            passed[j].start()
        copy(0, sibling, me).wait_recv()
        for j, chip in enumerate(chips):
            copy(4 + j, _flip(chip, 1), me).wait_recv()
        for cp in first + passed:
            cp.wait_send()
        mine.wait()

    return pl.pallas_call(
        body, name=name,
        out_shape=jax.ShapeDtypeStruct((N_DEV, rows, cols), x.dtype),
        in_specs=[pl.BlockSpec(memory_space=pl.ANY)],
        out_specs=pl.BlockSpec(memory_space=pl.ANY),
        scratch_shapes=[pltpu.SemaphoreType.DMA((7,)), pltpu.SemaphoreType.DMA((7,)), pltpu.SemaphoreType.DMA],
    )(x)


def _all_to_all(x, *, name):
    _, rows, cols = x.shape

    def body(x_ref, out_ref, send_sems, recv_sems, local_sem):
        me = _mesh_pos()
        my_slot = _dev_index(me)
        mine = pltpu.make_async_copy(x_ref.at[my_slot], out_ref.at[my_slot], local_sem)
        mine.start()

        def copy(k):
            peer = _flip(me, k)
            return pltpu.make_async_remote_copy(
                src_ref=x_ref.at[_dev_index(peer)], dst_ref=out_ref.at[my_slot],
                send_sem=send_sems.at[k - 1], recv_sem=recv_sems.at[k - 1],
                device_id=peer, device_id_type=pl.DeviceIdType.MESH)

        copies = [copy(k) for k in range(1, N_DEV)]
        for cp in copies:
            cp.start()
        for cp in copies:
            cp.wait_recv()
        for cp in copies:
            cp.wait_send()
        mine.wait()

    return pl.pallas_call(
        body, name=name,
        out_shape=jax.ShapeDtypeStruct((N_DEV, rows, cols), x.dtype),
        in_specs=[pl.BlockSpec(memory_space=pl.ANY)],
        out_specs=pl.BlockSpec(memory_space=pl.ANY),
        scratch_shapes=[pltpu.SemaphoreType.DMA((7,)), pltpu.SemaphoreType.DMA((7,)), pltpu.SemaphoreType.DMA],
    )(x)


def _pack(arrays, dtype):
    flat = jnp.concatenate([a.reshape(-1).astype(dtype) for a in arrays])
    n = flat.shape[0]
    tile = PACK_W * PACK_ROW_ALIGN
    total = -(-n // tile) * tile
    return jnp.pad(flat, (0, total - n)).reshape(total // PACK_W, PACK_W)


def _unpack(buf, shapes):
    lead = buf.shape[:-2]
    flat = buf.reshape(lead + (-1,))
    out, off = [], 0
    for shp in shapes:
        n = 1
        for dim in shp:
            n *= dim
        out.append(flat[..., off:off + n].reshape(lead + tuple(shp)))
        off += n
    return out


def _shard_of(full, axis, dev, n_dev=N_DEV):
    size = full.shape[axis] // n_dev
    return lax.slice_in_dim(full, dev * size, (dev + 1) * size, axis=axis)


def _join_shards(blocks, axis):
    if axis == 0:
        return blocks.reshape((-1,) + blocks.shape[2:])
    return jnp.moveaxis(blocks, 0, 1).reshape(blocks.shape[1], -1)


def _layer_fwd(x, xb, memb, w, alpha):
    cc = w['conv_w'].shape[1]
    n_pairs = (w['w_out'].shape[0] - cc) // LANES
    proj = _mm(xb, w['w_in'], name="mm_proj")
    u1 = _glu_conv_fwd(proj, w['conv_w'], w['conv_b'], cc=cc, name="glu_conv_fwd")
    u = _cln_silu_fwd(u1, w['conv_ln_g'], w['conv_ln_b'], name="cln_silu_fwd")
    att, t_sum = _sb_fwd(proj, col0=2 * cc, n_pairs=n_pairs, name="sb_fwd")
    ua = jnp.concatenate([u, att], axis=1)
    mix = _mm(ua, w['w_out'], name="mm_mix")
    r1, x1, x1b = _res_ln_fwd(x, mix, w['ln1_g'], w['ln1_b'], alpha=alpha, name="res_ln_fwd")
    qm = _mm(x1b, w['mem_wq'], name="mm_memq")
    km = _mm(memb, w['mem_wk'], name="mm_memkv")
    vm = _mm(memb, w['mem_wv'], name="mm_memkv")
    o = _mem_attn_fwd(qm, km, vm, name="mem_attn_fwd")
    cross = _mm(o, w['mem_wo'], name="mm_mix")
    r2, x2, x2b = _res_ln_fwd(x1, cross, w['ln2_g'], w['ln2_b'], alpha=alpha, name="res_ln_fwd")
    up = _mm(x2b, w['ffn_up'], name="mm_up")
    act = _ffn_act_fwd(up, w['ffn_conv_w'], w['ffn_conv_b'], name="ffn_act_fwd")
    ffn = _mm(act, w['ffn_down'], name="mm_down")
    r3, x3, x3b = _res_ln_fwd(x2, ffn, w['ln3_g'], w['ln3_b'], alpha=alpha, name="res_ln_fwd")
    saved = dict(xb=xb, proj=proj, u1=u1, t_sum=t_sum, ua=ua, r1=r1, x1b=x1b, qm=qm, km=km, vm=vm, o=o,
                 r2=r2, x2b=x2b, up=up, act=act, r3=r3)
    return x3, x3b, saved


def _layer_bwd(da, dres, sv, memb, w, alpha):
    g = {}
    cc = w['conv_w'].shape[1]
    n_pairs = (w['w_out'].shape[0] - cc) // LANES
    dr3, dr3b, g['ln3_g'], g['ln3_b'] = _ln_bwd(da, dres, sv['r3'], w['ln3_g'], alpha=alpha, name="ln_bwd")
    g['ffn_down'] = _mm(sv['act'], dr3b, ta=True, name="mm_dw_down")
    dact = _mm(dr3b, w['ffn_down'], tb=True, name="mm_dact")
    dup, g['ffn_conv_w'], g['ffn_conv_b'] = _ffn_act_bwd(sv['up'], dact, w['ffn_conv_w'], w['ffn_conv_b'], name="ffn_act_bwd")
    g['ffn_up'] = _mm(sv['x2b'], dup, ta=True, name="mm_dw_up")
    da2 = _mm(dup, w['ffn_up'], tb=True, name="mm_dx_up")
    dr2, dr2b, g['ln2_g'], g['ln2_b'] = _ln_bwd(da2, dr3, sv['r2'], w['ln2_g'], alpha=alpha, name="ln_bwd")
    g['mem_wo'] = _mm(sv['o'], dr2b, ta=True, name="mm_dw_sq")
    do = _mm(dr2b, w['mem_wo'], tb=True, name="mm_dx_sq")
    dqm, dkm, dvm = _mem_attn_bwd(sv['qm'], sv['km'], sv['vm'], do, name="mem_attn_bwd")
    g['mem_wq'] = _mm(sv['x1b'], dqm, ta=True, name="mm_dw_sq")
    g['mem_wk'] = _mm(memb, dkm, ta=True, name="mm_dw_memkv")
    g['mem_wv'] = _mm(memb, dvm, ta=True, name="mm_dw_memkv")
    da1 = _mm(dqm, w['mem_wq'], tb=True, name="mm_dx_sq")
    dr1, dr1b, g['ln1_g'], g['ln1_b'] = _ln_bwd(da1, dr2, sv['r1'], w['ln1_g'], alpha=alpha, name="ln_bwd")
    g['w_out'] = _mm(sv['ua'], dr1b, ta=True, name="mm_dw_sq")
    dua = _mm(dr1b, w['w_out'], tb=True, name="mm_dx_sq")
    du1, g['conv_ln_g'], g['conv_ln_b'] = _cln_silu_bwd(dua, sv['u1'], w['conv_ln_g'], w['conv_ln_b'], name="cln_silu_bwd")
    dga, dgg, g['conv_w'], g['conv_b'] = _glu_conv_bwd(du1, sv['proj'], w['conv_w'], cc=cc, name="glu_conv_bwd")
    dq, dk, dv = _sb_bwd(sv['proj'], sv['t_sum'], dua, col0=2 * cc, n_pairs=n_pairs, do_col0=cc, name="sb_bwd")
    dproj = jnp.concatenate([dga, dgg, dq.astype(BF16), dk.astype(BF16), dv.astype(BF16)], axis=1)
    g['w_in'] = _mm(sv['xb'], dproj, ta=True, name="mm_dw_in")
    da0 = _mm(dproj, w['w_in'], tb=True, name="mm_dx_in")
    return da0, dr1, g


def kernel(x, mem, w_in, conv_w, conv_b, conv_ln_g, conv_ln_b, w_out, ln1_g, ln1_b, mem_wq, mem_wk, mem_wv, mem_wo, ln2_g, ln2_b, ffn_up, ffn_conv_w, ffn_conv_b, ffn_down, ln3_g, ln3_b, loss_target, m_w_in, m_conv_w, m_conv_b, m_conv_ln_g, m_conv_ln_b, m_w_out, m_ln1_g, m_ln1_b, m_mem_wq, m_mem_wk, m_mem_wv, m_mem_wo, m_ln2_g, m_ln2_b, m_ffn_up, m_ffn_conv_w, m_ffn_conv_b, m_ffn_down, m_ln3_g, m_ln3_b, v_w_in, v_conv_w, v_conv_b, v_conv_ln_g, v_conv_ln_b, v_w_out, v_ln1_g, v_ln1_b, v_mem_wq, v_mem_wk, v_mem_wv, v_mem_wo, v_ln2_g, v_ln2_b, v_ffn_up, v_ffn_conv_w, v_ffn_conv_b, v_ffn_down, v_ln3_g, v_ln3_b):
    wts = dict(zip(WEIGHTS, (w_in, conv_w, conv_b, conv_ln_g, conv_ln_b, w_out, ln1_g, ln1_b, mem_wq, mem_wk, mem_wv,
                             mem_wo, ln2_g, ln2_b, ffn_up, ffn_conv_w, ffn_conv_b, ffn_down, ln3_g, ln3_b)))
    mom = dict(zip(WEIGHTS, (m_w_in, m_conv_w, m_conv_b, m_conv_ln_g, m_conv_ln_b, m_w_out, m_ln1_g, m_ln1_b, m_mem_wq,
                             m_mem_wk, m_mem_wv, m_mem_wo, m_ln2_g, m_ln2_b, m_ffn_up, m_ffn_conv_w, m_ffn_conv_b,
                             m_ffn_down, m_ln3_g, m_ln3_b)))
    var = dict(zip(WEIGHTS, (v_w_in, v_conv_w, v_conv_b, v_conv_ln_g, v_conv_ln_b, v_w_out, v_ln1_g, v_ln1_b, v_mem_wq,
                             v_mem_wk, v_mem_wv, v_mem_wo, v_ln2_g, v_ln2_b, v_ffn_up, v_ffn_conv_w, v_ffn_conv_b,
                             v_ffn_down, v_ln3_g, v_ln3_b)))
    depth = w_in.shape[0]
    alpha = (2.0 * depth) ** 0.25
    sh_names = [n for n, _ in SHARDED]
    shard_shapes = [wts[n].shape for n in sh_names]
    layer_shard_shapes = [s[1:] for s in shard_shapes]

    gathered = _all_gather(_pack([wts[n] for n in sh_names], BF16), name="gather_weights")
    taps = _all_gather(_pack([wts[n] for n in F32_GATHERED], F32), name="gather_taps")
    full = dict(zip(sh_names, _unpack(gathered, shard_shapes)))
    full.update(zip(F32_GATHERED, _unpack(taps, [wts[n].shape for n in F32_GATHERED])))

    def layer_weights(l):
        w = {n: _join_shards(full[n][:, l], SHARDED_AXIS[n]) for n in sh_names}
        w.update({n: wts[n][l] for n in REPLICATED})
        return w

    xs = x[0]
    memb = mem[0].astype(BF16)
    h, hb = xs, xs.astype(BF16)
    saved = []
    for l in range(depth):
        h, hb, sv = _layer_fwd(h, hb, memb, layer_weights(l), alpha)
        saved.append(sv)

    dy, loss_row = _loss_and_grad(h, loss_target[0], name="loss")
    loss = lax.psum(_row_sum(loss_row, name="loss_sum")[0, 0], ("x", "y", "c"))

    da, dres = dy, None
    grads = [None] * depth
    for l in reversed(range(depth)):
        da, dres, grads[l] = _layer_bwd(da, dres, saved[l], memb, layer_weights(l), alpha)
    grad_x = _axpy(da, dres, alpha=alpha, name="grad_x")[None]

    outgoing = jnp.stack([
        _pack([jnp.stack([_shard_of(grads[l][n], SHARDED_AXIS[n], d) for l in range(depth)]) for n in sh_names], BF16)
        for d in range(N_DEV)])
    incoming = _all_to_all(outgoing, name="scatter_grads")
    packed = [_pack([src[n] for n in sh_names], F32) for src in (wts, mom, var)]
    res = _sum_adamw(incoming, *packed, name="adamw_sharded")
    g_sh, d_sh, m_sh, v_sh = [dict(zip(sh_names, _unpack(r, shard_shapes))) for r in res]

    rep_shapes = [wts[n].shape for n in REPLICATED]
    part = _pack([jnp.stack([grads[l][n] for l in range(depth)]) for n in REPLICATED], F32)
    parts = _all_gather(part, name="gather_small_grads")
    packed = [_pack([src[n] for n in REPLICATED], F32) for src in (wts, mom, var)]
    res = _sum_adamw(parts, *packed, name="adamw_replicated")
    g_rp, d_rp, m_rp, v_rp = [dict(zip(REPLICATED, _unpack(r, rep_shapes))) for r in res]

    outs = [loss, grad_x]
    for sharded, replicated in ((g_sh, g_rp), (d_sh, d_rp), (m_sh, m_rp), (v_sh, v_rp)):
        outs += [sharded[n] if n in sharded else replicated[n] for n in WEIGHTS]
    return tuple(outs)
```

```python
import functools

import jax
import jax.numpy as jnp
from jax import lax
from jax.experimental import pallas as pl
from jax.experimental.pallas import tpu as pltpu

F32 = jnp.float32
BF16 = jnp.bfloat16

N_DEV = 8
LANES = 128
PACK_W = 1024
PACK_ROW_ALIGN = 16
SB_HEAD_DIM = 64
MEM_HEAD_DIM = 256
LN_EPS = 1e-5
EXP_UNDERFLOW = 104.0
VMEM_LIMIT = 56 * 1024 * 1024

ADAM_LR = 0.001
ADAM_B1 = 0.9
ADAM_B2 = 0.999
ADAM_EPS = 1e-08
ADAM_WD = 0.01
ADAM_STEP = 10

IN_NAMES = ['x', 'mem', 'w_in', 'conv_w', 'conv_b', 'conv_ln_g', 'conv_ln_b', 'w_out', 'ln1_g', 'ln1_b',
            'mem_wq', 'mem_wk', 'mem_wv', 'mem_wo', 'ln2_g', 'ln2_b', 'ffn_up', 'ffn_conv_w', 'ffn_conv_b',
            'ffn_down', 'ln3_g', 'ln3_b']
WEIGHTS = IN_NAMES[2:]
MATRICES = [('w_in', True), ('w_out', False), ('mem_wq', False), ('mem_wk', False), ('mem_wv', False),
            ('mem_wo', False), ('ffn_up', True), ('ffn_down', False)]
TAPS = ['conv_w', 'ffn_conv_w']
REPLICATED = ['conv_b', 'conv_ln_g', 'conv_ln_b', 'ln1_g', 'ln1_b', 'ln2_g', 'ln2_b', 'ffn_conv_b', 'ln3_g', 'ln3_b']


def _pick(dim, pref, align=LANES):
    if dim <= pref:
        return dim
    t = (pref // align) * align
    while t >= align:
        if dim % t == 0:
            return t
        t -= align
    return dim


def _params(sem):
    return pltpu.CompilerParams(dimension_semantics=sem, vmem_limit_bytes=VMEM_LIMIT)


def _mm(a, b, *, ta=False, tb=False, out_dtype=F32, name, tm=512, tn=1024, tk=1024):
    if ta:
        kdim, m = a.shape
    else:
        m, kdim = a.shape
    if tb:
        n, kb = b.shape
    else:
        kb, n = b.shape
    assert kdim == kb, (a.shape, b.shape, ta, tb)
    tm, tn, tk = _pick(m, tm), _pick(n, tn), _pick(kdim, tk)
    nk = kdim // tk
    dims = (((0 if ta else 1,), (1 if tb else 0,)), ((), ()))

    def body(a_ref, b_ref, o_ref, *scratch):
        prod = lax.dot_general(a_ref[...].astype(BF16), b_ref[...].astype(BF16), dims,
                               preferred_element_type=F32)
        if nk == 1:
            o_ref[...] = prod.astype(out_dtype)
        else:
            acc_ref, = scratch
            k = pl.program_id(2)

            @pl.when(k == 0)
            def _():
                acc_ref[...] = prod

            @pl.when(k > 0)
            def _():
                acc_ref[...] += prod

            @pl.when(k == nk - 1)
            def _():
                o_ref[...] = acc_ref[...].astype(out_dtype)

    a_spec = pl.BlockSpec((tk, tm), lambda i, j, k: (k, i)) if ta else pl.BlockSpec((tm, tk), lambda i, j, k: (i, k))
    b_spec = pl.BlockSpec((tn, tk), lambda i, j, k: (j, k)) if tb else pl.BlockSpec((tk, tn), lambda i, j, k: (k, j))
    return pl.pallas_call(
        body, name=name,
        out_shape=jax.ShapeDtypeStruct((m, n), out_dtype),
        grid=(m // tm, n // tn, nk),
        in_specs=[a_spec, b_spec],
        out_specs=pl.BlockSpec((tm, tn), lambda i, j, k: (i, j)),
        scratch_shapes=[] if nk == 1 else [pltpu.VMEM((tm, tn), F32)],
        compiler_params=_params(("parallel", "parallel", "arbitrary")),
    )(a, b)


def _ln_stats(r):
    mu = jnp.mean(r, axis=-1, keepdims=True)
    xc = r - mu
    var = jnp.mean(xc * xc, axis=-1, keepdims=True)
    rstd = lax.rsqrt(var + LN_EPS)
    return xc * rstd, rstd


def _res_ln_fwd(x, f, g, b, *, alpha, name, ts=512):
    s, d = x.shape
    ts = _pick(s, ts)

    def body(x_ref, f_ref, g_ref, b_ref, r_ref, y_ref, yb_ref):
        r = alpha * x_ref[...] + f_ref[...]
        xhat, _ = _ln_stats(r)
        y = xhat * g_ref[...] + b_ref[...]
        r_ref[...] = r
        y_ref[...] = y
        yb_ref[...] = y.astype(BF16)

    tok = pl.BlockSpec((ts, d), lambda i: (i, 0))
    vec = pl.BlockSpec((1, d), lambda i: (0, 0))
    return pl.pallas_call(
        body, name=name,
        out_shape=(jax.ShapeDtypeStruct((s, d), F32), jax.ShapeDtypeStruct((s, d), F32),
                   jax.ShapeDtypeStruct((s, d), BF16)),
        grid=(s // ts,), in_specs=[tok, tok, vec, vec], out_specs=(tok, tok, tok),
        compiler_params=_params(("parallel",)),
    )(x, f, g.reshape(1, d), b.reshape(1, d))


def _ln_bwd(da, dres, r, g, *, alpha, name, ts=512):
    s, d = r.shape
    ts = _pick(s, ts)
    has_res = dres is not None

    def body(*refs):
        if has_res:
            da_ref, dres_ref, r_ref, g_ref, dr_ref, drb_ref, dg_ref, db_ref = refs
            dy = da_ref[...] + alpha * dres_ref[...]
        else:
            da_ref, r_ref, g_ref, dr_ref, drb_ref, dg_ref, db_ref = refs
            dy = da_ref[...]
        xhat, rstd = _ln_stats(r_ref[...])
        dxhat = dy * g_ref[...]
        m1 = jnp.mean(dxhat, axis=-1, keepdims=True)
        m2 = jnp.mean(dxhat * xhat, axis=-1, keepdims=True)
        dr = rstd * (dxhat - m1 - xhat * m2)
        dr_ref[...] = dr
        drb_ref[...] = dr.astype(BF16)

        @pl.when(pl.program_id(0) == 0)
        def _():
            dg_ref[...] = jnp.zeros_like(dg_ref)
            db_ref[...] = jnp.zeros_like(db_ref)

        dg_ref[...] += jnp.sum(dy * xhat, axis=0, keepdims=True)
        db_ref[...] += jnp.sum(dy, axis=0, keepdims=True)

    tok = pl.BlockSpec((ts, d), lambda i: (i, 0))
    vec = pl.BlockSpec((1, d), lambda i: (0, 0))
    ins = [da, dres, r, g.reshape(1, d)] if has_res else [da, r, g.reshape(1, d)]
    dr, drb, dg, db = pl.pallas_call(
        body, name=name,
        out_shape=(jax.ShapeDtypeStruct((s, d), F32), jax.ShapeDtypeStruct((s, d), BF16),
                   jax.ShapeDtypeStruct((1, d), F32), jax.ShapeDtypeStruct((1, d), F32)),
        grid=(s // ts,), in_specs=[tok] * (len(ins) - 1) + [vec], out_specs=(tok, tok, vec, vec),
        compiler_params=_params(("arbitrary",)),
    )(*ins)
    return dr, drb, dg.reshape(d), db.reshape(d)


def _axpy(a, b, *, alpha, name, ts=512):
    s, d = a.shape
    ts = _pick(s, ts)

    def body(a_ref, b_ref, o_ref):
        o_ref[...] = a_ref[...] + alpha * b_ref[...]

    tok = pl.BlockSpec((ts, d), lambda i: (i, 0))
    return pl.pallas_call(body, name=name, out_shape=jax.ShapeDtypeStruct((s, d), F32), grid=(s // ts,),
                          in_specs=[tok, tok], out_specs=tok, compiler_params=_params(("parallel",)))(a, b)


def _loss_and_grad(y, target, *, name, ts=512):
    s, d = y.shape
    ts = _pick(s, ts)
    inv_d = 1.0 / d

    def body(y_ref, t_ref, dy_ref, loss_ref):
        e = y_ref[...] - t_ref[...]
        dy_ref[...] = e * inv_d

        @pl.when(pl.program_id(0) == 0)
        def _():
            loss_ref[...] = jnp.zeros_like(loss_ref)

        loss_ref[...] += jnp.sum(e * e, axis=0, keepdims=True) * (0.5 * inv_d)

    tok = pl.BlockSpec((ts, d), lambda i: (i, 0))
    vec = pl.BlockSpec((1, d), lambda i: (0, 0))
    dy, part = pl.pallas_call(
        body, name=name,
        out_shape=(jax.ShapeDtypeStruct((s, d), F32), jax.ShapeDtypeStruct((1, d), F32)),
        grid=(s // ts,), in_specs=[tok, tok], out_specs=(tok, vec),
        compiler_params=_params(("arbitrary",)),
    )(y, target)
    return dy, part


def _row_sum(v, *, name):
    def body(v_ref, o_ref):
        o_ref[...] = jnp.sum(v_ref[...], axis=1, keepdims=True)

    return pl.pallas_call(body, name=name, out_shape=jax.ShapeDtypeStruct((1, 1), F32))(v)


def _sigmoid(x):
    return 1.0 / (1.0 + jnp.exp(-x))


def _row_chunks(s, pref=512):
    c = _pick(s, pref, 8)
    return [(i * c, c) for i in range(s // c)]


def _glu_conv_fwd(proj, w, b, *, cc, name):
    s = proj.shape[0]
    kw = w.shape[0]
    pad = 32
    assert kw - 1 <= pad
    ncb = cc // LANES
    chunks = _row_chunks(s)

    def body(a_ref, g_ref, w_ref, b_ref, o_ref, u0_ref):
        u0_ref[pl.ds(0, pad), :] = jnp.zeros((pad, LANES), F32)
        for r0, rc in chunks:
            u0_ref[pl.ds(pad + r0, rc), :] = a_ref[pl.ds(r0, rc), :] * _sigmoid(g_ref[pl.ds(r0, rc), :])
        for r0, rc in chunks:
            acc = jnp.zeros((rc, LANES), F32) + b_ref[...]
            for k in range(kw):
                acc = acc + w_ref[pl.ds(k, 1), :] * u0_ref[pl.ds(pad + r0 - (kw - 1) + k, rc), :]
            o_ref[pl.ds(r0, rc), :] = acc

    return pl.pallas_call(
        body, name=name,
        out_shape=jax.ShapeDtypeStruct((s, cc), F32),
        grid=(ncb,),
        in_specs=[pl.BlockSpec((s, LANES), lambda c: (0, c)), pl.BlockSpec((s, LANES), lambda c: (0, ncb + c)),
                  pl.BlockSpec((kw, LANES), lambda c: (0, c)), pl.BlockSpec((1, LANES), lambda c: (0, c))],
        out_specs=pl.BlockSpec((s, LANES), lambda c: (0, c)),
        scratch_shapes=[pltpu.VMEM((s + pad, LANES), F32)],
        compiler_params=_params(("parallel",)),
    )(proj, proj, w, b.reshape(1, cc))


def _glu_conv_bwd(du1, proj, w, *, cc, name):
    s = proj.shape[0]
    kw = w.shape[0]
    pad = 32
    ncb = cc // LANES
    chunks = _row_chunks(s)

    def body(d_ref, a_ref, g_ref, w_ref, da_ref, dg_ref, dw_ref, db_ref, u0_ref, dp_ref):
        u0_ref[pl.ds(0, pad), :] = jnp.zeros((pad, LANES), F32)
        dp_ref[pl.ds(s, pad), :] = jnp.zeros((pad, LANES), F32)
        for r0, rc in chunks:
            u0_ref[pl.ds(pad + r0, rc), :] = a_ref[pl.ds(r0, rc), :] * _sigmoid(g_ref[pl.ds(r0, rc), :])
            dp_ref[pl.ds(r0, rc), :] = d_ref[pl.ds(r0, rc), :]
        dws = [jnp.zeros((1, LANES), F32) for _ in range(kw)]
        dbs = jnp.zeros((1, LANES), F32)
        for r0, rc in chunks:
            d = dp_ref[pl.ds(r0, rc), :]
            dbs = dbs + jnp.sum(d, axis=0, keepdims=True)
            du0 = jnp.zeros((rc, LANES), F32)
            for k in range(kw):
                du0 = du0 + w_ref[pl.ds(k, 1), :] * dp_ref[pl.ds(r0 + (kw - 1) - k, rc), :]
                dws[k] = dws[k] + jnp.sum(d * u0_ref[pl.ds(pad + r0 - (kw - 1) + k, rc), :], axis=0, keepdims=True)
            sg = _sigmoid(g_ref[pl.ds(r0, rc), :])
            a = a_ref[pl.ds(r0, rc), :]
            da_ref[pl.ds(r0, rc), :] = (du0 * sg).astype(BF16)
            dg_ref[pl.ds(r0, rc), :] = (du0 * a * sg * (1.0 - sg)).astype(BF16)
        for k in range(kw):
            dw_ref[pl.ds(k, 1), :] = dws[k]
        db_ref[...] = dbs

    col = lambda off: pl.BlockSpec((s, LANES), lambda c: (0, off + c))
    da, dg, dw, db = pl.pallas_call(
        body, name=name,
        out_shape=(jax.ShapeDtypeStruct((s, cc), BF16), jax.ShapeDtypeStruct((s, cc), BF16),
                   jax.ShapeDtypeStruct((kw, cc), F32), jax.ShapeDtypeStruct((1, cc), F32)),
        grid=(ncb,),
        in_specs=[col(0), col(0), col(ncb), pl.BlockSpec((kw, LANES), lambda c: (0, c))],
        out_specs=(col(0), col(0), pl.BlockSpec((kw, LANES), lambda c: (0, c)), pl.BlockSpec((1, LANES), lambda c: (0, c))),
        scratch_shapes=[pltpu.VMEM((s + pad, LANES), F32), pltpu.VMEM((s + pad, LANES), F32)],
        compiler_params=_params(("parallel",)),
    )(du1, proj, proj, w)
    return da, dg, dw, db.reshape(cc)


def _cln_silu_fwd(u1, g, b, *, name, ts=512):
    s, cc = u1.shape
    ts = _pick(s, ts)

    def body(u_ref, g_ref, b_ref, o_ref):
        xhat, _ = _ln_stats(u_ref[...])
        y = xhat * g_ref[...] + b_ref[...]
        o_ref[...] = (y * _sigmoid(y)).astype(BF16)

    tok = pl.BlockSpec((ts, cc), lambda i: (i, 0))
    vec = pl.BlockSpec((1, cc), lambda i: (0, 0))
    return pl.pallas_call(body, name=name, out_shape=jax.ShapeDtypeStruct((s, cc), BF16), grid=(s // ts,),
                          in_specs=[tok, vec, vec], out_specs=tok,
                          compiler_params=_params(("parallel",)))(u1, g.reshape(1, cc), b.reshape(1, cc))


def _cln_silu_bwd(dua, u1, g, b, *, name, ts=512):
    s, cc = u1.shape
    ts = _pick(s, ts)

    def body(d_ref, u_ref, g_ref, b_ref, du_ref, dg_ref, db_ref):
        xhat, rstd = _ln_stats(u_ref[...])
        y = xhat * g_ref[...] + b_ref[...]
        sg = _sigmoid(y)
        dy = d_ref[...] * (sg * (1.0 + y * (1.0 - sg)))
        dxhat = dy * g_ref[...]
        m1 = jnp.mean(dxhat, axis=-1, keepdims=True)
        m2 = jnp.mean(dxhat * xhat, axis=-1, keepdims=True)
        du_ref[...] = rstd * (dxhat - m1 - xhat * m2)

        @pl.when(pl.program_id(0) == 0)
        def _():
            dg_ref[...] = jnp.zeros_like(dg_ref)
            db_ref[...] = jnp.zeros_like(db_ref)

        dg_ref[...] += jnp.sum(dy * xhat, axis=0, keepdims=True)
        db_ref[...] += jnp.sum(dy, axis=0, keepdims=True)

    tok = pl.BlockSpec((ts, cc), lambda i: (i, 0))
    vec = pl.BlockSpec((1, cc), lambda i: (0, 0))
    du1, dg, db = pl.pallas_call(
        body, name=name,
        out_shape=(jax.ShapeDtypeStruct((s, cc), F32), jax.ShapeDtypeStruct((1, cc), F32),
                   jax.ShapeDtypeStruct((1, cc), F32)),
        grid=(s // ts,), in_specs=[tok, tok, vec, vec], out_specs=(tok, vec, vec),
        compiler_params=_params(("arbitrary",)),
    )(dua, u1, g.reshape(1, cc), b.reshape(1, cc))
    return du1, dg.reshape(cc), db.reshape(cc)


def _softplus_parts(z):
    sp = jnp.log1p(jnp.exp(-jnp.abs(z)))
    return jnp.minimum(-z, 0.0) - sp, jnp.minimum(z, 0.0) - sp


def _split_dot(x, m):
    hi = x.astype(BF16)
    lo = (x - hi.astype(F32)).astype(BF16)
    return jnp.dot(hi, m, preferred_element_type=F32) + jnp.dot(lo, m, preferred_element_type=F32)


_NT = (((1,), (1,)), ((), ()))
_TN = (((0,), (0,)), ((), ()))


def _sb_fwd(proj, *, col0, n_pairs, name, tq=256):
    s = proj.shape[0]
    tq = _pick(s, tq)
    nq = s // tq
    cb0 = col0 // LANES
    scale = SB_HEAD_DIM ** -0.5

    def body(q_ref, k_ref, v_ref, o_ref, t_ref, n_ref):
        i = pl.program_id(1)
        lane = lax.broadcasted_iota(jnp.int32, (1, LANES), 1)
        row = lax.broadcasted_iota(jnp.int32, (tq, tq), 0)
        col = lax.broadcasted_iota(jnp.int32, (tq, tq), 1)
        vis = col < row
        m_after = (row > col).astype(BF16)
        q = q_ref[...] * scale
        acc_tot = jnp.zeros((tq, LANES), F32)
        t_tot = jnp.zeros((tq, LANES), F32)
        n_tot = jnp.zeros((8, LANES), F32)
        for h in range(2):
            hm = (lane >= SB_HEAD_DIM * h) & (lane < SB_HEAD_DIM * (h + 1))
            qh = jnp.where(hm, q, 0.0).astype(BF16)

            def tile(j, c, acc, masked, hm=hm, qh=qh):
                start = pl.multiple_of(j * tq, tq)
                kb = k_ref[pl.ds(start, tq), :].astype(BF16)
                vb = jnp.where(hm, v_ref[pl.ds(start, tq), :], 0.0).astype(BF16)
                z = lax.dot_general(qh, kb, _NT, preferred_element_type=F32)
                lk, lb = _softplus_parts(z)
                if masked:
                    lk = jnp.where(vis, lk, 0.0)
                later = _split_dot(lk, m_after)
                a = jnp.exp(lb + later + c)
                if masked:
                    a = jnp.where(vis, a, 0.0)
                acc = acc + jnp.dot(a.astype(BF16), vb, preferred_element_type=F32)
                return c + jnp.sum(lk, axis=1, keepdims=True), acc

            def more(st):
                return jnp.logical_and(st[0] < i, jnp.max(st[1]) > -EXP_UNDERFLOW)

            def step(st, tile=tile):
                c, acc = tile(i - 1 - st[0], st[1], st[2], False)
                return st[0] + 1, c, acc

            c, acc = tile(i, jnp.zeros((tq, 1), F32), jnp.zeros((tq, LANES), F32), True)
            n, c, acc = lax.while_loop(more, step, (jnp.int32(0), c, acc))
            acc_tot = acc_tot + acc
            t_tot = jnp.where(hm, c, t_tot)
            n_tot = jnp.where(hm, n.astype(F32), n_tot)
        o_ref[...] = acc_tot.astype(BF16)
        t_ref[...] = t_tot
        n_ref[...] = n_tot

    seq = lambda off: pl.BlockSpec((s, LANES), lambda p, i: (0, cb0 + off + p))
    return pl.pallas_call(
        body, name=name,
        out_shape=(jax.ShapeDtypeStruct((s, n_pairs * LANES), BF16), jax.ShapeDtypeStruct((n_pairs, s, LANES), F32),
                   jax.ShapeDtypeStruct((n_pairs, nq * 8, LANES), F32)),
        grid=(n_pairs, nq),
        in_specs=[pl.BlockSpec((tq, LANES), lambda p, i: (i, cb0 + p)), seq(n_pairs), seq(2 * n_pairs)],
        out_specs=(pl.BlockSpec((tq, LANES), lambda p, i: (i, p)), pl.BlockSpec((None, tq, LANES), lambda p, i: (p, i, 0)),
                   pl.BlockSpec((None, 8, LANES), lambda p, i: (p, i, 0))),
        compiler_params=_params(("parallel", "arbitrary")),
    )(proj, proj, proj)


def _sb_bwd(proj, t_sum, n_walked, dua, *, col0, n_pairs, do_col0, name, tq=256):
    s = proj.shape[0]
    tq = _pick(s, tq)
    cb0 = col0 // LANES
    dcb0 = do_col0 // LANES
    scale = SB_HEAD_DIM ** -0.5

    def body(q_ref, k_ref, v_ref, t_ref, n_ref, do_ref, dq_ref, dk_ref, dv_ref):
        i = pl.program_id(1)

        @pl.when(i == 0)
        def _():
            dk_ref[...] = jnp.zeros_like(dk_ref)
            dv_ref[...] = jnp.zeros_like(dv_ref)

        lane = lax.broadcasted_iota(jnp.int32, (1, LANES), 1)
        row = lax.broadcasted_iota(jnp.int32, (tq, tq), 0)
        col = lax.broadcasted_iota(jnp.int32, (tq, tq), 1)
        vis = col < row
        m_upto = (row <= col).astype(BF16)
        m_before = (row < col).astype(BF16)
        q = q_ref[...] * scale
        do = do_ref[...]
        dq_tot = jnp.zeros((tq, LANES), F32)
        for h in range(2):
            hm = (lane >= SB_HEAD_DIM * h) & (lane < SB_HEAD_DIM * (h + 1))
            qh = jnp.where(hm, q, 0.0).astype(BF16)
            doh = jnp.where(hm, do, 0.0).astype(BF16)
            t_all = t_ref[:, SB_HEAD_DIM * h:SB_HEAD_DIM * h + 1]
            first = i - jnp.max(jnp.where(hm, n_ref[...], 0.0)).astype(jnp.int32)

            def tile(j, p_sum, g_sum, dq, masked, hm=hm, qh=qh, doh=doh, t_all=t_all):
                start = pl.multiple_of(j * tq, tq)
                kb = jnp.where(hm, k_ref[pl.ds(start, tq), :], 0.0).astype(BF16)
                vb = jnp.where(hm, v_ref[pl.ds(start, tq), :], 0.0).astype(BF16)
                z = lax.dot_general(qh, kb, _NT, preferred_element_type=F32)
                lk_raw, lb = _softplus_parts(z)
                lk = jnp.where(vis, lk_raw, 0.0) if masked else lk_raw
                upto = _split_dot(lk, m_upto)
                a = jnp.exp(lb + (t_all - p_sum) - upto)
                if masked:
                    a = jnp.where(vis, a, 0.0)
                ab = a.astype(BF16)
                da = lax.dot_general(doh, vb, _NT, preferred_element_type=F32)
                g = a * da
                dv_ref[pl.ds(start, tq), :] += lax.dot_general(ab, doh, _TN, preferred_element_type=F32)
                g_before = g_sum + _split_dot(g, m_before)
                dz = g * jnp.exp(lk_raw) - g_before * jnp.exp(lb)
                if masked:
                    dz = jnp.where(vis, dz, 0.0)
                dzb = dz.astype(BF16)
                dq = dq + jnp.dot(dzb, kb, preferred_element_type=F32)
                dk_ref[pl.ds(start, tq), :] += lax.dot_general(dzb, qh, _TN, preferred_element_type=F32)
                return (p_sum + jnp.sum(lk, axis=1, keepdims=True), g_sum + jnp.sum(g, axis=1, keepdims=True), dq)

            zero = jnp.zeros((tq, 1), F32)
            carry = lax.fori_loop(first, i, lambda j, c: tile(j, c[0], c[1], c[2], False),
                                  (zero, zero, jnp.zeros((tq, LANES), F32)))
            _, _, dq = tile(i, carry[0], carry[1], carry[2], True)
            dq_tot = dq_tot + dq
        dq_ref[...] = dq_tot * scale

    seq = lambda off: pl.BlockSpec((s, LANES), lambda p, i: (0, cb0 + off + p))
    out = jax.ShapeDtypeStruct((s, n_pairs * LANES), F32)
    res = pl.BlockSpec((s, LANES), lambda p, i: (0, p))
    return pl.pallas_call(
        body, name=name,
        out_shape=(out, out, out),
        grid=(n_pairs, s // tq),
        in_specs=[pl.BlockSpec((tq, LANES), lambda p, i: (i, cb0 + p)), seq(n_pairs), seq(2 * n_pairs),
                  pl.BlockSpec((None, tq, LANES), lambda p, i: (p, i, 0)),
                  pl.BlockSpec((None, 8, LANES), lambda p, i: (p, i, 0)),
                  pl.BlockSpec((tq, LANES), lambda p, i: (i, dcb0 + p))],
        out_specs=(pl.BlockSpec((tq, LANES), lambda p, i: (i, p)), res, res),
        compiler_params=_params(("arbitrary", "arbitrary")),
    )(proj, proj, proj, t_sum, n_walked, dua)


def _mem_attn_fwd(qm, km, vm, *, name, tq=512):
    s, d = qm.shape
    heads = d // MEM_HEAD_DIM
    mlen = km.shape[0]
    tq = _pick(s, tq)
    scale = MEM_HEAD_DIM ** -0.5

    def body(q_ref, k_ref, v_ref, o_ref):
        for h in range(heads):
            sl = slice(h * MEM_HEAD_DIM, (h + 1) * MEM_HEAD_DIM)
            q = (q_ref[:, sl] * scale).astype(BF16)
            sc = lax.dot_general(q, k_ref[:, sl].astype(BF16), _NT, preferred_element_type=F32)
            e = jnp.exp(sc - jnp.max(sc, axis=1, keepdims=True))
            p = e / jnp.sum(e, axis=1, keepdims=True)
            o_ref[:, sl] = jnp.dot(p.astype(BF16), v_ref[:, sl].astype(BF16), preferred_element_type=F32).astype(BF16)

    tok = pl.BlockSpec((tq, d), lambda i: (i, 0))
    kv = pl.BlockSpec((mlen, d), lambda i: (0, 0))
    return pl.pallas_call(body, name=name, out_shape=jax.ShapeDtypeStruct((s, d), BF16), grid=(s // tq,),
                          in_specs=[tok, kv, kv], out_specs=tok, compiler_params=_params(("parallel",)))(qm, km, vm)


def _mem_attn_bwd(qm, km, vm, do, *, name, tq=512):
    s, d = qm.shape
    heads = d // MEM_HEAD_DIM
    mlen = km.shape[0]
    tq = _pick(s, tq)
    scale = MEM_HEAD_DIM ** -0.5

    def body(q_ref, k_ref, v_ref, do_ref, dq_ref, dk_ref, dv_ref):
        @pl.when(pl.program_id(0) == 0)
        def _():
            dk_ref[...] = jnp.zeros_like(dk_ref)
            dv_ref[...] = jnp.zeros_like(dv_ref)

        for h in range(heads):
            sl = slice(h * MEM_HEAD_DIM, (h + 1) * MEM_HEAD_DIM)
            q = (q_ref[:, sl] * scale).astype(BF16)
            k = k_ref[:, sl].astype(BF16)
            v = v_ref[:, sl].astype(BF16)
            sc = lax.dot_general(q, k, _NT, preferred_element_type=F32)
            e = jnp.exp(sc - jnp.max(sc, axis=1, keepdims=True))
            p = e / jnp.sum(e, axis=1, keepdims=True)
            dob = do_ref[:, sl].astype(BF16)
            dv_ref[:, sl] += lax.dot_general(p.astype(BF16), dob, _TN, preferred_element_type=F32)
            dp = lax.dot_general(dob, v, _NT, preferred_element_type=F32)
            ds = (p * (dp - jnp.sum(dp * p, axis=1, keepdims=True))).astype(BF16)
            dq_ref[:, sl] = (jnp.dot(ds, k, preferred_element_type=F32) * scale).astype(BF16)
            dk_ref[:, sl] += lax.dot_general(ds, q, _TN, preferred_element_type=F32)

    tok = pl.BlockSpec((tq, d), lambda i: (i, 0))
    kv = pl.BlockSpec((mlen, d), lambda i: (0, 0))
    return pl.pallas_call(
        body, name=name,
        out_shape=(jax.ShapeDtypeStruct((s, d), BF16), jax.ShapeDtypeStruct((mlen, d), F32),
                   jax.ShapeDtypeStruct((mlen, d), F32)),
        grid=(s // tq,), in_specs=[tok, kv, kv, tok], out_specs=(tok, kv, kv),
        compiler_params=_params(("arbitrary",)),
    )(qm, km, vm, do)


def _ffn_act_fwd(up, w, b, *, name):
    s, two_f = up.shape
    ff = two_f // 2
    nfb = ff // LANES
    kw = w.shape[0]
    pad = 8
    chunks = _row_chunks(s)

    def body(v_ref, g_ref, wv_ref, wg_ref, bv_ref, bg_ref, o_ref, vp_ref, gp_ref):
        vp_ref[pl.ds(0, pad), :] = jnp.zeros((pad, LANES), F32)
        gp_ref[pl.ds(0, pad), :] = jnp.zeros((pad, LANES), F32)
        for r0, rc in chunks:
            vp_ref[pl.ds(pad + r0, rc), :] = v_ref[pl.ds(r0, rc), :]
            gp_ref[pl.ds(pad + r0, rc), :] = g_ref[pl.ds(r0, rc), :]
        for r0, rc in chunks:
            vc = jnp.zeros((rc, LANES), F32) + bv_ref[...]
            gc = jnp.zeros((rc, LANES), F32) + bg_ref[...]
            for k in range(kw):
                off = pad + r0 - (kw - 1) + k
                vc = vc + wv_ref[pl.ds(k, 1), :] * vp_ref[pl.ds(off, rc), :]
                gc = gc + wg_ref[pl.ds(k, 1), :] * gp_ref[pl.ds(off, rc), :]
            o_ref[pl.ds(r0, rc), :] = (gc * _sigmoid(gc) * vc).astype(BF16)

    col = lambda off: pl.BlockSpec((s, LANES), lambda c: (0, off + c))
    tap = lambda off: pl.BlockSpec((kw, LANES), lambda c: (0, off + c))
    vec = lambda off: pl.BlockSpec((1, LANES), lambda c: (0, off + c))
    return pl.pallas_call(
        body, name=name, out_shape=jax.ShapeDtypeStruct((s, ff), BF16), grid=(nfb,),
        in_specs=[col(0), col(nfb), tap(0), tap(nfb), vec(0), vec(nfb)], out_specs=col(0),
        scratch_shapes=[pltpu.VMEM((s + pad, LANES), F32), pltpu.VMEM((s + pad, LANES), F32)],
        compiler_params=_params(("parallel",)),
    )(up, up, w, w, b.reshape(1, two_f), b.reshape(1, two_f))


def _ffn_act_bwd(up, dact, w, b, *, name):
    s, two_f = up.shape
    ff = two_f // 2
    nfb = ff // LANES
    kw = w.shape[0]
    pad = 8
    chunks = _row_chunks(s)

    def body(v_ref, g_ref, d_ref, wv_ref, wg_ref, bv_ref, bg_ref, dv_ref, dg_ref, dwv_ref, dwg_ref, dbv_ref, dbg_ref,
             vp_ref, gp_ref, dvc_ref, dgc_ref):
        vp_ref[pl.ds(0, pad), :] = jnp.zeros((pad, LANES), F32)
        gp_ref[pl.ds(0, pad), :] = jnp.zeros((pad, LANES), F32)
        dvc_ref[pl.ds(s, pad), :] = jnp.zeros((pad, LANES), F32)
        dgc_ref[pl.ds(s, pad), :] = jnp.zeros((pad, LANES), F32)
        for r0, rc in chunks:
            vp_ref[pl.ds(pad + r0, rc), :] = v_ref[pl.ds(r0, rc), :]
            gp_ref[pl.ds(pad + r0, rc), :] = g_ref[pl.ds(r0, rc), :]
        dwv = [jnp.zeros((1, LANES), F32) for _ in range(kw)]
        dwg = [jnp.zeros((1, LANES), F32) for _ in range(kw)]
        dbv = jnp.zeros((1, LANES), F32)
        dbg = jnp.zeros((1, LANES), F32)
        for r0, rc in chunks:
            vc = jnp.zeros((rc, LANES), F32) + bv_ref[...]
            gc = jnp.zeros((rc, LANES), F32) + bg_ref[...]
            for k in range(kw):
                off = pad + r0 - (kw - 1) + k
                vc = vc + wv_ref[pl.ds(k, 1), :] * vp_ref[pl.ds(off, rc), :]
                gc = gc + wg_ref[pl.ds(k, 1), :] * gp_ref[pl.ds(off, rc), :]
            sg = _sigmoid(gc)
            d = d_ref[pl.ds(r0, rc), :]
            dvc = d * (gc * sg)
            dgc = d * vc * (sg * (1.0 + gc * (1.0 - sg)))
            dvc_ref[pl.ds(r0, rc), :] = dvc
            dgc_ref[pl.ds(r0, rc), :] = dgc
            dbv = dbv + jnp.sum(dvc, axis=0, keepdims=True)
            dbg = dbg + jnp.sum(dgc, axis=0, keepdims=True)
            for k in range(kw):
                off = pad + r0 - (kw - 1) + k
                dwv[k] = dwv[k] + jnp.sum(dvc * vp_ref[pl.ds(off, rc), :], axis=0, keepdims=True)
                dwg[k] = dwg[k] + jnp.sum(dgc * gp_ref[pl.ds(off, rc), :], axis=0, keepdims=True)
        for r0, rc in chunks:
            dv = jnp.zeros((rc, LANES), F32)
            dg = jnp.zeros((rc, LANES), F32)
            for k in range(kw):
                off = r0 + (kw - 1) - k
                dv = dv + wv_ref[pl.ds(k, 1), :] * dvc_ref[pl.ds(off, rc), :]
                dg = dg + wg_ref[pl.ds(k, 1), :] * dgc_ref[pl.ds(off, rc), :]
            dv_ref[pl.ds(r0, rc), :] = dv.astype(BF16)
            dg_ref[pl.ds(r0, rc), :] = dg.astype(BF16)
        for k in range(kw):
            dwv_ref[pl.ds(k, 1), :] = dwv[k]
            dwg_ref[pl.ds(k, 1), :] = dwg[k]
        dbv_ref[...] = dbv
        dbg_ref[...] = dbg

    col = lambda off: pl.BlockSpec((s, LANES), lambda c: (0, off + c))
    tap = lambda off: pl.BlockSpec((kw, LANES), lambda c: (0, off + c))
    vec = lambda off: pl.BlockSpec((1, LANES), lambda c: (0, off + c))
    big = lambda: pltpu.VMEM((s + pad, LANES), F32)
    dv, dg, dwv, dwg, dbv, dbg = pl.pallas_call(
        body, name=name,
        out_shape=(jax.ShapeDtypeStruct((s, ff), BF16), jax.ShapeDtypeStruct((s, ff), BF16),
                   jax.ShapeDtypeStruct((kw, ff), F32), jax.ShapeDtypeStruct((kw, ff), F32),
                   jax.ShapeDtypeStruct((1, ff), F32), jax.ShapeDtypeStruct((1, ff), F32)),
        grid=(nfb,),
        in_specs=[col(0), col(nfb), col(0), tap(0), tap(nfb), vec(0), vec(nfb)],
        out_specs=(col(0), col(0), tap(0), tap(0), vec(0), vec(0)),
        scratch_shapes=[big(), big(), big(), big()],
        compiler_params=_params(("parallel",)),
    )(up, up, dact, w, w, b.reshape(1, two_f), b.reshape(1, two_f))
    return (jnp.concatenate([dv, dg], axis=1), jnp.concatenate([dwv, dwg], axis=1),
            jnp.concatenate([dbv, dbg], axis=1).reshape(two_f))


def _sum_parts(parts, *, name, tr=256):
    n_parts, rows, cols = parts.shape
    tr = _pick(rows, tr, 16)

    def body(p_ref, o_ref):
        g = p_ref[0].astype(F32)
        for k in range(1, n_parts):
            g = g + p_ref[k].astype(F32)
        o_ref[...] = g

    return pl.pallas_call(
        body, name=name, out_shape=jax.ShapeDtypeStruct((rows, cols), F32), grid=(rows // tr,),
        in_specs=[pl.BlockSpec((n_parts, tr, cols), lambda i: (0, i, 0))],
        out_specs=pl.BlockSpec((tr, cols), lambda i: (i, 0)), compiler_params=_params(("parallel",)),
    )(parts)


def _sum_adamw(parts, w, m, v, *, name, tr=256):
    n_parts, rows, cols = parts.shape
    tr = _pick(rows, tr, 16)
    c1 = 1.0 / (1.0 - ADAM_B1 ** ADAM_STEP)
    c2 = 1.0 / (1.0 - ADAM_B2 ** ADAM_STEP)

    def body(p_ref, w_ref, m_ref, v_ref, g_ref, d_ref, nm_ref, nv_ref):
        g = p_ref[0].astype(F32)
        for k in range(1, n_parts):
            g = g + p_ref[k].astype(F32)
        nm = ADAM_B1 * m_ref[...] + (1.0 - ADAM_B1) * g
        nv = ADAM_B2 * v_ref[...] + (1.0 - ADAM_B2) * (g * g)
        g_ref[...] = g
        nm_ref[...] = nm
        nv_ref[...] = nv
        d_ref[...] = -ADAM_LR * ((nm * c1) / (jnp.sqrt(nv * c2) + ADAM_EPS) + ADAM_WD * w_ref[...])

    blk = pl.BlockSpec((tr, cols), lambda i: (i, 0))
    out = jax.ShapeDtypeStruct((rows, cols), F32)
    return pl.pallas_call(
        body, name=name, out_shape=(out, out, out, out), grid=(rows // tr,),
        in_specs=[pl.BlockSpec((n_parts, tr, cols), lambda i: (0, i, 0)), blk, blk, blk],
        out_specs=(blk, blk, blk, blk), compiler_params=_params(("parallel",)),
    )(parts, w, m, v)


def _mesh_pos():
    return lax.axis_index("x"), lax.axis_index("y"), lax.axis_index("c")


def _flip(pos, k):
    x, y, c = pos
    return (1 - x if k & 4 else x, 1 - y if k & 2 else y, 1 - c if k & 1 else c)


def _dev_index(pos):
    return 4 * pos[0] + 2 * pos[1] + pos[2]


N_PEERS = N_DEV - 1


def _all_gather(xs, *, name):
    n = len(xs)

    def body(*refs):
        x_refs, out_refs = refs[:n], refs[n:2 * n]
        send_sems, recv_sems, local_sems = refs[2 * n:]
        me = _mesh_pos()
        sibling = _flip(me, 1)
        chips = [_flip(me, 4), _flip(me, 2), _flip(me, 6)]

        def copy(a, k, block, to, from_input=False):
            slot = out_refs[a].at[_dev_index(block)]
            return pltpu.make_async_remote_copy(
                src_ref=x_refs[a] if from_input else slot, dst_ref=slot,
                send_sem=send_sems.at[a * N_PEERS + k], recv_sem=recv_sems.at[a * N_PEERS + k],
                device_id=to, device_id_type=pl.DeviceIdType.MESH)

        mine = [pltpu.make_async_copy(x_refs[a], out_refs[a].at[_dev_index(me)], local_sems.at[a]) for a in range(n)]
        first = [copy(a, 0, me, sibling, True) for a in range(n)]
        first += [copy(a, 1 + j, me, chip, True) for j, chip in enumerate(chips) for a in range(n)]
        for cp in mine + first:
            cp.start()
        passed = []
        for j, chip in enumerate(chips):
            for a in range(n):
                copy(a, 1 + j, chip, me).wait_recv()
                passed.append(copy(a, 4 + j, chip, sibling))
                passed[-1].start()
        for a in range(n):
            copy(a, 0, sibling, me).wait_recv()
        for j, chip in enumerate(chips):
            for a in range(n):
                copy(a, 4 + j, _flip(chip, 1), me).wait_recv()
        for cp in first + passed:
            cp.wait_send()
        for cp in mine:
            cp.wait()

    hbm = pl.BlockSpec(memory_space=pl.ANY)
    return pl.pallas_call(
        body, name=name,
        out_shape=tuple(jax.ShapeDtypeStruct((N_DEV,) + x.shape, x.dtype) for x in xs),
        in_specs=[hbm] * n, out_specs=tuple([hbm] * n),
        scratch_shapes=[pltpu.SemaphoreType.DMA((n * N_PEERS,)), pltpu.SemaphoreType.DMA((n * N_PEERS,)),
                        pltpu.SemaphoreType.DMA((n,))],
    )(*xs)


def _all_to_all(xs, *, name):
    n = len(xs)

    def body(*refs):
        x_refs, out_refs = refs[:n], refs[n:2 * n]
        send_sems, recv_sems, local_sems = refs[2 * n:]
        me = _mesh_pos()
        my_slot = _dev_index(me)

        def copy(a, k):
            peer = _flip(me, k)
            return pltpu.make_async_remote_copy(
                src_ref=x_refs[a].at[_dev_index(peer)], dst_ref=out_refs[a].at[my_slot],
                send_sem=send_sems.at[a * N_PEERS + k - 1], recv_sem=recv_sems.at[a * N_PEERS + k - 1],
                device_id=peer, device_id_type=pl.DeviceIdType.MESH)

        mine = [pltpu.make_async_copy(x_refs[a].at[my_slot], out_refs[a].at[my_slot], local_sems.at[a]) for a in range(n)]
        copies = [copy(a, k) for k in range(1, N_DEV) for a in range(n)]
        for cp in mine + copies:
            cp.start()
        for cp in copies:
            cp.wait_recv()
        for cp in copies:
            cp.wait_send()
        for cp in mine:
            cp.wait()

    hbm = pl.BlockSpec(memory_space=pl.ANY)
    return pl.pallas_call(
        body, name=name,
        out_shape=tuple(jax.ShapeDtypeStruct(x.shape, x.dtype) for x in xs),
        in_specs=[hbm] * n, out_specs=tuple([hbm] * n),
        scratch_shapes=[pltpu.SemaphoreType.DMA((n * N_PEERS,)), pltpu.SemaphoreType.DMA((n * N_PEERS,)),
                        pltpu.SemaphoreType.DMA((n,))],
    )(*xs)


def _pack(arrays):
    flat = jnp.concatenate([a.reshape(-1) for a in arrays])
    n = flat.shape[0]
    tile = PACK_W * PACK_ROW_ALIGN
    total = -(-n // tile) * tile
    return jnp.pad(flat, (0, total - n)).reshape(total // PACK_W, PACK_W)


def _unpack(buf, shapes):
    lead = buf.shape[:-2]
    flat = buf.reshape(lead + (-1,))
    out, off = [], 0
    for shp in shapes:
        n = 1
        for dim in shp:
            n *= dim
        out.append(flat[..., off:off + n].reshape(lead + tuple(shp)))
        off += n
    return out


def _join_columns(blocks):
    return jnp.moveaxis(blocks, 0, 2).reshape(blocks.shape[1], blocks.shape[2], -1)


def _layer_fwd(x, xb, memb, w, alpha):
    cc = w['conv_w'].shape[1]
    n_pairs = (w['w_out'].shape[0] - cc) // LANES
    proj = _mm(xb, w['w_in'], tb=True, name="mm_proj")
    u1 = _glu_conv_fwd(proj, w['conv_w'], w['conv_b'], cc=cc, name="glu_conv_fwd")
    u = _cln_silu_fwd(u1, w['conv_ln_g'], w['conv_ln_b'], name="cln_silu_fwd")
    att, t_sum, n_walked = _sb_fwd(proj, col0=2 * cc, n_pairs=n_pairs, name="sb_fwd")
    ua = jnp.concatenate([u, att], axis=1)
    mix = _mm(ua, w['w_out'], name="mm_mix")
    r1, x1, x1b = _res_ln_fwd(x, mix, w['ln1_g'], w['ln1_b'], alpha=alpha, name="res_ln_fwd")
    qm = _mm(x1b, w['mem_wq'], name="mm_memq")
    km = _mm(memb, w['mem_wk'], name="mm_memkv")
    vm = _mm(memb, w['mem_wv'], name="mm_memkv")
    o = _mem_attn_fwd(qm, km, vm, name="mem_attn_fwd")
    cross = _mm(o, w['mem_wo'], name="mm_mix")
    r2, x2, x2b = _res_ln_fwd(x1, cross, w['ln2_g'], w['ln2_b'], alpha=alpha, name="res_ln_fwd")
    up = _mm(x2b, w['ffn_up'], tb=True, name="mm_up")
    act = _ffn_act_fwd(up, w['ffn_conv_w'], w['ffn_conv_b'], name="ffn_act_fwd")
    ffn = _mm(act, w['ffn_down'], name="mm_down")
    r3, x3, x3b = _res_ln_fwd(x2, ffn, w['ln3_g'], w['ln3_b'], alpha=alpha, name="res_ln_fwd")
    saved = dict(xb=xb, proj=proj, u1=u1, t_sum=t_sum, n_walked=n_walked, ua=ua, r1=r1, x1b=x1b, qm=qm, km=km, vm=vm,
                 o=o, r2=r2, x2b=x2b, up=up, act=act, r3=r3)
    return x3, x3b, saved


def _layer_bwd(da, dres, sv, memb, w, alpha):
    g = {}
    cc = w['conv_w'].shape[1]
    n_pairs = (w['w_out'].shape[0] - cc) // LANES
    dr3, dr3b, g['ln3_g'], g['ln3_b'] = _ln_bwd(da, dres, sv['r3'], w['ln3_g'], alpha=alpha, name="ln_bwd")
    g['ffn_down'] = _mm(sv['act'], dr3b, ta=True, out_dtype=BF16, name="mm_dw_down")
    dact = _mm(dr3b, w['ffn_down'], tb=True, name="mm_dact")
    dup, g['ffn_conv_w'], g['ffn_conv_b'] = _ffn_act_bwd(sv['up'], dact, w['ffn_conv_w'], w['ffn_conv_b'], name="ffn_act_bwd")
    g['ffn_up'] = _mm(dup, sv['x2b'], ta=True, out_dtype=BF16, name="mm_dw_up")
    da2 = _mm(dup, w['ffn_up'], name="mm_dx_up")
    dr2, dr2b, g['ln2_g'], g['ln2_b'] = _ln_bwd(da2, dr3, sv['r2'], w['ln2_g'], alpha=alpha, name="ln_bwd")
    g['mem_wo'] = _mm(sv['o'], dr2b, ta=True, out_dtype=BF16, name="mm_dw_sq")
    do = _mm(dr2b, w['mem_wo'], tb=True, name="mm_dx_sq")
    dqm, dkm, dvm = _mem_attn_bwd(sv['qm'], sv['km'], sv['vm'], do, name="mem_attn_bwd")
    g['mem_wq'] = _mm(sv['x1b'], dqm, ta=True, out_dtype=BF16, name="mm_dw_sq")
    g['mem_wk'] = _mm(memb, dkm, ta=True, out_dtype=BF16, name="mm_dw_memkv")
    g['mem_wv'] = _mm(memb, dvm, ta=True, out_dtype=BF16, name="mm_dw_memkv")
    da1 = _mm(dqm, w['mem_wq'], tb=True, name="mm_dx_sq")
    dr1, dr1b, g['ln1_g'], g['ln1_b'] = _ln_bwd(da1, dr2, sv['r1'], w['ln1_g'], alpha=alpha, name="ln_bwd")
    g['w_out'] = _mm(sv['ua'], dr1b, ta=True, out_dtype=BF16, name="mm_dw_sq")
    dua = _mm(dr1b, w['w_out'], tb=True, name="mm_dx_sq")
    du1, g['conv_ln_g'], g['conv_ln_b'] = _cln_silu_bwd(dua, sv['u1'], w['conv_ln_g'], w['conv_ln_b'], name="cln_silu_bwd")
    dga, dgg, g['conv_w'], g['conv_b'] = _glu_conv_bwd(du1, sv['proj'], w['conv_w'], cc=cc, name="glu_conv_bwd")
    dq, dk, dv = _sb_bwd(sv['proj'], sv['t_sum'], sv['n_walked'], dua, col0=2 * cc, n_pairs=n_pairs, do_col0=cc, name="sb_bwd")
    dproj = jnp.concatenate([dga, dgg, dq.astype(BF16), dk.astype(BF16), dv.astype(BF16)], axis=1)
    g['w_in'] = _mm(dproj, sv['xb'], ta=True, out_dtype=BF16, name="mm_dw_in")
    da0 = _mm(dproj, w['w_in'], name="mm_dx_in")
    return da0, dr1, g


def kernel(x, mem, w_in, conv_w, conv_b, conv_ln_g, conv_ln_b, w_out, ln1_g, ln1_b, mem_wq, mem_wk, mem_wv, mem_wo, ln2_g, ln2_b, ffn_up, ffn_conv_w, ffn_conv_b, ffn_down, ln3_g, ln3_b, loss_target, m_w_in, m_conv_w, m_conv_b, m_conv_ln_g, m_conv_ln_b, m_w_out, m_ln1_g, m_ln1_b, m_mem_wq, m_mem_wk, m_mem_wv, m_mem_wo, m_ln2_g, m_ln2_b, m_ffn_up, m_ffn_conv_w, m_ffn_conv_b, m_ffn_down, m_ln3_g, m_ln3_b, v_w_in, v_conv_w, v_conv_b, v_conv_ln_g, v_conv_ln_b, v_w_out, v_ln1_g, v_ln1_b, v_mem_wq, v_mem_wk, v_mem_wv, v_mem_wo, v_ln2_g, v_ln2_b, v_ffn_up, v_ffn_conv_w, v_ffn_conv_b, v_ffn_down, v_ln3_g, v_ln3_b):
    wts = dict(zip(WEIGHTS, (w_in, conv_w, conv_b, conv_ln_g, conv_ln_b, w_out, ln1_g, ln1_b, mem_wq, mem_wk, mem_wv,
                             mem_wo, ln2_g, ln2_b, ffn_up, ffn_conv_w, ffn_conv_b, ffn_down, ln3_g, ln3_b)))
    mom = dict(zip(WEIGHTS, (m_w_in, m_conv_w, m_conv_b, m_conv_ln_g, m_conv_ln_b, m_w_out, m_ln1_g, m_ln1_b, m_mem_wq,
                             m_mem_wk, m_mem_wv, m_mem_wo, m_ln2_g, m_ln2_b, m_ffn_up, m_ffn_conv_w, m_ffn_conv_b,
                             m_ffn_down, m_ln3_g, m_ln3_b)))
    var = dict(zip(WEIGHTS, (v_w_in, v_conv_w, v_conv_b, v_conv_ln_g, v_conv_ln_b, v_w_out, v_ln1_g, v_ln1_b, v_mem_wq,
                             v_mem_wk, v_mem_wv, v_mem_wo, v_ln2_g, v_ln2_b, v_ffn_up, v_ffn_conv_w, v_ffn_conv_b,
                             v_ffn_down, v_ln3_g, v_ln3_b)))
    depth = w_in.shape[0]
    alpha = (2.0 * depth) ** 0.25
    my_index = _dev_index(_mesh_pos())

    def row_shard(src, n, col_sharded, l):
        return src[n][l].T if col_sharded else src[n][l]

    tap_shapes = [wts[n].shape for n in TAPS]
    gathered_taps, = _all_gather([_pack([wts[n] for n in TAPS])], name="gather_taps")
    full_taps = {n: _join_columns(t) for n, t in zip(TAPS, _unpack(gathered_taps, tap_shapes))}

    def layer_weights(l):
        shards = [row_shard(wts, n, cs, l).astype(BF16) for n, cs in MATRICES]
        full = _all_gather(shards, name="gather_weights")
        w = {n: f.reshape(-1, f.shape[-1]) for (n, _), f in zip(MATRICES, full)}
        w.update({n: full_taps[n][l] for n in TAPS})
        w.update({n: wts[n][l] for n in REPLICATED})
        return w

    weights = [layer_weights(l) for l in range(depth)]

    xs = x[0]
    memb = mem[0].astype(BF16)
    h, hb = xs, xs.astype(BF16)
    saved = []
    for l in range(depth):
        h, hb, sv = _layer_fwd(h, hb, memb, weights[l], alpha)
        saved.append(sv)

    dy, loss_row = _loss_and_grad(h, loss_target[0], name="loss")
    loss = lax.psum(_row_sum(loss_row, name="loss_sum")[0, 0], ("x", "y", "c"))

    results = [{n: [None] * depth for n in WEIGHTS} for _ in range(4)]
    da, dres = dy, None
    grads = [None] * depth
    for l in reversed(range(depth)):
        da, dres, grads[l] = _layer_bwd(da, dres, saved[l], memb, weights[l], alpha)
        send = [grads[l][n].reshape(N_DEV, -1, grads[l][n].shape[-1]) for n, _ in MATRICES]
        recv = _all_to_all(send, name="scatter_grads")
        for (n, cs), parts in zip(MATRICES, recv):
            res = _sum_adamw(parts, *[row_shard(src, n, cs, l) for src in (wts, mom, var)], name="adamw_matrix")
            for kind in range(4):
                results[kind][n][l] = res[kind].T if cs else res[kind]
    grad_x = _axpy(da, dres, alpha=alpha, name="grad_x")[None]

    small = REPLICATED + TAPS
    part = _pack([jnp.stack([grads[l][n] for l in range(depth)]) for n in small])
    parts, = _all_gather([part], name="gather_small_grads")
    total = _sum_parts(parts, name="sum_small_grads")
    summed = dict(zip(small, _unpack(total, [wts[n].shape for n in REPLICATED] + [full_taps[n].shape for n in TAPS])))
    for n in TAPS:
        cols = wts[n].shape[-1]
        summed[n] = lax.dynamic_slice_in_dim(summed[n], my_index * cols, cols, axis=2)
    res = _sum_adamw(_pack([summed[n] for n in small])[None], *[_pack([src[n] for n in small]) for src in (wts, mom, var)],
                     name="adamw_small")
    for kind in range(4):
        for n, val in zip(small, _unpack(res[kind], [wts[n].shape for n in small])):
            results[kind][n] = val

    outs = [loss, grad_x]
    for kind in range(4):
        outs += [results[kind][n] if n in small else jnp.stack(results[kind][n]) for n in WEIGHTS]
    return tuple(outs)
```

```python
import functools

import jax
import jax.numpy as jnp
from jax import lax
from jax.experimental import pallas as pl
from jax.experimental.pallas import tpu as pltpu

F32 = jnp.float32
BF16 = jnp.bfloat16

N_DEV = 8
LANES = 128
PACK_W = 1024
PACK_ROW_ALIGN = 16
SB_HEAD_DIM = 64
MEM_HEAD_DIM = 256
LN_EPS = 1e-5
EXP_UNDERFLOW = 104.0
VMEM_LIMIT = 56 * 1024 * 1024

ADAM_LR = 0.001
ADAM_B1 = 0.9
ADAM_B2 = 0.999
ADAM_EPS = 1e-08
ADAM_WD = 0.01
ADAM_STEP = 10

IN_NAMES = ['x', 'mem', 'w_in', 'conv_w', 'conv_b', 'conv_ln_g', 'conv_ln_b', 'w_out', 'ln1_g', 'ln1_b',
            'mem_wq', 'mem_wk', 'mem_wv', 'mem_wo', 'ln2_g', 'ln2_b', 'ffn_up', 'ffn_conv_w', 'ffn_conv_b',
            'ffn_down', 'ln3_g', 'ln3_b']
WEIGHTS = IN_NAMES[2:]
MATRICES = [('w_in', True), ('w_out', False), ('mem_wq', False), ('mem_wk', False), ('mem_wv', False),
            ('mem_wo', False), ('ffn_up', True), ('ffn_down', False)]
TAPS = ['conv_w', 'ffn_conv_w']
REPLICATED = ['conv_b', 'conv_ln_g', 'conv_ln_b', 'ln1_g', 'ln1_b', 'ln2_g', 'ln2_b', 'ffn_conv_b', 'ln3_g', 'ln3_b']


def _pick(dim, pref, align=LANES):
    if dim <= pref:
        return dim
    fits = [t for t in range(align, pref + 1, align) if dim % t == 0]
    return fits[-1] if fits else dim


def _params(sem):
    return pltpu.CompilerParams(dimension_semantics=sem, vmem_limit_bytes=VMEM_LIMIT)


def _mm(a, b, *, ta=False, tb=False, out_dtype=F32, name, exchange=None):
    if ta:
        kdim, m = a.shape
        tm, tn, tk = _pick(m, 1408), _pick(b.shape[0 if tb else 1], 1024), _pick(kdim, 1024)
    else:
        m, kdim = a.shape
        tm, tn, tk = _pick(m, 512), _pick(b.shape[0 if tb else 1], 1536), _pick(kdim, 2816)
    if tb:
        n, kb = b.shape
    else:
        kb, n = b.shape
    assert kdim == kb, (a.shape, b.shape, ta, tb)
    grid = (m // tm, n // tn, kdim // tk)
    nk = grid[2]
    dims = (((0 if ta else 1,), (1 if tb else 0,)), ((), ()))
    n_ex_in = len(exchange.inputs) if exchange else 0
    n_ex_out = len(exchange.out_shapes) if exchange else 0

    def body(*refs):
        a_ref, b_ref = refs[:2]
        ex_in = refs[2:2 + n_ex_in]
        o_ref = refs[2 + n_ex_in]
        ex_out = refs[3 + n_ex_in:3 + n_ex_in + n_ex_out]
        scratch = refs[3 + n_ex_in + n_ex_out:]
        if nk > 1:
            acc_ref, scratch = scratch[0], scratch[1:]
        ids = [pl.program_id(d) for d in range(3)]
        if exchange:
            @pl.when((ids[0] == 0) & (ids[1] == 0) & (ids[2] == 0))
            def _():
                exchange.start(ex_in, ex_out, scratch)

        prod = lax.dot_general(a_ref[...].astype(BF16), b_ref[...].astype(BF16), dims,
                               preferred_element_type=F32)
        if nk == 1:
            o_ref[...] = prod.astype(out_dtype)
        else:
            k = ids[2]

            @pl.when(k == 0)
            def _():
                acc_ref[...] = prod

            @pl.when(k > 0)
            def _():
                acc_ref[...] += prod

            @pl.when(k == nk - 1)
            def _():
                o_ref[...] = acc_ref[...].astype(out_dtype)

        if exchange:
            @pl.when((ids[0] == grid[0] - 1) & (ids[1] == grid[1] - 1) & (ids[2] == grid[2] - 1))
            def _():
                exchange.finish(ex_in, ex_out, scratch)

    a_spec = pl.BlockSpec((tk, tm), lambda i, j, k: (k, i)) if ta else pl.BlockSpec((tm, tk), lambda i, j, k: (i, k))
    b_spec = pl.BlockSpec((tn, tk), lambda i, j, k: (j, k)) if tb else pl.BlockSpec((tk, tn), lambda i, j, k: (k, j))
    hbm = pl.BlockSpec(memory_space=pl.ANY)
    outs = pl.pallas_call(
        body, name=name,
        out_shape=(jax.ShapeDtypeStruct((m, n), out_dtype),) + tuple(exchange.out_shapes if exchange else ()),
        grid=grid,
        in_specs=[a_spec, b_spec] + [hbm] * n_ex_in,
        out_specs=(pl.BlockSpec((tm, tn), lambda i, j, k: (i, j)),) + (hbm,) * n_ex_out,
        scratch_shapes=([] if nk == 1 else [pltpu.VMEM((tm, tn), F32)]) + list(exchange.scratch_shapes if exchange else []),
        compiler_params=_params(("arbitrary",) * 3 if exchange else ("parallel", "parallel", "arbitrary")),
    )(a, b, *(exchange.inputs if exchange else ()))
    return (outs[0], list(outs[1:])) if exchange else outs[0]


def _ln_stats(r):
    mu = jnp.mean(r, axis=-1, keepdims=True)
    xc = r - mu
    var = jnp.mean(xc * xc, axis=-1, keepdims=True)
    rstd = lax.rsqrt(var + LN_EPS)
    return xc * rstd, rstd


def _res_ln_fwd(x, f, g, b, *, alpha, name, ts=512):
    s, d = x.shape
    ts = _pick(s, ts)

    def body(x_ref, f_ref, g_ref, b_ref, r_ref, y_ref, yb_ref):
        r = alpha * x_ref[...] + f_ref[...]
        xhat, _ = _ln_stats(r)
        y = xhat * g_ref[...] + b_ref[...]
        r_ref[...] = r
        y_ref[...] = y
        yb_ref[...] = y.astype(BF16)

    tok = pl.BlockSpec((ts, d), lambda i: (i, 0))
    vec = pl.BlockSpec((1, d), lambda i: (0, 0))
    return pl.pallas_call(
        body, name=name,
        out_shape=(jax.ShapeDtypeStruct((s, d), F32), jax.ShapeDtypeStruct((s, d), F32),
                   jax.ShapeDtypeStruct((s, d), BF16)),
        grid=(s // ts,), in_specs=[tok, tok, vec, vec], out_specs=(tok, tok, tok),
        compiler_params=_params(("parallel",)),
    )(x, f, g.reshape(1, d), b.reshape(1, d))


def _ln_bwd(da, dres, r, g, *, alpha, name, ts=512):
    s, d = r.shape
    ts = _pick(s, ts)
    has_res = dres is not None

    def body(*refs):
        if has_res:
            da_ref, dres_ref, r_ref, g_ref, dr_ref, drb_ref, dg_ref, db_ref = refs
            dy = da_ref[...] + alpha * dres_ref[...]
        else:
            da_ref, r_ref, g_ref, dr_ref, drb_ref, dg_ref, db_ref = refs
            dy = da_ref[...]
        xhat, rstd = _ln_stats(r_ref[...])
        dxhat = dy * g_ref[...]
        m1 = jnp.mean(dxhat, axis=-1, keepdims=True)
        m2 = jnp.mean(dxhat * xhat, axis=-1, keepdims=True)
        dr = rstd * (dxhat - m1 - xhat * m2)
        dr_ref[...] = dr
        drb_ref[...] = dr.astype(BF16)

        @pl.when(pl.program_id(0) == 0)
        def _():
            dg_ref[...] = jnp.zeros_like(dg_ref)
            db_ref[...] = jnp.zeros_like(db_ref)

        dg_ref[...] += jnp.sum(dy * xhat, axis=0, keepdims=True)
        db_ref[...] += jnp.sum(dy, axis=0, keepdims=True)

    tok = pl.BlockSpec((ts, d), lambda i: (i, 0))
    vec = pl.BlockSpec((1, d), lambda i: (0, 0))
    ins = [da, dres, r, g.reshape(1, d)] if has_res else [da, r, g.reshape(1, d)]
    dr, drb, dg, db = pl.pallas_call(
        body, name=name,
        out_shape=(jax.ShapeDtypeStruct((s, d), F32), jax.ShapeDtypeStruct((s, d), BF16),
                   jax.ShapeDtypeStruct((1, d), F32), jax.ShapeDtypeStruct((1, d), F32)),
        grid=(s // ts,), in_specs=[tok] * (len(ins) - 1) + [vec], out_specs=(tok, tok, vec, vec),
        compiler_params=_params(("arbitrary",)),
    )(*ins)
    return dr, drb, dg.reshape(d), db.reshape(d)


def _axpy(a, b, *, alpha, name, ts=512):
    s, d = a.shape
    ts = _pick(s, ts)

    def body(a_ref, b_ref, o_ref):
        o_ref[...] = a_ref[...] + alpha * b_ref[...]

    tok = pl.BlockSpec((ts, d), lambda i: (i, 0))
    return pl.pallas_call(body, name=name, out_shape=jax.ShapeDtypeStruct((s, d), F32), grid=(s // ts,),
                          in_specs=[tok, tok], out_specs=tok, compiler_params=_params(("parallel",)))(a, b)


def _loss_and_grad(y, target, *, name, ts=512):
    s, d = y.shape
    ts = _pick(s, ts)
    inv_d = 1.0 / d

    def body(y_ref, t_ref, dy_ref, loss_ref):
        e = y_ref[...] - t_ref[...]
        dy_ref[...] = e * inv_d

        @pl.when(pl.program_id(0) == 0)
        def _():
            loss_ref[...] = jnp.zeros_like(loss_ref)

        loss_ref[...] += jnp.sum(e * e, axis=0, keepdims=True) * (0.5 * inv_d)

    tok = pl.BlockSpec((ts, d), lambda i: (i, 0))
    vec = pl.BlockSpec((1, d), lambda i: (0, 0))
    dy, part = pl.pallas_call(
        body, name=name,
        out_shape=(jax.ShapeDtypeStruct((s, d), F32), jax.ShapeDtypeStruct((1, d), F32)),
        grid=(s // ts,), in_specs=[tok, tok], out_specs=(tok, vec),
        compiler_params=_params(("arbitrary",)),
    )(y, target)
    return dy, part


def _row_sum(v, *, name):
    def body(v_ref, o_ref):
        o_ref[...] = jnp.sum(v_ref[...], axis=1, keepdims=True)

    return pl.pallas_call(body, name=name, out_shape=jax.ShapeDtypeStruct((1, 1), F32))(v)


def _sigmoid(x):
    return 1.0 / (1.0 + jnp.exp(-x))


def _row_chunks(s, pref=512):
    c = _pick(s, pref, 8)
    return [(i * c, c) for i in range(s // c)]


def _glu_conv_fwd(proj, w, b, *, cc, name):
    s = proj.shape[0]
    kw = w.shape[0]
    pad = 32
    assert kw - 1 <= pad
    ncb = cc // LANES
    chunks = _row_chunks(s)

    def body(a_ref, g_ref, w_ref, b_ref, o_ref, u0_ref):
        u0_ref[pl.ds(0, pad), :] = jnp.zeros((pad, LANES), F32)
        for r0, rc in chunks:
            u0_ref[pl.ds(pad + r0, rc), :] = a_ref[pl.ds(r0, rc), :] * _sigmoid(g_ref[pl.ds(r0, rc), :])
        for r0, rc in chunks:
            acc = jnp.zeros((rc, LANES), F32) + b_ref[...]
            for k in range(kw):
                acc = acc + w_ref[pl.ds(k, 1), :] * u0_ref[pl.ds(pad + r0 - (kw - 1) + k, rc), :]
            o_ref[pl.ds(r0, rc), :] = acc

    return pl.pallas_call(
        body, name=name,
        out_shape=jax.ShapeDtypeStruct((s, cc), F32),
        grid=(ncb,),
        in_specs=[pl.BlockSpec((s, LANES), lambda c: (0, c)), pl.BlockSpec((s, LANES), lambda c: (0, ncb + c)),
                  pl.BlockSpec((kw, LANES), lambda c: (0, c)), pl.BlockSpec((1, LANES), lambda c: (0, c))],
        out_specs=pl.BlockSpec((s, LANES), lambda c: (0, c)),
        scratch_shapes=[pltpu.VMEM((s + pad, LANES), F32)],
        compiler_params=_params(("parallel",)),
    )(proj, proj, w, b.reshape(1, cc))


def _glu_conv_bwd(du1, proj, w, *, cc, name):
    s = proj.shape[0]
    kw = w.shape[0]
    pad = 32
    ncb = cc // LANES
    chunks = _row_chunks(s)

    def body(d_ref, a_ref, g_ref, w_ref, da_ref, dg_ref, dw_ref, db_ref, u0_ref, dp_ref):
        u0_ref[pl.ds(0, pad), :] = jnp.zeros((pad, LANES), F32)
        dp_ref[pl.ds(s, pad), :] = jnp.zeros((pad, LANES), F32)
        for r0, rc in chunks:
            u0_ref[pl.ds(pad + r0, rc), :] = a_ref[pl.ds(r0, rc), :] * _sigmoid(g_ref[pl.ds(r0, rc), :])
            dp_ref[pl.ds(r0, rc), :] = d_ref[pl.ds(r0, rc), :]
        dws = [jnp.zeros((1, LANES), F32) for _ in range(kw)]
        dbs = jnp.zeros((1, LANES), F32)
        for r0, rc in chunks:
            d = dp_ref[pl.ds(r0, rc), :]
            dbs = dbs + jnp.sum(d, axis=0, keepdims=True)
            du0 = jnp.zeros((rc, LANES), F32)
            for k in range(kw):
                du0 = du0 + w_ref[pl.ds(k, 1), :] * dp_ref[pl.ds(r0 + (kw - 1) - k, rc), :]
                dws[k] = dws[k] + jnp.sum(d * u0_ref[pl.ds(pad + r0 - (kw - 1) + k, rc), :], axis=0, keepdims=True)
            sg = _sigmoid(g_ref[pl.ds(r0, rc), :])
            a = a_ref[pl.ds(r0, rc), :]
            da_ref[pl.ds(r0, rc), :] = (du0 * sg).astype(BF16)
            dg_ref[pl.ds(r0, rc), :] = (du0 * a * sg * (1.0 - sg)).astype(BF16)
        for k in range(kw):
            dw_ref[pl.ds(k, 1), :] = dws[k]
        db_ref[...] = dbs

    col = lambda off: pl.BlockSpec((s, LANES), lambda c: (0, off + c))
    da, dg, dw, db = pl.pallas_call(
        body, name=name,
        out_shape=(jax.ShapeDtypeStruct((s, cc), BF16), jax.ShapeDtypeStruct((s, cc), BF16),
                   jax.ShapeDtypeStruct((kw, cc), F32), jax.ShapeDtypeStruct((1, cc), F32)),
        grid=(ncb,),
        in_specs=[col(0), col(0), col(ncb), pl.BlockSpec((kw, LANES), lambda c: (0, c))],
        out_specs=(col(0), col(0), pl.BlockSpec((kw, LANES), lambda c: (0, c)), pl.BlockSpec((1, LANES), lambda c: (0, c))),
        scratch_shapes=[pltpu.VMEM((s + pad, LANES), F32), pltpu.VMEM((s + pad, LANES), F32)],
        compiler_params=_params(("parallel",)),
    )(du1, proj, proj, w)
    return da, dg, dw, db.reshape(cc)


def _cln_silu_fwd(u1, g, b, *, name, ts=512):
    s, cc = u1.shape
    ts = _pick(s, ts)

    def body(u_ref, g_ref, b_ref, o_ref):
        xhat, _ = _ln_stats(u_ref[...])
        y = xhat * g_ref[...] + b_ref[...]
        o_ref[...] = (y * _sigmoid(y)).astype(BF16)

    tok = pl.BlockSpec((ts, cc), lambda i: (i, 0))
    vec = pl.BlockSpec((1, cc), lambda i: (0, 0))
    return pl.pallas_call(body, name=name, out_shape=jax.ShapeDtypeStruct((s, cc), BF16), grid=(s // ts,),
                          in_specs=[tok, vec, vec], out_specs=tok,
                          compiler_params=_params(("parallel",)))(u1, g.reshape(1, cc), b.reshape(1, cc))


def _cln_silu_bwd(dua, u1, g, b, *, name, ts=512):
    s, cc = u1.shape
    ts = _pick(s, ts)

    def body(d_ref, u_ref, g_ref, b_ref, du_ref, dg_ref, db_ref):
        xhat, rstd = _ln_stats(u_ref[...])
        y = xhat * g_ref[...] + b_ref[...]
        sg = _sigmoid(y)
        dy = d_ref[...] * (sg * (1.0 + y * (1.0 - sg)))
        dxhat = dy * g_ref[...]
        m1 = jnp.mean(dxhat, axis=-1, keepdims=True)
        m2 = jnp.mean(dxhat * xhat, axis=-1, keepdims=True)
        du_ref[...] = rstd * (dxhat - m1 - xhat * m2)

        @pl.when(pl.program_id(0) == 0)
        def _():
            dg_ref[...] = jnp.zeros_like(dg_ref)
            db_ref[...] = jnp.zeros_like(db_ref)

        dg_ref[...] += jnp.sum(dy * xhat, axis=0, keepdims=True)
        db_ref[...] += jnp.sum(dy, axis=0, keepdims=True)

    tok = pl.BlockSpec((ts, cc), lambda i: (i, 0))
    vec = pl.BlockSpec((1, cc), lambda i: (0, 0))
    du1, dg, db = pl.pallas_call(
        body, name=name,
        out_shape=(jax.ShapeDtypeStruct((s, cc), F32), jax.ShapeDtypeStruct((1, cc), F32),
                   jax.ShapeDtypeStruct((1, cc), F32)),
        grid=(s // ts,), in_specs=[tok, tok, vec, vec], out_specs=(tok, vec, vec),
        compiler_params=_params(("arbitrary",)),
    )(dua, u1, g.reshape(1, cc), b.reshape(1, cc))
    return du1, dg.reshape(cc), db.reshape(cc)


def _softplus_parts(z):
    sp = jnp.log1p(jnp.exp(-jnp.abs(z)))
    return jnp.minimum(-z, 0.0) - sp, jnp.minimum(z, 0.0) - sp


def _split_dot(x, m):
    hi = x.astype(BF16)
    lo = (x - hi.astype(F32)).astype(BF16)
    return jnp.dot(hi, m, preferred_element_type=F32) + jnp.dot(lo, m, preferred_element_type=F32)


_NT = (((1,), (1,)), ((), ()))
_TN = (((0,), (0,)), ((), ()))


def _sb_fwd(proj, *, col0, n_pairs, name, tq=256):
    s = proj.shape[0]
    tq = _pick(s, tq)
    nq = s // tq
    cb0 = col0 // LANES
    scale = SB_HEAD_DIM ** -0.5

    def body(q_ref, k_ref, v_ref, o_ref, t_ref, n_ref):
        i = pl.program_id(1)
        lane = lax.broadcasted_iota(jnp.int32, (1, LANES), 1)
        row = lax.broadcasted_iota(jnp.int32, (tq, tq), 0)
        col = lax.broadcasted_iota(jnp.int32, (tq, tq), 1)
        vis = col < row
        m_after = (row > col).astype(BF16)
        q = q_ref[...] * scale
        acc_tot = jnp.zeros((tq, LANES), F32)
        t_tot = jnp.zeros((tq, LANES), F32)
        n_tot = jnp.zeros((8, LANES), F32)
        for h in range(2):
            hm = (lane >= SB_HEAD_DIM * h) & (lane < SB_HEAD_DIM * (h + 1))
            qh = jnp.where(hm, q, 0.0).astype(BF16)

            def tile(j, c, acc, masked, hm=hm, qh=qh):
                start = pl.multiple_of(j * tq, tq)
                kb = k_ref[pl.ds(start, tq), :].astype(BF16)
                vb = jnp.where(hm, v_ref[pl.ds(start, tq), :], 0.0).astype(BF16)
                z = lax.dot_general(qh, kb, _NT, preferred_element_type=F32)
                lk, lb = _softplus_parts(z)
                if masked:
                    lk = jnp.where(vis, lk, 0.0)
                later = _split_dot(lk, m_after)
                a = jnp.exp(lb + later + c)
                if masked:
                    a = jnp.where(vis, a, 0.0)
                acc = acc + jnp.dot(a.astype(BF16), vb, preferred_element_type=F32)
                return c + jnp.sum(lk, axis=1, keepdims=True), acc

            def more(st):
                return jnp.logical_and(st[0] < i, jnp.max(st[1]) > -EXP_UNDERFLOW)

            def step(st, tile=tile):
                c, acc = tile(i - 1 - st[0], st[1], st[2], False)
                return st[0] + 1, c, acc

            c, acc = tile(i, jnp.zeros((tq, 1), F32), jnp.zeros((tq, LANES), F32), True)
            n, c, acc = lax.while_loop(more, step, (jnp.int32(0), c, acc))
            acc_tot = acc_tot + acc
            t_tot = jnp.where(hm, c, t_tot)
            n_tot = jnp.where(hm, n.astype(F32), n_tot)
        o_ref[...] = acc_tot.astype(BF16)
        t_ref[...] = t_tot
        n_ref[...] = n_tot

    seq = lambda off: pl.BlockSpec((s, LANES), lambda p, i: (0, cb0 + off + p))
    return pl.pallas_call(
        body, name=name,
        out_shape=(jax.ShapeDtypeStruct((s, n_pairs * LANES), BF16), jax.ShapeDtypeStruct((n_pairs, s, LANES), F32),
                   jax.ShapeDtypeStruct((n_pairs, nq * 8, LANES), F32)),
        grid=(n_pairs, nq),
        in_specs=[pl.BlockSpec((tq, LANES), lambda p, i: (i, cb0 + p)), seq(n_pairs), seq(2 * n_pairs)],
        out_specs=(pl.BlockSpec((tq, LANES), lambda p, i: (i, p)), pl.BlockSpec((None, tq, LANES), lambda p, i: (p, i, 0)),
                   pl.BlockSpec((None, 8, LANES), lambda p, i: (p, i, 0))),
        compiler_params=_params(("parallel", "arbitrary")),
    )(proj, proj, proj)


def _sb_bwd(proj, t_sum, n_walked, dua, *, col0, n_pairs, do_col0, name, tq=256):
    s = proj.shape[0]
    tq = _pick(s, tq)
    cb0 = col0 // LANES
    dcb0 = do_col0 // LANES
    scale = SB_HEAD_DIM ** -0.5

    def body(q_ref, k_ref, v_ref, t_ref, n_ref, do_ref, dq_ref, dk_ref, dv_ref):
        i = pl.program_id(1)

        @pl.when(i == 0)
        def _():
            dk_ref[...] = jnp.zeros_like(dk_ref)
            dv_ref[...] = jnp.zeros_like(dv_ref)

        lane = lax.broadcasted_iota(jnp.int32, (1, LANES), 1)
        row = lax.broadcasted_iota(jnp.int32, (tq, tq), 0)
        col = lax.broadcasted_iota(jnp.int32, (tq, tq), 1)
        vis = col < row
        m_upto = (row <= col).astype(BF16)
        m_before = (row < col).astype(BF16)
        q = q_ref[...] * scale
        do = do_ref[...]
        dq_tot = jnp.zeros((tq, LANES), F32)
        for h in range(2):
            hm = (lane >= SB_HEAD_DIM * h) & (lane < SB_HEAD_DIM * (h + 1))
            qh = jnp.where(hm, q, 0.0).astype(BF16)
            doh = jnp.where(hm, do, 0.0).astype(BF16)
            t_all = t_ref[:, SB_HEAD_DIM * h:SB_HEAD_DIM * h + 1]
            first = i - jnp.max(jnp.where(hm, n_ref[...], 0.0)).astype(jnp.int32)

            def tile(j, p_sum, g_sum, dq, masked, hm=hm, qh=qh, doh=doh, t_all=t_all):
                start = pl.multiple_of(j * tq, tq)
                kb = jnp.where(hm, k_ref[pl.ds(start, tq), :], 0.0).astype(BF16)
                vb = jnp.where(hm, v_ref[pl.ds(start, tq), :], 0.0).astype(BF16)
                z = lax.dot_general(qh, kb, _NT, preferred_element_type=F32)
                lk_raw, lb = _softplus_parts(z)
                lk = jnp.where(vis, lk_raw, 0.0) if masked else lk_raw
                upto = _split_dot(lk, m_upto)
                a = jnp.exp(lb + (t_all - p_sum) - upto)
                if masked:
                    a = jnp.where(vis, a, 0.0)
                ab = a.astype(BF16)
                da = lax.dot_general(doh, vb, _NT, preferred_element_type=F32)
                g = a * da
                dv_ref[pl.ds(start, tq), :] += lax.dot_general(ab, doh, _TN, preferred_element_type=F32)
                g_before = g_sum + _split_dot(g, m_before)
                dz = g * jnp.exp(lk_raw) - g_before * jnp.exp(lb)
                if masked:
                    dz = jnp.where(vis, dz, 0.0)
                dzb = dz.astype(BF16)
                dq = dq + jnp.dot(dzb, kb, preferred_element_type=F32)
                dk_ref[pl.ds(start, tq), :] += lax.dot_general(dzb, qh, _TN, preferred_element_type=F32)
                return (p_sum + jnp.sum(lk, axis=1, keepdims=True), g_sum + jnp.sum(g, axis=1, keepdims=True), dq)

            zero = jnp.zeros((tq, 1), F32)
            carry = lax.fori_loop(first, i, lambda j, c: tile(j, c[0], c[1], c[2], False),
                                  (zero, zero, jnp.zeros((tq, LANES), F32)))
            _, _, dq = tile(i, carry[0], carry[1], carry[2], True)
            dq_tot = dq_tot + dq
        dq_ref[...] = dq_tot * scale

    seq = lambda off: pl.BlockSpec((s, LANES), lambda p, i: (0, cb0 + off + p))
    out = jax.ShapeDtypeStruct((s, n_pairs * LANES), F32)
    res = pl.BlockSpec((s, LANES), lambda p, i: (0, p))
    return pl.pallas_call(
        body, name=name,
        out_shape=(out, out, out),
        grid=(n_pairs, s // tq),
        in_specs=[pl.BlockSpec((tq, LANES), lambda p, i: (i, cb0 + p)), seq(n_pairs), seq(2 * n_pairs),
                  pl.BlockSpec((None, tq, LANES), lambda p, i: (p, i, 0)),
                  pl.BlockSpec((None, 8, LANES), lambda p, i: (p, i, 0)),
                  pl.BlockSpec((tq, LANES), lambda p, i: (i, dcb0 + p))],
        out_specs=(pl.BlockSpec((tq, LANES), lambda p, i: (i, p)), res, res),
        compiler_params=_params(("arbitrary", "arbitrary")),
    )(proj, proj, proj, t_sum, n_walked, dua)


def _mem_attn_fwd(qm, km, vm, *, name, tq=512):
    s, d = qm.shape
    heads = d // MEM_HEAD_DIM
    mlen = km.shape[0]
    tq = _pick(s, tq)
    scale = MEM_HEAD_DIM ** -0.5

    def body(q_ref, k_ref, v_ref, o_ref):
        for h in range(heads):
            sl = slice(h * MEM_HEAD_DIM, (h + 1) * MEM_HEAD_DIM)
            q = (q_ref[:, sl] * scale).astype(BF16)
            sc = lax.dot_general(q, k_ref[:, sl].astype(BF16), _NT, preferred_element_type=F32)
            e = jnp.exp(sc - jnp.max(sc, axis=1, keepdims=True))
            p = e / jnp.sum(e, axis=1, keepdims=True)
            o_ref[:, sl] = jnp.dot(p.astype(BF16), v_ref[:, sl].astype(BF16), preferred_element_type=F32).astype(BF16)

    tok = pl.BlockSpec((tq, d), lambda i: (i, 0))
    kv = pl.BlockSpec((mlen, d), lambda i: (0, 0))
    return pl.pallas_call(body, name=name, out_shape=jax.ShapeDtypeStruct((s, d), BF16), grid=(s // tq,),
                          in_specs=[tok, kv, kv], out_specs=tok, compiler_params=_params(("parallel",)))(qm, km, vm)


def _mem_attn_bwd(qm, km, vm, do, *, name, tq=512):
    s, d = qm.shape
    heads = d // MEM_HEAD_DIM
    mlen = km.shape[0]
    tq = _pick(s, tq)
    scale = MEM_HEAD_DIM ** -0.5

    def body(q_ref, k_ref, v_ref, do_ref, dq_ref, dk_ref, dv_ref):
        @pl.when(pl.program_id(0) == 0)
        def _():
            dk_ref[...] = jnp.zeros_like(dk_ref)
            dv_ref[...] = jnp.zeros_like(dv_ref)

        for h in range(heads):
            sl = slice(h * MEM_HEAD_DIM, (h + 1) * MEM_HEAD_DIM)
            q = (q_ref[:, sl] * scale).astype(BF16)
            k = k_ref[:, sl].astype(BF16)
            v = v_ref[:, sl].astype(BF16)
            sc = lax.dot_general(q, k, _NT, preferred_element_type=F32)
            e = jnp.exp(sc - jnp.max(sc, axis=1, keepdims=True))
            p = e / jnp.sum(e, axis=1, keepdims=True)
            dob = do_ref[:, sl].astype(BF16)
            dv_ref[:, sl] += lax.dot_general(p.astype(BF16), dob, _TN, preferred_element_type=F32)
            dp = lax.dot_general(dob, v, _NT, preferred_element_type=F32)
            ds = (p * (dp - jnp.sum(dp * p, axis=1, keepdims=True))).astype(BF16)
            dq_ref[:, sl] = (jnp.dot(ds, k, preferred_element_type=F32) * scale).astype(BF16)
            dk_ref[:, sl] += lax.dot_general(ds, q, _TN, preferred_element_type=F32)

    tok = pl.BlockSpec((tq, d), lambda i: (i, 0))
    kv = pl.BlockSpec((mlen, d), lambda i: (0, 0))
    return pl.pallas_call(
        body, name=name,
        out_shape=(jax.ShapeDtypeStruct((s, d), BF16), jax.ShapeDtypeStruct((mlen, d), F32),
                   jax.ShapeDtypeStruct((mlen, d), F32)),
        grid=(s // tq,), in_specs=[tok, kv, kv, tok], out_specs=(tok, kv, kv),
        compiler_params=_params(("arbitrary",)),
    )(qm, km, vm, do)


def _ffn_act_fwd(up, w, b, *, name):
    s, two_f = up.shape
    ff = two_f // 2
    nfb = ff // LANES
    kw = w.shape[0]
    pad = 8
    chunks = _row_chunks(s)

    def body(v_ref, g_ref, wv_ref, wg_ref, bv_ref, bg_ref, o_ref, vp_ref, gp_ref):
        vp_ref[pl.ds(0, pad), :] = jnp.zeros((pad, LANES), F32)
        gp_ref[pl.ds(0, pad), :] = jnp.zeros((pad, LANES), F32)
        for r0, rc in chunks:
            vp_ref[pl.ds(pad + r0, rc), :] = v_ref[pl.ds(r0, rc), :]
            gp_ref[pl.ds(pad + r0, rc), :] = g_ref[pl.ds(r0, rc), :]
        for r0, rc in chunks:
            vc = jnp.zeros((rc, LANES), F32) + bv_ref[...]
            gc = jnp.zeros((rc, LANES), F32) + bg_ref[...]
            for k in range(kw):
                off = pad + r0 - (kw - 1) + k
                vc = vc + wv_ref[pl.ds(k, 1), :] * vp_ref[pl.ds(off, rc), :]
                gc = gc + wg_ref[pl.ds(k, 1), :] * gp_ref[pl.ds(off, rc), :]
            o_ref[pl.ds(r0, rc), :] = (gc * _sigmoid(gc) * vc).astype(BF16)

    col = lambda off: pl.BlockSpec((s, LANES), lambda c: (0, off + c))
    tap = lambda off: pl.BlockSpec((kw, LANES), lambda c: (0, off + c))
    vec = lambda off: pl.BlockSpec((1, LANES), lambda c: (0, off + c))
    return pl.pallas_call(
        body, name=name, out_shape=jax.ShapeDtypeStruct((s, ff), BF16), grid=(nfb,),
        in_specs=[col(0), col(nfb), tap(0), tap(nfb), vec(0), vec(nfb)], out_specs=col(0),
        scratch_shapes=[pltpu.VMEM((s + pad, LANES), F32), pltpu.VMEM((s + pad, LANES), F32)],
        compiler_params=_params(("parallel",)),
    )(up, up, w, w, b.reshape(1, two_f), b.reshape(1, two_f))


def _ffn_act_bwd(up, dact, w, b, *, name):
    s, two_f = up.shape
    ff = two_f // 2
    nfb = ff // LANES
    kw = w.shape[0]
    pad = 8
    chunks = _row_chunks(s)

    def body(v_ref, g_ref, d_ref, wv_ref, wg_ref, bv_ref, bg_ref, dv_ref, dg_ref, dwv_ref, dwg_ref, dbv_ref, dbg_ref,
             vp_ref, gp_ref, dvc_ref, dgc_ref):
        vp_ref[pl.ds(0, pad), :] = jnp.zeros((pad, LANES), F32)
        gp_ref[pl.ds(0, pad), :] = jnp.zeros((pad, LANES), F32)
        dvc_ref[pl.ds(s, pad), :] = jnp.zeros((pad, LANES), F32)
        dgc_ref[pl.ds(s, pad), :] = jnp.zeros((pad, LANES), F32)
        for r0, rc in chunks:
            vp_ref[pl.ds(pad + r0, rc), :] = v_ref[pl.ds(r0, rc), :]
            gp_ref[pl.ds(pad + r0, rc), :] = g_ref[pl.ds(r0, rc), :]
        dwv = [jnp.zeros((1, LANES), F32) for _ in range(kw)]
        dwg = [jnp.zeros((1, LANES), F32) for _ in range(kw)]
        dbv = jnp.zeros((1, LANES), F32)
        dbg = jnp.zeros((1, LANES), F32)
        for r0, rc in chunks:
            vc = jnp.zeros((rc, LANES), F32) + bv_ref[...]
            gc = jnp.zeros((rc, LANES), F32) + bg_ref[...]
            for k in range(kw):
                off = pad + r0 - (kw - 1) + k
                vc = vc + wv_ref[pl.ds(k, 1), :] * vp_ref[pl.ds(off, rc), :]
                gc = gc + wg_ref[pl.ds(k, 1), :] * gp_ref[pl.ds(off, rc), :]
            sg = _sigmoid(gc)
            d = d_ref[pl.ds(r0, rc), :]
            dvc = d * (gc * sg)
            dgc = d * vc * (sg * (1.0 + gc * (1.0 - sg)))
            dvc_ref[pl.ds(r0, rc), :] = dvc
            dgc_ref[pl.ds(r0, rc), :] = dgc
            dbv = dbv + jnp.sum(dvc, axis=0, keepdims=True)
            dbg = dbg + jnp.sum(dgc, axis=0, keepdims=True)
            for k in range(kw):
                off = pad + r0 - (kw - 1) + k
                dwv[k] = dwv[k] + jnp.sum(dvc * vp_ref[pl.ds(off, rc), :], axis=0, keepdims=True)
                dwg[k] = dwg[k] + jnp.sum(dgc * gp_ref[pl.ds(off, rc), :], axis=0, keepdims=True)
        for r0, rc in chunks:
            dv = jnp.zeros((rc, LANES), F32)
            dg = jnp.zeros((rc, LANES), F32)
            for k in range(kw):
                off = r0 + (kw - 1) - k
                dv = dv + wv_ref[pl.ds(k, 1), :] * dvc_ref[pl.ds(off, rc), :]
                dg = dg + wg_ref[pl.ds(k, 1), :] * dgc_ref[pl.ds(off, rc), :]
            dv_ref[pl.ds(r0, rc), :] = dv.astype(BF16)
            dg_ref[pl.ds(r0, rc), :] = dg.astype(BF16)
        for k in range(kw):
            dwv_ref[pl.ds(k, 1), :] = dwv[k]
            dwg_ref[pl.ds(k, 1), :] = dwg[k]
        dbv_ref[...] = dbv
        dbg_ref[...] = dbg

    col = lambda off: pl.BlockSpec((s, LANES), lambda c: (0, off + c))
    tap = lambda off: pl.BlockSpec((kw, LANES), lambda c: (0, off + c))
    vec = lambda off: pl.BlockSpec((1, LANES), lambda c: (0, off + c))
    big = lambda: pltpu.VMEM((s + pad, LANES), F32)
    dv, dg, dwv, dwg, dbv, dbg = pl.pallas_call(
        body, name=name,
        out_shape=(jax.ShapeDtypeStruct((s, ff), BF16), jax.ShapeDtypeStruct((s, ff), BF16),
                   jax.ShapeDtypeStruct((kw, ff), F32), jax.ShapeDtypeStruct((kw, ff), F32),
                   jax.ShapeDtypeStruct((1, ff), F32), jax.ShapeDtypeStruct((1, ff), F32)),
        grid=(nfb,),
        in_specs=[col(0), col(nfb), col(0), tap(0), tap(nfb), vec(0), vec(nfb)],
        out_specs=(col(0), col(0), tap(0), tap(0), vec(0), vec(0)),
        scratch_shapes=[big(), big(), big(), big()],
        compiler_params=_params(("parallel",)),
    )(up, up, dact, w, w, b.reshape(1, two_f), b.reshape(1, two_f))
    return (jnp.concatenate([dv, dg], axis=1), jnp.concatenate([dwv, dwg], axis=1),
            jnp.concatenate([dbv, dbg], axis=1).reshape(two_f))


def _sum_parts(parts, *, name, tr=256):
    n_parts, rows, cols = parts.shape
    tr = _pick(rows, tr, 16)

    def body(p_ref, o_ref):
        g = p_ref[0].astype(F32)
        for k in range(1, n_parts):
            g = g + p_ref[k].astype(F32)
        o_ref[...] = g

    return pl.pallas_call(
        body, name=name, out_shape=jax.ShapeDtypeStruct((rows, cols), F32), grid=(rows // tr,),
        in_specs=[pl.BlockSpec((n_parts, tr, cols), lambda i: (0, i, 0))],
        out_specs=pl.BlockSpec((tr, cols), lambda i: (i, 0)), compiler_params=_params(("parallel",)),
    )(parts)


def _sum_adamw(parts, w, m, v, *, name, tr=256):
    n_parts, rows, cols = parts.shape
    tr = _pick(rows, tr, 16)
    c1 = 1.0 / (1.0 - ADAM_B1 ** ADAM_STEP)
    c2 = 1.0 / (1.0 - ADAM_B2 ** ADAM_STEP)

    def body(p_ref, w_ref, m_ref, v_ref, g_ref, d_ref, nm_ref, nv_ref):
        g = p_ref[0].astype(F32)
        for k in range(1, n_parts):
            g = g + p_ref[k].astype(F32)
        nm = ADAM_B1 * m_ref[...] + (1.0 - ADAM_B1) * g
        nv = ADAM_B2 * v_ref[...] + (1.0 - ADAM_B2) * (g * g)
        g_ref[...] = g
        nm_ref[...] = nm
        nv_ref[...] = nv
        d_ref[...] = -ADAM_LR * ((nm * c1) / (jnp.sqrt(nv * c2) + ADAM_EPS) + ADAM_WD * w_ref[...])

    blk = pl.BlockSpec((tr, cols), lambda i: (i, 0))
    out = jax.ShapeDtypeStruct((rows, cols), F32)
    return pl.pallas_call(
        body, name=name, out_shape=(out, out, out, out), grid=(rows // tr,),
        in_specs=[pl.BlockSpec((n_parts, tr, cols), lambda i: (0, i, 0)), blk, blk, blk],
        out_specs=(blk, blk, blk, blk), compiler_params=_params(("parallel",)),
    )(parts, w, m, v)


def _mesh_pos():
    return lax.axis_index("x"), lax.axis_index("y"), lax.axis_index("c")


def _flip(pos, k):
    x, y, c = pos
    return (1 - x if k & 4 else x, 1 - y if k & 2 else y, 1 - c if k & 1 else c)


def _dev_index(pos):
    return 4 * pos[0] + 2 * pos[1] + pos[2]


N_PEERS = N_DEV - 1


class _Exchange:
    def __init__(self, inputs, out_shapes, start, finish):
        n = len(inputs)
        self.inputs, self.out_shapes, self.start, self.finish = list(inputs), list(out_shapes), start, finish
        self.scratch_shapes = [pltpu.SemaphoreType.DMA((n * N_PEERS,)), pltpu.SemaphoreType.DMA((n * N_PEERS,)),
                               pltpu.SemaphoreType.DMA((n,))]


def _gather_exchange(xs):
    n = len(xs)

    def plan(x_refs, out_refs, sems):
        send_sems, recv_sems, local_sems = sems
        me = _mesh_pos()
        sibling = _flip(me, 1)
        chips = [_flip(me, 4), _flip(me, 2), _flip(me, 6)]

        def copy(a, k, block, to, from_input=False):
            slot = out_refs[a].at[_dev_index(block)]
            return pltpu.make_async_remote_copy(
                src_ref=x_refs[a] if from_input else slot, dst_ref=slot,
                send_sem=send_sems.at[a * N_PEERS + k], recv_sem=recv_sems.at[a * N_PEERS + k],
                device_id=to, device_id_type=pl.DeviceIdType.MESH)

        mine = [pltpu.make_async_copy(x_refs[a], out_refs[a].at[_dev_index(me)], local_sems.at[a]) for a in range(n)]
        first = [copy(a, 0, me, sibling, True) for a in range(n)]
        first += [copy(a, 1 + j, me, chip, True) for j, chip in enumerate(chips) for a in range(n)]
        return me, sibling, chips, copy, mine, first

    def start(x_refs, out_refs, sems):
        _, _, _, _, mine, first = plan(x_refs, out_refs, sems)
        for cp in mine + first:
            cp.start()

    def finish(x_refs, out_refs, sems):
        me, sibling, chips, copy, mine, first = plan(x_refs, out_refs, sems)
        passed = []
        for j, chip in enumerate(chips):
            for a in range(n):
                copy(a, 1 + j, chip, me).wait_recv()
                passed.append(copy(a, 4 + j, chip, sibling))
                passed[-1].start()
        for a in range(n):
            copy(a, 0, sibling, me).wait_recv()
        for j, chip in enumerate(chips):
            for a in range(n):
                copy(a, 4 + j, _flip(chip, 1), me).wait_recv()
        for cp in first + passed:
            cp.wait_send()
        for cp in mine:
            cp.wait()

    return _Exchange(xs, [jax.ShapeDtypeStruct((N_DEV,) + x.shape, x.dtype) for x in xs], start, finish)


def _scatter_exchange(xs):
    n = len(xs)

    def plan(x_refs, out_refs, sems):
        send_sems, recv_sems, local_sems = sems
        me = _mesh_pos()
        my_slot = _dev_index(me)

        def copy(a, k):
            peer = _flip(me, k)
            return pltpu.make_async_remote_copy(
                src_ref=x_refs[a].at[_dev_index(peer)], dst_ref=out_refs[a].at[my_slot],
                send_sem=send_sems.at[a * N_PEERS + k - 1], recv_sem=recv_sems.at[a * N_PEERS + k - 1],
                device_id=peer, device_id_type=pl.DeviceIdType.MESH)

        mine = [pltpu.make_async_copy(x_refs[a].at[my_slot], out_refs[a].at[my_slot], local_sems.at[a]) for a in range(n)]
        return mine, [copy(a, k) for k in range(1, N_DEV) for a in range(n)]

    def start(x_refs, out_refs, sems):
        mine, copies = plan(x_refs, out_refs, sems)
        for cp in mine + copies:
            cp.start()

    def finish(x_refs, out_refs, sems):
        mine, copies = plan(x_refs, out_refs, sems)
        for cp in copies:
            cp.wait_recv()
        for cp in copies:
            cp.wait_send()
        for cp in mine:
            cp.wait()

    return _Exchange(xs, [jax.ShapeDtypeStruct(x.shape, x.dtype) for x in xs], start, finish)


def _run_exchange(ex, *, name):
    n = len(ex.inputs)

    def body(*refs):
        ex.start(refs[:n], refs[n:2 * n], refs[2 * n:])
        ex.finish(refs[:n], refs[n:2 * n], refs[2 * n:])

    hbm = pl.BlockSpec(memory_space=pl.ANY)
    return pl.pallas_call(body, name=name, out_shape=tuple(ex.out_shapes), in_specs=[hbm] * n,
                          out_specs=tuple([hbm] * n), scratch_shapes=ex.scratch_shapes)(*ex.inputs)


def _pack(arrays):
    flat = jnp.concatenate([a.reshape(-1) for a in arrays])
    n = flat.shape[0]
    tile = PACK_W * PACK_ROW_ALIGN
    total = -(-n // tile) * tile
    return jnp.pad(flat, (0, total - n)).reshape(total // PACK_W, PACK_W)


def _unpack(buf, shapes):
    lead = buf.shape[:-2]
    flat = buf.reshape(lead + (-1,))
    out, off = [], 0
    for shp in shapes:
        n = 1
        for dim in shp:
            n *= dim
        out.append(flat[..., off:off + n].reshape(lead + tuple(shp)))
        off += n
    return out


def _join_columns(blocks):
    return jnp.moveaxis(blocks, 0, 2).reshape(blocks.shape[1], blocks.shape[2], -1)


def _mm_hosting(a, b, exchange, **kw):
    if exchange is None:
        return _mm(a, b, **kw), []
    return _mm(a, b, exchange=exchange, **kw)


GATHER_HOSTS = {"mm_proj": ['w_in', 'w_out'], "mm_up": ['ffn_up', 'mem_wq', 'mem_wk'],
                "mm_down": ['ffn_down', 'mem_wv', 'mem_wo']}


def _layer_fwd(x, xb, memb, w, alpha, next_shards):
    cc = w['conv_w'].shape[1]
    n_pairs = (w['w_out'].shape[0] - cc) // LANES
    gathered = {}

    def host(a, b, name, **kw):
        names = GATHER_HOSTS[name]
        ex = _gather_exchange([next_shards[n] for n in names]) if next_shards else None
        out, got = _mm_hosting(a, b, ex, name=name, **kw)
        gathered.update({n: f.reshape(-1, f.shape[-1]) for n, f in zip(names, got)})
        return out

    proj = host(xb, w['w_in'], "mm_proj", tb=True)
    u1 = _glu_conv_fwd(proj, w['conv_w'], w['conv_b'], cc=cc, name="glu_conv_fwd")
    u = _cln_silu_fwd(u1, w['conv_ln_g'], w['conv_ln_b'], name="cln_silu_fwd")
    att, t_sum, n_walked = _sb_fwd(proj, col0=2 * cc, n_pairs=n_pairs, name="sb_fwd")
    ua = jnp.concatenate([u, att], axis=1)
    mix = _mm(ua, w['w_out'], name="mm_mix")
    r1, x1, x1b = _res_ln_fwd(x, mix, w['ln1_g'], w['ln1_b'], alpha=alpha, name="res_ln_fwd")
    qm = _mm(x1b, w['mem_wq'], name="mm_memq")
    km = _mm(memb, w['mem_wk'], name="mm_memkv")
    vm = _mm(memb, w['mem_wv'], name="mm_memkv")
    o = _mem_attn_fwd(qm, km, vm, name="mem_attn_fwd")
    cross = _mm(o, w['mem_wo'], name="mm_mix")
    r2, x2, x2b = _res_ln_fwd(x1, cross, w['ln2_g'], w['ln2_b'], alpha=alpha, name="res_ln_fwd")
    up = host(x2b, w['ffn_up'], "mm_up", tb=True)
    act = _ffn_act_fwd(up, w['ffn_conv_w'], w['ffn_conv_b'], name="ffn_act_fwd")
    ffn = host(act, w['ffn_down'], "mm_down")
    r3, x3, x3b = _res_ln_fwd(x2, ffn, w['ln3_g'], w['ln3_b'], alpha=alpha, name="res_ln_fwd")
    saved = dict(xb=xb, proj=proj, u1=u1, t_sum=t_sum, n_walked=n_walked, ua=ua, r1=r1, x1b=x1b, qm=qm, km=km, vm=vm,
                 o=o, r2=r2, x2b=x2b, up=up, act=act, r3=r3)
    return x3, x3b, saved, gathered


def _layer_bwd(da, dres, sv, memb, w, alpha, carried):
    g, received = {}, {}
    cc = w['conv_w'].shape[1]
    n_pairs = (w['w_out'].shape[0] - cc) // LANES

    def host(a, b, sends, **kw):
        ex = _scatter_exchange([gm.reshape(N_DEV, -1, gm.shape[-1]) for _, gm in sends]) if sends else None
        out, got = _mm_hosting(a, b, ex, **kw)
        received.update({key: blocks for (key, _), blocks in zip(sends, got)})
        return out

    dr3, dr3b, g['ln3_g'], g['ln3_b'] = _ln_bwd(da, dres, sv['r3'], w['ln3_g'], alpha=alpha, name="ln_bwd")
    g_down = host(sv['act'], dr3b, carried, ta=True, out_dtype=BF16, name="mm_dw_down")
    dact = host(dr3b, w['ffn_down'], [('ffn_down', g_down)], tb=True, name="mm_dact")
    dup, g['ffn_conv_w'], g['ffn_conv_b'] = _ffn_act_bwd(sv['up'], dact, w['ffn_conv_w'], w['ffn_conv_b'], name="ffn_act_bwd")
    g_up = _mm(dup, sv['x2b'], ta=True, out_dtype=BF16, name="mm_dw_up")
    da2 = host(dup, w['ffn_up'], [('ffn_up', g_up)], name="mm_dx_up")
    dr2, dr2b, g['ln2_g'], g['ln2_b'] = _ln_bwd(da2, dr3, sv['r2'], w['ln2_g'], alpha=alpha, name="ln_bwd")
    g_wo = _mm(sv['o'], dr2b, ta=True, out_dtype=BF16, name="mm_dw_sq")
    do = _mm(dr2b, w['mem_wo'], tb=True, name="mm_dx_sq")
    dqm, dkm, dvm = _mem_attn_bwd(sv['qm'], sv['km'], sv['vm'], do, name="mem_attn_bwd")
    g_wq = _mm(sv['x1b'], dqm, ta=True, out_dtype=BF16, name="mm_dw_sq")
    g_wk = _mm(memb, dkm, ta=True, out_dtype=BF16, name="mm_dw_memkv")
    g_wv = _mm(memb, dvm, ta=True, out_dtype=BF16, name="mm_dw_memkv")
    da1 = _mm(dqm, w['mem_wq'], tb=True, name="mm_dx_sq")
    dr1, dr1b, g['ln1_g'], g['ln1_b'] = _ln_bwd(da1, dr2, sv['r1'], w['ln1_g'], alpha=alpha, name="ln_bwd")
    g_out = _mm(sv['ua'], dr1b, ta=True, out_dtype=BF16, name="mm_dw_sq")
    dua = _mm(dr1b, w['w_out'], tb=True, name="mm_dx_sq")
    du1, g['conv_ln_g'], g['conv_ln_b'] = _cln_silu_bwd(dua, sv['u1'], w['conv_ln_g'], w['conv_ln_b'], name="cln_silu_bwd")
    dga, dgg, g['conv_w'], g['conv_b'] = _glu_conv_bwd(du1, sv['proj'], w['conv_w'], cc=cc, name="glu_conv_bwd")
    dq, dk, dv = _sb_bwd(sv['proj'], sv['t_sum'], sv['n_walked'], dua, col0=2 * cc, n_pairs=n_pairs, do_col0=cc, name="sb_bwd")
    dproj = jnp.concatenate([dga, dgg, dq.astype(BF16), dk.astype(BF16), dv.astype(BF16)], axis=1)
    g_in = _mm(dproj, sv['xb'], ta=True, out_dtype=BF16, name="mm_dw_in")
    da0 = host(dproj, w['w_in'], [('mem_wo', g_wo), ('mem_wq', g_wq), ('mem_wk', g_wk), ('mem_wv', g_wv)], name="mm_dx_in")
    return da0, dr1, g, received, [('w_in', g_in), ('w_out', g_out)]


def kernel(x, mem, w_in, conv_w, conv_b, conv_ln_g, conv_ln_b, w_out, ln1_g, ln1_b, mem_wq, mem_wk, mem_wv, mem_wo, ln2_g, ln2_b, ffn_up, ffn_conv_w, ffn_conv_b, ffn_down, ln3_g, ln3_b, loss_target, m_w_in, m_conv_w, m_conv_b, m_conv_ln_g, m_conv_ln_b, m_w_out, m_ln1_g, m_ln1_b, m_mem_wq, m_mem_wk, m_mem_wv, m_mem_wo, m_ln2_g, m_ln2_b, m_ffn_up, m_ffn_conv_w, m_ffn_conv_b, m_ffn_down, m_ln3_g, m_ln3_b, v_w_in, v_conv_w, v_conv_b, v_conv_ln_g, v_conv_ln_b, v_w_out, v_ln1_g, v_ln1_b, v_mem_wq, v_mem_wk, v_mem_wv, v_mem_wo, v_ln2_g, v_ln2_b, v_ffn_up, v_ffn_conv_w, v_ffn_conv_b, v_ffn_down, v_ln3_g, v_ln3_b):
    wts = dict(zip(WEIGHTS, (w_in, conv_w, conv_b, conv_ln_g, conv_ln_b, w_out, ln1_g, ln1_b, mem_wq, mem_wk, mem_wv,
                             mem_wo, ln2_g, ln2_b, ffn_up, ffn_conv_w, ffn_conv_b, ffn_down, ln3_g, ln3_b)))
    mom = dict(zip(WEIGHTS, (m_w_in, m_conv_w, m_conv_b, m_conv_ln_g, m_conv_ln_b, m_w_out, m_ln1_g, m_ln1_b, m_mem_wq,
                             m_mem_wk, m_mem_wv, m_mem_wo, m_ln2_g, m_ln2_b, m_ffn_up, m_ffn_conv_w, m_ffn_conv_b,
                             m_ffn_down, m_ln3_g, m_ln3_b)))
    var = dict(zip(WEIGHTS, (v_w_in, v_conv_w, v_conv_b, v_conv_ln_g, v_conv_ln_b, v_w_out, v_ln1_g, v_ln1_b, v_mem_wq,
                             v_mem_wk, v_mem_wv, v_mem_wo, v_ln2_g, v_ln2_b, v_ffn_up, v_ffn_conv_w, v_ffn_conv_b,
                             v_ffn_down, v_ln3_g, v_ln3_b)))
    depth = w_in.shape[0]
    alpha = (2.0 * depth) ** 0.25
    my_index = _dev_index(_mesh_pos())

    def row_shard(src, n, col_sharded, l):
        return src[n][l].T if col_sharded else src[n][l]

    tap_shapes = [wts[n].shape for n in TAPS]
    gathered_taps, = _run_exchange(_gather_exchange([_pack([wts[n] for n in TAPS])]), name="gather_taps")
    full_taps = {n: _join_columns(t) for n, t in zip(TAPS, _unpack(gathered_taps, tap_shapes))}

    def matrix_shards(l):
        return {n: row_shard(wts, n, cs, l).astype(BF16) for n, cs in MATRICES}

    def layer_weights(l, matrices):
        w = dict(matrices)
        w.update({n: full_taps[n][l] for n in TAPS})
        w.update({n: wts[n][l] for n in REPLICATED})
        return w

    first = matrix_shards(0)
    got = _run_exchange(_gather_exchange([first[n] for n, _ in MATRICES]), name="gather_weights")
    matrices = {n: f.reshape(-1, f.shape[-1]) for (n, _), f in zip(MATRICES, got)}

    xs = x[0]
    memb = mem[0].astype(BF16)
    h, hb = xs, xs.astype(BF16)
    saved, weights = [], []
    for l in range(depth):
        weights.append(layer_weights(l, matrices))
        h, hb, sv, matrices = _layer_fwd(h, hb, memb, weights[l], alpha, matrix_shards(l + 1) if l + 1 < depth else None)
        saved.append(sv)

    dy, loss_row = _loss_and_grad(h, loss_target[0], name="loss")
    loss = lax.psum(_row_sum(loss_row, name="loss_sum")[0, 0], ("x", "y", "c"))

    results = [{n: [None] * depth for n in WEIGHTS} for _ in range(4)]
    col_sharded = dict(MATRICES)

    def update(n, l, parts):
        cs = col_sharded[n]
        res = _sum_adamw(parts, *[row_shard(src, n, cs, l) for src in (wts, mom, var)], name="adamw_matrix")
        for kind in range(4):
            results[kind][n][l] = res[kind].T if cs else res[kind]

    da, dres = dy, None
    grads = [None] * depth
    carried = []
    for l in reversed(range(depth)):
        da, dres, grads[l], received, left = _layer_bwd(da, dres, saved[l], memb, weights[l], alpha, carried)
        for n, parts in received.items():
            update(n, l + 1 if n in dict(carried) else l, parts)
        carried = left
    last = _run_exchange(_scatter_exchange([gm.reshape(N_DEV, -1, gm.shape[-1]) for _, gm in carried]), name="scatter_grads")
    for (n, _), parts in zip(carried, last):
        update(n, 0, parts)
    grad_x = _axpy(da, dres, alpha=alpha, name="grad_x")[None]

    small = REPLICATED + TAPS
    part = _pack([jnp.stack([grads[l][n] for l in range(depth)]) for n in small])
    parts, = _run_exchange(_gather_exchange([part]), name="gather_small_grads")
    total = _sum_parts(parts, name="sum_small_grads")
    summed = dict(zip(small, _unpack(total, [wts[n].shape for n in REPLICATED] + [full_taps[n].shape for n in TAPS])))
    for n in TAPS:
        cols = wts[n].shape[-1]
        summed[n] = lax.dynamic_slice_in_dim(summed[n], my_index * cols, cols, axis=2)
    res = _sum_adamw(_pack([summed[n] for n in small])[None], *[_pack([src[n] for n in small]) for src in (wts, mom, var)],
                     name="adamw_small")
    for kind in range(4):
        for n, val in zip(small, _unpack(res[kind], [wts[n].shape for n in small])):
            results[kind][n] = val

    outs = [loss, grad_x]
    for kind in range(4):
        outs += [results[kind][n] if n in small else jnp.stack(results[kind][n]) for n in WEIGHTS]
    return tuple(outs)
```

```python
import functools

import jax
import jax.numpy as jnp
from jax import lax
from jax.experimental import pallas as pl
from jax.experimental.pallas import tpu as pltpu

F32 = jnp.float32
BF16 = jnp.bfloat16

N_DEV = 8
LANES = 128
PACK_W = 1024
PACK_ROW_ALIGN = 16
SB_HEAD_DIM = 64
MEM_HEAD_DIM = 256
LN_EPS = 1e-5
EXP_UNDERFLOW = 104.0
VMEM_LIMIT = 56 * 1024 * 1024

ADAM_LR = 0.001
ADAM_B1 = 0.9
ADAM_B2 = 0.999
ADAM_EPS = 1e-08
ADAM_WD = 0.01
ADAM_STEP = 10

IN_NAMES = ['x', 'mem', 'w_in', 'conv_w', 'conv_b', 'conv_ln_g', 'conv_ln_b', 'w_out', 'ln1_g', 'ln1_b',
            'mem_wq', 'mem_wk', 'mem_wv', 'mem_wo', 'ln2_g', 'ln2_b', 'ffn_up', 'ffn_conv_w', 'ffn_conv_b',
            'ffn_down', 'ln3_g', 'ln3_b']
WEIGHTS = IN_NAMES[2:]
MATRICES = [('w_in', True), ('w_out', False), ('mem_wq', False), ('mem_wk', False), ('mem_wv', False),
            ('mem_wo', False), ('ffn_up', True), ('ffn_down', False)]
TAPS = ['conv_w', 'ffn_conv_w']
REPLICATED = ['conv_b', 'conv_ln_g', 'conv_ln_b', 'ln1_g', 'ln1_b', 'ln2_g', 'ln2_b', 'ffn_conv_b', 'ln3_g', 'ln3_b']


def _pick(dim, pref, align=LANES):
    if dim <= pref:
        return dim
    fits = [t for t in range(align, pref + 1, align) if dim % t == 0]
    return fits[-1] if fits else dim


def _params(sem):
    return pltpu.CompilerParams(dimension_semantics=sem, vmem_limit_bytes=VMEM_LIMIT)


def _mm(a, b, *, ta=False, tb=False, out_dtype=F32, name, exchange=None):
    if ta:
        kdim, m = a.shape
        tm, tn, tk = _pick(m, 1408), _pick(b.shape[0 if tb else 1], 1024), _pick(kdim, 1024)
    else:
        m, kdim = a.shape
        tm, tn, tk = _pick(m, 512), _pick(b.shape[0 if tb else 1], 1536), _pick(kdim, 2816)
    if tb:
        n, kb = b.shape
    else:
        kb, n = b.shape
    assert kdim == kb, (a.shape, b.shape, ta, tb)
    grid = (m // tm, n // tn, kdim // tk)
    nk = grid[2]
    dims = (((0 if ta else 1,), (1 if tb else 0,)), ((), ()))
    n_ex_in = len(exchange.inputs) if exchange else 0
    n_ex_out = len(exchange.out_shapes) if exchange else 0

    def body(*refs):
        a_ref, b_ref = refs[:2]
        ex_in = refs[2:2 + n_ex_in]
        o_ref = refs[2 + n_ex_in]
        ex_out = refs[3 + n_ex_in:3 + n_ex_in + n_ex_out]
        scratch = refs[3 + n_ex_in + n_ex_out:]
        if nk > 1:
            acc_ref, scratch = scratch[0], scratch[1:]
        ids = [pl.program_id(d) for d in range(3)]
        if exchange:
            @pl.when((ids[0] == 0) & (ids[1] == 0) & (ids[2] == 0))
            def _():
                exchange.start(ex_in, ex_out, scratch)

        prod = lax.dot_general(a_ref[...].astype(BF16), b_ref[...].astype(BF16), dims,
                               preferred_element_type=F32)
        if nk == 1:
            o_ref[...] = prod.astype(out_dtype)
        else:
            k = ids[2]

            @pl.when(k == 0)
            def _():
                acc_ref[...] = prod

            @pl.when(k > 0)
            def _():
                acc_ref[...] += prod

            @pl.when(k == nk - 1)
            def _():
                o_ref[...] = acc_ref[...].astype(out_dtype)

        if exchange:
            @pl.when((ids[0] == grid[0] - 1) & (ids[1] == grid[1] - 1) & (ids[2] == grid[2] - 1))
            def _():
                exchange.finish(ex_in, ex_out, scratch)

    a_spec = pl.BlockSpec((tk, tm), lambda i, j, k: (k, i)) if ta else pl.BlockSpec((tm, tk), lambda i, j, k: (i, k))
    b_spec = pl.BlockSpec((tn, tk), lambda i, j, k: (j, k)) if tb else pl.BlockSpec((tk, tn), lambda i, j, k: (k, j))
    hbm = pl.BlockSpec(memory_space=pl.ANY)
    outs = pl.pallas_call(
        body, name=name,
        out_shape=(jax.ShapeDtypeStruct((m, n), out_dtype),) + tuple(exchange.out_shapes if exchange else ()),
        grid=grid,
        in_specs=[a_spec, b_spec] + [hbm] * n_ex_in,
        out_specs=(pl.BlockSpec((tm, tn), lambda i, j, k: (i, j)),) + (hbm,) * n_ex_out,
        scratch_shapes=([] if nk == 1 else [pltpu.VMEM((tm, tn), F32)]) + list(exchange.scratch_shapes if exchange else []),
        compiler_params=_params(("arbitrary",) * 3 if exchange else ("parallel", "parallel", "arbitrary")),
    )(a, b, *(exchange.inputs if exchange else ()))
    return (outs[0], list(outs[1:])) if exchange else outs[0]


def _ln_stats(r):
    mu = jnp.mean(r, axis=-1, keepdims=True)
    xc = r - mu
    var = jnp.mean(xc * xc, axis=-1, keepdims=True)
    rstd = lax.rsqrt(var + LN_EPS)
    return xc * rstd, rstd


def _res_ln_fwd(x, f, g, b, *, alpha, name, ts=512):
    s, d = x.shape
    ts = _pick(s, ts)

    def body(x_ref, f_ref, g_ref, b_ref, r_ref, y_ref, yb_ref):
        r = alpha * x_ref[...] + f_ref[...]
        xhat, _ = _ln_stats(r)
        y = xhat * g_ref[...] + b_ref[...]
        r_ref[...] = r
        y_ref[...] = y
        yb_ref[...] = y.astype(BF16)

    tok = pl.BlockSpec((ts, d), lambda i: (i, 0))
    vec = pl.BlockSpec((1, d), lambda i: (0, 0))
    return pl.pallas_call(
        body, name=name,
        out_shape=(jax.ShapeDtypeStruct((s, d), F32), jax.ShapeDtypeStruct((s, d), F32),
                   jax.ShapeDtypeStruct((s, d), BF16)),
        grid=(s // ts,), in_specs=[tok, tok, vec, vec], out_specs=(tok, tok, tok),
        compiler_params=_params(("parallel",)),
    )(x, f, g.reshape(1, d), b.reshape(1, d))


def _ln_bwd(da, dres, r, g, *, alpha, name, ts=512):
    s, d = r.shape
    ts = _pick(s, ts)
    has_res = dres is not None

    def body(*refs):
        if has_res:
            da_ref, dres_ref, r_ref, g_ref, dr_ref, drb_ref, dg_ref, db_ref = refs
            dy = da_ref[...] + alpha * dres_ref[...]
        else:
            da_ref, r_ref, g_ref, dr_ref, drb_ref, dg_ref, db_ref = refs
            dy = da_ref[...]
        xhat, rstd = _ln_stats(r_ref[...])
        dxhat = dy * g_ref[...]
        m1 = jnp.mean(dxhat, axis=-1, keepdims=True)
        m2 = jnp.mean(dxhat * xhat, axis=-1, keepdims=True)
        dr = rstd * (dxhat - m1 - xhat * m2)
        dr_ref[...] = dr
        drb_ref[...] = dr.astype(BF16)

        @pl.when(pl.program_id(0) == 0)
        def _():
            dg_ref[...] = jnp.zeros_like(dg_ref)
            db_ref[...] = jnp.zeros_like(db_ref)

        dg_ref[...] += jnp.sum(dy * xhat, axis=0, keepdims=True)
        db_ref[...] += jnp.sum(dy, axis=0, keepdims=True)

    tok = pl.BlockSpec((ts, d), lambda i: (i, 0))
    vec = pl.BlockSpec((1, d), lambda i: (0, 0))
    ins = [da, dres, r, g.reshape(1, d)] if has_res else [da, r, g.reshape(1, d)]
    dr, drb, dg, db = pl.pallas_call(
        body, name=name,
        out_shape=(jax.ShapeDtypeStruct((s, d), F32), jax.ShapeDtypeStruct((s, d), BF16),
                   jax.ShapeDtypeStruct((1, d), F32), jax.ShapeDtypeStruct((1, d), F32)),
        grid=(s // ts,), in_specs=[tok] * (len(ins) - 1) + [vec], out_specs=(tok, tok, vec, vec),
        compiler_params=_params(("arbitrary",)),
    )(*ins)
    return dr, drb, dg.reshape(d), db.reshape(d)


def _axpy(a, b, *, alpha, name, ts=512):
    s, d = a.shape
    ts = _pick(s, ts)

    def body(a_ref, b_ref, o_ref):
        o_ref[...] = a_ref[...] + alpha * b_ref[...]

    tok = pl.BlockSpec((ts, d), lambda i: (i, 0))
    return pl.pallas_call(body, name=name, out_shape=jax.ShapeDtypeStruct((s, d), F32), grid=(s // ts,),
                          in_specs=[tok, tok], out_specs=tok, compiler_params=_params(("parallel",)))(a, b)


def _loss_and_grad(y, target, *, name, ts=512):
    s, d = y.shape
    ts = _pick(s, ts)
    inv_d = 1.0 / d

    def body(y_ref, t_ref, dy_ref, loss_ref):
        e = y_ref[...] - t_ref[...]
        dy_ref[...] = e * inv_d

        @pl.when(pl.program_id(0) == 0)
        def _():
            loss_ref[...] = jnp.zeros_like(loss_ref)

        loss_ref[...] += jnp.sum(e * e, axis=0, keepdims=True) * (0.5 * inv_d)

    tok = pl.BlockSpec((ts, d), lambda i: (i, 0))
    vec = pl.BlockSpec((1, d), lambda i: (0, 0))
    dy, part = pl.pallas_call(
        body, name=name,
        out_shape=(jax.ShapeDtypeStruct((s, d), F32), jax.ShapeDtypeStruct((1, d), F32)),
        grid=(s // ts,), in_specs=[tok, tok], out_specs=(tok, vec),
        compiler_params=_params(("arbitrary",)),
    )(y, target)
    return dy, part


def _row_sum(v, *, name):
    def body(v_ref, o_ref):
        o_ref[...] = jnp.sum(v_ref[...], axis=1, keepdims=True)

    return pl.pallas_call(body, name=name, out_shape=jax.ShapeDtypeStruct((1, 1), F32))(v)


def _sigmoid(x):
    return 1.0 / (1.0 + jnp.exp(-x))


def _row_chunks(s, pref=512):
    c = _pick(s, pref, 8)
    return [(i * c, c) for i in range(s // c)]


def _glu_conv_fwd(proj, w, b, *, cc, name):
    s = proj.shape[0]
    kw = w.shape[0]
    pad = 32
    assert kw - 1 <= pad
    ncb = cc // LANES
    chunks = _row_chunks(s)

    def body(a_ref, g_ref, w_ref, b_ref, o_ref, u0_ref):
        u0_ref[pl.ds(0, pad), :] = jnp.zeros((pad, LANES), F32)
        for r0, rc in chunks:
            u0_ref[pl.ds(pad + r0, rc), :] = a_ref[pl.ds(r0, rc), :] * _sigmoid(g_ref[pl.ds(r0, rc), :])
        for r0, rc in chunks:
            acc = jnp.zeros((rc, LANES), F32) + b_ref[...]
            for k in range(kw):
                acc = acc + w_ref[pl.ds(k, 1), :] * u0_ref[pl.ds(pad + r0 - (kw - 1) + k, rc), :]
            o_ref[pl.ds(r0, rc), :] = acc

    return pl.pallas_call(
        body, name=name,
        out_shape=jax.ShapeDtypeStruct((s, cc), F32),
        grid=(ncb,),
        in_specs=[pl.BlockSpec((s, LANES), lambda c: (0, c)), pl.BlockSpec((s, LANES), lambda c: (0, ncb + c)),
                  pl.BlockSpec((kw, LANES), lambda c: (0, c)), pl.BlockSpec((1, LANES), lambda c: (0, c))],
        out_specs=pl.BlockSpec((s, LANES), lambda c: (0, c)),
        scratch_shapes=[pltpu.VMEM((s + pad, LANES), F32)],
        compiler_params=_params(("parallel",)),
    )(proj, proj, w, b.reshape(1, cc))


def _glu_conv_bwd(du1, proj, w, *, cc, name):
    s = proj.shape[0]
    kw = w.shape[0]
    pad = 32
    ncb = cc // LANES
    chunks = _row_chunks(s)

    def body(d_ref, a_ref, g_ref, w_ref, da_ref, dg_ref, dw_ref, db_ref, u0_ref, dp_ref):
        u0_ref[pl.ds(0, pad), :] = jnp.zeros((pad, LANES), F32)
        dp_ref[pl.ds(s, pad), :] = jnp.zeros((pad, LANES), F32)
        for r0, rc in chunks:
            u0_ref[pl.ds(pad + r0, rc), :] = a_ref[pl.ds(r0, rc), :] * _sigmoid(g_ref[pl.ds(r0, rc), :])
            dp_ref[pl.ds(r0, rc), :] = d_ref[pl.ds(r0, rc), :]
        dws = [jnp.zeros((1, LANES), F32) for _ in range(kw)]
        dbs = jnp.zeros((1, LANES), F32)
        for r0, rc in chunks:
            d = dp_ref[pl.ds(r0, rc), :]
            dbs = dbs + jnp.sum(d, axis=0, keepdims=True)
            du0 = jnp.zeros((rc, LANES), F32)
            for k in range(kw):
                du0 = du0 + w_ref[pl.ds(k, 1), :] * dp_ref[pl.ds(r0 + (kw - 1) - k, rc), :]
                dws[k] = dws[k] + jnp.sum(d * u0_ref[pl.ds(pad + r0 - (kw - 1) + k, rc), :], axis=0, keepdims=True)
            sg = _sigmoid(g_ref[pl.ds(r0, rc), :])
            a = a_ref[pl.ds(r0, rc), :]
            da_ref[pl.ds(r0, rc), :] = (du0 * sg).astype(BF16)
            dg_ref[pl.ds(r0, rc), :] = (du0 * a * sg * (1.0 - sg)).astype(BF16)
        for k in range(kw):
            dw_ref[pl.ds(k, 1), :] = dws[k]
        db_ref[...] = dbs

    col = lambda off: pl.BlockSpec((s, LANES), lambda c: (0, off + c))
    da, dg, dw, db = pl.pallas_call(
        body, name=name,
        out_shape=(jax.ShapeDtypeStruct((s, cc), BF16), jax.ShapeDtypeStruct((s, cc), BF16),
                   jax.ShapeDtypeStruct((kw, cc), F32), jax.ShapeDtypeStruct((1, cc), F32)),
        grid=(ncb,),
        in_specs=[col(0), col(0), col(ncb), pl.BlockSpec((kw, LANES), lambda c: (0, c))],
        out_specs=(col(0), col(0), pl.BlockSpec((kw, LANES), lambda c: (0, c)), pl.BlockSpec((1, LANES), lambda c: (0, c))),
        scratch_shapes=[pltpu.VMEM((s + pad, LANES), F32), pltpu.VMEM((s + pad, LANES), F32)],
        compiler_params=_params(("parallel",)),
    )(du1, proj, proj, w)
    return da, dg, dw, db.reshape(cc)


def _cln_silu_fwd(u1, g, b, *, name, ts=512):
    s, cc = u1.shape
    ts = _pick(s, ts)

    def body(u_ref, g_ref, b_ref, o_ref):
        xhat, _ = _ln_stats(u_ref[...])
        y = xhat * g_ref[...] + b_ref[...]
        o_ref[...] = (y * _sigmoid(y)).astype(BF16)

    tok = pl.BlockSpec((ts, cc), lambda i: (i, 0))
    vec = pl.BlockSpec((1, cc), lambda i: (0, 0))
    return pl.pallas_call(body, name=name, out_shape=jax.ShapeDtypeStruct((s, cc), BF16), grid=(s // ts,),
                          in_specs=[tok, vec, vec], out_specs=tok,
                          compiler_params=_params(("parallel",)))(u1, g.reshape(1, cc), b.reshape(1, cc))


def _cln_silu_bwd(dua, u1, g, b, *, name, ts=512):
    s, cc = u1.shape
    ts = _pick(s, ts)

    def body(d_ref, u_ref, g_ref, b_ref, du_ref, dg_ref, db_ref):
        xhat, rstd = _ln_stats(u_ref[...])
        y = xhat * g_ref[...] + b_ref[...]
        sg = _sigmoid(y)
        dy = d_ref[...] * (sg * (1.0 + y * (1.0 - sg)))
        dxhat = dy * g_ref[...]
        m1 = jnp.mean(dxhat, axis=-1, keepdims=True)
        m2 = jnp.mean(dxhat * xhat, axis=-1, keepdims=True)
        du_ref[...] = rstd * (dxhat - m1 - xhat * m2)

        @pl.when(pl.program_id(0) == 0)
        def _():
            dg_ref[...] = jnp.zeros_like(dg_ref)
            db_ref[...] = jnp.zeros_like(db_ref)

        dg_ref[...] += jnp.sum(dy * xhat, axis=0, keepdims=True)
        db_ref[...] += jnp.sum(dy, axis=0, keepdims=True)

    tok = pl.BlockSpec((ts, cc), lambda i: (i, 0))
    vec = pl.BlockSpec((1, cc), lambda i: (0, 0))
    du1, dg, db = pl.pallas_call(
        body, name=name,
        out_shape=(jax.ShapeDtypeStruct((s, cc), F32), jax.ShapeDtypeStruct((1, cc), F32),
                   jax.ShapeDtypeStruct((1, cc), F32)),
        grid=(s // ts,), in_specs=[tok, tok, vec, vec], out_specs=(tok, vec, vec),
        compiler_params=_params(("arbitrary",)),
    )(dua, u1, g.reshape(1, cc), b.reshape(1, cc))
    return du1, dg.reshape(cc), db.reshape(cc)


def _softplus_parts(z):
    sp = jnp.log1p(jnp.exp(-jnp.abs(z)))
    return jnp.minimum(-z, 0.0) - sp, jnp.minimum(z, 0.0) - sp


def _split_dot(x, m):
    hi = x.astype(BF16)
    lo = (x - hi.astype(F32)).astype(BF16)
    return jnp.dot(hi, m, preferred_element_type=F32) + jnp.dot(lo, m, preferred_element_type=F32)


_NT = (((1,), (1,)), ((), ()))
_TN = (((0,), (0,)), ((), ()))


def _head_masks():
    lane = lax.broadcasted_iota(jnp.int32, (1, LANES), 1)
    return [(lane >= SB_HEAD_DIM * h) & (lane < SB_HEAD_DIM * (h + 1)) for h in range(2)]


def _hosted(exchange, grid, body):
    if exchange is None:
        return body, [], [], []
    n_in, n_out = len(exchange.inputs), len(exchange.out_shapes)

    def wrapped(*refs, n_own_in, n_own_out):
        own_in, ex_in = refs[:n_own_in], refs[n_own_in:n_own_in + n_in]
        rest = refs[n_own_in + n_in:]
        own_out, ex_out, sems = rest[:n_own_out], rest[n_own_out:n_own_out + n_out], rest[n_own_out + n_out:]
        ids = [pl.program_id(d) for d in range(len(grid))]
        first = functools.reduce(lambda x, y: x & y, [i == 0 for i in ids])
        last = functools.reduce(lambda x, y: x & y, [i == g - 1 for i, g in zip(ids, grid)])

        @pl.when(first)
        def _():
            exchange.start(ex_in, ex_out, sems)

        body(*own_in, *own_out)

        @pl.when(last)
        def _():
            exchange.finish(ex_in, ex_out, sems)

    return wrapped, list(exchange.inputs), list(exchange.out_shapes), list(exchange.scratch_shapes)


def _sb_fwd(proj, *, col0, n_pairs, name, tq=256, exchange=None):
    s = proj.shape[0]
    tq = _pick(s, tq)
    nq = s // tq
    cb0 = col0 // LANES
    scale = SB_HEAD_DIM ** -0.5

    def body(q_ref, k_ref, v_ref, o_ref, t_ref, n_ref):
        i = pl.program_id(1)
        hms = _head_masks()
        row = lax.broadcasted_iota(jnp.int32, (tq, tq), 0)
        col = lax.broadcasted_iota(jnp.int32, (tq, tq), 1)
        vis = col < row
        m_after = (row > col).astype(BF16)
        q = q_ref[...] * scale
        qs = [jnp.where(hm, q, 0.0).astype(BF16) for hm in hms]

        def tile(j, cs, accs, masked):
            start = pl.multiple_of(j * tq, tq)
            kb = k_ref[pl.ds(start, tq), :].astype(BF16)
            v = v_ref[pl.ds(start, tq), :]
            out_c, out_acc = [], []
            for h in range(2):
                vb = jnp.where(hms[h], v, 0.0).astype(BF16)
                z = lax.dot_general(qs[h], kb, _NT, preferred_element_type=F32)
                lk, lb = _softplus_parts(z)
                if masked:
                    lk = jnp.where(vis, lk, 0.0)
                later = _split_dot(lk, m_after)
                a = jnp.exp(lb + later + cs[h])
                if masked:
                    a = jnp.where(vis, a, 0.0)
                out_acc.append(accs[h] + jnp.dot(a.astype(BF16), vb, preferred_element_type=F32))
                out_c.append(cs[h] + jnp.sum(lk, axis=1, keepdims=True))
            return out_c, out_acc

        def more(st):
            return jnp.logical_and(st[0] < i, jnp.maximum(jnp.max(st[1]), jnp.max(st[2])) > -EXP_UNDERFLOW)

        def step(st):
            cs, accs = tile(i - 1 - st[0], st[1:3], st[3:5], False)
            return (st[0] + 1, *cs, *accs)

        zero_c, zero_acc = jnp.zeros((tq, 1), F32), jnp.zeros((tq, LANES), F32)
        cs, accs = tile(i, [zero_c, zero_c], [zero_acc, zero_acc], True)
        n, c0, c1, acc0, acc1 = lax.while_loop(more, step, (jnp.int32(0), *cs, *accs))
        o_ref[...] = (acc0 + acc1).astype(BF16)
        t_ref[...] = jnp.where(hms[0], c0, c1)
        n_ref[...] = jnp.zeros((8, LANES), F32) + n.astype(F32)

    grid = (n_pairs, nq)
    body, ex_in, ex_out, ex_scratch = _hosted(exchange, grid, body)
    if exchange is not None:
        body = functools.partial(body, n_own_in=3, n_own_out=3)
    hbm = pl.BlockSpec(memory_space=pl.ANY)
    seq = lambda off: pl.BlockSpec((s, LANES), lambda p, i: (0, cb0 + off + p))
    outs = pl.pallas_call(
        body, name=name,
        out_shape=(jax.ShapeDtypeStruct((s, n_pairs * LANES), BF16), jax.ShapeDtypeStruct((n_pairs, s, LANES), F32),
                   jax.ShapeDtypeStruct((n_pairs, nq * 8, LANES), F32), *ex_out),
        grid=grid,
        in_specs=[pl.BlockSpec((tq, LANES), lambda p, i: (i, cb0 + p)), seq(n_pairs), seq(2 * n_pairs)] + [hbm] * len(ex_in),
        out_specs=(pl.BlockSpec((tq, LANES), lambda p, i: (i, p)), pl.BlockSpec((None, tq, LANES), lambda p, i: (p, i, 0)),
                   pl.BlockSpec((None, 8, LANES), lambda p, i: (p, i, 0)), *([hbm] * len(ex_out))),
        scratch_shapes=ex_scratch,
        compiler_params=_params(("arbitrary", "arbitrary") if exchange else ("parallel", "arbitrary")),
    )(proj, proj, proj, *ex_in)
    return outs[0], outs[1], outs[2], list(outs[3:])


def _sb_bwd(proj, t_sum, n_walked, dua, *, col0, n_pairs, do_col0, name, tq=256, exchange=None):
    s = proj.shape[0]
    tq = _pick(s, tq)
    cb0 = col0 // LANES
    dcb0 = do_col0 // LANES
    scale = SB_HEAD_DIM ** -0.5

    def body(q_ref, k_ref, v_ref, t_ref, n_ref, do_ref, dq_ref, dk_ref, dv_ref):
        i = pl.program_id(1)

        @pl.when(i == 0)
        def _():
            dk_ref[...] = jnp.zeros_like(dk_ref)
            dv_ref[...] = jnp.zeros_like(dv_ref)

        hms = _head_masks()
        row = lax.broadcasted_iota(jnp.int32, (tq, tq), 0)
        col = lax.broadcasted_iota(jnp.int32, (tq, tq), 1)
        vis = col < row
        m_upto = (row <= col).astype(BF16)
        m_before = (row < col).astype(BF16)
        q = q_ref[...] * scale
        do = do_ref[...]
        qs = [jnp.where(hm, q, 0.0).astype(BF16) for hm in hms]
        dos = [jnp.where(hm, do, 0.0).astype(BF16) for hm in hms]
        t_alls = [t_ref[:, SB_HEAD_DIM * h:SB_HEAD_DIM * h + 1] for h in range(2)]
        first = i - jnp.max(n_ref[...]).astype(jnp.int32)

        def tile(j, p_sums, g_sums, dqs, masked):
            start = pl.multiple_of(j * tq, tq)
            k = k_ref[pl.ds(start, tq), :]
            v = v_ref[pl.ds(start, tq), :]
            out_p, out_g, out_dq = [], [], []
            dk_add = dv_add = None
            for h in range(2):
                kb = jnp.where(hms[h], k, 0.0).astype(BF16)
                vb = jnp.where(hms[h], v, 0.0).astype(BF16)
                z = lax.dot_general(qs[h], kb, _NT, preferred_element_type=F32)
                lk_raw, lb = _softplus_parts(z)
                lk = jnp.where(vis, lk_raw, 0.0) if masked else lk_raw
                upto = _split_dot(lk, m_upto)
                a = jnp.exp(lb + (t_alls[h] - p_sums[h]) - upto)
                if masked:
                    a = jnp.where(vis, a, 0.0)
                da = lax.dot_general(dos[h], vb, _NT, preferred_element_type=F32)
                g = a * da
                dv_h = lax.dot_general(a.astype(BF16), dos[h], _TN, preferred_element_type=F32)
                g_before = g_sums[h] + _split_dot(g, m_before)
                dz = g * jnp.exp(lk_raw) - g_before * jnp.exp(lb)
                if masked:
                    dz = jnp.where(vis, dz, 0.0)
                dzb = dz.astype(BF16)
                dk_h = lax.dot_general(dzb, qs[h], _TN, preferred_element_type=F32)
                dk_add = dk_h if dk_add is None else dk_add + dk_h
                dv_add = dv_h if dv_add is None else dv_add + dv_h
                out_dq.append(dqs[h] + jnp.dot(dzb, kb, preferred_element_type=F32))
                out_p.append(p_sums[h] + jnp.sum(lk, axis=1, keepdims=True))
                out_g.append(g_sums[h] + jnp.sum(g, axis=1, keepdims=True))
            dk_ref[pl.ds(start, tq), :] += dk_add
            dv_ref[pl.ds(start, tq), :] += dv_add
            return out_p, out_g, out_dq

        def step(j, st):
            ps, gs, dqs = tile(j, st[0:2], st[2:4], st[4:6], False)
            return (*ps, *gs, *dqs)

        zero, zero_dq = jnp.zeros((tq, 1), F32), jnp.zeros((tq, LANES), F32)
        st = lax.fori_loop(first, i, step, (zero, zero, zero, zero, zero_dq, zero_dq))
        _, _, dqs = tile(i, st[0:2], st[2:4], st[4:6], True)
        dq_ref[...] = (dqs[0] + dqs[1]) * scale

    grid = (n_pairs, s // tq)
    body, ex_in, ex_out, ex_scratch = _hosted(exchange, grid, body)
    if exchange is not None:
        body = functools.partial(body, n_own_in=6, n_own_out=3)
    hbm = pl.BlockSpec(memory_space=pl.ANY)
    seq = lambda off: pl.BlockSpec((s, LANES), lambda p, i: (0, cb0 + off + p))
    out = jax.ShapeDtypeStruct((s, n_pairs * LANES), F32)
    res = pl.BlockSpec((s, LANES), lambda p, i: (0, p))
    outs = pl.pallas_call(
        body, name=name,
        out_shape=(out, out, out, *ex_out),
        grid=grid,
        in_specs=[pl.BlockSpec((tq, LANES), lambda p, i: (i, cb0 + p)), seq(n_pairs), seq(2 * n_pairs),
                  pl.BlockSpec((None, tq, LANES), lambda p, i: (p, i, 0)),
                  pl.BlockSpec((None, 8, LANES), lambda p, i: (p, i, 0)),
                  pl.BlockSpec((tq, LANES), lambda p, i: (i, dcb0 + p))] + [hbm] * len(ex_in),
        out_specs=(pl.BlockSpec((tq, LANES), lambda p, i: (i, p)), res, res, *([hbm] * len(ex_out))),
        scratch_shapes=ex_scratch,
        compiler_params=_params(("arbitrary", "arbitrary")),
    )(proj, proj, proj, t_sum, n_walked, dua, *ex_in)
    return outs[0], outs[1], outs[2], list(outs[3:])


def _mem_attn_fwd(qm, km, vm, *, name, tq=512):
    s, d = qm.shape
    heads = d // MEM_HEAD_DIM
    mlen = km.shape[0]
    tq = _pick(s, tq)
    scale = MEM_HEAD_DIM ** -0.5

    def body(q_ref, k_ref, v_ref, o_ref):
        for h in range(heads):
            sl = slice(h * MEM_HEAD_DIM, (h + 1) * MEM_HEAD_DIM)
            q = (q_ref[:, sl] * scale).astype(BF16)
            sc = lax.dot_general(q, k_ref[:, sl].astype(BF16), _NT, preferred_element_type=F32)
            e = jnp.exp(sc - jnp.max(sc, axis=1, keepdims=True))
            p = e / jnp.sum(e, axis=1, keepdims=True)
            o_ref[:, sl] = jnp.dot(p.astype(BF16), v_ref[:, sl].astype(BF16), preferred_element_type=F32).astype(BF16)

    tok = pl.BlockSpec((tq, d), lambda i: (i, 0))
    kv = pl.BlockSpec((mlen, d), lambda i: (0, 0))
    return pl.pallas_call(body, name=name, out_shape=jax.ShapeDtypeStruct((s, d), BF16), grid=(s // tq,),
                          in_specs=[tok, kv, kv], out_specs=tok, compiler_params=_params(("parallel",)))(qm, km, vm)


def _mem_attn_bwd(qm, km, vm, do, *, name, tq=512):
    s, d = qm.shape
    heads = d // MEM_HEAD_DIM
    mlen = km.shape[0]
    tq = _pick(s, tq)
    scale = MEM_HEAD_DIM ** -0.5

    def body(q_ref, k_ref, v_ref, do_ref, dq_ref, dk_ref, dv_ref):
        @pl.when(pl.program_id(0) == 0)
        def _():
            dk_ref[...] = jnp.zeros_like(dk_ref)
            dv_ref[...] = jnp.zeros_like(dv_ref)

        for h in range(heads):
            sl = slice(h * MEM_HEAD_DIM, (h + 1) * MEM_HEAD_DIM)
            q = (q_ref[:, sl] * scale).astype(BF16)
            k = k_ref[:, sl].astype(BF16)
            v = v_ref[:, sl].astype(BF16)
            sc = lax.dot_general(q, k, _NT, preferred_element_type=F32)
            e = jnp.exp(sc - jnp.max(sc, axis=1, keepdims=True))
            p = e / jnp.sum(e, axis=1, keepdims=True)
            dob = do_ref[:, sl].astype(BF16)
            dv_ref[:, sl] += lax.dot_general(p.astype(BF16), dob, _TN, preferred_element_type=F32)
            dp = lax.dot_general(dob, v, _NT, preferred_element_type=F32)
            ds = (p * (dp - jnp.sum(dp * p, axis=1, keepdims=True))).astype(BF16)
            dq_ref[:, sl] = (jnp.dot(ds, k, preferred_element_type=F32) * scale).astype(BF16)
            dk_ref[:, sl] += lax.dot_general(ds, q, _TN, preferred_element_type=F32)

    tok = pl.BlockSpec((tq, d), lambda i: (i, 0))
    kv = pl.BlockSpec((mlen, d), lambda i: (0, 0))
    return pl.pallas_call(
        body, name=name,
        out_shape=(jax.ShapeDtypeStruct((s, d), BF16), jax.ShapeDtypeStruct((mlen, d), F32),
                   jax.ShapeDtypeStruct((mlen, d), F32)),
        grid=(s // tq,), in_specs=[tok, kv, kv, tok], out_specs=(tok, kv, kv),
        compiler_params=_params(("arbitrary",)),
    )(qm, km, vm, do)


def _ffn_act_fwd(up, w, b, *, name):
    s, two_f = up.shape
    ff = two_f // 2
    nfb = ff // LANES
    kw = w.shape[0]
    pad = 8
    chunks = _row_chunks(s)

    def body(v_ref, g_ref, wv_ref, wg_ref, bv_ref, bg_ref, o_ref, vp_ref, gp_ref):
        vp_ref[pl.ds(0, pad), :] = jnp.zeros((pad, LANES), F32)
        gp_ref[pl.ds(0, pad), :] = jnp.zeros((pad, LANES), F32)
        for r0, rc in chunks:
            vp_ref[pl.ds(pad + r0, rc), :] = v_ref[pl.ds(r0, rc), :]
            gp_ref[pl.ds(pad + r0, rc), :] = g_ref[pl.ds(r0, rc), :]
        for r0, rc in chunks:
            vc = jnp.zeros((rc, LANES), F32) + bv_ref[...]
            gc = jnp.zeros((rc, LANES), F32) + bg_ref[...]
            for k in range(kw):
                off = pad + r0 - (kw - 1) + k
                vc = vc + wv_ref[pl.ds(k, 1), :] * vp_ref[pl.ds(off, rc), :]
                gc = gc + wg_ref[pl.ds(k, 1), :] * gp_ref[pl.ds(off, rc), :]
            o_ref[pl.ds(r0, rc), :] = (gc * _sigmoid(gc) * vc).astype(BF16)

    col = lambda off: pl.BlockSpec((s, LANES), lambda c: (0, off + c))
    tap = lambda off: pl.BlockSpec((kw, LANES), lambda c: (0, off + c))
    vec = lambda off: pl.BlockSpec((1, LANES), lambda c: (0, off + c))
    return pl.pallas_call(
        body, name=name, out_shape=jax.ShapeDtypeStruct((s, ff), BF16), grid=(nfb,),
        in_specs=[col(0), col(nfb), tap(0), tap(nfb), vec(0), vec(nfb)], out_specs=col(0),
        scratch_shapes=[pltpu.VMEM((s + pad, LANES), F32), pltpu.VMEM((s + pad, LANES), F32)],
        compiler_params=_params(("parallel",)),
    )(up, up, w, w, b.reshape(1, two_f), b.reshape(1, two_f))


def _ffn_act_bwd(up, dact, w, b, *, name):
    s, two_f = up.shape
    ff = two_f // 2
    nfb = ff // LANES
    kw = w.shape[0]
    pad = 8
    chunks = _row_chunks(s)

    def body(v_ref, g_ref, d_ref, wv_ref, wg_ref, bv_ref, bg_ref, dv_ref, dg_ref, dwv_ref, dwg_ref, dbv_ref, dbg_ref,
             vp_ref, gp_ref, dvc_ref, dgc_ref):
        vp_ref[pl.ds(0, pad), :] = jnp.zeros((pad, LANES), F32)
        gp_ref[pl.ds(0, pad), :] = jnp.zeros((pad, LANES), F32)
        dvc_ref[pl.ds(s, pad), :] = jnp.zeros((pad, LANES), F32)
        dgc_ref[pl.ds(s, pad), :] = jnp.zeros((pad, LANES), F32)
        for r0, rc in chunks:
            vp_ref[pl.ds(pad + r0, rc), :] = v_ref[pl.ds(r0, rc), :]
            gp_ref[pl.ds(pad + r0, rc), :] = g_ref[pl.ds(r0, rc), :]
        dwv = [jnp.zeros((1, LANES), F32) for _ in range(kw)]
        dwg = [jnp.zeros((1, LANES), F32) for _ in range(kw)]
        dbv = jnp.zeros((1, LANES), F32)
        dbg = jnp.zeros((1, LANES), F32)
        for r0, rc in chunks:
            vc = jnp.zeros((rc, LANES), F32) + bv_ref[...]
            gc = jnp.zeros((rc, LANES), F32) + bg_ref[...]
            for k in range(kw):
                off = pad + r0 - (kw - 1) + k
                vc = vc + wv_ref[pl.ds(k, 1), :] * vp_ref[pl.ds(off, rc), :]
                gc = gc + wg_ref[pl.ds(k, 1), :] * gp_ref[pl.ds(off, rc), :]
            sg = _sigmoid(gc)
            d = d_ref[pl.ds(r0, rc), :]
            dvc = d * (gc * sg)
            dgc = d * vc * (sg * (1.0 + gc * (1.0 - sg)))
            dvc_ref[pl.ds(r0, rc), :] = dvc
            dgc_ref[pl.ds(r0, rc), :] = dgc
            dbv = dbv + jnp.sum(dvc, axis=0, keepdims=True)
            dbg = dbg + jnp.sum(dgc, axis=0, keepdims=True)
            for k in range(kw):
                off = pad + r0 - (kw - 1) + k
                dwv[k] = dwv[k] + jnp.sum(dvc * vp_ref[pl.ds(off, rc), :], axis=0, keepdims=True)
                dwg[k] = dwg[k] + jnp.sum(dgc * gp_ref[pl.ds(off, rc), :], axis=0, keepdims=True)
        for r0, rc in chunks:
            dv = jnp.zeros((rc, LANES), F32)
            dg = jnp.zeros((rc, LANES), F32)
            for k in range(kw):
                off = r0 + (kw - 1) - k
                dv = dv + wv_ref[pl.ds(k, 1), :] * dvc_ref[pl.ds(off, rc), :]
                dg = dg + wg_ref[pl.ds(k, 1), :] * dgc_ref[pl.ds(off, rc), :]
            dv_ref[pl.ds(r0, rc), :] = dv.astype(BF16)
            dg_ref[pl.ds(r0, rc), :] = dg.astype(BF16)
        for k in range(kw):
            dwv_ref[pl.ds(k, 1), :] = dwv[k]
            dwg_ref[pl.ds(k, 1), :] = dwg[k]
        dbv_ref[...] = dbv
        dbg_ref[...] = dbg

    col = lambda off: pl.BlockSpec((s, LANES), lambda c: (0, off + c))
    tap = lambda off: pl.BlockSpec((kw, LANES), lambda c: (0, off + c))
    vec = lambda off: pl.BlockSpec((1, LANES), lambda c: (0, off + c))
    big = lambda: pltpu.VMEM((s + pad, LANES), F32)
    dv, dg, dwv, dwg, dbv, dbg = pl.pallas_call(
        body, name=name,
        out_shape=(jax.ShapeDtypeStruct((s, ff), BF16), jax.ShapeDtypeStruct((s, ff), BF16),
                   jax.ShapeDtypeStruct((kw, ff), F32), jax.ShapeDtypeStruct((kw, ff), F32),
                   jax.ShapeDtypeStruct((1, ff), F32), jax.ShapeDtypeStruct((1, ff), F32)),
        grid=(nfb,),
        in_specs=[col(0), col(nfb), col(0), tap(0), tap(nfb), vec(0), vec(nfb)],
        out_specs=(col(0), col(0), tap(0), tap(0), vec(0), vec(0)),
        scratch_shapes=[big(), big(), big(), big()],
        compiler_params=_params(("parallel",)),
    )(up, up, dact, w, w, b.reshape(1, two_f), b.reshape(1, two_f))
    return (jnp.concatenate([dv, dg], axis=1), jnp.concatenate([dwv, dwg], axis=1),
            jnp.concatenate([dbv, dbg], axis=1).reshape(two_f))


def _sum_parts(parts, *, name, tr=256):
    n_parts, rows, cols = parts.shape
    tr = _pick(rows, tr, 16)

    def body(p_ref, o_ref):
        g = p_ref[0].astype(F32)
        for k in range(1, n_parts):
            g = g + p_ref[k].astype(F32)
        o_ref[...] = g

    return pl.pallas_call(
        body, name=name, out_shape=jax.ShapeDtypeStruct((rows, cols), F32), grid=(rows // tr,),
        in_specs=[pl.BlockSpec((n_parts, tr, cols), lambda i: (0, i, 0))],
        out_specs=pl.BlockSpec((tr, cols), lambda i: (i, 0)), compiler_params=_params(("parallel",)),
    )(parts)


def _sum_adamw(parts, w, m, v, *, name, tr=256):
    n_parts, rows, cols = parts.shape
    tr = _pick(rows, tr, 16)
    c1 = 1.0 / (1.0 - ADAM_B1 ** ADAM_STEP)
    c2 = 1.0 / (1.0 - ADAM_B2 ** ADAM_STEP)

    def body(p_ref, w_ref, m_ref, v_ref, g_ref, d_ref, nm_ref, nv_ref):
        g = p_ref[0].astype(F32)
        for k in range(1, n_parts):
            g = g + p_ref[k].astype(F32)
        nm = ADAM_B1 * m_ref[...] + (1.0 - ADAM_B1) * g
        nv = ADAM_B2 * v_ref[...] + (1.0 - ADAM_B2) * (g * g)
        g_ref[...] = g
        nm_ref[...] = nm
        nv_ref[...] = nv
        d_ref[...] = -ADAM_LR * ((nm * c1) / (jnp.sqrt(nv * c2) + ADAM_EPS) + ADAM_WD * w_ref[...])

    blk = pl.BlockSpec((tr, cols), lambda i: (i, 0))
    out = jax.ShapeDtypeStruct((rows, cols), F32)
    return pl.pallas_call(
        body, name=name, out_shape=(out, out, out, out), grid=(rows // tr,),
        in_specs=[pl.BlockSpec((n_parts, tr, cols), lambda i: (0, i, 0)), blk, blk, blk],
        out_specs=(blk, blk, blk, blk), compiler_params=_params(("parallel",)),
    )(parts, w, m, v)


def _mesh_pos():
    return lax.axis_index("x"), lax.axis_index("y"), lax.axis_index("c")


def _flip(pos, k):
    x, y, c = pos
    return (1 - x if k & 4 else x, 1 - y if k & 2 else y, 1 - c if k & 1 else c)


def _dev_index(pos):
    return 4 * pos[0] + 2 * pos[1] + pos[2]


N_PEERS = N_DEV - 1


class _Exchange:
    def __init__(self, inputs, out_shapes, start, finish):
        n = len(inputs)
        self.inputs, self.out_shapes, self.start, self.finish = list(inputs), list(out_shapes), start, finish
        self.scratch_shapes = [pltpu.SemaphoreType.DMA((n * N_PEERS,)), pltpu.SemaphoreType.DMA((n * N_PEERS,)),
                               pltpu.SemaphoreType.DMA((n,))]


def _gather_exchange(xs):
    n = len(xs)

    def plan(x_refs, out_refs, sems):
        send_sems, recv_sems, local_sems = sems
        me = _mesh_pos()
        sibling = _flip(me, 1)
        chips = [_flip(me, 4), _flip(me, 2), _flip(me, 6)]

        def copy(a, k, block, to, from_input=False):
            slot = out_refs[a].at[_dev_index(block)]
            return pltpu.make_async_remote_copy(
                src_ref=x_refs[a] if from_input else slot, dst_ref=slot,
                send_sem=send_sems.at[a * N_PEERS + k], recv_sem=recv_sems.at[a * N_PEERS + k],
                device_id=to, device_id_type=pl.DeviceIdType.MESH)

        mine = [pltpu.make_async_copy(x_refs[a], out_refs[a].at[_dev_index(me)], local_sems.at[a]) for a in range(n)]
        first = [copy(a, 0, me, sibling, True) for a in range(n)]
        first += [copy(a, 1 + j, me, chip, True) for j, chip in enumerate(chips) for a in range(n)]
        return me, sibling, chips, copy, mine, first

    def start(x_refs, out_refs, sems):
        _, _, _, _, mine, first = plan(x_refs, out_refs, sems)
        for cp in mine + first:
            cp.start()

    def finish(x_refs, out_refs, sems):
        me, sibling, chips, copy, mine, first = plan(x_refs, out_refs, sems)
        passed = []
        for j, chip in enumerate(chips):
            for a in range(n):
                copy(a, 1 + j, chip, me).wait_recv()
                passed.append(copy(a, 4 + j, chip, sibling))
                passed[-1].start()
        for a in range(n):
            copy(a, 0, sibling, me).wait_recv()
        for j, chip in enumerate(chips):
            for a in range(n):
                copy(a, 4 + j, _flip(chip, 1), me).wait_recv()
        for cp in first + passed:
            cp.wait_send()
        for cp in mine:
            cp.wait()

    return _Exchange(xs, [jax.ShapeDtypeStruct((N_DEV,) + x.shape, x.dtype) for x in xs], start, finish)


def _scatter_exchange(xs):
    n = len(xs)

    def plan(x_refs, out_refs, sems):
        send_sems, recv_sems, local_sems = sems
        me = _mesh_pos()
        my_slot = _dev_index(me)

        def copy(a, k):
            peer = _flip(me, k)
            return pltpu.make_async_remote_copy(
                src_ref=x_refs[a].at[_dev_index(peer)], dst_ref=out_refs[a].at[my_slot],
                send_sem=send_sems.at[a * N_PEERS + k - 1], recv_sem=recv_sems.at[a * N_PEERS + k - 1],
                device_id=peer, device_id_type=pl.DeviceIdType.MESH)

        mine = [pltpu.make_async_copy(x_refs[a].at[my_slot], out_refs[a].at[my_slot], local_sems.at[a]) for a in range(n)]
        return mine, [copy(a, k) for k in range(1, N_DEV) for a in range(n)]

    def start(x_refs, out_refs, sems):
        mine, copies = plan(x_refs, out_refs, sems)
        for cp in mine + copies:
            cp.start()

    def finish(x_refs, out_refs, sems):
        mine, copies = plan(x_refs, out_refs, sems)
        for cp in copies:
            cp.wait_recv()
        for cp in copies:
            cp.wait_send()
        for cp in mine:
            cp.wait()

    return _Exchange(xs, [jax.ShapeDtypeStruct(x.shape, x.dtype) for x in xs], start, finish)


def _run_exchange(ex, *, name):
    n = len(ex.inputs)

    def body(*refs):
        ex.start(refs[:n], refs[n:2 * n], refs[2 * n:])
        ex.finish(refs[:n], refs[n:2 * n], refs[2 * n:])

    hbm = pl.BlockSpec(memory_space=pl.ANY)
    return pl.pallas_call(body, name=name, out_shape=tuple(ex.out_shapes), in_specs=[hbm] * n,
                          out_specs=tuple([hbm] * n), scratch_shapes=ex.scratch_shapes)(*ex.inputs)


def _pack(arrays):
    flat = jnp.concatenate([a.reshape(-1) for a in arrays])
    n = flat.shape[0]
    tile = PACK_W * PACK_ROW_ALIGN
    total = -(-n // tile) * tile
    return jnp.pad(flat, (0, total - n)).reshape(total // PACK_W, PACK_W)


def _unpack(buf, shapes):
    lead = buf.shape[:-2]
    flat = buf.reshape(lead + (-1,))
    out, off = [], 0
    for shp in shapes:
        n = 1
        for dim in shp:
            n *= dim
        out.append(flat[..., off:off + n].reshape(lead + tuple(shp)))
        off += n
    return out


def _join_columns(blocks):
    return jnp.moveaxis(blocks, 0, 2).reshape(blocks.shape[1], blocks.shape[2], -1)


def _mm_hosting(a, b, exchange, **kw):
    if exchange is None:
        return _mm(a, b, **kw), []
    return _mm(a, b, exchange=exchange, **kw)


GATHERED_BY_PROJ = ['w_out', 'mem_wq', 'mem_wk', 'mem_wv', 'mem_wo']
GATHERED_BY_ATTENTION = ['ffn_up', 'ffn_down']


def _as_matrix(blocks):
    return blocks.reshape(-1, blocks.shape[-1])


def _layer_fwd(x, xb, memb, w, alpha, shards, next_w_in):
    w = dict(w)
    cc = w['conv_w'].shape[1]
    n_pairs = (N_DEV * shards['w_out'].shape[0] - cc) // LANES

    proj, got = _mm(xb, w['w_in'], tb=True, name="mm_proj", exchange=_gather_exchange([shards[n] for n in GATHERED_BY_PROJ]))
    w.update({n: _as_matrix(f) for n, f in zip(GATHERED_BY_PROJ, got)})
    u1 = _glu_conv_fwd(proj, w['conv_w'], w['conv_b'], cc=cc, name="glu_conv_fwd")
    u = _cln_silu_fwd(u1, w['conv_ln_g'], w['conv_ln_b'], name="cln_silu_fwd")
    att, t_sum, n_walked, got = _sb_fwd(proj, col0=2 * cc, n_pairs=n_pairs, name="sb_fwd",
                                        exchange=_gather_exchange([shards[n] for n in GATHERED_BY_ATTENTION]))
    w.update({n: _as_matrix(f) for n, f in zip(GATHERED_BY_ATTENTION, got)})
    ua = jnp.concatenate([u, att], axis=1)
    mix = _mm(ua, w['w_out'], name="mm_mix")
    r1, x1, x1b = _res_ln_fwd(x, mix, w['ln1_g'], w['ln1_b'], alpha=alpha, name="res_ln_fwd")
    qm = _mm(x1b, w['mem_wq'], name="mm_memq")
    km = _mm(memb, w['mem_wk'], name="mm_memkv")
    vm = _mm(memb, w['mem_wv'], name="mm_memkv")
    o = _mem_attn_fwd(qm, km, vm, name="mem_attn_fwd")
    cross = _mm(o, w['mem_wo'], name="mm_mix")
    r2, x2, x2b = _res_ln_fwd(x1, cross, w['ln2_g'], w['ln2_b'], alpha=alpha, name="res_ln_fwd")
    up, got = _mm_hosting(x2b, w['ffn_up'], _gather_exchange([next_w_in]) if next_w_in is not None else None,
                          tb=True, name="mm_up")
    act = _ffn_act_fwd(up, w['ffn_conv_w'], w['ffn_conv_b'], name="ffn_act_fwd")
    ffn = _mm(act, w['ffn_down'], name="mm_down")
    r3, x3, x3b = _res_ln_fwd(x2, ffn, w['ln3_g'], w['ln3_b'], alpha=alpha, name="res_ln_fwd")
    saved = dict(xb=xb, proj=proj, u1=u1, t_sum=t_sum, n_walked=n_walked, ua=ua, r1=r1, x1b=x1b, qm=qm, km=km, vm=vm,
                 o=o, r2=r2, x2b=x2b, up=up, act=act, r3=r3)
    return x3, x3b, saved, w, _as_matrix(got[0]) if got else None


def _layer_bwd(da, dres, sv, memb, w, alpha, carried):
    g, received = {}, {}
    cc = w['conv_w'].shape[1]
    n_pairs = (w['w_out'].shape[0] - cc) // LANES

    def sending(sends):
        ex = _scatter_exchange([gm.reshape(N_DEV, -1, gm.shape[-1]) for _, gm in sends]) if sends else None
        return ex, lambda got: received.update({key: blocks for (key, _), blocks in zip(sends, got)})

    dr3, dr3b, g['ln3_g'], g['ln3_b'] = _ln_bwd(da, dres, sv['r3'], w['ln3_g'], alpha=alpha, name="ln_bwd")
    ex, file = sending(carried)
    g_down, got = _mm_hosting(sv['act'], dr3b, ex, ta=True, out_dtype=BF16, name="mm_dw_down")
    file(got)
    ex, file = sending([('ffn_down', g_down)])
    dact, got = _mm_hosting(dr3b, w['ffn_down'], ex, tb=True, name="mm_dact")
    file(got)
    dup, g['ffn_conv_w'], g['ffn_conv_b'] = _ffn_act_bwd(sv['up'], dact, w['ffn_conv_w'], w['ffn_conv_b'], name="ffn_act_bwd")
    g_up = _mm(dup, sv['x2b'], ta=True, out_dtype=BF16, name="mm_dw_up")
    da2 = _mm(dup, w['ffn_up'], name="mm_dx_up")
    dr2, dr2b, g['ln2_g'], g['ln2_b'] = _ln_bwd(da2, dr3, sv['r2'], w['ln2_g'], alpha=alpha, name="ln_bwd")
    g_wo = _mm(sv['o'], dr2b, ta=True, out_dtype=BF16, name="mm_dw_sq")
    do = _mm(dr2b, w['mem_wo'], tb=True, name="mm_dx_sq")
    dqm, dkm, dvm = _mem_attn_bwd(sv['qm'], sv['km'], sv['vm'], do, name="mem_attn_bwd")
    g_wq = _mm(sv['x1b'], dqm, ta=True, out_dtype=BF16, name="mm_dw_sq")
    g_wk = _mm(memb, dkm, ta=True, out_dtype=BF16, name="mm_dw_memkv")
    g_wv = _mm(memb, dvm, ta=True, out_dtype=BF16, name="mm_dw_memkv")
    da1 = _mm(dqm, w['mem_wq'], tb=True, name="mm_dx_sq")
    dr1, dr1b, g['ln1_g'], g['ln1_b'] = _ln_bwd(da1, dr2, sv['r1'], w['ln1_g'], alpha=alpha, name="ln_bwd")
    g_out = _mm(sv['ua'], dr1b, ta=True, out_dtype=BF16, name="mm_dw_sq")
    dua = _mm(dr1b, w['w_out'], tb=True, name="mm_dx_sq")
    du1, g['conv_ln_g'], g['conv_ln_b'] = _cln_silu_bwd(dua, sv['u1'], w['conv_ln_g'], w['conv_ln_b'], name="cln_silu_bwd")
    dga, dgg, g['conv_w'], g['conv_b'] = _glu_conv_bwd(du1, sv['proj'], w['conv_w'], cc=cc, name="glu_conv_bwd")
    ex, file = sending([('ffn_up', g_up), ('mem_wo', g_wo), ('mem_wq', g_wq), ('mem_wk', g_wk), ('mem_wv', g_wv), ('w_out', g_out)])
    dq, dk, dv, got = _sb_bwd(sv['proj'], sv['t_sum'], sv['n_walked'], dua, col0=2 * cc, n_pairs=n_pairs, do_col0=cc,
                              name="sb_bwd", exchange=ex)
    file(got)
    dproj = jnp.concatenate([dga, dgg, dq.astype(BF16), dk.astype(BF16), dv.astype(BF16)], axis=1)
    g_in = _mm(dproj, sv['xb'], ta=True, out_dtype=BF16, name="mm_dw_in")
    da0 = _mm(dproj, w['w_in'], name="mm_dx_in")
    return da0, dr1, g, received, [('w_in', g_in)]


def kernel(x, mem, w_in, conv_w, conv_b, conv_ln_g, conv_ln_b, w_out, ln1_g, ln1_b, mem_wq, mem_wk, mem_wv, mem_wo, ln2_g, ln2_b, ffn_up, ffn_conv_w, ffn_conv_b, ffn_down, ln3_g, ln3_b, loss_target, m_w_in, m_conv_w, m_conv_b, m_conv_ln_g, m_conv_ln_b, m_w_out, m_ln1_g, m_ln1_b, m_mem_wq, m_mem_wk, m_mem_wv, m_mem_wo, m_ln2_g, m_ln2_b, m_ffn_up, m_ffn_conv_w, m_ffn_conv_b, m_ffn_down, m_ln3_g, m_ln3_b, v_w_in, v_conv_w, v_conv_b, v_conv_ln_g, v_conv_ln_b, v_w_out, v_ln1_g, v_ln1_b, v_mem_wq, v_mem_wk, v_mem_wv, v_mem_wo, v_ln2_g, v_ln2_b, v_ffn_up, v_ffn_conv_w, v_ffn_conv_b, v_ffn_down, v_ln3_g, v_ln3_b):
    wts = dict(zip(WEIGHTS, (w_in, conv_w, conv_b, conv_ln_g, conv_ln_b, w_out, ln1_g, ln1_b, mem_wq, mem_wk, mem_wv,
                             mem_wo, ln2_g, ln2_b, ffn_up, ffn_conv_w, ffn_conv_b, ffn_down, ln3_g, ln3_b)))
    mom = dict(zip(WEIGHTS, (m_w_in, m_conv_w, m_conv_b, m_conv_ln_g, m_conv_ln_b, m_w_out, m_ln1_g, m_ln1_b, m_mem_wq,
                             m_mem_wk, m_mem_wv, m_mem_wo, m_ln2_g, m_ln2_b, m_ffn_up, m_ffn_conv_w, m_ffn_conv_b,
                             m_ffn_down, m_ln3_g, m_ln3_b)))
    var = dict(zip(WEIGHTS, (v_w_in, v_conv_w, v_conv_b, v_conv_ln_g, v_conv_ln_b, v_w_out, v_ln1_g, v_ln1_b, v_mem_wq,
                             v_mem_wk, v_mem_wv, v_mem_wo, v_ln2_g, v_ln2_b, v_ffn_up, v_ffn_conv_w, v_ffn_conv_b,
                             v_ffn_down, v_ln3_g, v_ln3_b)))
    depth = w_in.shape[0]
    alpha = (2.0 * depth) ** 0.25
    my_index = _dev_index(_mesh_pos())

    def row_shard(src, n, col_sharded, l):
        return src[n][l].T if col_sharded else src[n][l]

    tap_shapes = [wts[n].shape for n in TAPS]
    gathered_taps, = _run_exchange(_gather_exchange([_pack([wts[n] for n in TAPS])]), name="gather_taps")
    full_taps = {n: _join_columns(t) for n, t in zip(TAPS, _unpack(gathered_taps, tap_shapes))}

    def matrix_shards(l):
        return {n: row_shard(wts, n, cs, l).astype(BF16) for n, cs in MATRICES}

    shards = [matrix_shards(l) for l in range(depth)]
    got, = _run_exchange(_gather_exchange([shards[0]['w_in']]), name="gather_w_in")
    full_w_in = _as_matrix(got)

    xs = x[0]
    memb = mem[0].astype(BF16)
    h, hb = xs, xs.astype(BF16)
    saved, weights = [], []
    for l in range(depth):
        w = {'w_in': full_w_in}
        w.update({n: full_taps[n][l] for n in TAPS})
        w.update({n: wts[n][l] for n in REPLICATED})
        h, hb, sv, w, full_w_in = _layer_fwd(h, hb, memb, w, alpha, shards[l],
                                             shards[l + 1]['w_in'] if l + 1 < depth else None)
        saved.append(sv)
        weights.append(w)

    dy, loss_row = _loss_and_grad(h, loss_target[0], name="loss")
    loss = lax.psum(_row_sum(loss_row, name="loss_sum")[0, 0], ("x", "y", "c"))

    results = [{n: [None] * depth for n in WEIGHTS} for _ in range(4)]
    col_sharded = dict(MATRICES)

    def update(n, l, parts):
        cs = col_sharded[n]
        res = _sum_adamw(parts, *[row_shard(src, n, cs, l) for src in (wts, mom, var)], name="adamw_matrix")
        for kind in range(4):
            results[kind][n][l] = res[kind].T if cs else res[kind]

    da, dres = dy, None
    grads = [None] * depth
    carried = []
    for l in reversed(range(depth)):
        da, dres, grads[l], received, left = _layer_bwd(da, dres, saved[l], memb, weights[l], alpha, carried)
        for n, parts in received.items():
            update(n, l + 1 if n in dict(carried) else l, parts)
        carried = left
    last = _run_exchange(_scatter_exchange([gm.reshape(N_DEV, -1, gm.shape[-1]) for _, gm in carried]), name="scatter_grads")
    for (n, _), parts in zip(carried, last):
        update(n, 0, parts)
    grad_x = _axpy(da, dres, alpha=alpha, name="grad_x")[None]

    small = REPLICATED + TAPS
    part = _pack([jnp.stack([grads[l][n] for l in range(depth)]) for n in small])
    parts, = _run_exchange(_gather_exchange([part]), name="gather_small_grads")
    total = _sum_parts(parts, name="sum_small_grads")
    summed = dict(zip(small, _unpack(total, [wts[n].shape for n in REPLICATED] + [full_taps[n].shape for n in TAPS])))
    for n in TAPS:
        cols = wts[n].shape[-1]
        summed[n] = lax.dynamic_slice_in_dim(summed[n], my_index * cols, cols, axis=2)
    res = _sum_adamw(_pack([summed[n] for n in small])[None], *[_pack([src[n] for n in small]) for src in (wts, mom, var)],
                     name="adamw_small")
    for kind in range(4):
        for n, val in zip(small, _unpack(res[kind], [wts[n].shape for n in small])):
            results[kind][n] = val

    outs = [loss, grad_x]
    for kind in range(4):
        outs += [results[kind][n] if n in small else jnp.stack(results[kind][n]) for n in WEIGHTS]
    return tuple(outs)
```

```python
import functools

import jax
import jax.numpy as jnp
from jax import lax
from jax.experimental import pallas as pl
from jax.experimental.pallas import tpu as pltpu

F32 = jnp.float32
BF16 = jnp.bfloat16

N_DEV = 8
LANES = 128
PACK_W = 1024
PACK_ROW_ALIGN = 16
SB_HEAD_DIM = 64
MEM_HEAD_DIM = 256
LN_EPS = 1e-5
EXP_UNDERFLOW = 104.0
VMEM_LIMIT = 56 * 1024 * 1024

ADAM_LR = 0.001
ADAM_B1 = 0.9
ADAM_B2 = 0.999
ADAM_EPS = 1e-08
ADAM_WD = 0.01
ADAM_STEP = 10

IN_NAMES = ['x', 'mem', 'w_in', 'conv_w', 'conv_b', 'conv_ln_g', 'conv_ln_b', 'w_out', 'ln1_g', 'ln1_b',
            'mem_wq', 'mem_wk', 'mem_wv', 'mem_wo', 'ln2_g', 'ln2_b', 'ffn_up', 'ffn_conv_w', 'ffn_conv_b',
            'ffn_down', 'ln3_g', 'ln3_b']
WEIGHTS = IN_NAMES[2:]
MATRICES = [('w_in', True), ('w_out', False), ('mem_wq', False), ('mem_wk', False), ('mem_wv', False),
            ('mem_wo', False), ('ffn_up', True), ('ffn_down', False)]
TAPS = ['conv_w', 'ffn_conv_w']
REPLICATED = ['conv_b', 'conv_ln_g', 'conv_ln_b', 'ln1_g', 'ln1_b', 'ln2_g', 'ln2_b', 'ffn_conv_b', 'ln3_g', 'ln3_b']


def _pick(dim, pref, align=LANES):
    if dim <= pref:
        return dim
    fits = [t for t in range(align, pref + 1, align) if dim % t == 0]
    return fits[-1] if fits else dim


def _params(sem):
    return pltpu.CompilerParams(dimension_semantics=sem, vmem_limit_bytes=VMEM_LIMIT)


def _mm(a, b, *, ta=False, tb=False, out_dtype=F32, name, exchange=None):
    if ta:
        kdim, m = a.shape
        tm, tn, tk = _pick(m, 1408), _pick(b.shape[0 if tb else 1], 1024), _pick(kdim, 1024)
    else:
        m, kdim = a.shape
        tm, tn, tk = _pick(m, 512), _pick(b.shape[0 if tb else 1], 1536), _pick(kdim, 2816)
    if tb:
        n, kb = b.shape
    else:
        kb, n = b.shape
    assert kdim == kb, (a.shape, b.shape, ta, tb)
    grid = (m // tm, n // tn, kdim // tk)
    nk = grid[2]
    dims = (((0 if ta else 1,), (1 if tb else 0,)), ((), ()))
    n_ex_in = len(exchange.inputs) if exchange else 0
    n_ex_out = len(exchange.out_shapes) if exchange else 0

    def body(*refs):
        a_ref, b_ref = refs[:2]
        ex_in = refs[2:2 + n_ex_in]
        o_ref = refs[2 + n_ex_in]
        ex_out = refs[3 + n_ex_in:3 + n_ex_in + n_ex_out]
        scratch = refs[3 + n_ex_in + n_ex_out:]
        if nk > 1:
            acc_ref, scratch = scratch[0], scratch[1:]
        ids = [pl.program_id(d) for d in range(3)]
        if exchange:
            @pl.when((ids[0] == 0) & (ids[1] == 0) & (ids[2] == 0))
            def _():
                exchange.start(ex_in, ex_out, scratch)

        prod = lax.dot_general(a_ref[...].astype(BF16), b_ref[...].astype(BF16), dims,
                               preferred_element_type=F32)
        if nk == 1:
            o_ref[...] = prod.astype(out_dtype)
        else:
            k = ids[2]

            @pl.when(k == 0)
            def _():
                acc_ref[...] = prod

            @pl.when(k > 0)
            def _():
                acc_ref[...] += prod

            @pl.when(k == nk - 1)
            def _():
                o_ref[...] = acc_ref[...].astype(out_dtype)

        if exchange:
            @pl.when((ids[0] == grid[0] - 1) & (ids[1] == grid[1] - 1) & (ids[2] == grid[2] - 1))
            def _():
                exchange.finish(ex_in, ex_out, scratch)

    a_spec = pl.BlockSpec((tk, tm), lambda i, j, k: (k, i)) if ta else pl.BlockSpec((tm, tk), lambda i, j, k: (i, k))
    b_spec = pl.BlockSpec((tn, tk), lambda i, j, k: (j, k)) if tb else pl.BlockSpec((tk, tn), lambda i, j, k: (k, j))
    hbm = pl.BlockSpec(memory_space=pl.ANY)
    outs = pl.pallas_call(
        body, name=name,
        out_shape=(jax.ShapeDtypeStruct((m, n), out_dtype),) + tuple(exchange.out_shapes if exchange else ()),
        grid=grid,
        in_specs=[a_spec, b_spec] + [hbm] * n_ex_in,
        out_specs=(pl.BlockSpec((tm, tn), lambda i, j, k: (i, j)),) + (hbm,) * n_ex_out,
        scratch_shapes=([] if nk == 1 else [pltpu.VMEM((tm, tn), F32)]) + list(exchange.scratch_shapes if exchange else []),
        compiler_params=_params(("arbitrary",) * 3 if exchange else ("parallel", "parallel", "arbitrary")),
    )(a, b, *(exchange.inputs if exchange else ()))
    return (outs[0], list(outs[1:])) if exchange else outs[0]


def _ln_stats(r):
    mu = jnp.mean(r, axis=-1, keepdims=True)
    xc = r - mu
    var = jnp.mean(xc * xc, axis=-1, keepdims=True)
    rstd = lax.rsqrt(var + LN_EPS)
    return xc * rstd, rstd


def _res_ln_fwd(x, f, g, b, *, alpha, name, ts=512):
    s, d = x.shape
    ts = _pick(s, ts)

    def body(x_ref, f_ref, g_ref, b_ref, r_ref, y_ref, yb_ref):
        r = alpha * x_ref[...] + f_ref[...]
        xhat, _ = _ln_stats(r)
        y = xhat * g_ref[...] + b_ref[...]
        r_ref[...] = r
        y_ref[...] = y
        yb_ref[...] = y.astype(BF16)

    tok = pl.BlockSpec((ts, d), lambda i: (i, 0))
    vec = pl.BlockSpec((1, d), lambda i: (0, 0))
    return pl.pallas_call(
        body, name=name,
        out_shape=(jax.ShapeDtypeStruct((s, d), F32), jax.ShapeDtypeStruct((s, d), F32),
                   jax.ShapeDtypeStruct((s, d), BF16)),
        grid=(s // ts,), in_specs=[tok, tok, vec, vec], out_specs=(tok, tok, tok),
        compiler_params=_params(("parallel",)),
    )(x, f, g.reshape(1, d), b.reshape(1, d))


def _ln_bwd(da, dres, r, g, *, alpha, name, ts=512):
    s, d = r.shape
    ts = _pick(s, ts)
    has_res = dres is not None

    def body(*refs):
        if has_res:
            da_ref, dres_ref, r_ref, g_ref, dr_ref, drb_ref, dg_ref, db_ref = refs
            dy = da_ref[...] + alpha * dres_ref[...]
        else:
            da_ref, r_ref, g_ref, dr_ref, drb_ref, dg_ref, db_ref = refs
            dy = da_ref[...]
        xhat, rstd = _ln_stats(r_ref[...])
        dxhat = dy * g_ref[...]
        m1 = jnp.mean(dxhat, axis=-1, keepdims=True)
        m2 = jnp.mean(dxhat * xhat, axis=-1, keepdims=True)
        dr = rstd * (dxhat - m1 - xhat * m2)
        dr_ref[...] = dr
        drb_ref[...] = dr.astype(BF16)

        @pl.when(pl.program_id(0) == 0)
        def _():
            dg_ref[...] = jnp.zeros_like(dg_ref)
            db_ref[...] = jnp.zeros_like(db_ref)

        dg_ref[...] += jnp.sum(dy * xhat, axis=0, keepdims=True)
        db_ref[...] += jnp.sum(dy, axis=0, keepdims=True)

    tok = pl.BlockSpec((ts, d), lambda i: (i, 0))
    vec = pl.BlockSpec((1, d), lambda i: (0, 0))
    ins = [da, dres, r, g.reshape(1, d)] if has_res else [da, r, g.reshape(1, d)]
    dr, drb, dg, db = pl.pallas_call(
        body, name=name,
        out_shape=(jax.ShapeDtypeStruct((s, d), F32), jax.ShapeDtypeStruct((s, d), BF16),
                   jax.ShapeDtypeStruct((1, d), F32), jax.ShapeDtypeStruct((1, d), F32)),
        grid=(s // ts,), in_specs=[tok] * (len(ins) - 1) + [vec], out_specs=(tok, tok, vec, vec),
        compiler_params=_params(("arbitrary",)),
    )(*ins)
    return dr, drb, dg.reshape(d), db.reshape(d)


def _axpy(a, b, *, alpha, name, ts=512):
    s, d = a.shape
    ts = _pick(s, ts)

    def body(a_ref, b_ref, o_ref):
        o_ref[...] = a_ref[...] + alpha * b_ref[...]

    tok = pl.BlockSpec((ts, d), lambda i: (i, 0))
    return pl.pallas_call(body, name=name, out_shape=jax.ShapeDtypeStruct((s, d), F32), grid=(s // ts,),
                          in_specs=[tok, tok], out_specs=tok, compiler_params=_params(("parallel",)))(a, b)


def _loss_and_grad(y, target, *, name, ts=512):
    s, d = y.shape
    ts = _pick(s, ts)
    inv_d = 1.0 / d

    def body(y_ref, t_ref, dy_ref, loss_ref):
        e = y_ref[...] - t_ref[...]
        dy_ref[...] = e * inv_d

        @pl.when(pl.program_id(0) == 0)
        def _():
            loss_ref[...] = jnp.zeros_like(loss_ref)

        loss_ref[...] += jnp.sum(e * e, axis=0, keepdims=True) * (0.5 * inv_d)

    tok = pl.BlockSpec((ts, d), lambda i: (i, 0))
    vec = pl.BlockSpec((1, d), lambda i: (0, 0))
    dy, part = pl.pallas_call(
        body, name=name,
        out_shape=(jax.ShapeDtypeStruct((s, d), F32), jax.ShapeDtypeStruct((1, d), F32)),
        grid=(s // ts,), in_specs=[tok, tok], out_specs=(tok, vec),
        compiler_params=_params(("arbitrary",)),
    )(y, target)
    return dy, part


def _row_sum(v, *, name):
    def body(v_ref, o_ref):
        o_ref[...] = jnp.sum(v_ref[...], axis=1, keepdims=True)

    return pl.pallas_call(body, name=name, out_shape=jax.ShapeDtypeStruct((1, 1), F32))(v)


def _sigmoid(x):
    return 1.0 / (1.0 + jnp.exp(-x))


def _row_chunks(s, pref=512):
    c = _pick(s, pref, 8)
    return [(i * c, c) for i in range(s // c)]


def _glu_conv_fwd(proj, w, b, *, cc, name):
    s = proj.shape[0]
    kw = w.shape[0]
    pad = 32
    assert kw - 1 <= pad
    ncb = cc // LANES
    chunks = _row_chunks(s)

    def body(a_ref, g_ref, w_ref, b_ref, o_ref, u0_ref):
        u0_ref[pl.ds(0, pad), :] = jnp.zeros((pad, LANES), F32)
        for r0, rc in chunks:
            u0_ref[pl.ds(pad + r0, rc), :] = a_ref[pl.ds(r0, rc), :] * _sigmoid(g_ref[pl.ds(r0, rc), :])
        for r0, rc in chunks:
            acc = jnp.zeros((rc, LANES), F32) + b_ref[...]
            for k in range(kw):
                acc = acc + w_ref[pl.ds(k, 1), :] * u0_ref[pl.ds(pad + r0 - (kw - 1) + k, rc), :]
            o_ref[pl.ds(r0, rc), :] = acc

    return pl.pallas_call(
        body, name=name,
        out_shape=jax.ShapeDtypeStruct((s, cc), F32),
        grid=(ncb,),
        in_specs=[pl.BlockSpec((s, LANES), lambda c: (0, c)), pl.BlockSpec((s, LANES), lambda c: (0, ncb + c)),
                  pl.BlockSpec((kw, LANES), lambda c: (0, c)), pl.BlockSpec((1, LANES), lambda c: (0, c))],
        out_specs=pl.BlockSpec((s, LANES), lambda c: (0, c)),
        scratch_shapes=[pltpu.VMEM((s + pad, LANES), F32)],
        compiler_params=_params(("parallel",)),
    )(proj, proj, w, b.reshape(1, cc))


def _glu_conv_bwd(du1, proj, w, *, cc, name):
    s = proj.shape[0]
    kw = w.shape[0]
    pad = 32
    ncb = cc // LANES
    chunks = _row_chunks(s)

    def body(d_ref, a_ref, g_ref, w_ref, da_ref, dg_ref, dw_ref, db_ref, u0_ref, dp_ref):
        u0_ref[pl.ds(0, pad), :] = jnp.zeros((pad, LANES), F32)
        dp_ref[pl.ds(s, pad), :] = jnp.zeros((pad, LANES), F32)
        for r0, rc in chunks:
            u0_ref[pl.ds(pad + r0, rc), :] = a_ref[pl.ds(r0, rc), :] * _sigmoid(g_ref[pl.ds(r0, rc), :])
            dp_ref[pl.ds(r0, rc), :] = d_ref[pl.ds(r0, rc), :]
        dws = [jnp.zeros((1, LANES), F32) for _ in range(kw)]
        dbs = jnp.zeros((1, LANES), F32)
        for r0, rc in chunks:
            d = dp_ref[pl.ds(r0, rc), :]
            dbs = dbs + jnp.sum(d, axis=0, keepdims=True)
            du0 = jnp.zeros((rc, LANES), F32)
            for k in range(kw):
                du0 = du0 + w_ref[pl.ds(k, 1), :] * dp_ref[pl.ds(r0 + (kw - 1) - k, rc), :]
                dws[k] = dws[k] + jnp.sum(d * u0_ref[pl.ds(pad + r0 - (kw - 1) + k, rc), :], axis=0, keepdims=True)
            sg = _sigmoid(g_ref[pl.ds(r0, rc), :])
            a = a_ref[pl.ds(r0, rc), :]
            da_ref[pl.ds(r0, rc), :] = (du0 * sg).astype(BF16)
            dg_ref[pl.ds(r0, rc), :] = (du0 * a * sg * (1.0 - sg)).astype(BF16)
        for k in range(kw):
            dw_ref[pl.ds(k, 1), :] = dws[k]
        db_ref[...] = dbs

    col = lambda off: pl.BlockSpec((s, LANES), lambda c: (0, off + c))
    da, dg, dw, db = pl.pallas_call(
        body, name=name,
        out_shape=(jax.ShapeDtypeStruct((s, cc), BF16), jax.ShapeDtypeStruct((s, cc), BF16),
                   jax.ShapeDtypeStruct((kw, cc), F32), jax.ShapeDtypeStruct((1, cc), F32)),
        grid=(ncb,),
        in_specs=[col(0), col(0), col(ncb), pl.BlockSpec((kw, LANES), lambda c: (0, c))],
        out_specs=(col(0), col(0), pl.BlockSpec((kw, LANES), lambda c: (0, c)), pl.BlockSpec((1, LANES), lambda c: (0, c))),
        scratch_shapes=[pltpu.VMEM((s + pad, LANES), F32), pltpu.VMEM((s + pad, LANES), F32)],
        compiler_params=_params(("parallel",)),
    )(du1, proj, proj, w)
    return da, dg, dw, db.reshape(cc)


def _cln_silu_fwd(u1, g, b, *, name, ts=512):
    s, cc = u1.shape
    ts = _pick(s, ts)

    def body(u_ref, g_ref, b_ref, o_ref):
        xhat, _ = _ln_stats(u_ref[...])
        y = xhat * g_ref[...] + b_ref[...]
        o_ref[...] = (y * _sigmoid(y)).astype(BF16)

    tok = pl.BlockSpec((ts, cc), lambda i: (i, 0))
    vec = pl.BlockSpec((1, cc), lambda i: (0, 0))
    return pl.pallas_call(body, name=name, out_shape=jax.ShapeDtypeStruct((s, cc), BF16), grid=(s // ts,),
                          in_specs=[tok, vec, vec], out_specs=tok,
                          compiler_params=_params(("parallel",)))(u1, g.reshape(1, cc), b.reshape(1, cc))


def _cln_silu_bwd(dua, u1, g, b, *, name, ts=512):
    s, cc = u1.shape
    ts = _pick(s, ts)

    def body(d_ref, u_ref, g_ref, b_ref, du_ref, dg_ref, db_ref):
        xhat, rstd = _ln_stats(u_ref[...])
        y = xhat * g_ref[...] + b_ref[...]
        sg = _sigmoid(y)
        dy = d_ref[...] * (sg * (1.0 + y * (1.0 - sg)))
        dxhat = dy * g_ref[...]
        m1 = jnp.mean(dxhat, axis=-1, keepdims=True)
        m2 = jnp.mean(dxhat * xhat, axis=-1, keepdims=True)
        du_ref[...] = rstd * (dxhat - m1 - xhat * m2)

        @pl.when(pl.program_id(0) == 0)
        def _():
            dg_ref[...] = jnp.zeros_like(dg_ref)
            db_ref[...] = jnp.zeros_like(db_ref)

        dg_ref[...] += jnp.sum(dy * xhat, axis=0, keepdims=True)
        db_ref[...] += jnp.sum(dy, axis=0, keepdims=True)

    tok = pl.BlockSpec((ts, cc), lambda i: (i, 0))
    vec = pl.BlockSpec((1, cc), lambda i: (0, 0))
    du1, dg, db = pl.pallas_call(
        body, name=name,
        out_shape=(jax.ShapeDtypeStruct((s, cc), F32), jax.ShapeDtypeStruct((1, cc), F32),
                   jax.ShapeDtypeStruct((1, cc), F32)),
        grid=(s // ts,), in_specs=[tok, tok, vec, vec], out_specs=(tok, vec, vec),
        compiler_params=_params(("arbitrary",)),
    )(dua, u1, g.reshape(1, cc), b.reshape(1, cc))
    return du1, dg.reshape(cc), db.reshape(cc)


def _softplus_parts(z):
    lk = jnp.minimum(-z, 0.0) - jnp.log1p(jnp.exp(-jnp.abs(z)))
    return lk, z + lk


def _stack_heads(x, hms):
    return jnp.concatenate([jnp.where(hm, x, 0.0) for hm in hms], axis=0).astype(BF16)


def _heads_side_by_side(x_st, tq):
    return jnp.concatenate([x_st[:tq], x_st[tq:]], axis=1)


def _sb_tile_masks(tq):
    row = lax.broadcasted_iota(jnp.int32, (2 * tq, tq), 0)
    col = lax.broadcasted_iota(jnp.int32, (2 * tq, tq), 1)
    vis = col < jnp.where(row >= tq, row - tq, row)
    krow, kcol = row[:tq], col[:tq]
    return vis, (krow > kcol).astype(BF16), (krow < kcol).astype(BF16)


_NT = (((1,), (1,)), ((), ()))
_TN = (((0,), (0,)), ((), ()))


def _head_masks():
    lane = lax.broadcasted_iota(jnp.int32, (1, LANES), 1)
    return [(lane >= SB_HEAD_DIM * h) & (lane < SB_HEAD_DIM * (h + 1)) for h in range(2)]


def _hosted(exchange, grid, body):
    if exchange is None:
        return body, [], [], []
    n_in, n_out, n_sems = len(exchange.inputs), len(exchange.out_shapes), len(exchange.scratch_shapes)

    def wrapped(*refs, n_own_in, n_own_out):
        own_in, ex_in = refs[:n_own_in], refs[n_own_in:n_own_in + n_in]
        rest = refs[n_own_in + n_in:]
        own_out, ex_out = rest[:n_own_out], rest[n_own_out:n_own_out + n_out]
        own_scratch, sems = rest[n_own_out + n_out:len(rest) - n_sems], rest[len(rest) - n_sems:]
        ids = [pl.program_id(d) for d in range(len(grid))]
        first = functools.reduce(lambda x, y: x & y, [i == 0 for i in ids])
        last = functools.reduce(lambda x, y: x & y, [i == g - 1 for i, g in zip(ids, grid)])

        @pl.when(first)
        def _():
            exchange.start(ex_in, ex_out, sems)

        body(*own_in, *own_out, *own_scratch)

        @pl.when(last)
        def _():
            exchange.finish(ex_in, ex_out, sems)

    return wrapped, list(exchange.inputs), list(exchange.out_shapes), list(exchange.scratch_shapes)


def _sb_fwd(proj, *, col0, n_pairs, name, tq=256, exchange=None):
    s = proj.shape[0]
    tq = _pick(s, tq)
    nq = s // tq
    cb0 = col0 // LANES
    scale = SB_HEAD_DIM ** -0.5

    def body(q_ref, k_ref, v_ref, o_ref, t_ref, n_ref):
        i = pl.program_id(1)
        hms = _head_masks()
        vis, m_after = _sb_tile_masks(tq)[:2]
        q_st = _stack_heads(q_ref[...] * scale, hms)

        def tile(j, c, acc, masked):
            start = pl.multiple_of(j * tq, tq)
            kb = k_ref[pl.ds(start, tq), :].astype(BF16)
            v_st = _stack_heads(v_ref[pl.ds(start, tq), :], hms)
            z = lax.dot_general(q_st, kb, _NT, preferred_element_type=F32)
            lk, lb = _softplus_parts(z)
            if masked:
                lk = jnp.where(vis, lk, 0.0)
            later = jnp.dot(lk.astype(BF16), m_after, preferred_element_type=F32)
            a = jnp.exp(lb + later + c)
            if masked:
                a = jnp.where(vis, a, 0.0)
            acc = acc + jnp.dot(_heads_side_by_side(a.astype(BF16), tq), v_st, preferred_element_type=F32)
            return c + jnp.sum(lk, axis=1, keepdims=True), acc

        def more(st):
            return jnp.logical_and(st[0] < i, jnp.max(st[1]) > -EXP_UNDERFLOW)

        def step(st):
            c, acc = tile(i - 1 - st[0], st[1], st[2], False)
            return st[0] + 1, c, acc

        c, acc = tile(i, jnp.zeros((2 * tq, 1), F32), jnp.zeros((tq, LANES), F32), True)
        n, c, acc = lax.while_loop(more, step, (jnp.int32(0), c, acc))
        o_ref[...] = acc.astype(BF16)
        t_ref[...] = jnp.where(hms[0], c[:tq], c[tq:])
        n_ref[...] = jnp.zeros((8, LANES), F32) + n.astype(F32)

    grid = (n_pairs, nq)
    body, ex_in, ex_out, ex_scratch = _hosted(exchange, grid, body)
    if exchange is not None:
        body = functools.partial(body, n_own_in=3, n_own_out=3)
    hbm = pl.BlockSpec(memory_space=pl.ANY)
    seq = lambda off: pl.BlockSpec((s, LANES), lambda p, i: (0, cb0 + off + p))
    outs = pl.pallas_call(
        body, name=name,
        out_shape=(jax.ShapeDtypeStruct((s, n_pairs * LANES), BF16), jax.ShapeDtypeStruct((n_pairs, s, LANES), F32),
                   jax.ShapeDtypeStruct((n_pairs, nq * 8, LANES), F32), *ex_out),
        grid=grid,
        in_specs=[pl.BlockSpec((tq, LANES), lambda p, i: (i, cb0 + p)), seq(n_pairs), seq(2 * n_pairs)] + [hbm] * len(ex_in),
        out_specs=(pl.BlockSpec((tq, LANES), lambda p, i: (i, p)), pl.BlockSpec((None, tq, LANES), lambda p, i: (p, i, 0)),
                   pl.BlockSpec((None, 8, LANES), lambda p, i: (p, i, 0)), *([hbm] * len(ex_out))),
        scratch_shapes=ex_scratch,
        compiler_params=_params(("arbitrary", "arbitrary") if exchange else ("parallel", "arbitrary")),
    )(proj, proj, proj, *ex_in)
    return outs[0], outs[1], outs[2], list(outs[3:])


def _sb_bwd(proj, t_sum, n_walked, dua, *, col0, n_pairs, do_col0, name, tq=256, exchange=None):
    s = proj.shape[0]
    tq = _pick(s, tq)
    cb0 = col0 // LANES
    dcb0 = do_col0 // LANES
    scale = SB_HEAD_DIM ** -0.5

    def body(q_ref, k_ref, v_ref, t_ref, n_ref, do_ref, dq_ref, dk_ref, dv_ref):
        i = pl.program_id(1)

        @pl.when(i == 0)
        def _():
            dk_ref[...] = jnp.zeros_like(dk_ref)
            dv_ref[...] = jnp.zeros_like(dv_ref)

        hms = _head_masks()
        vis, m_after, m_before = _sb_tile_masks(tq)
        q_st = _stack_heads(q_ref[...] * scale, hms)
        do_st = _stack_heads(do_ref[...], hms)
        t_st = jnp.concatenate([t_ref[:, SB_HEAD_DIM * h:SB_HEAD_DIM * h + 1] for h in range(2)], axis=0)
        first = i - jnp.max(n_ref[...]).astype(jnp.int32)

        def tile(j, p_sum, g_sum, dq, masked):
            start = pl.multiple_of(j * tq, tq)
            k = k_ref[pl.ds(start, tq), :]
            z = lax.dot_general(q_st, k.astype(BF16), _NT, preferred_element_type=F32)
            lk_raw, lb = _softplus_parts(z)
            lk = jnp.where(vis, lk_raw, 0.0) if masked else lk_raw
            p_next = p_sum + jnp.sum(lk, axis=1, keepdims=True)
            later = jnp.dot(lk.astype(BF16), m_after, preferred_element_type=F32)
            a = jnp.exp(lb + (t_st - p_next) + later)
            if masked:
                a = jnp.where(vis, a, 0.0)
            da = lax.dot_general(do_st, v_ref[pl.ds(start, tq), :].astype(BF16), _NT, preferred_element_type=F32)
            g = a * da
            g_before = g_sum + jnp.dot(g.astype(BF16), m_before, preferred_element_type=F32)
            dz = g * jnp.exp(lk_raw) - g_before * jnp.exp(lb)
            if masked:
                dz = jnp.where(vis, dz, 0.0)
            dzb = dz.astype(BF16)
            dv_ref[pl.ds(start, tq), :] += lax.dot_general(a.astype(BF16), do_st, _TN, preferred_element_type=F32)
            dk_ref[pl.ds(start, tq), :] += lax.dot_general(dzb, q_st, _TN, preferred_element_type=F32)
            dq = dq + jnp.dot(_heads_side_by_side(dzb, tq), _stack_heads(k, hms), preferred_element_type=F32)
            return p_next, g_sum + jnp.sum(g, axis=1, keepdims=True), dq

        zero = jnp.zeros((2 * tq, 1), F32)
        st = lax.fori_loop(first, i, lambda j, st: tile(j, *st, False), (zero, zero, jnp.zeros((tq, LANES), F32)))
        dq_ref[...] = tile(i, *st, True)[2] * scale

    grid = (n_pairs, s // tq)
    body, ex_in, ex_out, ex_scratch = _hosted(exchange, grid, body)
    if exchange is not None:
        body = functools.partial(body, n_own_in=6, n_own_out=3)
    hbm = pl.BlockSpec(memory_space=pl.ANY)
    seq = lambda off: pl.BlockSpec((s, LANES), lambda p, i: (0, cb0 + off + p))
    out = jax.ShapeDtypeStruct((s, n_pairs * LANES), F32)
    res = pl.BlockSpec((s, LANES), lambda p, i: (0, p))
    outs = pl.pallas_call(
        body, name=name,
        out_shape=(out, out, out, *ex_out),
        grid=grid,
        in_specs=[pl.BlockSpec((tq, LANES), lambda p, i: (i, cb0 + p)), seq(n_pairs), seq(2 * n_pairs),
                  pl.BlockSpec((None, tq, LANES), lambda p, i: (p, i, 0)),
                  pl.BlockSpec((None, 8, LANES), lambda p, i: (p, i, 0)),
                  pl.BlockSpec((tq, LANES), lambda p, i: (i, dcb0 + p))] + [hbm] * len(ex_in),
        out_specs=(pl.BlockSpec((tq, LANES), lambda p, i: (i, p)), res, res, *([hbm] * len(ex_out))),
        scratch_shapes=ex_scratch,
        compiler_params=_params(("arbitrary", "arbitrary")),
    )(proj, proj, proj, t_sum, n_walked, dua, *ex_in)
    return outs[0], outs[1], outs[2], list(outs[3:])


def _mem_attn_fwd(qm, km, vm, *, name, tq=512):
    s, d = qm.shape
    heads = d // MEM_HEAD_DIM
    mlen = km.shape[0]
    tq = _pick(s, tq)
    scale = MEM_HEAD_DIM ** -0.5

    def body(q_ref, k_ref, v_ref, o_ref):
        for h in range(heads):
            sl = slice(h * MEM_HEAD_DIM, (h + 1) * MEM_HEAD_DIM)
            q = (q_ref[:, sl] * scale).astype(BF16)
            sc = lax.dot_general(q, k_ref[:, sl].astype(BF16), _NT, preferred_element_type=F32)
            e = jnp.exp(sc - jnp.max(sc, axis=1, keepdims=True))
            p = e / jnp.sum(e, axis=1, keepdims=True)
            o_ref[:, sl] = jnp.dot(p.astype(BF16), v_ref[:, sl].astype(BF16), preferred_element_type=F32).astype(BF16)

    tok = pl.BlockSpec((tq, d), lambda i: (i, 0))
    kv = pl.BlockSpec((mlen, d), lambda i: (0, 0))
    return pl.pallas_call(body, name=name, out_shape=jax.ShapeDtypeStruct((s, d), BF16), grid=(s // tq,),
                          in_specs=[tok, kv, kv], out_specs=tok, compiler_params=_params(("parallel",)))(qm, km, vm)


def _mem_attn_bwd(qm, km, vm, do, *, name, tq=512):
    s, d = qm.shape
    heads = d // MEM_HEAD_DIM
    mlen = km.shape[0]
    tq = _pick(s, tq)
    scale = MEM_HEAD_DIM ** -0.5

    def body(q_ref, k_ref, v_ref, do_ref, dq_ref, dk_ref, dv_ref):
        @pl.when(pl.program_id(0) == 0)
        def _():
            dk_ref[...] = jnp.zeros_like(dk_ref)
            dv_ref[...] = jnp.zeros_like(dv_ref)

        for h in range(heads):
            sl = slice(h * MEM_HEAD_DIM, (h + 1) * MEM_HEAD_DIM)
            q = (q_ref[:, sl] * scale).astype(BF16)
            k = k_ref[:, sl].astype(BF16)
            v = v_ref[:, sl].astype(BF16)
            sc = lax.dot_general(q, k, _NT, preferred_element_type=F32)
            e = jnp.exp(sc - jnp.max(sc, axis=1, keepdims=True))
            p = e / jnp.sum(e, axis=1, keepdims=True)
            dob = do_ref[:, sl].astype(BF16)
            dv_ref[:, sl] += lax.dot_general(p.astype(BF16), dob, _TN, preferred_element_type=F32)
            dp = lax.dot_general(dob, v, _NT, preferred_element_type=F32)
            ds = (p * (dp - jnp.sum(dp * p, axis=1, keepdims=True))).astype(BF16)
            dq_ref[:, sl] = (jnp.dot(ds, k, preferred_element_type=F32) * scale).astype(BF16)
            dk_ref[:, sl] += lax.dot_general(ds, q, _TN, preferred_element_type=F32)

    tok = pl.BlockSpec((tq, d), lambda i: (i, 0))
    kv = pl.BlockSpec((mlen, d), lambda i: (0, 0))
    return pl.pallas_call(
        body, name=name,
        out_shape=(jax.ShapeDtypeStruct((s, d), BF16), jax.ShapeDtypeStruct((mlen, d), F32),
                   jax.ShapeDtypeStruct((mlen, d), F32)),
        grid=(s // tq,), in_specs=[tok, kv, kv, tok], out_specs=(tok, kv, kv),
        compiler_params=_params(("arbitrary",)),
    )(qm, km, vm, do)


def _ffn_act_fwd(up, w, b, *, name):
    s, two_f = up.shape
    ff = two_f // 2
    nfb = ff // LANES
    kw = w.shape[0]
    pad = 8
    chunks = _row_chunks(s)

    def body(v_ref, g_ref, wv_ref, wg_ref, bv_ref, bg_ref, o_ref, vp_ref, gp_ref):
        vp_ref[pl.ds(0, pad), :] = jnp.zeros((pad, LANES), F32)
        gp_ref[pl.ds(0, pad), :] = jnp.zeros((pad, LANES), F32)
        for r0, rc in chunks:
            vp_ref[pl.ds(pad + r0, rc), :] = v_ref[pl.ds(r0, rc), :]
            gp_ref[pl.ds(pad + r0, rc), :] = g_ref[pl.ds(r0, rc), :]
        for r0, rc in chunks:
            vc = jnp.zeros((rc, LANES), F32) + bv_ref[...]
            gc = jnp.zeros((rc, LANES), F32) + bg_ref[...]
            for k in range(kw):
                off = pad + r0 - (kw - 1) + k
                vc = vc + wv_ref[pl.ds(k, 1), :] * vp_ref[pl.ds(off, rc), :]
                gc = gc + wg_ref[pl.ds(k, 1), :] * gp_ref[pl.ds(off, rc), :]
            o_ref[pl.ds(r0, rc), :] = (gc * _sigmoid(gc) * vc).astype(BF16)

    col = lambda off: pl.BlockSpec((s, LANES), lambda c: (0, off + c))
    tap = lambda off: pl.BlockSpec((kw, LANES), lambda c: (0, off + c))
    vec = lambda off: pl.BlockSpec((1, LANES), lambda c: (0, off + c))
    return pl.pallas_call(
        body, name=name, out_shape=jax.ShapeDtypeStruct((s, ff), BF16), grid=(nfb,),
        in_specs=[col(0), col(nfb), tap(0), tap(nfb), vec(0), vec(nfb)], out_specs=col(0),
        scratch_shapes=[pltpu.VMEM((s + pad, LANES), F32), pltpu.VMEM((s + pad, LANES), F32)],
        compiler_params=_params(("parallel",)),
    )(up, up, w, w, b.reshape(1, two_f), b.reshape(1, two_f))


def _ffn_act_bwd(up, dact, w, b, *, name, exchange=None):
    s, two_f = up.shape
    ff = two_f // 2
    nfb = ff // LANES
    kw = w.shape[0]
    pad = 8
    chunks = _row_chunks(s)

    def body(v_ref, g_ref, d_ref, wv_ref, wg_ref, bv_ref, bg_ref, dv_ref, dg_ref, dwv_ref, dwg_ref, dbv_ref, dbg_ref,
             vp_ref, gp_ref, dvc_ref, dgc_ref):
        vp_ref[pl.ds(0, pad), :] = jnp.zeros((pad, LANES), F32)
        gp_ref[pl.ds(0, pad), :] = jnp.zeros((pad, LANES), F32)
        dvc_ref[pl.ds(s, pad), :] = jnp.zeros((pad, LANES), F32)
        dgc_ref[pl.ds(s, pad), :] = jnp.zeros((pad, LANES), F32)
        for r0, rc in chunks:
            vp_ref[pl.ds(pad + r0, rc), :] = v_ref[pl.ds(r0, rc), :]
            gp_ref[pl.ds(pad + r0, rc), :] = g_ref[pl.ds(r0, rc), :]
        dwv = [jnp.zeros((1, LANES), F32) for _ in range(kw)]
        dwg = [jnp.zeros((1, LANES), F32) for _ in range(kw)]
        dbv = jnp.zeros((1, LANES), F32)
        dbg = jnp.zeros((1, LANES), F32)
        for r0, rc in chunks:
            vc = jnp.zeros((rc, LANES), F32) + bv_ref[...]
            gc = jnp.zeros((rc, LANES), F32) + bg_ref[...]
            for k in range(kw):
                off = pad + r0 - (kw - 1) + k
                vc = vc + wv_ref[pl.ds(k, 1), :] * vp_ref[pl.ds(off, rc), :]
                gc = gc + wg_ref[pl.ds(k, 1), :] * gp_ref[pl.ds(off, rc), :]
            sg = _sigmoid(gc)
            d = d_ref[pl.ds(r0, rc), :]
            dvc = d * (gc * sg)
            dgc = d * vc * (sg * (1.0 + gc * (1.0 - sg)))
            dvc_ref[pl.ds(r0, rc), :] = dvc
            dgc_ref[pl.ds(r0, rc), :] = dgc
            dbv = dbv + jnp.sum(dvc, axis=0, keepdims=True)
            dbg = dbg + jnp.sum(dgc, axis=0, keepdims=True)
            for k in range(kw):
                off = pad + r0 - (kw - 1) + k
                dwv[k] = dwv[k] + jnp.sum(dvc * vp_ref[pl.ds(off, rc), :], axis=0, keepdims=True)
                dwg[k] = dwg[k] + jnp.sum(dgc * gp_ref[pl.ds(off, rc), :], axis=0, keepdims=True)
        for r0, rc in chunks:
            dv = jnp.zeros((rc, LANES), F32)
            dg = jnp.zeros((rc, LANES), F32)
            for k in range(kw):
                off = r0 + (kw - 1) - k
                dv = dv + wv_ref[pl.ds(k, 1), :] * dvc_ref[pl.ds(off, rc), :]
                dg = dg + wg_ref[pl.ds(k, 1), :] * dgc_ref[pl.ds(off, rc), :]
            dv_ref[pl.ds(r0, rc), :] = dv.astype(BF16)
            dg_ref[pl.ds(r0, rc), :] = dg.astype(BF16)
        for k in range(kw):
            dwv_ref[pl.ds(k, 1), :] = dwv[k]
            dwg_ref[pl.ds(k, 1), :] = dwg[k]
        dbv_ref[...] = dbv
        dbg_ref[...] = dbg

    col = lambda off: pl.BlockSpec((s, LANES), lambda c: (0, off + c))
    tap = lambda off: pl.BlockSpec((kw, LANES), lambda c: (0, off + c))
    vec = lambda off: pl.BlockSpec((1, LANES), lambda c: (0, off + c))
    big = lambda: pltpu.VMEM((s + pad, LANES), F32)
    body, ex_in, ex_out, ex_scratch = _hosted(exchange, (nfb,), body)
    if exchange is not None:
        body = functools.partial(body, n_own_in=7, n_own_out=6)
    hbm = pl.BlockSpec(memory_space=pl.ANY)
    outs = pl.pallas_call(
        body, name=name,
        out_shape=(jax.ShapeDtypeStruct((s, ff), BF16), jax.ShapeDtypeStruct((s, ff), BF16),
                   jax.ShapeDtypeStruct((kw, ff), F32), jax.ShapeDtypeStruct((kw, ff), F32),
                   jax.ShapeDtypeStruct((1, ff), F32), jax.ShapeDtypeStruct((1, ff), F32), *ex_out),
        grid=(nfb,),
        in_specs=[col(0), col(nfb), col(0), tap(0), tap(nfb), vec(0), vec(nfb)] + [hbm] * len(ex_in),
        out_specs=(col(0), col(0), tap(0), tap(0), vec(0), vec(0), *([hbm] * len(ex_out))),
        scratch_shapes=[big(), big(), big(), big()] + ex_scratch,
        compiler_params=_params(("arbitrary",) if exchange else ("parallel",)),
    )(up, up, dact, w, w, b.reshape(1, two_f), b.reshape(1, two_f), *ex_in)
    dv, dg, dwv, dwg, dbv, dbg = outs[:6]
    return (jnp.concatenate([dv, dg], axis=1), jnp.concatenate([dwv, dwg], axis=1),
            jnp.concatenate([dbv, dbg], axis=1).reshape(two_f), list(outs[6:]))


def _sum_parts(parts, *, name, tr=256):
    n_parts, rows, cols = parts.shape
    tr = _pick(rows, tr, 16)

    def body(p_ref, o_ref):
        g = p_ref[0].astype(F32)
        for k in range(1, n_parts):
            g = g + p_ref[k].astype(F32)
        o_ref[...] = g

    return pl.pallas_call(
        body, name=name, out_shape=jax.ShapeDtypeStruct((rows, cols), F32), grid=(rows // tr,),
        in_specs=[pl.BlockSpec((n_parts, tr, cols), lambda i: (0, i, 0))],
        out_specs=pl.BlockSpec((tr, cols), lambda i: (i, 0)), compiler_params=_params(("parallel",)),
    )(parts)


def _sum_adamw(parts, w, m, v, *, name, tr=256):
    n_parts, rows, cols = parts.shape
    tr = _pick(rows, tr, 16)
    c1 = 1.0 / (1.0 - ADAM_B1 ** ADAM_STEP)
    c2 = 1.0 / (1.0 - ADAM_B2 ** ADAM_STEP)

    def body(p_ref, w_ref, m_ref, v_ref, g_ref, d_ref, nm_ref, nv_ref):
        g = p_ref[0].astype(F32)
        for k in range(1, n_parts):
            g = g + p_ref[k].astype(F32)
        nm = ADAM_B1 * m_ref[...] + (1.0 - ADAM_B1) * g
        nv = ADAM_B2 * v_ref[...] + (1.0 - ADAM_B2) * (g * g)
        g_ref[...] = g
        nm_ref[...] = nm
        nv_ref[...] = nv
        d_ref[...] = -ADAM_LR * ((nm * c1) / (jnp.sqrt(nv * c2) + ADAM_EPS) + ADAM_WD * w_ref[...])

    blk = pl.BlockSpec((tr, cols), lambda i: (i, 0))
    out = jax.ShapeDtypeStruct((rows, cols), F32)
    return pl.pallas_call(
        body, name=name, out_shape=(out, out, out, out), grid=(rows // tr,),
        in_specs=[pl.BlockSpec((n_parts, tr, cols), lambda i: (0, i, 0)), blk, blk, blk],
        out_specs=(blk, blk, blk, blk), compiler_params=_params(("parallel",)),
    )(parts, w, m, v)


def _mesh_pos():
    return lax.axis_index("x"), lax.axis_index("y"), lax.axis_index("c")


def _flip(pos, k):
    x, y, c = pos
    return (1 - x if k & 4 else x, 1 - y if k & 2 else y, 1 - c if k & 1 else c)


def _dev_index(pos):
    return 4 * pos[0] + 2 * pos[1] + pos[2]


N_PEERS = N_DEV - 1


class _Exchange:
    def __init__(self, inputs, out_shapes, start, finish):
        n = len(inputs)
        self.inputs, self.out_shapes, self.start, self.finish = list(inputs), list(out_shapes), start, finish
        self.scratch_shapes = [pltpu.SemaphoreType.DMA((n * N_PEERS,)), pltpu.SemaphoreType.DMA((n * N_PEERS,)),
                               pltpu.SemaphoreType.DMA((n,))]


def _gather_exchange(xs):
    n = len(xs)

    def plan(x_refs, out_refs, sems):
        send_sems, recv_sems, local_sems = sems
        me = _mesh_pos()
        sibling = _flip(me, 1)
        chips = [_flip(me, 4), _flip(me, 2), _flip(me, 6)]

        def copy(a, k, block, to, from_input=False):
            slot = out_refs[a].at[_dev_index(block)]
            return pltpu.make_async_remote_copy(
                src_ref=x_refs[a] if from_input else slot, dst_ref=slot,
                send_sem=send_sems.at[a * N_PEERS + k], recv_sem=recv_sems.at[a * N_PEERS + k],
                device_id=to, device_id_type=pl.DeviceIdType.MESH)

        mine = [pltpu.make_async_copy(x_refs[a], out_refs[a].at[_dev_index(me)], local_sems.at[a]) for a in range(n)]
        first = [copy(a, 0, me, sibling, True) for a in range(n)]
        first += [copy(a, 1 + j, me, chip, True) for j, chip in enumerate(chips) for a in range(n)]
        return me, sibling, chips, copy, mine, first

    def start(x_refs, out_refs, sems):
        _, _, _, _, mine, first = plan(x_refs, out_refs, sems)
        for cp in mine + first:
            cp.start()

    def finish(x_refs, out_refs, sems):
        me, sibling, chips, copy, mine, first = plan(x_refs, out_refs, sems)
        passed = []
        for j, chip in enumerate(chips):
            for a in range(n):
                copy(a, 1 + j, chip, me).wait_recv()
                passed.append(copy(a, 4 + j, chip, sibling))
                passed[-1].start()
        for a in range(n):
            copy(a, 0, sibling, me).wait_recv()
        for j, chip in enumerate(chips):
            for a in range(n):
                copy(a, 4 + j, _flip(chip, 1), me).wait_recv()
        for cp in first + passed:
            cp.wait_send()
        for cp in mine:
            cp.wait()

    return _Exchange(xs, [jax.ShapeDtypeStruct((N_DEV,) + x.shape, x.dtype) for x in xs], start, finish)


def _scatter_exchange(xs):
    n = len(xs)

    def plan(x_refs, out_refs, sems):
        send_sems, recv_sems, local_sems = sems
        me = _mesh_pos()
        my_slot = _dev_index(me)

        def copy(a, k):
            peer = _flip(me, k)
            return pltpu.make_async_remote_copy(
                src_ref=x_refs[a].at[_dev_index(peer)], dst_ref=out_refs[a].at[my_slot],
                send_sem=send_sems.at[a * N_PEERS + k - 1], recv_sem=recv_sems.at[a * N_PEERS + k - 1],
                device_id=peer, device_id_type=pl.DeviceIdType.MESH)

        mine = [pltpu.make_async_copy(x_refs[a].at[my_slot], out_refs[a].at[my_slot], local_sems.at[a]) for a in range(n)]
        return mine, [copy(a, k) for k in range(1, N_DEV) for a in range(n)]

    def start(x_refs, out_refs, sems):
        mine, copies = plan(x_refs, out_refs, sems)
        for cp in mine + copies:
            cp.start()

    def finish(x_refs, out_refs, sems):
        mine, copies = plan(x_refs, out_refs, sems)
        for cp in copies:
            cp.wait_recv()
        for cp in copies:
            cp.wait_send()
        for cp in mine:
            cp.wait()

    return _Exchange(xs, [jax.ShapeDtypeStruct(x.shape, x.dtype) for x in xs], start, finish)


def _run_exchange(ex, *, name):
    n = len(ex.inputs)

    def body(*refs):
        ex.start(refs[:n], refs[n:2 * n], refs[2 * n:])
        ex.finish(refs[:n], refs[n:2 * n], refs[2 * n:])

    hbm = pl.BlockSpec(memory_space=pl.ANY)
    return pl.pallas_call(body, name=name, out_shape=tuple(ex.out_shapes), in_specs=[hbm] * n,
                          out_specs=tuple([hbm] * n), scratch_shapes=ex.scratch_shapes)(*ex.inputs)


def _pack(arrays):
    flat = jnp.concatenate([a.reshape(-1) for a in arrays])
    n = flat.shape[0]
    tile = PACK_W * PACK_ROW_ALIGN
    total = -(-n // tile) * tile
    return jnp.pad(flat, (0, total - n)).reshape(total // PACK_W, PACK_W)


def _unpack(buf, shapes):
    lead = buf.shape[:-2]
    flat = buf.reshape(lead + (-1,))
    out, off = [], 0
    for shp in shapes:
        n = 1
        for dim in shp:
            n *= dim
        out.append(flat[..., off:off + n].reshape(lead + tuple(shp)))
        off += n
    return out


def _join_columns(blocks):
    return jnp.moveaxis(blocks, 0, 2).reshape(blocks.shape[1], blocks.shape[2], -1)


def _mm_hosting(a, b, exchange, **kw):
    if exchange is None:
        return _mm(a, b, **kw), []
    return _mm(a, b, exchange=exchange, **kw)


GATHERED_BY_ATTENTION = ['w_out', 'mem_wq', 'mem_wk', 'mem_wv', 'mem_wo', 'ffn_up', 'ffn_down']


def _as_matrix(blocks):
    return blocks.reshape(-1, blocks.shape[-1])


def _layer_fwd(x, xb, memb, w, alpha, shards, next_w_in):
    w = dict(w)
    cc = w['conv_w'].shape[1]
    n_pairs = (N_DEV * shards['w_out'].shape[0] - cc) // LANES

    proj = _mm(xb, w['w_in'], tb=True, name="mm_proj")
    u1 = _glu_conv_fwd(proj, w['conv_w'], w['conv_b'], cc=cc, name="glu_conv_fwd")
    u = _cln_silu_fwd(u1, w['conv_ln_g'], w['conv_ln_b'], name="cln_silu_fwd")
    att, t_sum, n_walked, got = _sb_fwd(proj, col0=2 * cc, n_pairs=n_pairs, name="sb_fwd",
                                        exchange=_gather_exchange([shards[n] for n in GATHERED_BY_ATTENTION]))
    w.update({n: _as_matrix(f) for n, f in zip(GATHERED_BY_ATTENTION, got)})
    ua = jnp.concatenate([u, att], axis=1)
    mix = _mm(ua, w['w_out'], name="mm_mix")
    r1, x1, x1b = _res_ln_fwd(x, mix, w['ln1_g'], w['ln1_b'], alpha=alpha, name="res_ln_fwd")
    qm = _mm(x1b, w['mem_wq'], name="mm_memq")
    km = _mm(memb, w['mem_wk'], name="mm_memkv")
    vm = _mm(memb, w['mem_wv'], name="mm_memkv")
    o = _mem_attn_fwd(qm, km, vm, name="mem_attn_fwd")
    cross = _mm(o, w['mem_wo'], name="mm_mix")
    r2, x2, x2b = _res_ln_fwd(x1, cross, w['ln2_g'], w['ln2_b'], alpha=alpha, name="res_ln_fwd")
    up, got = _mm_hosting(x2b, w['ffn_up'], _gather_exchange([next_w_in]) if next_w_in is not None else None,
                          tb=True, name="mm_up")
    act = _ffn_act_fwd(up, w['ffn_conv_w'], w['ffn_conv_b'], name="ffn_act_fwd")
    ffn = _mm(act, w['ffn_down'], name="mm_down")
    r3, x3, x3b = _res_ln_fwd(x2, ffn, w['ln3_g'], w['ln3_b'], alpha=alpha, name="res_ln_fwd")
    saved = dict(xb=xb, proj=proj, u1=u1, t_sum=t_sum, n_walked=n_walked, ua=ua, r1=r1, x1b=x1b, qm=qm, km=km, vm=vm,
                 o=o, r2=r2, x2b=x2b, up=up, act=act, r3=r3)
    return x3, x3b, saved, w, _as_matrix(got[0]) if got else None


def _layer_bwd(da, dres, sv, memb, w, alpha, carried):
    g, received = {}, {}
    cc = w['conv_w'].shape[1]
    n_pairs = (w['w_out'].shape[0] - cc) // LANES

    def sending(sends):
        ex = _scatter_exchange([gm.reshape(N_DEV, -1, gm.shape[-1]) for _, gm in sends]) if sends else None
        return ex, lambda got: received.update({key: blocks for (key, _), blocks in zip(sends, got)})

    dr3, dr3b, g['ln3_g'], g['ln3_b'] = _ln_bwd(da, dres, sv['r3'], w['ln3_g'], alpha=alpha, name="ln_bwd")
    g_down = _mm(sv['act'], dr3b, ta=True, out_dtype=BF16, name="mm_dw_down")
    dact = _mm(dr3b, w['ffn_down'], tb=True, name="mm_dact")
    ex, file = sending(carried + [('ffn_down', g_down)])
    dup, g['ffn_conv_w'], g['ffn_conv_b'], got = _ffn_act_bwd(sv['up'], dact, w['ffn_conv_w'], w['ffn_conv_b'],
                                                                name="ffn_act_bwd", exchange=ex)
    file(got)
    g_up = _mm(dup, sv['x2b'], ta=True, out_dtype=BF16, name="mm_dw_up")
    da2 = _mm(dup, w['ffn_up'], name="mm_dx_up")
    dr2, dr2b, g['ln2_g'], g['ln2_b'] = _ln_bwd(da2, dr3, sv['r2'], w['ln2_g'], alpha=alpha, name="ln_bwd")
    g_wo = _mm(sv['o'], dr2b, ta=True, out_dtype=BF16, name="mm_dw_sq")
    do = _mm(dr2b, w['mem_wo'], tb=True, name="mm_dx_sq")
    dqm, dkm, dvm = _mem_attn_bwd(sv['qm'], sv['km'], sv['vm'], do, name="mem_attn_bwd")
    g_wq = _mm(sv['x1b'], dqm, ta=True, out_dtype=BF16, name="mm_dw_sq")
    g_wk = _mm(memb, dkm, ta=True, out_dtype=BF16, name="mm_dw_memkv")
    g_wv = _mm(memb, dvm, ta=True, out_dtype=BF16, name="mm_dw_memkv")
    da1 = _mm(dqm, w['mem_wq'], tb=True, name="mm_dx_sq")
    dr1, dr1b, g['ln1_g'], g['ln1_b'] = _ln_bwd(da1, dr2, sv['r1'], w['ln1_g'], alpha=alpha, name="ln_bwd")
    g_out = _mm(sv['ua'], dr1b, ta=True, out_dtype=BF16, name="mm_dw_sq")
    dua = _mm(dr1b, w['w_out'], tb=True, name="mm_dx_sq")
    du1, g['conv_ln_g'], g['conv_ln_b'] = _cln_silu_bwd(dua, sv['u1'], w['conv_ln_g'], w['conv_ln_b'], name="cln_silu_bwd")
    dga, dgg, g['conv_w'], g['conv_b'] = _glu_conv_bwd(du1, sv['proj'], w['conv_w'], cc=cc, name="glu_conv_bwd")
    ex, file = sending([('ffn_up', g_up), ('mem_wo', g_wo), ('mem_wq', g_wq), ('mem_wk', g_wk), ('mem_wv', g_wv), ('w_out', g_out)])
    dq, dk, dv, got = _sb_bwd(sv['proj'], sv['t_sum'], sv['n_walked'], dua, col0=2 * cc, n_pairs=n_pairs, do_col0=cc,
                              name="sb_bwd", exchange=ex)
    file(got)
    dproj = jnp.concatenate([dga, dgg, dq.astype(BF16), dk.astype(BF16), dv.astype(BF16)], axis=1)
    g_in = _mm(dproj, sv['xb'], ta=True, out_dtype=BF16, name="mm_dw_in")
    da0 = _mm(dproj, w['w_in'], name="mm_dx_in")
    return da0, dr1, g, received, [('w_in', g_in)]


def kernel(x, mem, w_in, conv_w, conv_b, conv_ln_g, conv_ln_b, w_out, ln1_g, ln1_b, mem_wq, mem_wk, mem_wv, mem_wo, ln2_g, ln2_b, ffn_up, ffn_conv_w, ffn_conv_b, ffn_down, ln3_g, ln3_b, loss_target, m_w_in, m_conv_w, m_conv_b, m_conv_ln_g, m_conv_ln_b, m_w_out, m_ln1_g, m_ln1_b, m_mem_wq, m_mem_wk, m_mem_wv, m_mem_wo, m_ln2_g, m_ln2_b, m_ffn_up, m_ffn_conv_w, m_ffn_conv_b, m_ffn_down, m_ln3_g, m_ln3_b, v_w_in, v_conv_w, v_conv_b, v_conv_ln_g, v_conv_ln_b, v_w_out, v_ln1_g, v_ln1_b, v_mem_wq, v_mem_wk, v_mem_wv, v_mem_wo, v_ln2_g, v_ln2_b, v_ffn_up, v_ffn_conv_w, v_ffn_conv_b, v_ffn_down, v_ln3_g, v_ln3_b):
    wts = dict(zip(WEIGHTS, (w_in, conv_w, conv_b, conv_ln_g, conv_ln_b, w_out, ln1_g, ln1_b, mem_wq, mem_wk, mem_wv,
                             mem_wo, ln2_g, ln2_b, ffn_up, ffn_conv_w, ffn_conv_b, ffn_down, ln3_g, ln3_b)))
    mom = dict(zip(WEIGHTS, (m_w_in, m_conv_w, m_conv_b, m_conv_ln_g, m_conv_ln_b, m_w_out, m_ln1_g, m_ln1_b, m_mem_wq,
                             m_mem_wk, m_mem_wv, m_mem_wo, m_ln2_g, m_ln2_b, m_ffn_up, m_ffn_conv_w, m_ffn_conv_b,
                             m_ffn_down, m_ln3_g, m_ln3_b)))
    var = dict(zip(WEIGHTS, (v_w_in, v_conv_w, v_conv_b, v_conv_ln_g, v_conv_ln_b, v_w_out, v_ln1_g, v_ln1_b, v_mem_wq,
                             v_mem_wk, v_mem_wv, v_mem_wo, v_ln2_g, v_ln2_b, v_ffn_up, v_ffn_conv_w, v_ffn_conv_b,
                             v_ffn_down, v_ln3_g, v_ln3_b)))
    depth = w_in.shape[0]
    alpha = (2.0 * depth) ** 0.25
    my_index = _dev_index(_mesh_pos())

    def row_shard(src, n, col_sharded, l):
        return src[n][l].T if col_sharded else src[n][l]

    tap_shapes = [wts[n].shape for n in TAPS]
    gathered_taps, = _run_exchange(_gather_exchange([_pack([wts[n] for n in TAPS])]), name="gather_taps")
    full_taps = {n: _join_columns(t) for n, t in zip(TAPS, _unpack(gathered_taps, tap_shapes))}

    def matrix_shards(l):
        return {n: row_shard(wts, n, cs, l).astype(BF16) for n, cs in MATRICES}

    shards = [matrix_shards(l) for l in range(depth)]
    got, = _run_exchange(_gather_exchange([shards[0]['w_in']]), name="gather_w_in")
    full_w_in = _as_matrix(got)

    xs = x[0]
    memb = mem[0].astype(BF16)
    h, hb = xs, xs.astype(BF16)
    saved, weights = [], []
    for l in range(depth):
        w = {'w_in': full_w_in}
        w.update({n: full_taps[n][l] for n in TAPS})
        w.update({n: wts[n][l] for n in REPLICATED})
        h, hb, sv, w, full_w_in = _layer_fwd(h, hb, memb, w, alpha, shards[l],
                                             shards[l + 1]['w_in'] if l + 1 < depth else None)
        saved.append(sv)
        weights.append(w)

    dy, loss_row = _loss_and_grad(h, loss_target[0], name="loss")
    loss = lax.psum(_row_sum(loss_row, name="loss_sum")[0, 0], ("x", "y", "c"))

    results = [{n: [None] * depth for n in WEIGHTS} for _ in range(4)]
    col_sharded = dict(MATRICES)

    def update(n, l, parts):
        cs = col_sharded[n]
        res = _sum_adamw(parts, *[row_shard(src, n, cs, l) for src in (wts, mom, var)], name="adamw_matrix")
        for kind in range(4):
            results[kind][n][l] = res[kind].T if cs else res[kind]

    da, dres = dy, None
    grads = [None] * depth
    carried = []
    for l in reversed(range(depth)):
        da, dres, grads[l], received, left = _layer_bwd(da, dres, saved[l], memb, weights[l], alpha, carried)
        for n, parts in received.items():
            update(n, l + 1 if n in dict(carried) else l, parts)
        carried = left
    last = _run_exchange(_scatter_exchange([gm.reshape(N_DEV, -1, gm.shape[-1]) for _, gm in carried]), name="scatter_grads")
    for (n, _), parts in zip(carried, last):
        update(n, 0, parts)
    grad_x = _axpy(da, dres, alpha=alpha, name="grad_x")[None]

    small = REPLICATED + TAPS
    part = _pack([jnp.stack([grads[l][n] for l in range(depth)]) for n in small])
    parts, = _run_exchange(_gather_exchange([part]), name="gather_small_grads")
    total = _sum_parts(parts, name="sum_small_grads")
    summed = dict(zip(small, _unpack(total, [wts[n].shape for n in REPLICATED] + [full_taps[n].shape for n in TAPS])))
    for n in TAPS:
        cols = wts[n].shape[-1]
        summed[n] = lax.dynamic_slice_in_dim(summed[n], my_index * cols, cols, axis=2)
    res = _sum_adamw(_pack([summed[n] for n in small])[None], *[_pack([src[n] for n in small]) for src in (wts, mom, var)],
                     name="adamw_small")
    for kind in range(4):
        for n, val in zip(small, _unpack(res[kind], [wts[n].shape for n in small])):
            results[kind][n] = val

    outs = [loss, grad_x]
    for kind in range(4):
        outs += [results[kind][n] if n in small else jnp.stack(results[kind][n]) for n in WEIGHTS]
    return tuple(outs)
```

```python
import functools

import jax
import jax.numpy as jnp
from jax import lax
from jax.experimental import pallas as pl
from jax.experimental.pallas import tpu as pltpu

F32 = jnp.float32
BF16 = jnp.bfloat16

N_DEV = 8
LANES = 128
PACK_W = 1024
PACK_ROW_ALIGN = 16
SB_HEAD_DIM = 64
MEM_HEAD_DIM = 256
LN_EPS = 1e-5
EXP_UNDERFLOW = 104.0
VMEM_LIMIT = 56 * 1024 * 1024

ADAM_LR = 0.001
ADAM_B1 = 0.9
ADAM_B2 = 0.999
ADAM_EPS = 1e-08
ADAM_WD = 0.01
ADAM_STEP = 10

IN_NAMES = ['x', 'mem', 'w_in', 'conv_w', 'conv_b', 'conv_ln_g', 'conv_ln_b', 'w_out', 'ln1_g', 'ln1_b',
            'mem_wq', 'mem_wk', 'mem_wv', 'mem_wo', 'ln2_g', 'ln2_b', 'ffn_up', 'ffn_conv_w', 'ffn_conv_b',
            'ffn_down', 'ln3_g', 'ln3_b']
WEIGHTS = IN_NAMES[2:]
MATRICES = [('w_in', True), ('w_out', False), ('mem_wq', False), ('mem_wk', False), ('mem_wv', False),
            ('mem_wo', False), ('ffn_up', True), ('ffn_down', False)]
TAPS = ['conv_w', 'ffn_conv_w']
REPLICATED = ['conv_b', 'conv_ln_g', 'conv_ln_b', 'ln1_g', 'ln1_b', 'ln2_g', 'ln2_b', 'ffn_conv_b', 'ln3_g', 'ln3_b']


def _pick(dim, pref, align=LANES):
    if dim <= pref:
        return dim
    fits = [t for t in range(align, pref + 1, align) if dim % t == 0]
    return fits[-1] if fits else dim


def _params(sem):
    return pltpu.CompilerParams(dimension_semantics=sem, vmem_limit_bytes=VMEM_LIMIT)


def _mm(a, b, *, ta=False, tb=False, out_dtype=F32, name, exchange=None):
    if ta:
        kdim, m = a.shape
        tm, tn, tk = _pick(m, 1408), _pick(b.shape[0 if tb else 1], 1024), _pick(kdim, 1024)
    else:
        m, kdim = a.shape
        tm, tn, tk = _pick(m, 512), _pick(b.shape[0 if tb else 1], 1536), _pick(kdim, 2816)
    if tb:
        n, kb = b.shape
    else:
        kb, n = b.shape
    assert kdim == kb, (a.shape, b.shape, ta, tb)
    grid = (m // tm, n // tn, kdim // tk)
    nk = grid[2]
    dims = (((0 if ta else 1,), (1 if tb else 0,)), ((), ()))
    n_ex_in = len(exchange.inputs) if exchange else 0
    n_ex_out = len(exchange.out_shapes) if exchange else 0

    def body(*refs):
        a_ref, b_ref = refs[:2]
        ex_in = refs[2:2 + n_ex_in]
        o_ref = refs[2 + n_ex_in]
        ex_out = refs[3 + n_ex_in:3 + n_ex_in + n_ex_out]
        scratch = refs[3 + n_ex_in + n_ex_out:]
        if nk > 1:
            acc_ref, scratch = scratch[0], scratch[1:]
        ids = [pl.program_id(d) for d in range(3)]
        if exchange:
            @pl.when((ids[0] == 0) & (ids[1] == 0) & (ids[2] == 0))
            def _():
                exchange.start(ex_in, ex_out, scratch)

        prod = lax.dot_general(a_ref[...].astype(BF16), b_ref[...].astype(BF16), dims,
                               preferred_element_type=F32)
        if nk == 1:
            o_ref[...] = prod.astype(out_dtype)
        else:
            k = ids[2]

            @pl.when(k == 0)
            def _():
                acc_ref[...] = prod

            @pl.when(k > 0)
            def _():
                acc_ref[...] += prod

            @pl.when(k == nk - 1)
            def _():
                o_ref[...] = acc_ref[...].astype(out_dtype)

        if exchange:
            @pl.when((ids[0] == grid[0] - 1) & (ids[1] == grid[1] - 1) & (ids[2] == grid[2] - 1))
            def _():
                exchange.finish(ex_in, ex_out, scratch)

    a_spec = pl.BlockSpec((tk, tm), lambda i, j, k: (k, i)) if ta else pl.BlockSpec((tm, tk), lambda i, j, k: (i, k))
    b_spec = pl.BlockSpec((tn, tk), lambda i, j, k: (j, k)) if tb else pl.BlockSpec((tk, tn), lambda i, j, k: (k, j))
    hbm = pl.BlockSpec(memory_space=pl.ANY)
    outs = pl.pallas_call(
        body, name=name,
        out_shape=(jax.ShapeDtypeStruct((m, n), out_dtype),) + tuple(exchange.out_shapes if exchange else ()),
        grid=grid,
        in_specs=[a_spec, b_spec] + [hbm] * n_ex_in,
        out_specs=(pl.BlockSpec((tm, tn), lambda i, j, k: (i, j)),) + (hbm,) * n_ex_out,
        scratch_shapes=([] if nk == 1 else [pltpu.VMEM((tm, tn), F32)]) + list(exchange.scratch_shapes if exchange else []),
        compiler_params=_params(("arbitrary",) * 3 if exchange else ("parallel", "parallel", "arbitrary")),
    )(a, b, *(exchange.inputs if exchange else ()))
    return (outs[0], list(outs[1:])) if exchange else outs[0]


def _ln_stats(r):
    mu = jnp.mean(r, axis=-1, keepdims=True)
    xc = r - mu
    var = jnp.mean(xc * xc, axis=-1, keepdims=True)
    rstd = lax.rsqrt(var + LN_EPS)
    return xc * rstd, rstd


def _ln_bwd(da, dres, r, g, *, alpha, name, ts=512):
    s, d = r.shape
    ts = _pick(s, ts)
    has_res = dres is not None

    def body(*refs):
        if has_res:
            da_ref, dres_ref, r_ref, g_ref, dr_ref, drb_ref, dg_ref, db_ref = refs
            dy = da_ref[...] + alpha * dres_ref[...]
        else:
            da_ref, r_ref, g_ref, dr_ref, drb_ref, dg_ref, db_ref = refs
            dy = da_ref[...]
        xhat, rstd = _ln_stats(r_ref[...])
        dxhat = dy * g_ref[...]
        m1 = jnp.mean(dxhat, axis=-1, keepdims=True)
        m2 = jnp.mean(dxhat * xhat, axis=-1, keepdims=True)
        dr = rstd * (dxhat - m1 - xhat * m2)
        dr_ref[...] = dr
        drb_ref[...] = dr.astype(BF16)

        @pl.when(pl.program_id(0) == 0)
        def _():
            dg_ref[...] = jnp.zeros_like(dg_ref)
            db_ref[...] = jnp.zeros_like(db_ref)

        dg_ref[...] += jnp.sum(dy * xhat, axis=0, keepdims=True)
        db_ref[...] += jnp.sum(dy, axis=0, keepdims=True)

    tok = pl.BlockSpec((ts, d), lambda i: (i, 0))
    vec = pl.BlockSpec((1, d), lambda i: (0, 0))
    ins = [da, dres, r, g.reshape(1, d)] if has_res else [da, r, g.reshape(1, d)]
    dr, drb, dg, db = pl.pallas_call(
        body, name=name,
        out_shape=(jax.ShapeDtypeStruct((s, d), F32), jax.ShapeDtypeStruct((s, d), BF16),
                   jax.ShapeDtypeStruct((1, d), F32), jax.ShapeDtypeStruct((1, d), F32)),
        grid=(s // ts,), in_specs=[tok] * (len(ins) - 1) + [vec], out_specs=(tok, tok, vec, vec),
        compiler_params=_params(("arbitrary",)),
    )(*ins)
    return dr, drb, dg.reshape(d), db.reshape(d)


def _mm_fused(a, b, fn, *, rows, vecs, out_dtypes, n_sums, tb=False, name, tm=512):
    m, kdim = a.shape
    n = b.shape[0] if tb else b.shape[1]
    assert kdim == (b.shape[1] if tb else b.shape[0])
    tm, tk = _pick(m, tm), _pick(kdim, 2816)
    nk = kdim // tk
    dims = (((1,), (1 if tb else 0,)), ((), ()))
    n_rows, n_vecs, n_outs = len(rows), len(vecs), len(out_dtypes)

    def body(*refs):
        a_ref, b_ref = refs[:2]
        row_refs = refs[2:2 + n_rows]
        vec_refs = refs[2 + n_rows:2 + n_rows + n_vecs]
        out_refs = refs[2 + n_rows + n_vecs:2 + n_rows + n_vecs + n_outs]
        sum_refs = refs[2 + n_rows + n_vecs + n_outs:2 + n_rows + n_vecs + n_outs + n_sums]
        i, k = pl.program_id(0), pl.program_id(1)

        def finish(product):
            res = fn(product, *[r[...] for r in row_refs], *[v[...] for v in vec_refs])
            for o_ref, o in zip(out_refs, res[:n_outs]):
                o_ref[...] = o.astype(o_ref.dtype)
            if n_sums:
                @pl.when(i == 0)
                def _():
                    for s_ref in sum_refs:
                        s_ref[...] = jnp.zeros_like(s_ref)

                for s_ref, part in zip(sum_refs, res[n_outs:]):
                    s_ref[...] += part

        prod = lax.dot_general(a_ref[...].astype(BF16), b_ref[...].astype(BF16), dims, preferred_element_type=F32)
        if nk == 1:
            finish(prod)
        else:
            acc_ref = refs[-1]

            @pl.when(k == 0)
            def _():
                acc_ref[...] = prod

            @pl.when((k > 0) & (k < nk - 1))
            def _():
                acc_ref[...] += prod

            @pl.when(k == nk - 1)
            def _():
                finish(acc_ref[...] + prod)

    tok = pl.BlockSpec((tm, n), lambda i, k: (i, 0))
    vec = pl.BlockSpec((1, n), lambda i, k: (0, 0))
    b_spec = pl.BlockSpec((n, tk), lambda i, k: (0, k)) if tb else pl.BlockSpec((tk, n), lambda i, k: (k, 0))
    return pl.pallas_call(
        body, name=name,
        out_shape=tuple(jax.ShapeDtypeStruct((m, n), dt) for dt in out_dtypes) + (jax.ShapeDtypeStruct((1, n), F32),) * n_sums,
        grid=(m // tm, nk),
        in_specs=[pl.BlockSpec((tm, tk), lambda i, k: (i, k)), b_spec] + [tok] * n_rows + [vec] * n_vecs,
        out_specs=(tok,) * n_outs + (vec,) * n_sums,
        scratch_shapes=[] if nk == 1 else [pltpu.VMEM((tm, n), F32)],
        compiler_params=_params(("arbitrary", "arbitrary") if n_sums else ("parallel", "arbitrary")),
    )(a, b, *rows, *[v.reshape(1, n) for v in vecs])


def _mm_res_ln(a, b, x, g, beta, *, alpha, name):
    def fn(f, x_t, g_t, b_t):
        r = alpha * x_t + f
        xhat, _ = _ln_stats(r)
        y = xhat * g_t + b_t
        return r, y, y

    return _mm_fused(a, b, fn, rows=[x], vecs=[g, beta], out_dtypes=[F32, F32, BF16], n_sums=0, name=name)


def _mm_ln_bwd(a, b, dres, r, g, *, tb=False, alpha, name):
    def fn(f, dres_t, r_t, g_t):
        dy = f + alpha * dres_t
        xhat, rstd = _ln_stats(r_t)
        dxhat = dy * g_t
        m1 = jnp.mean(dxhat, axis=-1, keepdims=True)
        m2 = jnp.mean(dxhat * xhat, axis=-1, keepdims=True)
        dr = rstd * (dxhat - m1 - xhat * m2)
        return dr, dr, jnp.sum(dy * xhat, axis=0, keepdims=True), jnp.sum(dy, axis=0, keepdims=True)

    dr, drb, dg, db = _mm_fused(a, b, fn, rows=[dres, r], vecs=[g], out_dtypes=[F32, BF16], n_sums=2, tb=tb, name=name)
    return dr, drb, dg.reshape(-1), db.reshape(-1)


def _axpy(a, b, *, alpha, name, ts=512):
    s, d = a.shape
    ts = _pick(s, ts)

    def body(a_ref, b_ref, o_ref):
        o_ref[...] = a_ref[...] + alpha * b_ref[...]

    tok = pl.BlockSpec((ts, d), lambda i: (i, 0))
    return pl.pallas_call(body, name=name, out_shape=jax.ShapeDtypeStruct((s, d), F32), grid=(s // ts,),
                          in_specs=[tok, tok], out_specs=tok, compiler_params=_params(("parallel",)))(a, b)


def _loss_and_grad(y, target, *, name, ts=512):
    s, d = y.shape
    ts = _pick(s, ts)
    inv_d = 1.0 / d

    def body(y_ref, t_ref, dy_ref, loss_ref):
        e = y_ref[...] - t_ref[...]
        dy_ref[...] = e * inv_d

        @pl.when(pl.program_id(0) == 0)
        def _():
            loss_ref[...] = jnp.zeros_like(loss_ref)

        loss_ref[...] += jnp.sum(e * e, axis=0, keepdims=True) * (0.5 * inv_d)

    tok = pl.BlockSpec((ts, d), lambda i: (i, 0))
    vec = pl.BlockSpec((1, d), lambda i: (0, 0))
    dy, part = pl.pallas_call(
        body, name=name,
        out_shape=(jax.ShapeDtypeStruct((s, d), F32), jax.ShapeDtypeStruct((1, d), F32)),
        grid=(s // ts,), in_specs=[tok, tok], out_specs=(tok, vec),
        compiler_params=_params(("arbitrary",)),
    )(y, target)
    return dy, part


def _row_sum(v, *, name):
    def body(v_ref, o_ref):
        o_ref[...] = jnp.sum(v_ref[...], axis=1, keepdims=True)

    return pl.pallas_call(body, name=name, out_shape=jax.ShapeDtypeStruct((1, 1), F32))(v)


def _sigmoid(x):
    return 1.0 / (1.0 + jnp.exp(-x))


def _row_chunks(s, pref=512):
    c = _pick(s, pref, 8)
    return [(i * c, c) for i in range(s // c)]


def _glu_conv_fwd(proj, w, b, *, cc, name):
    s = proj.shape[0]
    kw = w.shape[0]
    pad = 32
    assert kw - 1 <= pad
    ncb = cc // LANES
    chunks = _row_chunks(s)

    def body(a_ref, g_ref, w_ref, b_ref, o_ref, u0_ref):
        u0_ref[pl.ds(0, pad), :] = jnp.zeros((pad, LANES), F32)
        for r0, rc in chunks:
            u0_ref[pl.ds(pad + r0, rc), :] = a_ref[pl.ds(r0, rc), :] * _sigmoid(g_ref[pl.ds(r0, rc), :])
        for r0, rc in chunks:
            acc = jnp.zeros((rc, LANES), F32) + b_ref[...]
            for k in range(kw):
                acc = acc + w_ref[pl.ds(k, 1), :] * u0_ref[pl.ds(pad + r0 - (kw - 1) + k, rc), :]
            o_ref[pl.ds(r0, rc), :] = acc

    return pl.pallas_call(
        body, name=name,
        out_shape=jax.ShapeDtypeStruct((s, cc), F32),
        grid=(ncb,),
        in_specs=[pl.BlockSpec((s, LANES), lambda c: (0, c)), pl.BlockSpec((s, LANES), lambda c: (0, ncb + c)),
                  pl.BlockSpec((kw, LANES), lambda c: (0, c)), pl.BlockSpec((1, LANES), lambda c: (0, c))],
        out_specs=pl.BlockSpec((s, LANES), lambda c: (0, c)),
        scratch_shapes=[pltpu.VMEM((s + pad, LANES), F32)],
        compiler_params=_params(("parallel",)),
    )(proj, proj, w, b.reshape(1, cc))


def _glu_conv_bwd(du1, proj, w, *, cc, name):
    s = proj.shape[0]
    kw = w.shape[0]
    pad = 32
    ncb = cc // LANES
    chunks = _row_chunks(s)

    def body(d_ref, a_ref, g_ref, w_ref, da_ref, dg_ref, dw_ref, db_ref, u0_ref, dp_ref):
        u0_ref[pl.ds(0, pad), :] = jnp.zeros((pad, LANES), F32)
        dp_ref[pl.ds(s, pad), :] = jnp.zeros((pad, LANES), F32)
        for r0, rc in chunks:
            u0_ref[pl.ds(pad + r0, rc), :] = a_ref[pl.ds(r0, rc), :] * _sigmoid(g_ref[pl.ds(r0, rc), :])
            dp_ref[pl.ds(r0, rc), :] = d_ref[pl.ds(r0, rc), :]
        dws = [jnp.zeros((1, LANES), F32) for _ in range(kw)]
        dbs = jnp.zeros((1, LANES), F32)
        for r0, rc in chunks:
            d = dp_ref[pl.ds(r0, rc), :]
            dbs = dbs + jnp.sum(d, axis=0, keepdims=True)
            du0 = jnp.zeros((rc, LANES), F32)
            for k in range(kw):
                du0 = du0 + w_ref[pl.ds(k, 1), :] * dp_ref[pl.ds(r0 + (kw - 1) - k, rc), :]
                dws[k] = dws[k] + jnp.sum(d * u0_ref[pl.ds(pad + r0 - (kw - 1) + k, rc), :], axis=0, keepdims=True)
            sg = _sigmoid(g_ref[pl.ds(r0, rc), :])
            a = a_ref[pl.ds(r0, rc), :]
            da_ref[pl.ds(r0, rc), :] = (du0 * sg).astype(BF16)
            dg_ref[pl.ds(r0, rc), :] = (du0 * a * sg * (1.0 - sg)).astype(BF16)
        for k in range(kw):
            dw_ref[pl.ds(k, 1), :] = dws[k]
        db_ref[...] = dbs

    col = lambda off: pl.BlockSpec((s, LANES), lambda c: (0, off + c))
    da, dg, dw, db = pl.pallas_call(
        body, name=name,
        out_shape=(jax.ShapeDtypeStruct((s, cc), BF16), jax.ShapeDtypeStruct((s, cc), BF16),
                   jax.ShapeDtypeStruct((kw, cc), F32), jax.ShapeDtypeStruct((1, cc), F32)),
        grid=(ncb,),
        in_specs=[col(0), col(0), col(ncb), pl.BlockSpec((kw, LANES), lambda c: (0, c))],
        out_specs=(col(0), col(0), pl.BlockSpec((kw, LANES), lambda c: (0, c)), pl.BlockSpec((1, LANES), lambda c: (0, c))),
        scratch_shapes=[pltpu.VMEM((s + pad, LANES), F32), pltpu.VMEM((s + pad, LANES), F32)],
        compiler_params=_params(("parallel",)),
    )(du1, proj, proj, w)
    return da, dg, dw, db.reshape(cc)


def _cln_silu_fwd(u1, g, b, *, name, ts=512):
    s, cc = u1.shape
    ts = _pick(s, ts)

    def body(u_ref, g_ref, b_ref, o_ref):
        xhat, _ = _ln_stats(u_ref[...])
        y = xhat * g_ref[...] + b_ref[...]
        o_ref[...] = (y * _sigmoid(y)).astype(BF16)

    tok = pl.BlockSpec((ts, cc), lambda i: (i, 0))
    vec = pl.BlockSpec((1, cc), lambda i: (0, 0))
    return pl.pallas_call(body, name=name, out_shape=jax.ShapeDtypeStruct((s, cc), BF16), grid=(s // ts,),
                          in_specs=[tok, vec, vec], out_specs=tok,
                          compiler_params=_params(("parallel",)))(u1, g.reshape(1, cc), b.reshape(1, cc))


def _cln_silu_bwd(dua, u1, g, b, *, name, ts=512):
    s, cc = u1.shape
    ts = _pick(s, ts)

    def body(d_ref, u_ref, g_ref, b_ref, du_ref, dg_ref, db_ref):
        xhat, rstd = _ln_stats(u_ref[...])
        y = xhat * g_ref[...] + b_ref[...]
        sg = _sigmoid(y)
        dy = d_ref[...] * (sg * (1.0 + y * (1.0 - sg)))
        dxhat = dy * g_ref[...]
        m1 = jnp.mean(dxhat, axis=-1, keepdims=True)
        m2 = jnp.mean(dxhat * xhat, axis=-1, keepdims=True)
        du_ref[...] = rstd * (dxhat - m1 - xhat * m2)

        @pl.when(pl.program_id(0) == 0)
        def _():
            dg_ref[...] = jnp.zeros_like(dg_ref)
            db_ref[...] = jnp.zeros_like(db_ref)

        dg_ref[...] += jnp.sum(dy * xhat, axis=0, keepdims=True)
        db_ref[...] += jnp.sum(dy, axis=0, keepdims=True)

    tok = pl.BlockSpec((ts, cc), lambda i: (i, 0))
    vec = pl.BlockSpec((1, cc), lambda i: (0, 0))
    du1, dg, db = pl.pallas_call(
        body, name=name,
        out_shape=(jax.ShapeDtypeStruct((s, cc), F32), jax.ShapeDtypeStruct((1, cc), F32),
                   jax.ShapeDtypeStruct((1, cc), F32)),
        grid=(s // ts,), in_specs=[tok, tok, vec, vec], out_specs=(tok, vec, vec),
        compiler_params=_params(("arbitrary",)),
    )(dua, u1, g.reshape(1, cc), b.reshape(1, cc))
    return du1, dg.reshape(cc), db.reshape(cc)


def _softplus_parts(z):
    lk = jnp.minimum(-z, 0.0) - jnp.log1p(jnp.exp(-jnp.abs(z)))
    return lk, z + lk


def _stack_heads(x, hms):
    return jnp.concatenate([jnp.where(hm, x, 0.0) for hm in hms], axis=0).astype(BF16)


def _heads_side_by_side(x_st, tq):
    return jnp.concatenate([x_st[:tq], x_st[tq:]], axis=1)


def _sb_tile_masks(tq):
    row = lax.broadcasted_iota(jnp.int32, (2 * tq, tq), 0)
    col = lax.broadcasted_iota(jnp.int32, (2 * tq, tq), 1)
    vis = col < jnp.where(row >= tq, row - tq, row)
    krow, kcol = row[:tq], col[:tq]
    return vis, (krow > kcol).astype(BF16), (krow < kcol).astype(BF16)


_NT = (((1,), (1,)), ((), ()))
_TN = (((0,), (0,)), ((), ()))


def _head_masks():
    lane = lax.broadcasted_iota(jnp.int32, (1, LANES), 1)
    return [(lane >= SB_HEAD_DIM * h) & (lane < SB_HEAD_DIM * (h + 1)) for h in range(2)]


def _hosted(exchange, grid, body):
    if exchange is None:
        return body, [], [], []
    n_in, n_out, n_sems = len(exchange.inputs), len(exchange.out_shapes), len(exchange.scratch_shapes)

    def wrapped(*refs, n_own_in, n_own_out):
        own_in, ex_in = refs[:n_own_in], refs[n_own_in:n_own_in + n_in]
        rest = refs[n_own_in + n_in:]
        own_out, ex_out = rest[:n_own_out], rest[n_own_out:n_own_out + n_out]
        own_scratch, sems = rest[n_own_out + n_out:len(rest) - n_sems], rest[len(rest) - n_sems:]
        ids = [pl.program_id(d) for d in range(len(grid))]
        first = functools.reduce(lambda x, y: x & y, [i == 0 for i in ids])
        last = functools.reduce(lambda x, y: x & y, [i == g - 1 for i, g in zip(ids, grid)])

        @pl.when(first)
        def _():
            exchange.start(ex_in, ex_out, sems)

        body(*own_in, *own_out, *own_scratch)

        @pl.when(last)
        def _():
            exchange.finish(ex_in, ex_out, sems)

    return wrapped, list(exchange.inputs), list(exchange.out_shapes), list(exchange.scratch_shapes)


def _sb_fwd(proj, *, col0, n_pairs, name, tq=256, exchange=None):
    s = proj.shape[0]
    tq = _pick(s, tq)
    nq = s // tq
    cb0 = col0 // LANES
    scale = SB_HEAD_DIM ** -0.5

    def body(q_ref, k_ref, v_ref, o_ref, t_ref, n_ref):
        i = pl.program_id(1)
        hms = _head_masks()
        vis, m_after = _sb_tile_masks(tq)[:2]
        q_st = _stack_heads(q_ref[...] * scale, hms)

        def tile(j, c, acc, masked):
            start = pl.multiple_of(j * tq, tq)
            kb = k_ref[pl.ds(start, tq), :].astype(BF16)
            v_st = _stack_heads(v_ref[pl.ds(start, tq), :], hms)
            z = lax.dot_general(q_st, kb, _NT, preferred_element_type=F32)
            lk, lb = _softplus_parts(z)
            if masked:
                lk = jnp.where(vis, lk, 0.0)
            later = jnp.dot(lk.astype(BF16), m_after, preferred_element_type=F32)
            a = jnp.exp(lb + later + c)
            if masked:
                a = jnp.where(vis, a, 0.0)
            acc = acc + jnp.dot(_heads_side_by_side(a.astype(BF16), tq), v_st, preferred_element_type=F32)
            return c + jnp.sum(lk, axis=1, keepdims=True), acc

        def more(st):
            return jnp.logical_and(st[0] < i, jnp.max(st[1]) > -EXP_UNDERFLOW)

        def step(st):
            c, acc = tile(i - 1 - st[0], st[1], st[2], False)
            return st[0] + 1, c, acc

        c, acc = tile(i, jnp.zeros((2 * tq, 1), F32), jnp.zeros((tq, LANES), F32), True)
        n, c, acc = lax.while_loop(more, step, (jnp.int32(0), c, acc))
        o_ref[...] = acc.astype(BF16)
        t_ref[...] = jnp.where(hms[0], c[:tq], c[tq:])
        n_ref[...] = jnp.zeros((8, LANES), F32) + n.astype(F32)

    grid = (n_pairs, nq)
    body, ex_in, ex_out, ex_scratch = _hosted(exchange, grid, body)
    if exchange is not None:
        body = functools.partial(body, n_own_in=3, n_own_out=3)
    hbm = pl.BlockSpec(memory_space=pl.ANY)
    seq = lambda off: pl.BlockSpec((s, LANES), lambda p, i: (0, cb0 + off + p))
    outs = pl.pallas_call(
        body, name=name,
        out_shape=(jax.ShapeDtypeStruct((s, n_pairs * LANES), BF16), jax.ShapeDtypeStruct((n_pairs, s, LANES), F32),
                   jax.ShapeDtypeStruct((n_pairs, nq * 8, LANES), F32), *ex_out),
        grid=grid,
        in_specs=[pl.BlockSpec((tq, LANES), lambda p, i: (i, cb0 + p)), seq(n_pairs), seq(2 * n_pairs)] + [hbm] * len(ex_in),
        out_specs=(pl.BlockSpec((tq, LANES), lambda p, i: (i, p)), pl.BlockSpec((None, tq, LANES), lambda p, i: (p, i, 0)),
                   pl.BlockSpec((None, 8, LANES), lambda p, i: (p, i, 0)), *([hbm] * len(ex_out))),
        scratch_shapes=ex_scratch,
        compiler_params=_params(("arbitrary", "arbitrary") if exchange else ("parallel", "arbitrary")),
    )(proj, proj, proj, *ex_in)
    return outs[0], outs[1], outs[2], list(outs[3:])


def _sb_bwd(proj, t_sum, n_walked, dua, *, col0, n_pairs, do_col0, name, tq=256, exchange=None):
    s = proj.shape[0]
    tq = _pick(s, tq)
    cb0 = col0 // LANES
    dcb0 = do_col0 // LANES
    scale = SB_HEAD_DIM ** -0.5

    def body(q_ref, k_ref, v_ref, t_ref, n_ref, do_ref, dq_ref, dk_ref, dv_ref):
        i = pl.program_id(1)

        @pl.when(i == 0)
        def _():
            dk_ref[...] = jnp.zeros_like(dk_ref)
            dv_ref[...] = jnp.zeros_like(dv_ref)

        hms = _head_masks()
        vis, m_after, m_before = _sb_tile_masks(tq)
        q_st = _stack_heads(q_ref[...] * scale, hms)
        do_st = _stack_heads(do_ref[...], hms)
        t_st = jnp.concatenate([t_ref[:, SB_HEAD_DIM * h:SB_HEAD_DIM * h + 1] for h in range(2)], axis=0)
        first = i - jnp.max(n_ref[...]).astype(jnp.int32)

        def tile(j, p_sum, g_sum, dq, masked):
            start = pl.multiple_of(j * tq, tq)
            k = k_ref[pl.ds(start, tq), :]
            z = lax.dot_general(q_st, k.astype(BF16), _NT, preferred_element_type=F32)
            lk_raw, lb = _softplus_parts(z)
            lk = jnp.where(vis, lk_raw, 0.0) if masked else lk_raw
            p_next = p_sum + jnp.sum(lk, axis=1, keepdims=True)
            later = jnp.dot(lk.astype(BF16), m_after, preferred_element_type=F32)
            a = jnp.exp(lb + (t_st - p_next) + later)
            if masked:
                a = jnp.where(vis, a, 0.0)
            da = lax.dot_general(do_st, v_ref[pl.ds(start, tq), :].astype(BF16), _NT, preferred_element_type=F32)
            g = a * da
            g_before = g_sum + jnp.dot(g.astype(BF16), m_before, preferred_element_type=F32)
            dz = g * jnp.exp(lk_raw) - g_before * jnp.exp(lb)
            if masked:
                dz = jnp.where(vis, dz, 0.0)
            dzb = dz.astype(BF16)
            dv_ref[pl.ds(start, tq), :] += lax.dot_general(a.astype(BF16), do_st, _TN, preferred_element_type=F32)
            dk_ref[pl.ds(start, tq), :] += lax.dot_general(dzb, q_st, _TN, preferred_element_type=F32)
            dq = dq + jnp.dot(_heads_side_by_side(dzb, tq), _stack_heads(k, hms), preferred_element_type=F32)
            return p_next, g_sum + jnp.sum(g, axis=1, keepdims=True), dq

        zero = jnp.zeros((2 * tq, 1), F32)
        st = lax.fori_loop(first, i, lambda j, st: tile(j, *st, False), (zero, zero, jnp.zeros((tq, LANES), F32)))
        dq_ref[...] = tile(i, *st, True)[2] * scale

    grid = (n_pairs, s // tq)
    body, ex_in, ex_out, ex_scratch = _hosted(exchange, grid, body)
    if exchange is not None:
        body = functools.partial(body, n_own_in=6, n_own_out=3)
    hbm = pl.BlockSpec(memory_space=pl.ANY)
    seq = lambda off: pl.BlockSpec((s, LANES), lambda p, i: (0, cb0 + off + p))
    out = jax.ShapeDtypeStruct((s, n_pairs * LANES), F32)
    res = pl.BlockSpec((s, LANES), lambda p, i: (0, p))
    outs = pl.pallas_call(
        body, name=name,
        out_shape=(out, out, out, *ex_out),
        grid=grid,
        in_specs=[pl.BlockSpec((tq, LANES), lambda p, i: (i, cb0 + p)), seq(n_pairs), seq(2 * n_pairs),
                  pl.BlockSpec((None, tq, LANES), lambda p, i: (p, i, 0)),
                  pl.BlockSpec((None, 8, LANES), lambda p, i: (p, i, 0)),
                  pl.BlockSpec((tq, LANES), lambda p, i: (i, dcb0 + p))] + [hbm] * len(ex_in),
        out_specs=(pl.BlockSpec((tq, LANES), lambda p, i: (i, p)), res, res, *([hbm] * len(ex_out))),
        scratch_shapes=ex_scratch,
        compiler_params=_params(("arbitrary", "arbitrary")),
    )(proj, proj, proj, t_sum, n_walked, dua, *ex_in)
    return outs[0], outs[1], outs[2], list(outs[3:])


def _mem_attn_fwd(qm, km, vm, *, name, tq=512):
    s, d = qm.shape
    heads = d // MEM_HEAD_DIM
    mlen = km.shape[0]
    tq = _pick(s, tq)
    scale = MEM_HEAD_DIM ** -0.5

    def body(q_ref, k_ref, v_ref, o_ref):
        for h in range(heads):
            sl = slice(h * MEM_HEAD_DIM, (h + 1) * MEM_HEAD_DIM)
            q = (q_ref[:, sl] * scale).astype(BF16)
            sc = lax.dot_general(q, k_ref[:, sl].astype(BF16), _NT, preferred_element_type=F32)
            e = jnp.exp(sc - jnp.max(sc, axis=1, keepdims=True))
            p = e / jnp.sum(e, axis=1, keepdims=True)
            o_ref[:, sl] = jnp.dot(p.astype(BF16), v_ref[:, sl].astype(BF16), preferred_element_type=F32).astype(BF16)

    tok = pl.BlockSpec((tq, d), lambda i: (i, 0))
    kv = pl.BlockSpec((mlen, d), lambda i: (0, 0))
    return pl.pallas_call(body, name=name, out_shape=jax.ShapeDtypeStruct((s, d), BF16), grid=(s // tq,),
                          in_specs=[tok, kv, kv], out_specs=tok, compiler_params=_params(("parallel",)))(qm, km, vm)


def _mem_attn_bwd(qm, km, vm, do, *, name, tq=512):
    s, d = qm.shape
    heads = d // MEM_HEAD_DIM
    mlen = km.shape[0]
    tq = _pick(s, tq)
    scale = MEM_HEAD_DIM ** -0.5

    def body(q_ref, k_ref, v_ref, do_ref, dq_ref, dk_ref, dv_ref):
        @pl.when(pl.program_id(0) == 0)
        def _():
            dk_ref[...] = jnp.zeros_like(dk_ref)
            dv_ref[...] = jnp.zeros_like(dv_ref)

        for h in range(heads):
            sl = slice(h * MEM_HEAD_DIM, (h + 1) * MEM_HEAD_DIM)
            q = (q_ref[:, sl] * scale).astype(BF16)
            k = k_ref[:, sl].astype(BF16)
            v = v_ref[:, sl].astype(BF16)
            sc = lax.dot_general(q, k, _NT, preferred_element_type=F32)
            e = jnp.exp(sc - jnp.max(sc, axis=1, keepdims=True))
            p = e / jnp.sum(e, axis=1, keepdims=True)
            dob = do_ref[:, sl].astype(BF16)
            dv_ref[:, sl] += lax.dot_general(p.astype(BF16), dob, _TN, preferred_element_type=F32)
            dp = lax.dot_general(dob, v, _NT, preferred_element_type=F32)
            ds = (p * (dp - jnp.sum(dp * p, axis=1, keepdims=True))).astype(BF16)
            dq_ref[:, sl] = (jnp.dot(ds, k, preferred_element_type=F32) * scale).astype(BF16)
            dk_ref[:, sl] += lax.dot_general(ds, q, _TN, preferred_element_type=F32)

    tok = pl.BlockSpec((tq, d), lambda i: (i, 0))
    kv = pl.BlockSpec((mlen, d), lambda i: (0, 0))
    return pl.pallas_call(
        body, name=name,
        out_shape=(jax.ShapeDtypeStruct((s, d), BF16), jax.ShapeDtypeStruct((mlen, d), F32),
                   jax.ShapeDtypeStruct((mlen, d), F32)),
        grid=(s // tq,), in_specs=[tok, kv, kv, tok], out_specs=(tok, kv, kv),
        compiler_params=_params(("arbitrary",)),
    )(qm, km, vm, do)


def _ffn_act_fwd(up, w, b, *, name):
    s, two_f = up.shape
    ff = two_f // 2
    nfb = ff // LANES
    kw = w.shape[0]
    pad = 8
    chunks = _row_chunks(s)

    def body(v_ref, g_ref, wv_ref, wg_ref, bv_ref, bg_ref, o_ref, vp_ref, gp_ref):
        vp_ref[pl.ds(0, pad), :] = jnp.zeros((pad, LANES), F32)
        gp_ref[pl.ds(0, pad), :] = jnp.zeros((pad, LANES), F32)
        for r0, rc in chunks:
            vp_ref[pl.ds(pad + r0, rc), :] = v_ref[pl.ds(r0, rc), :]
            gp_ref[pl.ds(pad + r0, rc), :] = g_ref[pl.ds(r0, rc), :]
        for r0, rc in chunks:
            vc = jnp.zeros((rc, LANES), F32) + bv_ref[...]
            gc = jnp.zeros((rc, LANES), F32) + bg_ref[...]
            for k in range(kw):
                off = pad + r0 - (kw - 1) + k
                vc = vc + wv_ref[pl.ds(k, 1), :] * vp_ref[pl.ds(off, rc), :]
                gc = gc + wg_ref[pl.ds(k, 1), :] * gp_ref[pl.ds(off, rc), :]
            o_ref[pl.ds(r0, rc), :] = (gc * _sigmoid(gc) * vc).astype(BF16)

    col = lambda off: pl.BlockSpec((s, LANES), lambda c: (0, off + c))
    tap = lambda off: pl.BlockSpec((kw, LANES), lambda c: (0, off + c))
    vec = lambda off: pl.BlockSpec((1, LANES), lambda c: (0, off + c))
    return pl.pallas_call(
        body, name=name, out_shape=jax.ShapeDtypeStruct((s, ff), BF16), grid=(nfb,),
        in_specs=[col(0), col(nfb), tap(0), tap(nfb), vec(0), vec(nfb)], out_specs=col(0),
        scratch_shapes=[pltpu.VMEM((s + pad, LANES), F32), pltpu.VMEM((s + pad, LANES), F32)],
        compiler_params=_params(("parallel",)),
    )(up, up, w, w, b.reshape(1, two_f), b.reshape(1, two_f))


def _ffn_act_bwd(up, dact, w, b, *, name, exchange=None):
    s, two_f = up.shape
    ff = two_f // 2
    nfb = ff // LANES
    kw = w.shape[0]
    pad = 8
    chunks = _row_chunks(s)

    def body(v_ref, g_ref, d_ref, wv_ref, wg_ref, bv_ref, bg_ref, dv_ref, dg_ref, dwv_ref, dwg_ref, dbv_ref, dbg_ref,
             vp_ref, gp_ref, dvc_ref, dgc_ref):
        vp_ref[pl.ds(0, pad), :] = jnp.zeros((pad, LANES), F32)
        gp_ref[pl.ds(0, pad), :] = jnp.zeros((pad, LANES), F32)
        dvc_ref[pl.ds(s, pad), :] = jnp.zeros((pad, LANES), F32)
        dgc_ref[pl.ds(s, pad), :] = jnp.zeros((pad, LANES), F32)
        for r0, rc in chunks:
            vp_ref[pl.ds(pad + r0, rc), :] = v_ref[pl.ds(r0, rc), :]
            gp_ref[pl.ds(pad + r0, rc), :] = g_ref[pl.ds(r0, rc), :]
        dwv = [jnp.zeros((1, LANES), F32) for _ in range(kw)]
        dwg = [jnp.zeros((1, LANES), F32) for _ in range(kw)]
        dbv = jnp.zeros((1, LANES), F32)
        dbg = jnp.zeros((1, LANES), F32)
        for r0, rc in chunks:
            vc = jnp.zeros((rc, LANES), F32) + bv_ref[...]
            gc = jnp.zeros((rc, LANES), F32) + bg_ref[...]
            for k in range(kw):
                off = pad + r0 - (kw - 1) + k
                vc = vc + wv_ref[pl.ds(k, 1), :] * vp_ref[pl.ds(off, rc), :]
                gc = gc + wg_ref[pl.ds(k, 1), :] * gp_ref[pl.ds(off, rc), :]
            sg = _sigmoid(gc)
            d = d_ref[pl.ds(r0, rc), :]
            dvc = d * (gc * sg)
            dgc = d * vc * (sg * (1.0 + gc * (1.0 - sg)))
            dvc_ref[pl.ds(r0, rc), :] = dvc
            dgc_ref[pl.ds(r0, rc), :] = dgc
            dbv = dbv + jnp.sum(dvc, axis=0, keepdims=True)
            dbg = dbg + jnp.sum(dgc, axis=0, keepdims=True)
            for k in range(kw):
                off = pad + r0 - (kw - 1) + k
                dwv[k] = dwv[k] + jnp.sum(dvc * vp_ref[pl.ds(off, rc), :], axis=0, keepdims=True)
                dwg[k] = dwg[k] + jnp.sum(dgc * gp_ref[pl.ds(off, rc), :], axis=0, keepdims=True)
        for r0, rc in chunks:
            dv = jnp.zeros((rc, LANES), F32)
            dg = jnp.zeros((rc, LANES), F32)
            for k in range(kw):
                off = r0 + (kw - 1) - k
                dv = dv + wv_ref[pl.ds(k, 1), :] * dvc_ref[pl.ds(off, rc), :]
                dg = dg + wg_ref[pl.ds(k, 1), :] * dgc_ref[pl.ds(off, rc), :]
            dv_ref[pl.ds(r0, rc), :] = dv.astype(BF16)
            dg_ref[pl.ds(r0, rc), :] = dg.astype(BF16)
        for k in range(kw):
            dwv_ref[pl.ds(k, 1), :] = dwv[k]
            dwg_ref[pl.ds(k, 1), :] = dwg[k]
        dbv_ref[...] = dbv
        dbg_ref[...] = dbg

    col = lambda off: pl.BlockSpec((s, LANES), lambda c: (0, off + c))
    tap = lambda off: pl.BlockSpec((kw, LANES), lambda c: (0, off + c))
    vec = lambda off: pl.BlockSpec((1, LANES), lambda c: (0, off + c))
    big = lambda: pltpu.VMEM((s + pad, LANES), F32)
    body, ex_in, ex_out, ex_scratch = _hosted(exchange, (nfb,), body)
    if exchange is not None:
        body = functools.partial(body, n_own_in=7, n_own_out=6)
    hbm = pl.BlockSpec(memory_space=pl.ANY)
    outs = pl.pallas_call(
        body, name=name,
        out_shape=(jax.ShapeDtypeStruct((s, ff), BF16), jax.ShapeDtypeStruct((s, ff), BF16),
                   jax.ShapeDtypeStruct((kw, ff), F32), jax.ShapeDtypeStruct((kw, ff), F32),
                   jax.ShapeDtypeStruct((1, ff), F32), jax.ShapeDtypeStruct((1, ff), F32), *ex_out),
        grid=(nfb,),
        in_specs=[col(0), col(nfb), col(0), tap(0), tap(nfb), vec(0), vec(nfb)] + [hbm] * len(ex_in),
        out_specs=(col(0), col(0), tap(0), tap(0), vec(0), vec(0), *([hbm] * len(ex_out))),
        scratch_shapes=[big(), big(), big(), big()] + ex_scratch,
        compiler_params=_params(("arbitrary",) if exchange else ("parallel",)),
    )(up, up, dact, w, w, b.reshape(1, two_f), b.reshape(1, two_f), *ex_in)
    dv, dg, dwv, dwg, dbv, dbg = outs[:6]
    return (jnp.concatenate([dv, dg], axis=1), jnp.concatenate([dwv, dwg], axis=1),
            jnp.concatenate([dbv, dbg], axis=1).reshape(two_f), list(outs[6:]))


def _sum_parts(parts, *, name, tr=256):
    n_parts, rows, cols = parts.shape
    tr = _pick(rows, tr, 16)

    def body(p_ref, o_ref):
        g = p_ref[0].astype(F32)
        for k in range(1, n_parts):
            g = g + p_ref[k].astype(F32)
        o_ref[...] = g

    return pl.pallas_call(
        body, name=name, out_shape=jax.ShapeDtypeStruct((rows, cols), F32), grid=(rows // tr,),
        in_specs=[pl.BlockSpec((n_parts, tr, cols), lambda i: (0, i, 0))],
        out_specs=pl.BlockSpec((tr, cols), lambda i: (i, 0)), compiler_params=_params(("parallel",)),
    )(parts)


def _sum_adamw(parts, w, m, v, *, layer, so_far, name, tr=256):
    n_parts, rows, cols = parts.shape
    depth = w.shape[0]
    tr = _pick(rows, tr, 16)
    c1 = 1.0 / (1.0 - ADAM_B1 ** ADAM_STEP)
    c2 = 1.0 / (1.0 - ADAM_B2 ** ADAM_STEP)

    def body(p_ref, w_ref, m_ref, v_ref, *rest):
        g_ref, d_ref, nm_ref, nv_ref = rest[-4:]
        g = p_ref[0].astype(F32)
        for k in range(1, n_parts):
            g = g + p_ref[k].astype(F32)
        nm = ADAM_B1 * m_ref[...] + (1.0 - ADAM_B1) * g
        nv = ADAM_B2 * v_ref[...] + (1.0 - ADAM_B2) * (g * g)
        g_ref[...] = g
        nm_ref[...] = nm
        nv_ref[...] = nv
        d_ref[...] = -ADAM_LR * ((nm * c1) / (jnp.sqrt(nv * c2) + ADAM_EPS) + ADAM_WD * w_ref[...])

    blk = pl.BlockSpec((None, tr, cols), lambda i: (layer, i, 0))
    out = jax.ShapeDtypeStruct((depth, rows, cols), F32)
    kept = list(so_far) if so_far is not None else []
    return pl.pallas_call(
        body, name=name, out_shape=(out, out, out, out), grid=(rows // tr,),
        in_specs=[pl.BlockSpec((n_parts, tr, cols), lambda i: (0, i, 0)), blk, blk, blk]
        + [pl.BlockSpec(memory_space=pl.ANY)] * len(kept),
        out_specs=(blk, blk, blk, blk),
        input_output_aliases={4 + k: k for k in range(len(kept))},
        compiler_params=_params(("parallel",)),
    )(parts, w, m, v, *kept)


def _mesh_pos():
    return lax.axis_index("x"), lax.axis_index("y"), lax.axis_index("c")


def _flip(pos, k):
    x, y, c = pos
    return (1 - x if k & 4 else x, 1 - y if k & 2 else y, 1 - c if k & 1 else c)


def _dev_index(pos):
    return 4 * pos[0] + 2 * pos[1] + pos[2]


N_PEERS = N_DEV - 1


class _Exchange:
    def __init__(self, inputs, out_shapes, start, finish):
        n = len(inputs)
        self.inputs, self.out_shapes, self.start, self.finish = list(inputs), list(out_shapes), start, finish
        self.scratch_shapes = [pltpu.SemaphoreType.DMA((n * N_PEERS,)), pltpu.SemaphoreType.DMA((n * N_PEERS,)),
                               pltpu.SemaphoreType.DMA((n,))]


def _gather_exchange(xs):
    n = len(xs)

    def plan(x_refs, out_refs, sems):
        send_sems, recv_sems, local_sems = sems
        me = _mesh_pos()
        sibling = _flip(me, 1)
        chips = [_flip(me, 4), _flip(me, 2), _flip(me, 6)]

        def copy(a, k, block, to, from_input=False):
            slot = out_refs[a].at[_dev_index(block)]
            return pltpu.make_async_remote_copy(
                src_ref=x_refs[a] if from_input else slot, dst_ref=slot,
                send_sem=send_sems.at[a * N_PEERS + k], recv_sem=recv_sems.at[a * N_PEERS + k],
                device_id=to, device_id_type=pl.DeviceIdType.MESH)

        mine = [pltpu.make_async_copy(x_refs[a], out_refs[a].at[_dev_index(me)], local_sems.at[a]) for a in range(n)]
        first = [copy(a, 0, me, sibling, True) for a in range(n)]
        first += [copy(a, 1 + j, me, chip, True) for j, chip in enumerate(chips) for a in range(n)]
        return me, sibling, chips, copy, mine, first

    def start(x_refs, out_refs, sems):
        _, _, _, _, mine, first = plan(x_refs, out_refs, sems)
        for cp in mine + first:
            cp.start()

    def finish(x_refs, out_refs, sems):
        me, sibling, chips, copy, mine, first = plan(x_refs, out_refs, sems)
        passed = []
        for j, chip in enumerate(chips):
            for a in range(n):
                copy(a, 1 + j, chip, me).wait_recv()
                passed.append(copy(a, 4 + j, chip, sibling))
                passed[-1].start()
        for a in range(n):
            copy(a, 0, sibling, me).wait_recv()
        for j, chip in enumerate(chips):
            for a in range(n):
                copy(a, 4 + j, _flip(chip, 1), me).wait_recv()
        for cp in first + passed:
            cp.wait_send()
        for cp in mine:
            cp.wait()

    return _Exchange(xs, [jax.ShapeDtypeStruct((N_DEV,) + x.shape, x.dtype) for x in xs], start, finish)


def _scatter_exchange(xs):
    n = len(xs)

    def plan(x_refs, out_refs, sems):
        send_sems, recv_sems, local_sems = sems
        me = _mesh_pos()
        my_slot = _dev_index(me)

        def copy(a, k):
            peer = _flip(me, k)
            return pltpu.make_async_remote_copy(
                src_ref=x_refs[a].at[_dev_index(peer)], dst_ref=out_refs[a].at[my_slot],
                send_sem=send_sems.at[a * N_PEERS + k - 1], recv_sem=recv_sems.at[a * N_PEERS + k - 1],
                device_id=peer, device_id_type=pl.DeviceIdType.MESH)

        mine = [pltpu.make_async_copy(x_refs[a].at[my_slot], out_refs[a].at[my_slot], local_sems.at[a]) for a in range(n)]
        return mine, [copy(a, k) for k in range(1, N_DEV) for a in range(n)]

    def start(x_refs, out_refs, sems):
        mine, copies = plan(x_refs, out_refs, sems)
        for cp in mine + copies:
            cp.start()

    def finish(x_refs, out_refs, sems):
        mine, copies = plan(x_refs, out_refs, sems)
        for cp in copies:
            cp.wait_recv()
        for cp in copies:
            cp.wait_send()
        for cp in mine:
            cp.wait()

    return _Exchange(xs, [jax.ShapeDtypeStruct(x.shape, x.dtype) for x in xs], start, finish)


def _run_exchange(ex, *, name):
    n = len(ex.inputs)

    def body(*refs):
        ex.start(refs[:n], refs[n:2 * n], refs[2 * n:])
        ex.finish(refs[:n], refs[n:2 * n], refs[2 * n:])

    hbm = pl.BlockSpec(memory_space=pl.ANY)
    return pl.pallas_call(body, name=name, out_shape=tuple(ex.out_shapes), in_specs=[hbm] * n,
                          out_specs=tuple([hbm] * n), scratch_shapes=ex.scratch_shapes)(*ex.inputs)


def _pack(arrays):
    flat = jnp.concatenate([a.reshape(-1) for a in arrays])
    n = flat.shape[0]
    tile = PACK_W * PACK_ROW_ALIGN
    total = -(-n // tile) * tile
    return jnp.pad(flat, (0, total - n)).reshape(total // PACK_W, PACK_W)


def _unpack(buf, shapes):
    lead = buf.shape[:-2]
    flat = buf.reshape(lead + (-1,))
    out, off = [], 0
    for shp in shapes:
        n = 1
        for dim in shp:
            n *= dim
        out.append(flat[..., off:off + n].reshape(lead + tuple(shp)))
        off += n
    return out


def _join_columns(blocks):
    return jnp.moveaxis(blocks, 0, 2).reshape(blocks.shape[1], blocks.shape[2], -1)


def _mm_hosting(a, b, exchange, **kw):
    if exchange is None:
        return _mm(a, b, **kw), []
    return _mm(a, b, exchange=exchange, **kw)


GATHERED_BY_ATTENTION = ['w_out', 'mem_wq', 'mem_wk', 'mem_wv', 'mem_wo', 'ffn_up', 'ffn_down']


def _as_matrix(blocks):
    return blocks.reshape(-1, blocks.shape[-1])


def _layer_fwd(x, xb, memb, w, alpha, shards, next_w_in):
    w = dict(w)
    cc = w['conv_w'].shape[1]
    n_pairs = (N_DEV * shards['w_out'].shape[0] - cc) // LANES

    proj = _mm(xb, w['w_in'], tb=True, name="mm_proj")
    u1 = _glu_conv_fwd(proj, w['conv_w'], w['conv_b'], cc=cc, name="glu_conv_fwd")
    u = _cln_silu_fwd(u1, w['conv_ln_g'], w['conv_ln_b'], name="cln_silu_fwd")
    att, t_sum, n_walked, got = _sb_fwd(proj, col0=2 * cc, n_pairs=n_pairs, name="sb_fwd",
                                        exchange=_gather_exchange([shards[n] for n in GATHERED_BY_ATTENTION]))
    w.update({n: _as_matrix(f) for n, f in zip(GATHERED_BY_ATTENTION, got)})
    ua = jnp.concatenate([u, att], axis=1)
    r1, x1, x1b = _mm_res_ln(ua, w['w_out'], x, w['ln1_g'], w['ln1_b'], alpha=alpha, name="mm_mix_ln")
    qm = _mm(x1b, w['mem_wq'], name="mm_memq")
    km = _mm(memb, w['mem_wk'], name="mm_memkv")
    vm = _mm(memb, w['mem_wv'], name="mm_memkv")
    o = _mem_attn_fwd(qm, km, vm, name="mem_attn_fwd")
    r2, x2, x2b = _mm_res_ln(o, w['mem_wo'], x1, w['ln2_g'], w['ln2_b'], alpha=alpha, name="mm_mix_ln")
    up, got = _mm_hosting(x2b, w['ffn_up'], _gather_exchange([next_w_in]) if next_w_in is not None else None,
                          tb=True, name="mm_up")
    act = _ffn_act_fwd(up, w['ffn_conv_w'], w['ffn_conv_b'], name="ffn_act_fwd")
    r3, x3, x3b = _mm_res_ln(act, w['ffn_down'], x2, w['ln3_g'], w['ln3_b'], alpha=alpha, name="mm_down_ln")
    saved = dict(xb=xb, proj=proj, u1=u1, t_sum=t_sum, n_walked=n_walked, ua=ua, r1=r1, x1b=x1b, qm=qm, km=km, vm=vm,
                 o=o, r2=r2, x2b=x2b, up=up, act=act, r3=r3)
    return x3, x3b, saved, w, _as_matrix(got[0]) if got else None


def _layer_bwd(top, sv, memb, w, alpha, carried):
    g, received = {}, {}
    cc = w['conv_w'].shape[1]
    n_pairs = (w['w_out'].shape[0] - cc) // LANES

    def sending(sends):
        ex = _scatter_exchange([gm.reshape(N_DEV, -1, gm.shape[-1]) for _, gm in sends]) if sends else None
        return ex, lambda got: received.update({key: blocks for (key, _), blocks in zip(sends, got)})

    if top[1] is None:
        dr3, dr3b, g['ln3_g'], g['ln3_b'] = _ln_bwd(top[0], None, sv['r3'], w['ln3_g'], alpha=alpha, name="ln_bwd")
    else:
        dr3, dr3b, g['ln3_g'], g['ln3_b'] = _mm_ln_bwd(*top, sv['r3'], w['ln3_g'], alpha=alpha, name="mm_dx_in_ln")
    g_down = _mm(sv['act'], dr3b, ta=True, out_dtype=BF16, name="mm_dw_down")
    dact = _mm(dr3b, w['ffn_down'], tb=True, name="mm_dact")
    ex, file = sending(carried + [('ffn_down', g_down)])
    dup, g['ffn_conv_w'], g['ffn_conv_b'], got = _ffn_act_bwd(sv['up'], dact, w['ffn_conv_w'], w['ffn_conv_b'],
                                                                name="ffn_act_bwd", exchange=ex)
    file(got)
    g_up = _mm(dup, sv['x2b'], ta=True, out_dtype=BF16, name="mm_dw_up")
    dr2, dr2b, g['ln2_g'], g['ln2_b'] = _mm_ln_bwd(dup, w['ffn_up'], dr3, sv['r2'], w['ln2_g'], alpha=alpha, name="mm_dx_up_ln")
    g_wo = _mm(sv['o'], dr2b, ta=True, out_dtype=BF16, name="mm_dw_sq")
    do = _mm(dr2b, w['mem_wo'], tb=True, out_dtype=BF16, name="mm_dx_sq")
    dqm, dkm, dvm = _mem_attn_bwd(sv['qm'], sv['km'], sv['vm'], do, name="mem_attn_bwd")
    g_wq = _mm(sv['x1b'], dqm, ta=True, out_dtype=BF16, name="mm_dw_sq")
    g_wk = _mm(memb, dkm, ta=True, out_dtype=BF16, name="mm_dw_memkv")
    g_wv = _mm(memb, dvm, ta=True, out_dtype=BF16, name="mm_dw_memkv")
    dr1, dr1b, g['ln1_g'], g['ln1_b'] = _mm_ln_bwd(dqm, w['mem_wq'], dr2, sv['r1'], w['ln1_g'], tb=True, alpha=alpha,
                                                    name="mm_dx_sq_ln")
    g_out = _mm(sv['ua'], dr1b, ta=True, out_dtype=BF16, name="mm_dw_sq")
    dua = _mm(dr1b, w['w_out'], tb=True, name="mm_dx_sq")
    du1, g['conv_ln_g'], g['conv_ln_b'] = _cln_silu_bwd(dua, sv['u1'], w['conv_ln_g'], w['conv_ln_b'], name="cln_silu_bwd")
    dga, dgg, g['conv_w'], g['conv_b'] = _glu_conv_bwd(du1, sv['proj'], w['conv_w'], cc=cc, name="glu_conv_bwd")
    ex, file = sending([('ffn_up', g_up), ('mem_wo', g_wo), ('mem_wq', g_wq), ('mem_wk', g_wk), ('mem_wv', g_wv), ('w_out', g_out)])
    dq, dk, dv, got = _sb_bwd(sv['proj'], sv['t_sum'], sv['n_walked'], dua, col0=2 * cc, n_pairs=n_pairs, do_col0=cc,
                              name="sb_bwd", exchange=ex)
    file(got)
    dproj = jnp.concatenate([dga, dgg, dq.astype(BF16), dk.astype(BF16), dv.astype(BF16)], axis=1)
    g_in = _mm(dproj, sv['xb'], ta=True, out_dtype=BF16, name="mm_dw_in")
    return (dproj, w['w_in'], dr1), g, received, [('w_in', g_in)]


def kernel(x, mem, w_in, conv_w, conv_b, conv_ln_g, conv_ln_b, w_out, ln1_g, ln1_b, mem_wq, mem_wk, mem_wv, mem_wo, ln2_g, ln2_b, ffn_up, ffn_conv_w, ffn_conv_b, ffn_down, ln3_g, ln3_b, loss_target, m_w_in, m_conv_w, m_conv_b, m_conv_ln_g, m_conv_ln_b, m_w_out, m_ln1_g, m_ln1_b, m_mem_wq, m_mem_wk, m_mem_wv, m_mem_wo, m_ln2_g, m_ln2_b, m_ffn_up, m_ffn_conv_w, m_ffn_conv_b, m_ffn_down, m_ln3_g, m_ln3_b, v_w_in, v_conv_w, v_conv_b, v_conv_ln_g, v_conv_ln_b, v_w_out, v_ln1_g, v_ln1_b, v_mem_wq, v_mem_wk, v_mem_wv, v_mem_wo, v_ln2_g, v_ln2_b, v_ffn_up, v_ffn_conv_w, v_ffn_conv_b, v_ffn_down, v_ln3_g, v_ln3_b):
    wts = dict(zip(WEIGHTS, (w_in, conv_w, conv_b, conv_ln_g, conv_ln_b, w_out, ln1_g, ln1_b, mem_wq, mem_wk, mem_wv,
                             mem_wo, ln2_g, ln2_b, ffn_up, ffn_conv_w, ffn_conv_b, ffn_down, ln3_g, ln3_b)))
    mom = dict(zip(WEIGHTS, (m_w_in, m_conv_w, m_conv_b, m_conv_ln_g, m_conv_ln_b, m_w_out, m_ln1_g, m_ln1_b, m_mem_wq,
                             m_mem_wk, m_mem_wv, m_mem_wo, m_ln2_g, m_ln2_b, m_ffn_up, m_ffn_conv_w, m_ffn_conv_b,
                             m_ffn_down, m_ln3_g, m_ln3_b)))
    var = dict(zip(WEIGHTS, (v_w_in, v_conv_w, v_conv_b, v_conv_ln_g, v_conv_ln_b, v_w_out, v_ln1_g, v_ln1_b, v_mem_wq,
                             v_mem_wk, v_mem_wv, v_mem_wo, v_ln2_g, v_ln2_b, v_ffn_up, v_ffn_conv_w, v_ffn_conv_b,
                             v_ffn_down, v_ln3_g, v_ln3_b)))
    depth = w_in.shape[0]
    alpha = (2.0 * depth) ** 0.25
    my_index = _dev_index(_mesh_pos())

    def row_blocks(src, n, col_sharded):
        return jnp.swapaxes(src[n], 1, 2) if col_sharded else src[n]

    tap_shapes = [wts[n].shape for n in TAPS]
    gathered_taps, = _run_exchange(_gather_exchange([_pack([wts[n] for n in TAPS])]), name="gather_taps")
    full_taps = {n: _join_columns(t) for n, t in zip(TAPS, _unpack(gathered_taps, tap_shapes))}

    bf16_blocks = {n: row_blocks(wts, n, cs).astype(BF16) for n, cs in MATRICES}
    shards = [{n: bf16_blocks[n][l] for n, _ in MATRICES} for l in range(depth)]
    got, = _run_exchange(_gather_exchange([shards[0]['w_in']]), name="gather_w_in")
    full_w_in = _as_matrix(got)

    xs = x[0]
    memb = mem[0].astype(BF16)
    h, hb = xs, xs.astype(BF16)
    saved, weights = [], []
    for l in range(depth):
        w = {'w_in': full_w_in}
        w.update({n: full_taps[n][l] for n in TAPS})
        w.update({n: wts[n][l] for n in REPLICATED})
        h, hb, sv, w, full_w_in = _layer_fwd(h, hb, memb, w, alpha, shards[l],
                                             shards[l + 1]['w_in'] if l + 1 < depth else None)
        saved.append(sv)
        weights.append(w)

    dy, loss_row = _loss_and_grad(h, loss_target[0], name="loss")
    loss = lax.psum(_row_sum(loss_row, name="loss_sum")[0, 0], ("x", "y", "c"))

    results = {}
    col_sharded = dict(MATRICES)
    state = {n: [row_blocks(src, n, cs) for src in (wts, mom, var)] for n, cs in MATRICES}

    def update(n, l, parts):
        results[n] = _sum_adamw(parts, *state[n], layer=l, so_far=results.get(n), name="adamw_matrix")

    top = (dy, None, None)
    grads = [None] * depth
    carried = []
    for l in reversed(range(depth)):
        top, grads[l], received, left = _layer_bwd(top, saved[l], memb, weights[l], alpha, carried)
        for n, parts in received.items():
            update(n, l + 1 if n in dict(carried) else l, parts)
        carried = left
    last = _run_exchange(_scatter_exchange([gm.reshape(N_DEV, -1, gm.shape[-1]) for _, gm in carried]), name="scatter_grads")
    for (n, _), parts in zip(carried, last):
        update(n, 0, parts)
    grad_x = _axpy(_mm(top[0], top[1], name="mm_dx_in"), top[2], alpha=alpha, name="grad_x")[None]
    for n, cs in MATRICES:
        if cs:
            results[n] = [jnp.swapaxes(r, 1, 2) for r in results[n]]

    small = REPLICATED + TAPS
    part = _pack([jnp.stack([grads[l][n] for l in range(depth)]) for n in small])
    parts, = _run_exchange(_gather_exchange([part]), name="gather_small_grads")
    total = _sum_parts(parts, name="sum_small_grads")
    summed = dict(zip(small, _unpack(total, [wts[n].shape for n in REPLICATED] + [full_taps[n].shape for n in TAPS])))
    for n in TAPS:
        cols = wts[n].shape[-1]
        summed[n] = lax.dynamic_slice_in_dim(summed[n], my_index * cols, cols, axis=2)
    res = _sum_adamw(_pack([summed[n] for n in small])[None], *[_pack([src[n] for n in small])[None] for src in (wts, mom, var)],
                     layer=0, so_far=None, name="adamw_small")
    unpacked = [_unpack(r[0], [wts[n].shape for n in small]) for r in res]
    for i, n in enumerate(small):
        results[n] = [u[i] for u in unpacked]

    outs = [loss, grad_x]
    for kind in range(4):
        outs += [results[n][kind] for n in WEIGHTS]
    return tuple(outs)
```

```python
import functools

import jax
import jax.numpy as jnp
from jax import lax
from jax.experimental import pallas as pl
from jax.experimental.pallas import tpu as pltpu

F32 = jnp.float32
BF16 = jnp.bfloat16

N_DEV = 8
LANES = 128
PACK_W = 1024
PACK_ROW_ALIGN = 16
SB_HEAD_DIM = 64
MEM_HEAD_DIM = 256
LN_EPS = 1e-5
EXP_UNDERFLOW = 104.0
VMEM_LIMIT = 56 * 1024 * 1024

ADAM_LR = 0.001
ADAM_B1 = 0.9
ADAM_B2 = 0.999
ADAM_EPS = 1e-08
ADAM_WD = 0.01
ADAM_STEP = 10

IN_NAMES = ['x', 'mem', 'w_in', 'conv_w', 'conv_b', 'conv_ln_g', 'conv_ln_b', 'w_out', 'ln1_g', 'ln1_b',
            'mem_wq', 'mem_wk', 'mem_wv', 'mem_wo', 'ln2_g', 'ln2_b', 'ffn_up', 'ffn_conv_w', 'ffn_conv_b',
            'ffn_down', 'ln3_g', 'ln3_b']
WEIGHTS = IN_NAMES[2:]
MATRICES = [('w_in', True), ('w_out', False), ('mem_wq', False), ('mem_wk', False), ('mem_wv', False),
            ('mem_wo', False), ('ffn_up', True), ('ffn_down', False)]
TAPS = ['conv_w', 'ffn_conv_w']
REPLICATED = ['conv_b', 'conv_ln_g', 'conv_ln_b', 'ln1_g', 'ln1_b', 'ln2_g', 'ln2_b', 'ffn_conv_b', 'ln3_g', 'ln3_b']


def _pick(dim, pref, align=LANES):
    if dim <= pref:
        return dim
    fits = [t for t in range(align, pref + 1, align) if dim % t == 0]
    return fits[-1] if fits else dim


def _params(sem):
    return pltpu.CompilerParams(dimension_semantics=sem, vmem_limit_bytes=VMEM_LIMIT)


def _mm(a, b, *, ta=False, tb=False, out_dtype=F32, name, exchange=None):
    if ta:
        kdim, m = a.shape
        tm, tn, tk = _pick(m, 1408), _pick(b.shape[0 if tb else 1], 1024), _pick(kdim, 1024)
    else:
        m, kdim = a.shape
        tm, tn, tk = _pick(m, 512), _pick(b.shape[0 if tb else 1], 1536), _pick(kdim, 2816)
    if tb:
        n, kb = b.shape
    else:
        kb, n = b.shape
    assert kdim == kb, (a.shape, b.shape, ta, tb)
    grid = (m // tm, n // tn, kdim // tk)
    nk = grid[2]
    dims = (((0 if ta else 1,), (1 if tb else 0,)), ((), ()))
    n_ex_in = len(exchange.inputs) if exchange else 0
    n_ex_out = len(exchange.out_shapes) if exchange else 0

    def body(*refs):
        a_ref, b_ref = refs[:2]
        ex_in = refs[2:2 + n_ex_in]
        o_ref = refs[2 + n_ex_in]
        ex_out = refs[3 + n_ex_in:3 + n_ex_in + n_ex_out]
        scratch = refs[3 + n_ex_in + n_ex_out:]
        if nk > 1:
            acc_ref, scratch = scratch[0], scratch[1:]
        ids = [pl.program_id(d) for d in range(3)]
        if exchange:
            @pl.when((ids[0] == 0) & (ids[1] == 0) & (ids[2] == 0))
            def _():
                exchange.start(ex_in, ex_out, scratch)

        prod = lax.dot_general(a_ref[...].astype(BF16), b_ref[...].astype(BF16), dims,
                               preferred_element_type=F32)
        if nk == 1:
            o_ref[...] = prod.astype(out_dtype)
        else:
            k = ids[2]

            @pl.when(k == 0)
            def _():
                acc_ref[...] = prod

            @pl.when(k > 0)
            def _():
                acc_ref[...] += prod

            @pl.when(k == nk - 1)
            def _():
                o_ref[...] = acc_ref[...].astype(out_dtype)

        if exchange:
            @pl.when((ids[0] == grid[0] - 1) & (ids[1] == grid[1] - 1) & (ids[2] == grid[2] - 1))
            def _():
                exchange.finish(ex_in, ex_out, scratch)

    a_spec = pl.BlockSpec((tk, tm), lambda i, j, k: (k, i)) if ta else pl.BlockSpec((tm, tk), lambda i, j, k: (i, k))
    b_spec = pl.BlockSpec((tn, tk), lambda i, j, k: (j, k)) if tb else pl.BlockSpec((tk, tn), lambda i, j, k: (k, j))
    hbm = pl.BlockSpec(memory_space=pl.ANY)
    outs = pl.pallas_call(
        body, name=name,
        out_shape=(jax.ShapeDtypeStruct((m, n), out_dtype),) + tuple(exchange.out_shapes if exchange else ()),
        grid=grid,
        in_specs=[a_spec, b_spec] + [hbm] * n_ex_in,
        out_specs=(pl.BlockSpec((tm, tn), lambda i, j, k: (i, j)),) + (hbm,) * n_ex_out,
        scratch_shapes=([] if nk == 1 else [pltpu.VMEM((tm, tn), F32)]) + list(exchange.scratch_shapes if exchange else []),
        compiler_params=_params(("arbitrary",) * 3 if exchange else ("parallel", "parallel", "arbitrary")),
    )(a, b, *(exchange.inputs if exchange else ()))
    return (outs[0], list(outs[1:])) if exchange else outs[0]


def _ln_stats(r):
    mu = jnp.mean(r, axis=-1, keepdims=True)
    xc = r - mu
    var = jnp.mean(xc * xc, axis=-1, keepdims=True)
    rstd = lax.rsqrt(var + LN_EPS)
    return xc * rstd, rstd


def _ln_bwd(da, dres, r, g, *, alpha, name, ts=512):
    s, d = r.shape
    ts = _pick(s, ts)
    has_res = dres is not None

    def body(*refs):
        if has_res:
            da_ref, dres_ref, r_ref, g_ref, dr_ref, drb_ref, dg_ref, db_ref = refs
            dy = da_ref[...] + alpha * dres_ref[...]
        else:
            da_ref, r_ref, g_ref, dr_ref, drb_ref, dg_ref, db_ref = refs
            dy = da_ref[...]
        xhat, rstd = _ln_stats(r_ref[...])
        dxhat = dy * g_ref[...]
        m1 = jnp.mean(dxhat, axis=-1, keepdims=True)
        m2 = jnp.mean(dxhat * xhat, axis=-1, keepdims=True)
        dr = rstd * (dxhat - m1 - xhat * m2)
        dr_ref[...] = dr
        drb_ref[...] = dr.astype(BF16)

        @pl.when(pl.program_id(0) == 0)
        def _():
            dg_ref[...] = jnp.zeros_like(dg_ref)
            db_ref[...] = jnp.zeros_like(db_ref)

        dg_ref[...] += jnp.sum(dy * xhat, axis=0, keepdims=True)
        db_ref[...] += jnp.sum(dy, axis=0, keepdims=True)

    tok = pl.BlockSpec((ts, d), lambda i: (i, 0))
    vec = pl.BlockSpec((1, d), lambda i: (0, 0))
    ins = [da, dres, r, g.reshape(1, d)] if has_res else [da, r, g.reshape(1, d)]
    dr, drb, dg, db = pl.pallas_call(
        body, name=name,
        out_shape=(jax.ShapeDtypeStruct((s, d), F32), jax.ShapeDtypeStruct((s, d), BF16),
                   jax.ShapeDtypeStruct((1, d), F32), jax.ShapeDtypeStruct((1, d), F32)),
        grid=(s // ts,), in_specs=[tok] * (len(ins) - 1) + [vec], out_specs=(tok, tok, vec, vec),
        compiler_params=_params(("arbitrary",)),
    )(*ins)
    return dr, drb, dg.reshape(d), db.reshape(d)


def _mm_fused(a, b, fn, *, rows, vecs, out_dtypes, n_sums, tb=False, name, tm=512):
    m, kdim = a.shape
    n = b.shape[0] if tb else b.shape[1]
    assert kdim == (b.shape[1] if tb else b.shape[0])
    tm, tk = _pick(m, tm), _pick(kdim, 2816)
    nk = kdim // tk
    dims = (((1,), (1 if tb else 0,)), ((), ()))
    n_rows, n_vecs, n_outs = len(rows), len(vecs), len(out_dtypes)

    def body(*refs):
        a_ref, b_ref = refs[:2]
        row_refs = refs[2:2 + n_rows]
        vec_refs = refs[2 + n_rows:2 + n_rows + n_vecs]
        out_refs = refs[2 + n_rows + n_vecs:2 + n_rows + n_vecs + n_outs]
        sum_refs = refs[2 + n_rows + n_vecs + n_outs:2 + n_rows + n_vecs + n_outs + n_sums]
        i, k = pl.program_id(0), pl.program_id(1)

        def finish(product):
            res = fn(product, *[r[...] for r in row_refs], *[v[...] for v in vec_refs])
            for o_ref, o in zip(out_refs, res[:n_outs]):
                o_ref[...] = o.astype(o_ref.dtype)
            if n_sums:
                @pl.when(i == 0)
                def _():
                    for s_ref in sum_refs:
                        s_ref[...] = jnp.zeros_like(s_ref)

                for s_ref, part in zip(sum_refs, res[n_outs:]):
                    s_ref[...] += part

        prod = lax.dot_general(a_ref[...].astype(BF16), b_ref[...].astype(BF16), dims, preferred_element_type=F32)
        if nk == 1:
            finish(prod)
        else:
            acc_ref = refs[-1]

            @pl.when(k == 0)
            def _():
                acc_ref[...] = prod

            @pl.when((k > 0) & (k < nk - 1))
            def _():
                acc_ref[...] += prod

            @pl.when(k == nk - 1)
            def _():
                finish(acc_ref[...] + prod)

    tok = pl.BlockSpec((tm, n), lambda i, k: (i, 0))
    vec = pl.BlockSpec((1, n), lambda i, k: (0, 0))
    b_spec = pl.BlockSpec((n, tk), lambda i, k: (0, k)) if tb else pl.BlockSpec((tk, n), lambda i, k: (k, 0))
    return pl.pallas_call(
        body, name=name,
        out_shape=tuple(jax.ShapeDtypeStruct((m, n), dt) for dt in out_dtypes) + (jax.ShapeDtypeStruct((1, n), F32),) * n_sums,
        grid=(m // tm, nk),
        in_specs=[pl.BlockSpec((tm, tk), lambda i, k: (i, k)), b_spec] + [tok] * n_rows + [vec] * n_vecs,
        out_specs=(tok,) * n_outs + (vec,) * n_sums,
        scratch_shapes=[] if nk == 1 else [pltpu.VMEM((tm, n), F32)],
        compiler_params=_params(("arbitrary", "arbitrary") if n_sums else ("parallel", "arbitrary")),
    )(a, b, *rows, *[v.reshape(1, n) for v in vecs])


def _mm_res_ln(a, b, x, g, beta, *, alpha, name):
    def fn(f, x_t, g_t, b_t):
        r = alpha * x_t + f
        xhat, _ = _ln_stats(r)
        y = xhat * g_t + b_t
        return r, y, y

    return _mm_fused(a, b, fn, rows=[x], vecs=[g, beta], out_dtypes=[F32, F32, BF16], n_sums=0, name=name)


def _mm_ln_bwd(a, b, dres, r, g, *, tb=False, alpha, name):
    def fn(f, dres_t, r_t, g_t):
        dy = f + alpha * dres_t
        xhat, rstd = _ln_stats(r_t)
        dxhat = dy * g_t
        m1 = jnp.mean(dxhat, axis=-1, keepdims=True)
        m2 = jnp.mean(dxhat * xhat, axis=-1, keepdims=True)
        dr = rstd * (dxhat - m1 - xhat * m2)
        return dr, dr, jnp.sum(dy * xhat, axis=0, keepdims=True), jnp.sum(dy, axis=0, keepdims=True)

    dr, drb, dg, db = _mm_fused(a, b, fn, rows=[dres, r], vecs=[g], out_dtypes=[F32, BF16], n_sums=2, tb=tb, name=name)
    return dr, drb, dg.reshape(-1), db.reshape(-1)


def _axpy(a, b, *, alpha, name, ts=512):
    s, d = a.shape
    ts = _pick(s, ts)

    def body(a_ref, b_ref, o_ref):
        o_ref[...] = a_ref[...] + alpha * b_ref[...]

    tok = pl.BlockSpec((ts, d), lambda i: (i, 0))
    return pl.pallas_call(body, name=name, out_shape=jax.ShapeDtypeStruct((s, d), F32), grid=(s // ts,),
                          in_specs=[tok, tok], out_specs=tok, compiler_params=_params(("parallel",)))(a, b)


def _loss_and_grad(y, target, *, name, ts=512):
    s, d = y.shape
    ts = _pick(s, ts)
    inv_d = 1.0 / d

    def body(y_ref, t_ref, dy_ref, loss_ref):
        e = y_ref[...] - t_ref[...]
        dy_ref[...] = e * inv_d

        @pl.when(pl.program_id(0) == 0)
        def _():
            loss_ref[...] = jnp.zeros_like(loss_ref)

        loss_ref[...] += jnp.sum(e * e, axis=0, keepdims=True) * (0.5 * inv_d)

    tok = pl.BlockSpec((ts, d), lambda i: (i, 0))
    vec = pl.BlockSpec((1, d), lambda i: (0, 0))
    dy, part = pl.pallas_call(
        body, name=name,
        out_shape=(jax.ShapeDtypeStruct((s, d), F32), jax.ShapeDtypeStruct((1, d), F32)),
        grid=(s // ts,), in_specs=[tok, tok], out_specs=(tok, vec),
        compiler_params=_params(("arbitrary",)),
    )(y, target)
    return dy, part


def _row_sum(v, *, name):
    def body(v_ref, o_ref):
        o_ref[...] = jnp.sum(v_ref[...], axis=1, keepdims=True)

    return pl.pallas_call(body, name=name, out_shape=jax.ShapeDtypeStruct((1, 1), F32))(v)


def _sigmoid(x):
    return 1.0 / (1.0 + jnp.exp(-x))


def _row_chunks(s, pref=512):
    c = _pick(s, pref, 8)
    return [(i * c, c) for i in range(s // c)]


def _glu_conv_fwd(proj, w, b, *, cc, name):
    s = proj.shape[0]
    kw = w.shape[0]
    pad = 32
    assert kw - 1 <= pad
    ncb = cc // LANES
    chunks = _row_chunks(s)

    def body(a_ref, g_ref, w_ref, b_ref, o_ref, u0_ref):
        u0_ref[pl.ds(0, pad), :] = jnp.zeros((pad, LANES), F32)
        for r0, rc in chunks:
            u0_ref[pl.ds(pad + r0, rc), :] = a_ref[pl.ds(r0, rc), :] * _sigmoid(g_ref[pl.ds(r0, rc), :])
        for r0, rc in chunks:
            acc = jnp.zeros((rc, LANES), F32) + b_ref[...]
            for k in range(kw):
                acc = acc + w_ref[pl.ds(k, 1), :] * u0_ref[pl.ds(pad + r0 - (kw - 1) + k, rc), :]
            o_ref[pl.ds(r0, rc), :] = acc

    return pl.pallas_call(
        body, name=name,
        out_shape=jax.ShapeDtypeStruct((s, cc), F32),
        grid=(ncb,),
        in_specs=[pl.BlockSpec((s, LANES), lambda c: (0, c)), pl.BlockSpec((s, LANES), lambda c: (0, ncb + c)),
                  pl.BlockSpec((kw, LANES), lambda c: (0, c)), pl.BlockSpec((1, LANES), lambda c: (0, c))],
        out_specs=pl.BlockSpec((s, LANES), lambda c: (0, c)),
        scratch_shapes=[pltpu.VMEM((s + pad, LANES), F32)],
        compiler_params=_params(("parallel",)),
    )(proj, proj, w, b.reshape(1, cc))


def _glu_conv_bwd(du1, proj, w, *, cc, name):
    s = proj.shape[0]
    kw = w.shape[0]
    pad = 32
    ncb = cc // LANES
    chunks = _row_chunks(s)

    def body(d_ref, a_ref, g_ref, w_ref, da_ref, dg_ref, dw_ref, db_ref, u0_ref, dp_ref):
        u0_ref[pl.ds(0, pad), :] = jnp.zeros((pad, LANES), F32)
        dp_ref[pl.ds(s, pad), :] = jnp.zeros((pad, LANES), F32)
        for r0, rc in chunks:
            u0_ref[pl.ds(pad + r0, rc), :] = a_ref[pl.ds(r0, rc), :] * _sigmoid(g_ref[pl.ds(r0, rc), :])
            dp_ref[pl.ds(r0, rc), :] = d_ref[pl.ds(r0, rc), :]
        dws = [jnp.zeros((1, LANES), F32) for _ in range(kw)]
        dbs = jnp.zeros((1, LANES), F32)
        for r0, rc in chunks:
            d = dp_ref[pl.ds(r0, rc), :]
            dbs = dbs + jnp.sum(d, axis=0, keepdims=True)
            du0 = jnp.zeros((rc, LANES), F32)
            for k in range(kw):
                du0 = du0 + w_ref[pl.ds(k, 1), :] * dp_ref[pl.ds(r0 + (kw - 1) - k, rc), :]
                dws[k] = dws[k] + jnp.sum(d * u0_ref[pl.ds(pad + r0 - (kw - 1) + k, rc), :], axis=0, keepdims=True)
            sg = _sigmoid(g_ref[pl.ds(r0, rc), :])
            a = a_ref[pl.ds(r0, rc), :]
            da_ref[pl.ds(r0, rc), :] = (du0 * sg).astype(BF16)
            dg_ref[pl.ds(r0, rc), :] = (du0 * a * sg * (1.0 - sg)).astype(BF16)
        for k in range(kw):
            dw_ref[pl.ds(k, 1), :] = dws[k]
        db_ref[...] = dbs

    col = lambda off: pl.BlockSpec((s, LANES), lambda c: (0, off + c))
    da, dg, dw, db = pl.pallas_call(
        body, name=name,
        out_shape=(jax.ShapeDtypeStruct((s, cc), BF16), jax.ShapeDtypeStruct((s, cc), BF16),
                   jax.ShapeDtypeStruct((kw, cc), F32), jax.ShapeDtypeStruct((1, cc), F32)),
        grid=(ncb,),
        in_specs=[col(0), col(0), col(ncb), pl.BlockSpec((kw, LANES), lambda c: (0, c))],
        out_specs=(col(0), col(0), pl.BlockSpec((kw, LANES), lambda c: (0, c)), pl.BlockSpec((1, LANES), lambda c: (0, c))),
        scratch_shapes=[pltpu.VMEM((s + pad, LANES), F32), pltpu.VMEM((s + pad, LANES), F32)],
        compiler_params=_params(("parallel",)),
    )(du1, proj, proj, w)
    return da, dg, dw, db.reshape(cc)


def _cln_silu_fwd(u1, g, b, *, name, ts=512):
    s, cc = u1.shape
    ts = _pick(s, ts)

    def body(u_ref, g_ref, b_ref, o_ref):
        xhat, _ = _ln_stats(u_ref[...])
        y = xhat * g_ref[...] + b_ref[...]
        o_ref[...] = (y * _sigmoid(y)).astype(BF16)

    tok = pl.BlockSpec((ts, cc), lambda i: (i, 0))
    vec = pl.BlockSpec((1, cc), lambda i: (0, 0))
    return pl.pallas_call(body, name=name, out_shape=jax.ShapeDtypeStruct((s, cc), BF16), grid=(s // ts,),
                          in_specs=[tok, vec, vec], out_specs=tok,
                          compiler_params=_params(("parallel",)))(u1, g.reshape(1, cc), b.reshape(1, cc))


def _cln_silu_bwd(dua, u1, g, b, *, name, ts=512):
    s, cc = u1.shape
    ts = _pick(s, ts)

    def body(d_ref, u_ref, g_ref, b_ref, du_ref, dg_ref, db_ref):
        xhat, rstd = _ln_stats(u_ref[...])
        y = xhat * g_ref[...] + b_ref[...]
        sg = _sigmoid(y)
        dy = d_ref[...] * (sg * (1.0 + y * (1.0 - sg)))
        dxhat = dy * g_ref[...]
        m1 = jnp.mean(dxhat, axis=-1, keepdims=True)
        m2 = jnp.mean(dxhat * xhat, axis=-1, keepdims=True)
        du_ref[...] = rstd * (dxhat - m1 - xhat * m2)

        @pl.when(pl.program_id(0) == 0)
        def _():
            dg_ref[...] = jnp.zeros_like(dg_ref)
            db_ref[...] = jnp.zeros_like(db_ref)

        dg_ref[...] += jnp.sum(dy * xhat, axis=0, keepdims=True)
        db_ref[...] += jnp.sum(dy, axis=0, keepdims=True)

    tok = pl.BlockSpec((ts, cc), lambda i: (i, 0))
    vec = pl.BlockSpec((1, cc), lambda i: (0, 0))
    du1, dg, db = pl.pallas_call(
        body, name=name,
        out_shape=(jax.ShapeDtypeStruct((s, cc), F32), jax.ShapeDtypeStruct((1, cc), F32),
                   jax.ShapeDtypeStruct((1, cc), F32)),
        grid=(s // ts,), in_specs=[tok, tok, vec, vec], out_specs=(tok, vec, vec),
        compiler_params=_params(("arbitrary",)),
    )(dua, u1, g.reshape(1, cc), b.reshape(1, cc))
    return du1, dg.reshape(cc), db.reshape(cc)


def _softplus_parts(z):
    lk = jnp.minimum(-z, 0.0) - jnp.log1p(jnp.exp(-jnp.abs(z)))
    return lk, z + lk


def _stack_heads(x, hms):
    return jnp.concatenate([jnp.where(hm, x, 0.0) for hm in hms], axis=0).astype(BF16)


def _heads_side_by_side(x_st, tq):
    return jnp.concatenate([x_st[:tq], x_st[tq:]], axis=1)


def _sb_tile_masks(tq):
    row = lax.broadcasted_iota(jnp.int32, (2 * tq, tq), 0)
    col = lax.broadcasted_iota(jnp.int32, (2 * tq, tq), 1)
    vis = col < jnp.where(row >= tq, row - tq, row)
    krow, kcol = row[:tq], col[:tq]
    return vis, (krow > kcol).astype(BF16), (krow < kcol).astype(BF16)


_NT = (((1,), (1,)), ((), ()))
_TN = (((0,), (0,)), ((), ()))


def _head_masks():
    lane = lax.broadcasted_iota(jnp.int32, (1, LANES), 1)
    return [(lane >= SB_HEAD_DIM * h) & (lane < SB_HEAD_DIM * (h + 1)) for h in range(2)]


def _hosted(exchange, grid, body):
    if exchange is None:
        return body, [], [], []
    n_in, n_out, n_sems = len(exchange.inputs), len(exchange.out_shapes), len(exchange.scratch_shapes)

    def wrapped(*refs, n_own_in, n_own_out):
        own_in, ex_in = refs[:n_own_in], refs[n_own_in:n_own_in + n_in]
        rest = refs[n_own_in + n_in:]
        own_out, ex_out = rest[:n_own_out], rest[n_own_out:n_own_out + n_out]
        own_scratch, sems = rest[n_own_out + n_out:len(rest) - n_sems], rest[len(rest) - n_sems:]
        ids = [pl.program_id(d) for d in range(len(grid))]
        first = functools.reduce(lambda x, y: x & y, [i == 0 for i in ids])
        last = functools.reduce(lambda x, y: x & y, [i == g - 1 for i, g in zip(ids, grid)])

        @pl.when(first)
        def _():
            exchange.start(ex_in, ex_out, sems)

        body(*own_in, *own_out, *own_scratch)

        @pl.when(last)
        def _():
            exchange.finish(ex_in, ex_out, sems)

    return wrapped, list(exchange.inputs), list(exchange.out_shapes), list(exchange.scratch_shapes)


def _sb_fwd(proj, *, col0, n_pairs, name, tq=256, exchange=None):
    s = proj.shape[0]
    tq = _pick(s, tq)
    nq = s // tq
    cb0 = col0 // LANES
    scale = SB_HEAD_DIM ** -0.5

    def body(q_ref, k_ref, v_ref, o_ref, t_ref, n_ref):
        i = pl.program_id(1)
        hms = _head_masks()
        vis, m_after = _sb_tile_masks(tq)[:2]
        q_st = _stack_heads(q_ref[...] * scale, hms)

        def tile(j, c, acc, masked):
            start = pl.multiple_of(j * tq, tq)
            kb = k_ref[pl.ds(start, tq), :].astype(BF16)
            v_st = _stack_heads(v_ref[pl.ds(start, tq), :], hms)
            z = lax.dot_general(q_st, kb, _NT, preferred_element_type=F32)
            lk, lb = _softplus_parts(z)
            if masked:
                lk = jnp.where(vis, lk, 0.0)
            later = jnp.dot(lk.astype(BF16), m_after, preferred_element_type=F32)
            a = jnp.exp(lb + later + c)
            if masked:
                a = jnp.where(vis, a, 0.0)
            acc = acc + jnp.dot(_heads_side_by_side(a.astype(BF16), tq), v_st, preferred_element_type=F32)
            return c + jnp.sum(lk, axis=1, keepdims=True), acc

        def more(st):
            return jnp.logical_and(st[0] < i, jnp.max(st[1]) > -EXP_UNDERFLOW)

        def step(st):
            c, acc = tile(i - 1 - st[0], st[1], st[2], False)
            return st[0] + 1, c, acc

        c, acc = tile(i, jnp.zeros((2 * tq, 1), F32), jnp.zeros((tq, LANES), F32), True)
        n, c, acc = lax.while_loop(more, step, (jnp.int32(0), c, acc))
        o_ref[...] = acc.astype(BF16)
        t_ref[...] = jnp.where(hms[0], c[:tq], c[tq:])
        n_ref[...] = jnp.zeros((8, LANES), F32) + n.astype(F32)

    grid = (n_pairs, nq)
    body, ex_in, ex_out, ex_scratch = _hosted(exchange, grid, body)
    if exchange is not None:
        body = functools.partial(body, n_own_in=3, n_own_out=3)
    hbm = pl.BlockSpec(memory_space=pl.ANY)
    seq = lambda off: pl.BlockSpec((s, LANES), lambda p, i: (0, cb0 + off + p))
    outs = pl.pallas_call(
        body, name=name,
        out_shape=(jax.ShapeDtypeStruct((s, n_pairs * LANES), BF16), jax.ShapeDtypeStruct((n_pairs, s, LANES), F32),
                   jax.ShapeDtypeStruct((n_pairs, nq * 8, LANES), F32), *ex_out),
        grid=grid,
        in_specs=[pl.BlockSpec((tq, LANES), lambda p, i: (i, cb0 + p)), seq(n_pairs), seq(2 * n_pairs)] + [hbm] * len(ex_in),
        out_specs=(pl.BlockSpec((tq, LANES), lambda p, i: (i, p)), pl.BlockSpec((None, tq, LANES), lambda p, i: (p, i, 0)),
                   pl.BlockSpec((None, 8, LANES), lambda p, i: (p, i, 0)), *([hbm] * len(ex_out))),
        scratch_shapes=ex_scratch,
        compiler_params=_params(("arbitrary", "arbitrary") if exchange else ("parallel", "arbitrary")),
    )(proj, proj, proj, *ex_in)
    return outs[0], outs[1], outs[2], list(outs[3:])


def _sb_bwd(proj, t_sum, n_walked, dua, *, col0, n_pairs, do_col0, name, tq=256, exchange=None):
    s = proj.shape[0]
    tq = _pick(s, tq)
    cb0 = col0 // LANES
    dcb0 = do_col0 // LANES
    scale = SB_HEAD_DIM ** -0.5

    def body(q_ref, k_ref, v_ref, t_ref, n_ref, do_ref, dq_ref, dk_ref, dv_ref):
        i = pl.program_id(1)

        @pl.when(i == 0)
        def _():
            dk_ref[...] = jnp.zeros_like(dk_ref)
            dv_ref[...] = jnp.zeros_like(dv_ref)

        hms = _head_masks()
        vis, m_after, m_before = _sb_tile_masks(tq)
        q_st = _stack_heads(q_ref[...] * scale, hms)
        do_st = _stack_heads(do_ref[...], hms)
        t_st = jnp.concatenate([t_ref[:, SB_HEAD_DIM * h:SB_HEAD_DIM * h + 1] for h in range(2)], axis=0)
        first = i - jnp.max(n_ref[...]).astype(jnp.int32)

        def tile(j, p_sum, g_sum, dq, masked):
            start = pl.multiple_of(j * tq, tq)
            k = k_ref[pl.ds(start, tq), :]
            z = lax.dot_general(q_st, k.astype(BF16), _NT, preferred_element_type=F32)
            lk_raw, lb = _softplus_parts(z)
            lk = jnp.where(vis, lk_raw, 0.0) if masked else lk_raw
            p_next = p_sum + jnp.sum(lk, axis=1, keepdims=True)
            later = jnp.dot(lk.astype(BF16), m_after, preferred_element_type=F32)
            a = jnp.exp(lb + (t_st - p_next) + later)
            if masked:
                a = jnp.where(vis, a, 0.0)
            da = lax.dot_general(do_st, v_ref[pl.ds(start, tq), :].astype(BF16), _NT, preferred_element_type=F32)
            g = a * da
            g_before = g_sum + jnp.dot(g.astype(BF16), m_before, preferred_element_type=F32)
            dz = g * jnp.exp(lk_raw) - g_before * jnp.exp(lb)
            if masked:
                dz = jnp.where(vis, dz, 0.0)
            dzb = dz.astype(BF16)
            dv_ref[pl.ds(start, tq), :] += lax.dot_general(a.astype(BF16), do_st, _TN, preferred_element_type=F32)
            dk_ref[pl.ds(start, tq), :] += lax.dot_general(dzb, q_st, _TN, preferred_element_type=F32)
            dq = dq + jnp.dot(_heads_side_by_side(dzb, tq), _stack_heads(k, hms), preferred_element_type=F32)
            return p_next, g_sum + jnp.sum(g, axis=1, keepdims=True), dq

        zero = jnp.zeros((2 * tq, 1), F32)
        st = lax.fori_loop(first, i, lambda j, st: tile(j, *st, False), (zero, zero, jnp.zeros((tq, LANES), F32)))
        dq_ref[...] = tile(i, *st, True)[2] * scale

    grid = (n_pairs, s // tq)
    body, ex_in, ex_out, ex_scratch = _hosted(exchange, grid, body)
    if exchange is not None:
        body = functools.partial(body, n_own_in=6, n_own_out=3)
    hbm = pl.BlockSpec(memory_space=pl.ANY)
    seq = lambda off: pl.BlockSpec((s, LANES), lambda p, i: (0, cb0 + off + p))
    out = jax.ShapeDtypeStruct((s, n_pairs * LANES), F32)
    res = pl.BlockSpec((s, LANES), lambda p, i: (0, p))
    outs = pl.pallas_call(
        body, name=name,
        out_shape=(out, out, out, *ex_out),
        grid=grid,
        in_specs=[pl.BlockSpec((tq, LANES), lambda p, i: (i, cb0 + p)), seq(n_pairs), seq(2 * n_pairs),
                  pl.BlockSpec((None, tq, LANES), lambda p, i: (p, i, 0)),
                  pl.BlockSpec((None, 8, LANES), lambda p, i: (p, i, 0)),
                  pl.BlockSpec((tq, LANES), lambda p, i: (i, dcb0 + p))] + [hbm] * len(ex_in),
        out_specs=(pl.BlockSpec((tq, LANES), lambda p, i: (i, p)), res, res, *([hbm] * len(ex_out))),
        scratch_shapes=ex_scratch,
        compiler_params=_params(("arbitrary", "arbitrary")),
    )(proj, proj, proj, t_sum, n_walked, dua, *ex_in)
    return outs[0], outs[1], outs[2], list(outs[3:])


def _mem_attn_fwd(qm, km, vm, *, name, tq=512):
    s, d = qm.shape
    heads = d // MEM_HEAD_DIM
    mlen = km.shape[0]
    tq = _pick(s, tq)
    scale = MEM_HEAD_DIM ** -0.5

    def body(q_ref, k_ref, v_ref, o_ref):
        for h in range(heads):
            sl = slice(h * MEM_HEAD_DIM, (h + 1) * MEM_HEAD_DIM)
            q = (q_ref[:, sl] * scale).astype(BF16)
            sc = lax.dot_general(q, k_ref[:, sl].astype(BF16), _NT, preferred_element_type=F32)
            e = jnp.exp(sc - jnp.max(sc, axis=1, keepdims=True))
            p = e / jnp.sum(e, axis=1, keepdims=True)
            o_ref[:, sl] = jnp.dot(p.astype(BF16), v_ref[:, sl].astype(BF16), preferred_element_type=F32).astype(BF16)

    tok = pl.BlockSpec((tq, d), lambda i: (i, 0))
    kv = pl.BlockSpec((mlen, d), lambda i: (0, 0))
    return pl.pallas_call(body, name=name, out_shape=jax.ShapeDtypeStruct((s, d), BF16), grid=(s // tq,),
                          in_specs=[tok, kv, kv], out_specs=tok, compiler_params=_params(("parallel",)))(qm, km, vm)


def _mem_attn_bwd(qm, km, vm, do, *, name, tq=512):
    s, d = qm.shape
    heads = d // MEM_HEAD_DIM
    mlen = km.shape[0]
    tq = _pick(s, tq)
    scale = MEM_HEAD_DIM ** -0.5

    def body(q_ref, k_ref, v_ref, do_ref, dq_ref, dk_ref, dv_ref):
        @pl.when(pl.program_id(0) == 0)
        def _():
            dk_ref[...] = jnp.zeros_like(dk_ref)
            dv_ref[...] = jnp.zeros_like(dv_ref)

        for h in range(heads):
            sl = slice(h * MEM_HEAD_DIM, (h + 1) * MEM_HEAD_DIM)
            q = (q_ref[:, sl] * scale).astype(BF16)
            k = k_ref[:, sl].astype(BF16)
            v = v_ref[:, sl].astype(BF16)
            sc = lax.dot_general(q, k, _NT, preferred_element_type=F32)
            e = jnp.exp(sc - jnp.max(sc, axis=1, keepdims=True))
            p = e / jnp.sum(e, axis=1, keepdims=True)
            dob = do_ref[:, sl].astype(BF16)
            dv_ref[:, sl] += lax.dot_general(p.astype(BF16), dob, _TN, preferred_element_type=F32)
            dp = lax.dot_general(dob, v, _NT, preferred_element_type=F32)
            ds = (p * (dp - jnp.sum(dp * p, axis=1, keepdims=True))).astype(BF16)
            dq_ref[:, sl] = (jnp.dot(ds, k, preferred_element_type=F32) * scale).astype(BF16)
            dk_ref[:, sl] += lax.dot_general(ds, q, _TN, preferred_element_type=F32)

    tok = pl.BlockSpec((tq, d), lambda i: (i, 0))
    kv = pl.BlockSpec((mlen, d), lambda i: (0, 0))
    return pl.pallas_call(
        body, name=name,
        out_shape=(jax.ShapeDtypeStruct((s, d), BF16), jax.ShapeDtypeStruct((mlen, d), F32),
                   jax.ShapeDtypeStruct((mlen, d), F32)),
        grid=(s // tq,), in_specs=[tok, kv, kv, tok], out_specs=(tok, kv, kv),
        compiler_params=_params(("arbitrary",)),
    )(qm, km, vm, do)


def _ffn_act_fwd(up, w, b, *, name):
    s, two_f = up.shape
    ff = two_f // 2
    nfb = ff // LANES
    kw = w.shape[0]
    pad = 8
    chunks = _row_chunks(s)

    def body(v_ref, g_ref, wv_ref, wg_ref, bv_ref, bg_ref, o_ref, vp_ref, gp_ref):
        vp_ref[pl.ds(0, pad), :] = jnp.zeros((pad, LANES), F32)
        gp_ref[pl.ds(0, pad), :] = jnp.zeros((pad, LANES), F32)
        for r0, rc in chunks:
            vp_ref[pl.ds(pad + r0, rc), :] = v_ref[pl.ds(r0, rc), :].astype(F32)
            gp_ref[pl.ds(pad + r0, rc), :] = g_ref[pl.ds(r0, rc), :].astype(F32)
        for r0, rc in chunks:
            vc = jnp.zeros((rc, LANES), F32) + bv_ref[...]
            gc = jnp.zeros((rc, LANES), F32) + bg_ref[...]
            for k in range(kw):
                off = pad + r0 - (kw - 1) + k
                vc = vc + wv_ref[pl.ds(k, 1), :] * vp_ref[pl.ds(off, rc), :]
                gc = gc + wg_ref[pl.ds(k, 1), :] * gp_ref[pl.ds(off, rc), :]
            o_ref[pl.ds(r0, rc), :] = (gc * _sigmoid(gc) * vc).astype(BF16)

    col = lambda off: pl.BlockSpec((s, LANES), lambda c: (0, off + c))
    tap = lambda off: pl.BlockSpec((kw, LANES), lambda c: (0, off + c))
    vec = lambda off: pl.BlockSpec((1, LANES), lambda c: (0, off + c))
    return pl.pallas_call(
        body, name=name, out_shape=jax.ShapeDtypeStruct((s, ff), BF16), grid=(nfb,),
        in_specs=[col(0), col(nfb), tap(0), tap(nfb), vec(0), vec(nfb)], out_specs=col(0),
        scratch_shapes=[pltpu.VMEM((s + pad, LANES), F32), pltpu.VMEM((s + pad, LANES), F32)],
        compiler_params=_params(("parallel",)),
    )(up, up, w, w, b.reshape(1, two_f), b.reshape(1, two_f))


def _ffn_act_bwd(up, dact, w, b, *, name, exchange=None):
    s, two_f = up.shape
    ff = two_f // 2
    nfb = ff // LANES
    kw = w.shape[0]
    pad = 8
    chunks = _row_chunks(s)

    def body(v_ref, g_ref, d_ref, wv_ref, wg_ref, bv_ref, bg_ref, dv_ref, dg_ref, dwv_ref, dwg_ref, dbv_ref, dbg_ref,
             vp_ref, gp_ref, dvc_ref, dgc_ref):
        vp_ref[pl.ds(0, pad), :] = jnp.zeros((pad, LANES), F32)
        gp_ref[pl.ds(0, pad), :] = jnp.zeros((pad, LANES), F32)
        dvc_ref[pl.ds(s, pad), :] = jnp.zeros((pad, LANES), F32)
        dgc_ref[pl.ds(s, pad), :] = jnp.zeros((pad, LANES), F32)
        for r0, rc in chunks:
            vp_ref[pl.ds(pad + r0, rc), :] = v_ref[pl.ds(r0, rc), :].astype(F32)
            gp_ref[pl.ds(pad + r0, rc), :] = g_ref[pl.ds(r0, rc), :].astype(F32)
        dwv = [jnp.zeros((1, LANES), F32) for _ in range(kw)]
        dwg = [jnp.zeros((1, LANES), F32) for _ in range(kw)]
        dbv = jnp.zeros((1, LANES), F32)
        dbg = jnp.zeros((1, LANES), F32)
        for r0, rc in chunks:
            vc = jnp.zeros((rc, LANES), F32) + bv_ref[...]
            gc = jnp.zeros((rc, LANES), F32) + bg_ref[...]
            for k in range(kw):
                off = pad + r0 - (kw - 1) + k
                vc = vc + wv_ref[pl.ds(k, 1), :] * vp_ref[pl.ds(off, rc), :]
                gc = gc + wg_ref[pl.ds(k, 1), :] * gp_ref[pl.ds(off, rc), :]
            sg = _sigmoid(gc)
            d = d_ref[pl.ds(r0, rc), :].astype(F32)
            dvc = d * (gc * sg)
            dgc = d * vc * (sg * (1.0 + gc * (1.0 - sg)))
            dvc_ref[pl.ds(r0, rc), :] = dvc
            dgc_ref[pl.ds(r0, rc), :] = dgc
            dbv = dbv + jnp.sum(dvc, axis=0, keepdims=True)
            dbg = dbg + jnp.sum(dgc, axis=0, keepdims=True)
            for k in range(kw):
                off = pad + r0 - (kw - 1) + k
                dwv[k] = dwv[k] + jnp.sum(dvc * vp_ref[pl.ds(off, rc), :], axis=0, keepdims=True)
                dwg[k] = dwg[k] + jnp.sum(dgc * gp_ref[pl.ds(off, rc), :], axis=0, keepdims=True)
        for r0, rc in chunks:
            dv = jnp.zeros((rc, LANES), F32)
            dg = jnp.zeros((rc, LANES), F32)
            for k in range(kw):
                off = r0 + (kw - 1) - k
                dv = dv + wv_ref[pl.ds(k, 1), :] * dvc_ref[pl.ds(off, rc), :]
                dg = dg + wg_ref[pl.ds(k, 1), :] * dgc_ref[pl.ds(off, rc), :]
            dv_ref[pl.ds(r0, rc), :] = dv.astype(BF16)
            dg_ref[pl.ds(r0, rc), :] = dg.astype(BF16)
        for k in range(kw):
            dwv_ref[pl.ds(k, 1), :] = dwv[k]
            dwg_ref[pl.ds(k, 1), :] = dwg[k]
        dbv_ref[...] = dbv
        dbg_ref[...] = dbg

    col = lambda off: pl.BlockSpec((s, LANES), lambda c: (0, off + c))
    tap = lambda off: pl.BlockSpec((kw, LANES), lambda c: (0, off + c))
    vec = lambda off: pl.BlockSpec((1, LANES), lambda c: (0, off + c))
    big = lambda: pltpu.VMEM((s + pad, LANES), F32)
    body, ex_in, ex_out, ex_scratch = _hosted(exchange, (nfb,), body)
    if exchange is not None:
        body = functools.partial(body, n_own_in=7, n_own_out=6)
    hbm = pl.BlockSpec(memory_space=pl.ANY)
    outs = pl.pallas_call(
        body, name=name,
        out_shape=(jax.ShapeDtypeStruct((s, ff), BF16), jax.ShapeDtypeStruct((s, ff), BF16),
                   jax.ShapeDtypeStruct((kw, ff), F32), jax.ShapeDtypeStruct((kw, ff), F32),
                   jax.ShapeDtypeStruct((1, ff), F32), jax.ShapeDtypeStruct((1, ff), F32), *ex_out),
        grid=(nfb,),
        in_specs=[col(0), col(nfb), col(0), tap(0), tap(nfb), vec(0), vec(nfb)] + [hbm] * len(ex_in),
        out_specs=(col(0), col(0), tap(0), tap(0), vec(0), vec(0), *([hbm] * len(ex_out))),
        scratch_shapes=[big(), big(), big(), big()] + ex_scratch,
        compiler_params=_params(("arbitrary",) if exchange else ("parallel",)),
    )(up, up, dact, w, w, b.reshape(1, two_f), b.reshape(1, two_f), *ex_in)
    dv, dg, dwv, dwg, dbv, dbg = outs[:6]
    return (jnp.concatenate([dv, dg], axis=1), jnp.concatenate([dwv, dwg], axis=1),
            jnp.concatenate([dbv, dbg], axis=1).reshape(two_f), list(outs[6:]))


def _sum_parts(parts, *, name, tr=256):
    n_parts, rows, cols = parts.shape
    tr = _pick(rows, tr, 16)

    def body(p_ref, o_ref):
        g = p_ref[0].astype(F32)
        for k in range(1, n_parts):
            g = g + p_ref[k].astype(F32)
        o_ref[...] = g

    return pl.pallas_call(
        body, name=name, out_shape=jax.ShapeDtypeStruct((rows, cols), F32), grid=(rows // tr,),
        in_specs=[pl.BlockSpec((n_parts, tr, cols), lambda i: (0, i, 0))],
        out_specs=pl.BlockSpec((tr, cols), lambda i: (i, 0)), compiler_params=_params(("parallel",)),
    )(parts)


def _sum_adamw(parts, w, m, v, *, layer, so_far, name, tr=256):
    n_parts, rows, cols = parts.shape
    depth = w.shape[0]
    tr = _pick(rows, tr, 16)
    c1 = 1.0 / (1.0 - ADAM_B1 ** ADAM_STEP)
    c2 = 1.0 / (1.0 - ADAM_B2 ** ADAM_STEP)

    def body(p_ref, w_ref, m_ref, v_ref, *rest):
        g_ref, d_ref, nm_ref, nv_ref = rest[-4:]
        g = p_ref[0].astype(F32)
        for k in range(1, n_parts):
            g = g + p_ref[k].astype(F32)
        nm = ADAM_B1 * m_ref[...] + (1.0 - ADAM_B1) * g
        nv = ADAM_B2 * v_ref[...] + (1.0 - ADAM_B2) * (g * g)
        g_ref[...] = g
        nm_ref[...] = nm
        nv_ref[...] = nv
        d_ref[...] = -ADAM_LR * ((nm * c1) / (jnp.sqrt(nv * c2) + ADAM_EPS) + ADAM_WD * w_ref[...])

    blk = pl.BlockSpec((None, tr, cols), lambda i: (layer, i, 0))
    out = jax.ShapeDtypeStruct((depth, rows, cols), F32)
    kept = list(so_far) if so_far is not None else []
    return pl.pallas_call(
        body, name=name, out_shape=(out, out, out, out), grid=(rows // tr,),
        in_specs=[pl.BlockSpec((n_parts, tr, cols), lambda i: (0, i, 0)), blk, blk, blk]
        + [pl.BlockSpec(memory_space=pl.ANY)] * len(kept),
        out_specs=(blk, blk, blk, blk),
        input_output_aliases={4 + k: k for k in range(len(kept))},
        compiler_params=_params(("parallel",)),
    )(parts, w, m, v, *kept)


def _mesh_pos():
    return lax.axis_index("x"), lax.axis_index("y"), lax.axis_index("c")


def _flip(pos, k):
    x, y, c = pos
    return (1 - x if k & 4 else x, 1 - y if k & 2 else y, 1 - c if k & 1 else c)


def _dev_index(pos):
    return 4 * pos[0] + 2 * pos[1] + pos[2]


N_PEERS = N_DEV - 1


class _Exchange:
    def __init__(self, inputs, out_shapes, start, finish):
        n = len(inputs)
        self.inputs, self.out_shapes, self.start, self.finish = list(inputs), list(out_shapes), start, finish
        self.scratch_shapes = [pltpu.SemaphoreType.DMA((n * N_PEERS,)), pltpu.SemaphoreType.DMA((n * N_PEERS,)),
                               pltpu.SemaphoreType.DMA((n,))]


def _gather_exchange(xs):
    n = len(xs)

    def plan(x_refs, out_refs, sems):
        send_sems, recv_sems, local_sems = sems
        me = _mesh_pos()
        sibling = _flip(me, 1)
        chips = [_flip(me, 4), _flip(me, 2), _flip(me, 6)]

        def copy(a, k, block, to, from_input=False):
            slot = out_refs[a].at[_dev_index(block)]
            return pltpu.make_async_remote_copy(
                src_ref=x_refs[a] if from_input else slot, dst_ref=slot,
                send_sem=send_sems.at[a * N_PEERS + k], recv_sem=recv_sems.at[a * N_PEERS + k],
                device_id=to, device_id_type=pl.DeviceIdType.MESH)

        mine = [pltpu.make_async_copy(x_refs[a], out_refs[a].at[_dev_index(me)], local_sems.at[a]) for a in range(n)]
        first = [copy(a, 0, me, sibling, True) for a in range(n)]
        first += [copy(a, 1 + j, me, chip, True) for j, chip in enumerate(chips) for a in range(n)]
        return me, sibling, chips, copy, mine, first

    def start(x_refs, out_refs, sems):
        _, _, _, _, mine, first = plan(x_refs, out_refs, sems)
        for cp in mine + first:
            cp.start()

    def finish(x_refs, out_refs, sems):
        me, sibling, chips, copy, mine, first = plan(x_refs, out_refs, sems)
        passed = []
        for j, chip in enumerate(chips):
            for a in range(n):
                copy(a, 1 + j, chip, me).wait_recv()
                passed.append(copy(a, 4 + j, chip, sibling))
                passed[-1].start()
        for a in range(n):
            copy(a, 0, sibling, me).wait_recv()
        for j, chip in enumerate(chips):
            for a in range(n):
                copy(a, 4 + j, _flip(chip, 1), me).wait_recv()
        for cp in first + passed:
            cp.wait_send()
        for cp in mine:
            cp.wait()

    return _Exchange(xs, [jax.ShapeDtypeStruct((N_DEV,) + x.shape, x.dtype) for x in xs], start, finish)


def _scatter_exchange(xs):
    n = len(xs)

    def plan(x_refs, out_refs, sems):
        send_sems, recv_sems, local_sems = sems
        me = _mesh_pos()
        my_slot = _dev_index(me)

        def copy(a, k):
            peer = _flip(me, k)
            return pltpu.make_async_remote_copy(
                src_ref=x_refs[a].at[_dev_index(peer)], dst_ref=out_refs[a].at[my_slot],
                send_sem=send_sems.at[a * N_PEERS + k - 1], recv_sem=recv_sems.at[a * N_PEERS + k - 1],
                device_id=peer, device_id_type=pl.DeviceIdType.MESH)

        mine = [pltpu.make_async_copy(x_refs[a].at[my_slot], out_refs[a].at[my_slot], local_sems.at[a]) for a in range(n)]
        return mine, [copy(a, k) for k in range(1, N_DEV) for a in range(n)]

    def start(x_refs, out_refs, sems):
        mine, copies = plan(x_refs, out_refs, sems)
        for cp in mine + copies:
            cp.start()

    def finish(x_refs, out_refs, sems):
        mine, copies = plan(x_refs, out_refs, sems)
        for cp in copies:
            cp.wait_recv()
        for cp in copies:
            cp.wait_send()
        for cp in mine:
            cp.wait()

    return _Exchange(xs, [jax.ShapeDtypeStruct(x.shape, x.dtype) for x in xs], start, finish)


def _run_exchanges(exchanges, *, name):
    counts = [len(ex.inputs) for ex in exchanges]
    n = sum(counts)

    def body(*refs):
        offsets = [sum(counts[:e]) for e in range(len(exchanges))]
        views = [(refs[o:o + c], refs[n + o:n + o + c], refs[2 * n + 3 * e:2 * n + 3 * e + 3])
                 for e, (o, c) in enumerate(zip(offsets, counts))]
        for ex, view in zip(exchanges, views):
            ex.start(*view)
        for ex, view in zip(exchanges, views):
            ex.finish(*view)

    hbm = pl.BlockSpec(memory_space=pl.ANY)
    outs = pl.pallas_call(
        body, name=name, out_shape=tuple(s for ex in exchanges for s in ex.out_shapes), in_specs=[hbm] * n,
        out_specs=tuple([hbm] * n), scratch_shapes=[s for ex in exchanges for s in ex.scratch_shapes],
    )(*[x for ex in exchanges for x in ex.inputs])
    return [list(outs[sum(counts[:e]):sum(counts[:e + 1])]) for e in range(len(exchanges))]


def _pack(arrays):
    flat = jnp.concatenate([a.reshape(-1) for a in arrays])
    n = flat.shape[0]
    tile = PACK_W * PACK_ROW_ALIGN
    total = -(-n // tile) * tile
    return jnp.pad(flat, (0, total - n)).reshape(total // PACK_W, PACK_W)


def _unpack(buf, shapes):
    lead = buf.shape[:-2]
    flat = buf.reshape(lead + (-1,))
    out, off = [], 0
    for shp in shapes:
        n = 1
        for dim in shp:
            n *= dim
        out.append(flat[..., off:off + n].reshape(lead + tuple(shp)))
        off += n
    return out


def _join_columns(blocks):
    return jnp.moveaxis(blocks, 0, 2).reshape(blocks.shape[1], blocks.shape[2], -1)


def _mm_hosting(a, b, exchange, **kw):
    if exchange is None:
        return _mm(a, b, **kw), []
    return _mm(a, b, exchange=exchange, **kw)


GATHERED_BY_ATTENTION = ['w_out', 'mem_wq', 'mem_wk', 'mem_wv', 'mem_wo', 'ffn_up', 'ffn_down']


def _as_matrix(blocks):
    return blocks.reshape(-1, blocks.shape[-1])


def _layer_fwd(x, xb, memb, w, alpha, shards, next_w_in):
    w = dict(w)
    cc = w['conv_w'].shape[1]
    n_pairs = (N_DEV * shards['w_out'].shape[0] - cc) // LANES

    proj = _mm(xb, w['w_in'], tb=True, name="mm_proj")
    u1 = _glu_conv_fwd(proj, w['conv_w'], w['conv_b'], cc=cc, name="glu_conv_fwd")
    u = _cln_silu_fwd(u1, w['conv_ln_g'], w['conv_ln_b'], name="cln_silu_fwd")
    att, t_sum, n_walked, got = _sb_fwd(proj, col0=2 * cc, n_pairs=n_pairs, name="sb_fwd",
                                        exchange=_gather_exchange([shards[n] for n in GATHERED_BY_ATTENTION]))
    w.update({n: _as_matrix(f) for n, f in zip(GATHERED_BY_ATTENTION, got)})
    ua = jnp.concatenate([u, att], axis=1)
    r1, x1, x1b = _mm_res_ln(ua, w['w_out'], x, w['ln1_g'], w['ln1_b'], alpha=alpha, name="mm_mix_ln")
    qm = _mm(x1b, w['mem_wq'], name="mm_memq")
    km = _mm(memb, w['mem_wk'], name="mm_memkv")
    vm = _mm(memb, w['mem_wv'], name="mm_memkv")
    o = _mem_attn_fwd(qm, km, vm, name="mem_attn_fwd")
    r2, x2, x2b = _mm_res_ln(o, w['mem_wo'], x1, w['ln2_g'], w['ln2_b'], alpha=alpha, name="mm_mix_ln")
    up, got = _mm_hosting(x2b, w['ffn_up'], _gather_exchange([next_w_in]) if next_w_in is not None else None,
                          tb=True, out_dtype=BF16, name="mm_up")
    act = _ffn_act_fwd(up, w['ffn_conv_w'], w['ffn_conv_b'], name="ffn_act_fwd")
    r3, x3, x3b = _mm_res_ln(act, w['ffn_down'], x2, w['ln3_g'], w['ln3_b'], alpha=alpha, name="mm_down_ln")
    saved = dict(xb=xb, proj=proj, u1=u1, t_sum=t_sum, n_walked=n_walked, ua=ua, r1=r1, x1b=x1b, qm=qm, km=km, vm=vm,
                 o=o, r2=r2, x2b=x2b, up=up, act=act, r3=r3)
    return x3, x3b, saved, w, _as_matrix(got[0]) if got else None


def _layer_bwd(top, sv, memb, w, alpha, carried):
    g, received = {}, {}
    cc = w['conv_w'].shape[1]
    n_pairs = (w['w_out'].shape[0] - cc) // LANES

    def sending(sends):
        ex = _scatter_exchange([gm.reshape(N_DEV, -1, gm.shape[-1]) for _, gm in sends]) if sends else None
        return ex, lambda got: received.update({key: blocks for (key, _), blocks in zip(sends, got)})

    if top[1] is None:
        dr3, dr3b, g['ln3_g'], g['ln3_b'] = _ln_bwd(top[0], None, sv['r3'], w['ln3_g'], alpha=alpha, name="ln_bwd")
    else:
        dr3, dr3b, g['ln3_g'], g['ln3_b'] = _mm_ln_bwd(*top, sv['r3'], w['ln3_g'], alpha=alpha, name="mm_dx_in_ln")
    g_down = _mm(sv['act'], dr3b, ta=True, out_dtype=BF16, name="mm_dw_down")
    dact = _mm(dr3b, w['ffn_down'], tb=True, out_dtype=BF16, name="mm_dact")
    ex, file = sending(carried + [('ffn_down', g_down)])
    dup, g['ffn_conv_w'], g['ffn_conv_b'], got = _ffn_act_bwd(sv['up'], dact, w['ffn_conv_w'], w['ffn_conv_b'],
                                                                name="ffn_act_bwd", exchange=ex)
    file(got)
    g_up = _mm(dup, sv['x2b'], ta=True, out_dtype=BF16, name="mm_dw_up")
    dr2, dr2b, g['ln2_g'], g['ln2_b'] = _mm_ln_bwd(dup, w['ffn_up'], dr3, sv['r2'], w['ln2_g'], alpha=alpha, name="mm_dx_up_ln")
    g_wo = _mm(sv['o'], dr2b, ta=True, out_dtype=BF16, name="mm_dw_sq")
    do = _mm(dr2b, w['mem_wo'], tb=True, out_dtype=BF16, name="mm_dx_sq")
    dqm, dkm, dvm = _mem_attn_bwd(sv['qm'], sv['km'], sv['vm'], do, name="mem_attn_bwd")
    g_wq = _mm(sv['x1b'], dqm, ta=True, out_dtype=BF16, name="mm_dw_sq")
    g_wk = _mm(memb, dkm, ta=True, out_dtype=BF16, name="mm_dw_memkv")
    g_wv = _mm(memb, dvm, ta=True, out_dtype=BF16, name="mm_dw_memkv")
    dr1, dr1b, g['ln1_g'], g['ln1_b'] = _mm_ln_bwd(dqm, w['mem_wq'], dr2, sv['r1'], w['ln1_g'], tb=True, alpha=alpha,
                                                    name="mm_dx_sq_ln")
    g_out = _mm(sv['ua'], dr1b, ta=True, out_dtype=BF16, name="mm_dw_sq")
    dua = _mm(dr1b, w['w_out'], tb=True, name="mm_dx_sq")
    du1, g['conv_ln_g'], g['conv_ln_b'] = _cln_silu_bwd(dua, sv['u1'], w['conv_ln_g'], w['conv_ln_b'], name="cln_silu_bwd")
    dga, dgg, g['conv_w'], g['conv_b'] = _glu_conv_bwd(du1, sv['proj'], w['conv_w'], cc=cc, name="glu_conv_bwd")
    ex, file = sending([('ffn_up', g_up), ('mem_wo', g_wo), ('mem_wq', g_wq), ('mem_wk', g_wk), ('mem_wv', g_wv), ('w_out', g_out)])
    dq, dk, dv, got = _sb_bwd(sv['proj'], sv['t_sum'], sv['n_walked'], dua, col0=2 * cc, n_pairs=n_pairs, do_col0=cc,
                              name="sb_bwd", exchange=ex)
    file(got)
    dproj = jnp.concatenate([dga, dgg, dq.astype(BF16), dk.astype(BF16), dv.astype(BF16)], axis=1)
    g_in = _mm(dproj, sv['xb'], ta=True, out_dtype=BF16, name="mm_dw_in")
    return (dproj, w['w_in'], dr1), g, received, [('w_in', g_in)]


def kernel(x, mem, w_in, conv_w, conv_b, conv_ln_g, conv_ln_b, w_out, ln1_g, ln1_b, mem_wq, mem_wk, mem_wv, mem_wo, ln2_g, ln2_b, ffn_up, ffn_conv_w, ffn_conv_b, ffn_down, ln3_g, ln3_b, loss_target, m_w_in, m_conv_w, m_conv_b, m_conv_ln_g, m_conv_ln_b, m_w_out, m_ln1_g, m_ln1_b, m_mem_wq, m_mem_wk, m_mem_wv, m_mem_wo, m_ln2_g, m_ln2_b, m_ffn_up, m_ffn_conv_w, m_ffn_conv_b, m_ffn_down, m_ln3_g, m_ln3_b, v_w_in, v_conv_w, v_conv_b, v_conv_ln_g, v_conv_ln_b, v_w_out, v_ln1_g, v_ln1_b, v_mem_wq, v_mem_wk, v_mem_wv, v_mem_wo, v_ln2_g, v_ln2_b, v_ffn_up, v_ffn_conv_w, v_ffn_conv_b, v_ffn_down, v_ln3_g, v_ln3_b):
    wts = dict(zip(WEIGHTS, (w_in, conv_w, conv_b, conv_ln_g, conv_ln_b, w_out, ln1_g, ln1_b, mem_wq, mem_wk, mem_wv,
                             mem_wo, ln2_g, ln2_b, ffn_up, ffn_conv_w, ffn_conv_b, ffn_down, ln3_g, ln3_b)))
    mom = dict(zip(WEIGHTS, (m_w_in, m_conv_w, m_conv_b, m_conv_ln_g, m_conv_ln_b, m_w_out, m_ln1_g, m_ln1_b, m_mem_wq,
                             m_mem_wk, m_mem_wv, m_mem_wo, m_ln2_g, m_ln2_b, m_ffn_up, m_ffn_conv_w, m_ffn_conv_b,
                             m_ffn_down, m_ln3_g, m_ln3_b)))
    var = dict(zip(WEIGHTS, (v_w_in, v_conv_w, v_conv_b, v_conv_ln_g, v_conv_ln_b, v_w_out, v_ln1_g, v_ln1_b, v_mem_wq,
                             v_mem_wk, v_mem_wv, v_mem_wo, v_ln2_g, v_ln2_b, v_ffn_up, v_ffn_conv_w, v_ffn_conv_b,
                             v_ffn_down, v_ln3_g, v_ln3_b)))
    depth = w_in.shape[0]
    alpha = (2.0 * depth) ** 0.25
    my_index = _dev_index(_mesh_pos())

    def row_blocks(src, n, col_sharded):
        return jnp.swapaxes(src[n], 1, 2) if col_sharded else src[n]

    bf16_blocks = {n: row_blocks(wts, n, cs).astype(BF16) for n, cs in MATRICES}
    shards = [{n: bf16_blocks[n][l] for n, _ in MATRICES} for l in range(depth)]
    tap_shapes = [wts[n].shape for n in TAPS]
    (gathered_taps, got), = _run_exchanges([_gather_exchange([_pack([wts[n] for n in TAPS]), shards[0]['w_in']])],
                                           name="gather_first")
    full_taps = {n: _join_columns(t) for n, t in zip(TAPS, _unpack(gathered_taps, tap_shapes))}
    full_w_in = _as_matrix(got)

    xs = x[0]
    memb = mem[0].astype(BF16)
    h, hb = xs, xs.astype(BF16)
    saved, weights = [], []
    for l in range(depth):
        w = {'w_in': full_w_in}
        w.update({n: full_taps[n][l] for n in TAPS})
        w.update({n: wts[n][l] for n in REPLICATED})
        h, hb, sv, w, full_w_in = _layer_fwd(h, hb, memb, w, alpha, shards[l],
                                             shards[l + 1]['w_in'] if l + 1 < depth else None)
        saved.append(sv)
        weights.append(w)

    dy, loss_row = _loss_and_grad(h, loss_target[0], name="loss")
    loss = lax.psum(_row_sum(loss_row, name="loss_sum")[0, 0], ("x", "y", "c"))

    results = {}
    col_sharded = dict(MATRICES)
    state = {n: [row_blocks(src, n, cs) for src in (wts, mom, var)] for n, cs in MATRICES}

    def update(n, l, parts):
        results[n] = _sum_adamw(parts, *state[n], layer=l, so_far=results.get(n), name="adamw_matrix")

    top = (dy, None, None)
    grads = [None] * depth
    carried = []
    for l in reversed(range(depth)):
        top, grads[l], received, left = _layer_bwd(top, saved[l], memb, weights[l], alpha, carried)
        for n, parts in received.items():
            update(n, l + 1 if n in dict(carried) else l, parts)
        carried = left
    small = REPLICATED + TAPS
    part = _pack([jnp.stack([grads[l][n] for l in range(depth)]) for n in small])
    last, (parts,) = _run_exchanges([_scatter_exchange([gm.reshape(N_DEV, -1, gm.shape[-1]) for _, gm in carried]),
                                     _gather_exchange([part])], name="scatter_last_gather_small")
    for (n, _), blocks in zip(carried, last):
        update(n, 0, blocks)
    grad_x = _axpy(_mm(top[0], top[1], name="mm_dx_in"), top[2], alpha=alpha, name="grad_x")[None]
    for n, cs in MATRICES:
        if cs:
            results[n] = [jnp.swapaxes(r, 1, 2) for r in results[n]]
    total = _sum_parts(parts, name="sum_small_grads")
    summed = dict(zip(small, _unpack(total, [wts[n].shape for n in REPLICATED] + [full_taps[n].shape for n in TAPS])))
    for n in TAPS:
        cols = wts[n].shape[-1]
        summed[n] = lax.dynamic_slice_in_dim(summed[n], my_index * cols, cols, axis=2)
    res = _sum_adamw(_pack([summed[n] for n in small])[None], *[_pack([src[n] for n in small])[None] for src in (wts, mom, var)],
                     layer=0, so_far=None, name="adamw_small")
    unpacked = [_unpack(r[0], [wts[n].shape for n in small]) for r in res]
    for i, n in enumerate(small):
        results[n] = [u[i] for u in unpacked]

    outs = [loss, grad_x]
    for kind in range(4):
        outs += [results[n][kind] for n in WEIGHTS]
    return tuple(outs)
```

```python
import functools

import jax
import jax.numpy as jnp
from jax import lax
from jax.experimental import pallas as pl
from jax.experimental.pallas import tpu as pltpu

F32 = jnp.float32
BF16 = jnp.bfloat16

N_DEV = 8
LANES = 128
PACK_W = 1024
PACK_ROW_ALIGN = 16
SB_HEAD_DIM = 64
MEM_HEAD_DIM = 256
LN_EPS = 1e-5
EXP_UNDERFLOW = 104.0
VMEM_LIMIT = 56 * 1024 * 1024

ADAM_LR = 0.001
ADAM_B1 = 0.9
ADAM_B2 = 0.999
ADAM_EPS = 1e-08
ADAM_WD = 0.01
ADAM_STEP = 10

IN_NAMES = ['x', 'mem', 'w_in', 'conv_w', 'conv_b', 'conv_ln_g', 'conv_ln_b', 'w_out', 'ln1_g', 'ln1_b',
            'mem_wq', 'mem_wk', 'mem_wv', 'mem_wo', 'ln2_g', 'ln2_b', 'ffn_up', 'ffn_conv_w', 'ffn_conv_b',
            'ffn_down', 'ln3_g', 'ln3_b']
WEIGHTS = IN_NAMES[2:]
MATRICES = [('w_in', True), ('w_out', False), ('mem_wq', False), ('mem_wk', False), ('mem_wv', False),
            ('mem_wo', False), ('ffn_up', True), ('ffn_down', False)]
TAPS = ['conv_w', 'ffn_conv_w']
REPLICATED = ['conv_b', 'conv_ln_g', 'conv_ln_b', 'ln1_g', 'ln1_b', 'ln2_g', 'ln2_b', 'ffn_conv_b', 'ln3_g', 'ln3_b']


def _pick(dim, pref, align=LANES):
    if dim <= pref:
        return dim
    fits = [t for t in range(align, pref + 1, align) if dim % t == 0]
    return fits[-1] if fits else dim


def _params(sem):
    return pltpu.CompilerParams(dimension_semantics=sem, vmem_limit_bytes=VMEM_LIMIT)


def _mm(a, b, *, ta=False, tb=False, out_dtype=F32, name, exchange=None):
    if ta:
        kdim, m = a.shape
        tm, tn, tk = _pick(m, 1408), _pick(b.shape[0 if tb else 1], 1024), _pick(kdim, 1024)
    else:
        m, kdim = a.shape
        tm, tn, tk = _pick(m, 512), _pick(b.shape[0 if tb else 1], 1536), _pick(kdim, 2816)
    if tb:
        n, kb = b.shape
    else:
        kb, n = b.shape
    assert kdim == kb, (a.shape, b.shape, ta, tb)
    grid = (m // tm, n // tn, kdim // tk)
    nk = grid[2]
    dims = (((0 if ta else 1,), (1 if tb else 0,)), ((), ()))
    n_ex_in = len(exchange.inputs) if exchange else 0
    n_ex_out = len(exchange.out_shapes) if exchange else 0

    def body(*refs):
        a_ref, b_ref = refs[:2]
        ex_in = refs[2:2 + n_ex_in]
        o_ref = refs[2 + n_ex_in]
        ex_out = refs[3 + n_ex_in:3 + n_ex_in + n_ex_out]
        scratch = refs[3 + n_ex_in + n_ex_out:]
        if nk > 1:
            acc_ref, scratch = scratch[0], scratch[1:]
        ids = [pl.program_id(d) for d in range(3)]
        if exchange:
            @pl.when((ids[0] == 0) & (ids[1] == 0) & (ids[2] == 0))
            def _():
                exchange.start(ex_in, ex_out, scratch)

        prod = lax.dot_general(a_ref[...].astype(BF16), b_ref[...].astype(BF16), dims,
                               preferred_element_type=F32)
        if nk == 1:
            o_ref[...] = prod.astype(out_dtype)
        else:
            k = ids[2]

            @pl.when(k == 0)
            def _():
                acc_ref[...] = prod

            @pl.when(k > 0)
            def _():
                acc_ref[...] += prod

            @pl.when(k == nk - 1)
            def _():
                o_ref[...] = acc_ref[...].astype(out_dtype)

        if exchange:
            @pl.when((ids[0] == grid[0] - 1) & (ids[1] == grid[1] - 1) & (ids[2] == grid[2] - 1))
            def _():
                exchange.finish(ex_in, ex_out, scratch)

    a_spec = pl.BlockSpec((tk, tm), lambda i, j, k: (k, i)) if ta else pl.BlockSpec((tm, tk), lambda i, j, k: (i, k))
    b_spec = pl.BlockSpec((tn, tk), lambda i, j, k: (j, k)) if tb else pl.BlockSpec((tk, tn), lambda i, j, k: (k, j))
    hbm = pl.BlockSpec(memory_space=pl.ANY)
    outs = pl.pallas_call(
        body, name=name,
        out_shape=(jax.ShapeDtypeStruct((m, n), out_dtype),) + tuple(exchange.out_shapes if exchange else ()),
        grid=grid,
        in_specs=[a_spec, b_spec] + [hbm] * n_ex_in,
        out_specs=(pl.BlockSpec((tm, tn), lambda i, j, k: (i, j)),) + (hbm,) * n_ex_out,
        scratch_shapes=([] if nk == 1 else [pltpu.VMEM((tm, tn), F32)]) + list(exchange.scratch_shapes if exchange else []),
        compiler_params=_params(("arbitrary",) * 3 if exchange else ("parallel", "parallel", "arbitrary")),
    )(a, b, *(exchange.inputs if exchange else ()))
    return (outs[0], list(outs[1:])) if exchange else outs[0]


def _ln_stats(r):
    mu = jnp.mean(r, axis=-1, keepdims=True)
    xc = r - mu
    var = jnp.mean(xc * xc, axis=-1, keepdims=True)
    rstd = lax.rsqrt(var + LN_EPS)
    return xc * rstd, rstd


def _ln_bwd(da, dres, r, g, *, alpha, name, ts=512):
    s, d = r.shape
    ts = _pick(s, ts)
    has_res = dres is not None

    def body(*refs):
        if has_res:
            da_ref, dres_ref, r_ref, g_ref, dr_ref, drb_ref, dg_ref, db_ref = refs
            dy = da_ref[...] + alpha * dres_ref[...]
        else:
            da_ref, r_ref, g_ref, dr_ref, drb_ref, dg_ref, db_ref = refs
            dy = da_ref[...]
        xhat, rstd = _ln_stats(r_ref[...])
        dxhat = dy * g_ref[...]
        m1 = jnp.mean(dxhat, axis=-1, keepdims=True)
        m2 = jnp.mean(dxhat * xhat, axis=-1, keepdims=True)
        dr = rstd * (dxhat - m1 - xhat * m2)
        dr_ref[...] = dr
        drb_ref[...] = dr.astype(BF16)

        @pl.when(pl.program_id(0) == 0)
        def _():
            dg_ref[...] = jnp.zeros_like(dg_ref)
            db_ref[...] = jnp.zeros_like(db_ref)

        dg_ref[...] += jnp.sum(dy * xhat, axis=0, keepdims=True)
        db_ref[...] += jnp.sum(dy, axis=0, keepdims=True)

    tok = pl.BlockSpec((ts, d), lambda i: (i, 0))
    vec = pl.BlockSpec((1, d), lambda i: (0, 0))
    ins = [da, dres, r, g.reshape(1, d)] if has_res else [da, r, g.reshape(1, d)]
    dr, drb, dg, db = pl.pallas_call(
        body, name=name,
        out_shape=(jax.ShapeDtypeStruct((s, d), F32), jax.ShapeDtypeStruct((s, d), BF16),
                   jax.ShapeDtypeStruct((1, d), F32), jax.ShapeDtypeStruct((1, d), F32)),
        grid=(s // ts,), in_specs=[tok] * (len(ins) - 1) + [vec], out_specs=(tok, tok, vec, vec),
        compiler_params=_params(("arbitrary",)),
    )(*ins)
    return dr, drb, dg.reshape(d), db.reshape(d)


def _mm_fused(a, b, fn, *, rows, vecs, out_dtypes, n_sums, tb=False, name, tm=512):
    m, kdim = a.shape
    n = b.shape[0] if tb else b.shape[1]
    assert kdim == (b.shape[1] if tb else b.shape[0])
    tm, tk = _pick(m, tm), _pick(kdim, 2816)
    nk = kdim // tk
    dims = (((1,), (1 if tb else 0,)), ((), ()))
    n_rows, n_vecs, n_outs = len(rows), len(vecs), len(out_dtypes)

    def body(*refs):
        a_ref, b_ref = refs[:2]
        row_refs = refs[2:2 + n_rows]
        vec_refs = refs[2 + n_rows:2 + n_rows + n_vecs]
        out_refs = refs[2 + n_rows + n_vecs:2 + n_rows + n_vecs + n_outs]
        sum_refs = refs[2 + n_rows + n_vecs + n_outs:2 + n_rows + n_vecs + n_outs + n_sums]
        i, k = pl.program_id(0), pl.program_id(1)

        def finish(product):
            res = fn(product, *[r[...] for r in row_refs], *[v[...] for v in vec_refs])
            for o_ref, o in zip(out_refs, res[:n_outs]):
                o_ref[...] = o.astype(o_ref.dtype)
            if n_sums:
                @pl.when(i == 0)
                def _():
                    for s_ref in sum_refs:
                        s_ref[...] = jnp.zeros_like(s_ref)

                for s_ref, part in zip(sum_refs, res[n_outs:]):
                    s_ref[...] += part

        prod = lax.dot_general(a_ref[...].astype(BF16), b_ref[...].astype(BF16), dims, preferred_element_type=F32)
        if nk == 1:
            finish(prod)
        else:
            acc_ref = refs[-1]

            @pl.when(k == 0)
            def _():
                acc_ref[...] = prod

            @pl.when((k > 0) & (k < nk - 1))
            def _():
                acc_ref[...] += prod

            @pl.when(k == nk - 1)
            def _():
                finish(acc_ref[...] + prod)

    tok = pl.BlockSpec((tm, n), lambda i, k: (i, 0))
    vec = pl.BlockSpec((1, n), lambda i, k: (0, 0))
    b_spec = pl.BlockSpec((n, tk), lambda i, k: (0, k)) if tb else pl.BlockSpec((tk, n), lambda i, k: (k, 0))
    return pl.pallas_call(
        body, name=name,
        out_shape=tuple(jax.ShapeDtypeStruct((m, n), dt) for dt in out_dtypes) + (jax.ShapeDtypeStruct((1, n), F32),) * n_sums,
        grid=(m // tm, nk),
        in_specs=[pl.BlockSpec((tm, tk), lambda i, k: (i, k)), b_spec] + [tok] * n_rows + [vec] * n_vecs,
        out_specs=(tok,) * n_outs + (vec,) * n_sums,
        scratch_shapes=[] if nk == 1 else [pltpu.VMEM((tm, n), F32)],
        compiler_params=_params(("arbitrary", "arbitrary") if n_sums else ("parallel", "arbitrary")),
    )(a, b, *rows, *[v.reshape(1, n) for v in vecs])


def _mm_res_ln(a, b, x, g, beta, *, alpha, name):
    def fn(f, x_t, g_t, b_t):
        r = alpha * x_t + f
        xhat, _ = _ln_stats(r)
        y = xhat * g_t + b_t
        return r, y, y

    return _mm_fused(a, b, fn, rows=[x], vecs=[g, beta], out_dtypes=[F32, F32, BF16], n_sums=0, name=name)


def _mm_ln_bwd(a, b, dres, r, g, *, tb=False, alpha, name):
    def fn(f, dres_t, r_t, g_t):
        dy = f + alpha * dres_t
        xhat, rstd = _ln_stats(r_t)
        dxhat = dy * g_t
        m1 = jnp.mean(dxhat, axis=-1, keepdims=True)
        m2 = jnp.mean(dxhat * xhat, axis=-1, keepdims=True)
        dr = rstd * (dxhat - m1 - xhat * m2)
        return dr, dr, jnp.sum(dy * xhat, axis=0, keepdims=True), jnp.sum(dy, axis=0, keepdims=True)

    dr, drb, dg, db = _mm_fused(a, b, fn, rows=[dres, r], vecs=[g], out_dtypes=[F32, BF16], n_sums=2, tb=tb, name=name)
    return dr, drb, dg.reshape(-1), db.reshape(-1)


def _axpy(a, b, *, alpha, name, ts=512):
    s, d = a.shape
    ts = _pick(s, ts)

    def body(a_ref, b_ref, o_ref):
        o_ref[...] = a_ref[...] + alpha * b_ref[...]

    tok = pl.BlockSpec((ts, d), lambda i: (i, 0))
    return pl.pallas_call(body, name=name, out_shape=jax.ShapeDtypeStruct((s, d), F32), grid=(s // ts,),
                          in_specs=[tok, tok], out_specs=tok, compiler_params=_params(("parallel",)))(a, b)


def _loss_and_grad(y, target, *, name, ts=512):
    s, d = y.shape
    ts = _pick(s, ts)
    inv_d = 1.0 / d

    def body(y_ref, t_ref, dy_ref, loss_ref):
        e = y_ref[...] - t_ref[...]
        dy_ref[...] = e * inv_d

        @pl.when(pl.program_id(0) == 0)
        def _():
            loss_ref[...] = jnp.zeros_like(loss_ref)

        loss_ref[...] += jnp.sum(e * e, axis=0, keepdims=True) * (0.5 * inv_d)

    tok = pl.BlockSpec((ts, d), lambda i: (i, 0))
    vec = pl.BlockSpec((1, d), lambda i: (0, 0))
    dy, part = pl.pallas_call(
        body, name=name,
        out_shape=(jax.ShapeDtypeStruct((s, d), F32), jax.ShapeDtypeStruct((1, d), F32)),
        grid=(s // ts,), in_specs=[tok, tok], out_specs=(tok, vec),
        compiler_params=_params(("arbitrary",)),
    )(y, target)
    return dy, part


def _row_sum(v, *, name):
    def body(v_ref, o_ref):
        o_ref[...] = jnp.sum(v_ref[...], axis=1, keepdims=True)

    return pl.pallas_call(body, name=name, out_shape=jax.ShapeDtypeStruct((1, 1), F32))(v)


def _sigmoid(x):
    return 1.0 / (1.0 + jnp.exp(-x))


def _row_chunks(s, pref=512):
    c = _pick(s, pref, 8)
    return [(i * c, c) for i in range(s // c)]


def _glu_conv_fwd(proj, w, b, *, cc, name, exchange=None):
    s = proj.shape[0]
    kw = w.shape[0]
    pad = 32
    assert kw - 1 <= pad
    ncb = cc // LANES
    chunks = _row_chunks(s)

    def body(a_ref, g_ref, w_ref, b_ref, o_ref, u0_ref):
        u0_ref[pl.ds(0, pad), :] = jnp.zeros((pad, LANES), F32)
        for r0, rc in chunks:
            u0_ref[pl.ds(pad + r0, rc), :] = a_ref[pl.ds(r0, rc), :] * _sigmoid(g_ref[pl.ds(r0, rc), :])
        for r0, rc in chunks:
            acc = jnp.zeros((rc, LANES), F32) + b_ref[...]
            for k in range(kw):
                acc = acc + w_ref[pl.ds(k, 1), :] * u0_ref[pl.ds(pad + r0 - (kw - 1) + k, rc), :]
            o_ref[pl.ds(r0, rc), :] = acc

    body, ex_in, ex_out, ex_scratch = _hosted(exchange, (ncb,), body)
    if exchange is not None:
        body = functools.partial(body, n_own_in=4, n_own_out=1)
    hbm = pl.BlockSpec(memory_space=pl.ANY)
    outs = pl.pallas_call(
        body, name=name,
        out_shape=(jax.ShapeDtypeStruct((s, cc), F32), *ex_out),
        grid=(ncb,),
        in_specs=[pl.BlockSpec((s, LANES), lambda c: (0, c)), pl.BlockSpec((s, LANES), lambda c: (0, ncb + c)),
                  pl.BlockSpec((kw, LANES), lambda c: (0, c)), pl.BlockSpec((1, LANES), lambda c: (0, c))] + [hbm] * len(ex_in),
        out_specs=(pl.BlockSpec((s, LANES), lambda c: (0, c)), *([hbm] * len(ex_out))),
        scratch_shapes=[pltpu.VMEM((s + pad, LANES), F32)] + ex_scratch,
        compiler_params=_params(("arbitrary",) if exchange else ("parallel",)),
    )(proj, proj, w, b.reshape(1, cc), *ex_in)
    return outs[0], list(outs[1:])


def _glu_conv_bwd(du1, proj, w, *, cc, name, exchange=None):
    s = proj.shape[0]
    kw = w.shape[0]
    pad = 32
    ncb = cc // LANES
    chunks = _row_chunks(s)

    def body(d_ref, a_ref, g_ref, w_ref, da_ref, dg_ref, dw_ref, db_ref, u0_ref, dp_ref):
        u0_ref[pl.ds(0, pad), :] = jnp.zeros((pad, LANES), F32)
        dp_ref[pl.ds(s, pad), :] = jnp.zeros((pad, LANES), F32)
        for r0, rc in chunks:
            u0_ref[pl.ds(pad + r0, rc), :] = a_ref[pl.ds(r0, rc), :] * _sigmoid(g_ref[pl.ds(r0, rc), :])
            dp_ref[pl.ds(r0, rc), :] = d_ref[pl.ds(r0, rc), :]
        dws = [jnp.zeros((1, LANES), F32) for _ in range(kw)]
        dbs = jnp.zeros((1, LANES), F32)
        for r0, rc in chunks:
            d = dp_ref[pl.ds(r0, rc), :]
            dbs = dbs + jnp.sum(d, axis=0, keepdims=True)
            du0 = jnp.zeros((rc, LANES), F32)
            for k in range(kw):
                du0 = du0 + w_ref[pl.ds(k, 1), :] * dp_ref[pl.ds(r0 + (kw - 1) - k, rc), :]
                dws[k] = dws[k] + jnp.sum(d * u0_ref[pl.ds(pad + r0 - (kw - 1) + k, rc), :], axis=0, keepdims=True)
            sg = _sigmoid(g_ref[pl.ds(r0, rc), :])
            a = a_ref[pl.ds(r0, rc), :]
            da_ref[pl.ds(r0, rc), :] = (du0 * sg).astype(BF16)
            dg_ref[pl.ds(r0, rc), :] = (du0 * a * sg * (1.0 - sg)).astype(BF16)
        for k in range(kw):
            dw_ref[pl.ds(k, 1), :] = dws[k]
        db_ref[...] = dbs

    col = lambda off: pl.BlockSpec((s, LANES), lambda c: (0, off + c))
    body, ex_in, ex_out, ex_scratch = _hosted(exchange, (ncb,), body)
    if exchange is not None:
        body = functools.partial(body, n_own_in=4, n_own_out=4)
    hbm = pl.BlockSpec(memory_space=pl.ANY)
    outs = pl.pallas_call(
        body, name=name,
        out_shape=(jax.ShapeDtypeStruct((s, cc), BF16), jax.ShapeDtypeStruct((s, cc), BF16),
                   jax.ShapeDtypeStruct((kw, cc), F32), jax.ShapeDtypeStruct((1, cc), F32), *ex_out),
        grid=(ncb,),
        in_specs=[col(0), col(0), col(ncb), pl.BlockSpec((kw, LANES), lambda c: (0, c))] + [hbm] * len(ex_in),
        out_specs=(col(0), col(0), pl.BlockSpec((kw, LANES), lambda c: (0, c)), pl.BlockSpec((1, LANES), lambda c: (0, c)),
                   *([hbm] * len(ex_out))),
        scratch_shapes=[pltpu.VMEM((s + pad, LANES), F32), pltpu.VMEM((s + pad, LANES), F32)] + ex_scratch,
        compiler_params=_params(("arbitrary",) if exchange else ("parallel",)),
    )(du1, proj, proj, w, *ex_in)
    da, dg, dw, db = outs[:4]
    return da, dg, dw, db.reshape(cc), list(outs[4:])


def _cln_silu_fwd(u1, g, b, *, name, ts=512):
    s, cc = u1.shape
    ts = _pick(s, ts)

    def body(u_ref, g_ref, b_ref, o_ref):
        xhat, _ = _ln_stats(u_ref[...])
        y = xhat * g_ref[...] + b_ref[...]
        o_ref[...] = (y * _sigmoid(y)).astype(BF16)

    tok = pl.BlockSpec((ts, cc), lambda i: (i, 0))
    vec = pl.BlockSpec((1, cc), lambda i: (0, 0))
    return pl.pallas_call(body, name=name, out_shape=jax.ShapeDtypeStruct((s, cc), BF16), grid=(s // ts,),
                          in_specs=[tok, vec, vec], out_specs=tok,
                          compiler_params=_params(("parallel",)))(u1, g.reshape(1, cc), b.reshape(1, cc))


def _cln_silu_bwd(dua, u1, g, b, *, name, ts=512):
    s, cc = u1.shape
    ts = _pick(s, ts)

    def body(d_ref, u_ref, g_ref, b_ref, du_ref, dg_ref, db_ref):
        xhat, rstd = _ln_stats(u_ref[...])
        y = xhat * g_ref[...] + b_ref[...]
        sg = _sigmoid(y)
        dy = d_ref[...] * (sg * (1.0 + y * (1.0 - sg)))
        dxhat = dy * g_ref[...]
        m1 = jnp.mean(dxhat, axis=-1, keepdims=True)
        m2 = jnp.mean(dxhat * xhat, axis=-1, keepdims=True)
        du_ref[...] = rstd * (dxhat - m1 - xhat * m2)

        @pl.when(pl.program_id(0) == 0)
        def _():
            dg_ref[...] = jnp.zeros_like(dg_ref)
            db_ref[...] = jnp.zeros_like(db_ref)

        dg_ref[...] += jnp.sum(dy * xhat, axis=0, keepdims=True)
        db_ref[...] += jnp.sum(dy, axis=0, keepdims=True)

    tok = pl.BlockSpec((ts, cc), lambda i: (i, 0))
    vec = pl.BlockSpec((1, cc), lambda i: (0, 0))
    du1, dg, db = pl.pallas_call(
        body, name=name,
        out_shape=(jax.ShapeDtypeStruct((s, cc), F32), jax.ShapeDtypeStruct((1, cc), F32),
                   jax.ShapeDtypeStruct((1, cc), F32)),
        grid=(s // ts,), in_specs=[tok, tok, vec, vec], out_specs=(tok, vec, vec),
        compiler_params=_params(("arbitrary",)),
    )(dua, u1, g.reshape(1, cc), b.reshape(1, cc))
    return du1, dg.reshape(cc), db.reshape(cc)


def _softplus_parts(z):
    lk = jnp.minimum(-z, 0.0) - jnp.log1p(jnp.exp(-jnp.abs(z)))
    return lk, z + lk


def _stack_heads(x, hms):
    return jnp.concatenate([jnp.where(hm, x, 0.0) for hm in hms], axis=0).astype(BF16)


def _heads_side_by_side(x_st, tq):
    return jnp.concatenate([x_st[:tq], x_st[tq:]], axis=1)


def _sb_tile_masks(tq):
    row = lax.broadcasted_iota(jnp.int32, (2 * tq, tq), 0)
    col = lax.broadcasted_iota(jnp.int32, (2 * tq, tq), 1)
    vis = (col < jnp.where(row >= tq, row - tq, row)).astype(F32)
    krow, kcol = row[:tq], col[:tq]
    return vis, (krow > kcol).astype(BF16), (krow < kcol).astype(BF16)


_NT = (((1,), (1,)), ((), ()))
_TN = (((0,), (0,)), ((), ()))


def _head_masks():
    lane = lax.broadcasted_iota(jnp.int32, (1, LANES), 1)
    return [(lane >= SB_HEAD_DIM * h) & (lane < SB_HEAD_DIM * (h + 1)) for h in range(2)]


def _hosted(exchange, grid, body):
    if exchange is None:
        return body, [], [], []
    n_in, n_out, n_sems = len(exchange.inputs), len(exchange.out_shapes), len(exchange.scratch_shapes)

    def wrapped(*refs, n_own_in, n_own_out):
        own_in, ex_in = refs[:n_own_in], refs[n_own_in:n_own_in + n_in]
        rest = refs[n_own_in + n_in:]
        own_out, ex_out = rest[:n_own_out], rest[n_own_out:n_own_out + n_out]
        own_scratch, sems = rest[n_own_out + n_out:len(rest) - n_sems], rest[len(rest) - n_sems:]
        ids = [pl.program_id(d) for d in range(len(grid))]
        first = functools.reduce(lambda x, y: x & y, [i == 0 for i in ids])
        last = functools.reduce(lambda x, y: x & y, [i == g - 1 for i, g in zip(ids, grid)])

        @pl.when(first)
        def _():
            exchange.start(ex_in, ex_out, sems)

        body(*own_in, *own_out, *own_scratch)

        @pl.when(last)
        def _():
            exchange.finish(ex_in, ex_out, sems)

    return wrapped, list(exchange.inputs), list(exchange.out_shapes), list(exchange.scratch_shapes)


def _sb_fwd(proj, *, col0, n_pairs, name, tq=256, exchange=None):
    s = proj.shape[0]
    tq = _pick(s, tq)
    nq = s // tq
    cb0 = col0 // LANES
    scale = SB_HEAD_DIM ** -0.5

    def body(q_ref, k_ref, v_ref, vis_ref, after_ref, o_ref, t_ref, n_ref):
        i = pl.program_id(1)
        hms = _head_masks()
        q_st = _stack_heads(q_ref[...] * scale, hms)

        def tile(j, c, acc, masked):
            start = pl.multiple_of(j * tq, tq)
            kb = k_ref[pl.ds(start, tq), :].astype(BF16)
            v_st = _stack_heads(v_ref[pl.ds(start, tq), :], hms)
            z = lax.dot_general(q_st, kb, _NT, preferred_element_type=F32)
            lk, lb = _softplus_parts(z)
            if masked:
                lk = lk * vis_ref[...]
            later = jnp.dot(lk.astype(BF16), after_ref[...], preferred_element_type=F32)
            a = jnp.exp(lb + later + c)
            if masked:
                a = a * vis_ref[...]
            acc = acc + jnp.dot(_heads_side_by_side(a.astype(BF16), tq), v_st, preferred_element_type=F32)
            return c + jnp.sum(lk, axis=1, keepdims=True), acc

        def more(st):
            return jnp.logical_and(st[0] < i, jnp.max(st[1]) > -EXP_UNDERFLOW)

        def step(st):
            c, acc = tile(i - 1 - st[0], st[1], st[2], False)
            return st[0] + 1, c, acc

        c, acc = tile(i, jnp.zeros((2 * tq, 1), F32), jnp.zeros((tq, LANES), F32), True)
        n, c, acc = lax.while_loop(more, step, (jnp.int32(0), c, acc))
        o_ref[...] = acc.astype(BF16)
        t_ref[...] = jnp.where(hms[0], c[:tq], c[tq:])
        n_ref[...] = jnp.zeros((8, LANES), F32) + n.astype(F32)

    grid = (n_pairs, nq)
    body, ex_in, ex_out, ex_scratch = _hosted(exchange, grid, body)
    if exchange is not None:
        body = functools.partial(body, n_own_in=5, n_own_out=3)
    hbm = pl.BlockSpec(memory_space=pl.ANY)
    seq = lambda off: pl.BlockSpec((s, LANES), lambda p, i: (0, cb0 + off + p))
    whole = lambda rows: pl.BlockSpec((rows, tq), lambda p, i: (0, 0))
    vis, m_after, _ = _sb_tile_masks(tq)
    outs = pl.pallas_call(
        body, name=name,
        out_shape=(jax.ShapeDtypeStruct((s, n_pairs * LANES), BF16), jax.ShapeDtypeStruct((n_pairs, s, LANES), F32),
                   jax.ShapeDtypeStruct((n_pairs, nq * 8, LANES), F32), *ex_out),
        grid=grid,
        in_specs=[pl.BlockSpec((tq, LANES), lambda p, i: (i, cb0 + p)), seq(n_pairs), seq(2 * n_pairs),
                  whole(2 * tq), whole(tq)] + [hbm] * len(ex_in),
        out_specs=(pl.BlockSpec((tq, LANES), lambda p, i: (i, p)), pl.BlockSpec((None, tq, LANES), lambda p, i: (p, i, 0)),
                   pl.BlockSpec((None, 8, LANES), lambda p, i: (p, i, 0)), *([hbm] * len(ex_out))),
        scratch_shapes=ex_scratch,
        compiler_params=_params(("arbitrary", "arbitrary") if exchange else ("parallel", "arbitrary")),
    )(proj, proj, proj, vis, m_after, *ex_in)
    return outs[0], outs[1], outs[2], list(outs[3:])


def _sb_bwd(proj, t_sum, n_walked, dua, *, col0, n_pairs, do_col0, name, tq=256, exchange=None):
    s = proj.shape[0]
    tq = _pick(s, tq)
    cb0 = col0 // LANES
    dcb0 = do_col0 // LANES
    scale = SB_HEAD_DIM ** -0.5

    def body(q_ref, k_ref, v_ref, t_ref, n_ref, do_ref, vis_ref, after_ref, before_ref, dq_ref, dk_ref, dv_ref):
        i = pl.program_id(1)

        @pl.when(i == 0)
        def _():
            dk_ref[...] = jnp.zeros_like(dk_ref)
            dv_ref[...] = jnp.zeros_like(dv_ref)

        hms = _head_masks()
        q_st = _stack_heads(q_ref[...] * scale, hms)
        do_st = _stack_heads(do_ref[...], hms)
        t_st = jnp.concatenate([t_ref[:, SB_HEAD_DIM * h:SB_HEAD_DIM * h + 1] for h in range(2)], axis=0)
        first = i - jnp.max(n_ref[...]).astype(jnp.int32)

        def tile(j, p_sum, g_sum, dq, masked):
            start = pl.multiple_of(j * tq, tq)
            k = k_ref[pl.ds(start, tq), :]
            z = lax.dot_general(q_st, k.astype(BF16), _NT, preferred_element_type=F32)
            lk_raw, lb = _softplus_parts(z)
            lk = lk_raw * vis_ref[...] if masked else lk_raw
            p_next = p_sum + jnp.sum(lk, axis=1, keepdims=True)
            later = jnp.dot(lk.astype(BF16), after_ref[...], preferred_element_type=F32)
            a = jnp.exp(lb + (t_st - p_next) + later)
            if masked:
                a = a * vis_ref[...]
            da = lax.dot_general(do_st, v_ref[pl.ds(start, tq), :].astype(BF16), _NT, preferred_element_type=F32)
            g = a * da
            g_before = g_sum + jnp.dot(g.astype(BF16), before_ref[...], preferred_element_type=F32)
            dz = g * jnp.exp(lk_raw) - g_before * jnp.exp(lb)
            if masked:
                dz = dz * vis_ref[...]
            dzb = dz.astype(BF16)
            dv_ref[pl.ds(start, tq), :] += lax.dot_general(a.astype(BF16), do_st, _TN, preferred_element_type=F32)
            dk_ref[pl.ds(start, tq), :] += lax.dot_general(dzb, q_st, _TN, preferred_element_type=F32)
            dq = dq + jnp.dot(_heads_side_by_side(dzb, tq), _stack_heads(k, hms), preferred_element_type=F32)
            return p_next, g_sum + jnp.sum(g, axis=1, keepdims=True), dq

        zero = jnp.zeros((2 * tq, 1), F32)
        st = lax.fori_loop(first, i, lambda j, st: tile(j, *st, False), (zero, zero, jnp.zeros((tq, LANES), F32)))
        dq_ref[...] = tile(i, *st, True)[2] * scale

    grid = (n_pairs, s // tq)
    body, ex_in, ex_out, ex_scratch = _hosted(exchange, grid, body)
    if exchange is not None:
        body = functools.partial(body, n_own_in=9, n_own_out=3)
    hbm = pl.BlockSpec(memory_space=pl.ANY)
    seq = lambda off: pl.BlockSpec((s, LANES), lambda p, i: (0, cb0 + off + p))
    whole = lambda rows: pl.BlockSpec((rows, tq), lambda p, i: (0, 0))
    out = jax.ShapeDtypeStruct((s, n_pairs * LANES), F32)
    res = pl.BlockSpec((s, LANES), lambda p, i: (0, p))
    outs = pl.pallas_call(
        body, name=name,
        out_shape=(out, out, out, *ex_out),
        grid=grid,
        in_specs=[pl.BlockSpec((tq, LANES), lambda p, i: (i, cb0 + p)), seq(n_pairs), seq(2 * n_pairs),
                  pl.BlockSpec((None, tq, LANES), lambda p, i: (p, i, 0)),
                  pl.BlockSpec((None, 8, LANES), lambda p, i: (p, i, 0)),
                  pl.BlockSpec((tq, LANES), lambda p, i: (i, dcb0 + p)),
                  whole(2 * tq), whole(tq), whole(tq)] + [hbm] * len(ex_in),
        out_specs=(pl.BlockSpec((tq, LANES), lambda p, i: (i, p)), res, res, *([hbm] * len(ex_out))),
        scratch_shapes=ex_scratch,
        compiler_params=_params(("arbitrary", "arbitrary")),
    )(proj, proj, proj, t_sum, n_walked, dua, *_sb_tile_masks(tq), *ex_in)
    return outs[0], outs[1], outs[2], list(outs[3:])


def _mem_attn_fwd(qm, km, vm, *, name, tq=512):
    s, d = qm.shape
    heads = d // MEM_HEAD_DIM
    mlen = km.shape[0]
    tq = _pick(s, tq)
    scale = MEM_HEAD_DIM ** -0.5

    def body(q_ref, k_ref, v_ref, o_ref):
        for h in range(heads):
            sl = slice(h * MEM_HEAD_DIM, (h + 1) * MEM_HEAD_DIM)
            q = (q_ref[:, sl] * scale).astype(BF16)
            sc = lax.dot_general(q, k_ref[:, sl].astype(BF16), _NT, preferred_element_type=F32)
            e = jnp.exp(sc - jnp.max(sc, axis=1, keepdims=True))
            p = e / jnp.sum(e, axis=1, keepdims=True)
            o_ref[:, sl] = jnp.dot(p.astype(BF16), v_ref[:, sl].astype(BF16), preferred_element_type=F32).astype(BF16)

    tok = pl.BlockSpec((tq, d), lambda i: (i, 0))
    kv = pl.BlockSpec((mlen, d), lambda i: (0, 0))
    return pl.pallas_call(body, name=name, out_shape=jax.ShapeDtypeStruct((s, d), BF16), grid=(s // tq,),
                          in_specs=[tok, kv, kv], out_specs=tok, compiler_params=_params(("parallel",)))(qm, km, vm)


def _mem_attn_bwd(qm, km, vm, do, *, name, tq=512):
    s, d = qm.shape
    heads = d // MEM_HEAD_DIM
    mlen = km.shape[0]
    tq = _pick(s, tq)
    scale = MEM_HEAD_DIM ** -0.5

    def body(q_ref, k_ref, v_ref, do_ref, dq_ref, dk_ref, dv_ref):
        @pl.when(pl.program_id(0) == 0)
        def _():
            dk_ref[...] = jnp.zeros_like(dk_ref)
            dv_ref[...] = jnp.zeros_like(dv_ref)

        for h in range(heads):
            sl = slice(h * MEM_HEAD_DIM, (h + 1) * MEM_HEAD_DIM)
            q = (q_ref[:, sl] * scale).astype(BF16)
            k = k_ref[:, sl].astype(BF16)
            v = v_ref[:, sl].astype(BF16)
            sc = lax.dot_general(q, k, _NT, preferred_element_type=F32)
            e = jnp.exp(sc - jnp.max(sc, axis=1, keepdims=True))
            p = e / jnp.sum(e, axis=1, keepdims=True)
            dob = do_ref[:, sl].astype(BF16)
            dv_ref[:, sl] += lax.dot_general(p.astype(BF16), dob, _TN, preferred_element_type=F32)
            dp = lax.dot_general(dob, v, _NT, preferred_element_type=F32)
            ds = (p * (dp - jnp.sum(dp * p, axis=1, keepdims=True))).astype(BF16)
            dq_ref[:, sl] = (jnp.dot(ds, k, preferred_element_type=F32) * scale).astype(BF16)
            dk_ref[:, sl] += lax.dot_general(ds, q, _TN, preferred_element_type=F32)

    tok = pl.BlockSpec((tq, d), lambda i: (i, 0))
    kv = pl.BlockSpec((mlen, d), lambda i: (0, 0))
    return pl.pallas_call(
        body, name=name,
        out_shape=(jax.ShapeDtypeStruct((s, d), BF16), jax.ShapeDtypeStruct((mlen, d), F32),
                   jax.ShapeDtypeStruct((mlen, d), F32)),
        grid=(s // tq,), in_specs=[tok, kv, kv, tok], out_specs=(tok, kv, kv),
        compiler_params=_params(("arbitrary",)),
    )(qm, km, vm, do)


def _ffn_act_fwd(up, w, b, *, name):
    s, two_f = up.shape
    ff = two_f // 2
    nfb = ff // LANES
    kw = w.shape[0]
    pad = 8
    chunks = _row_chunks(s)

    def body(v_ref, g_ref, wv_ref, wg_ref, bv_ref, bg_ref, o_ref, vp_ref, gp_ref):
        vp_ref[pl.ds(0, pad), :] = jnp.zeros((pad, LANES), F32)
        gp_ref[pl.ds(0, pad), :] = jnp.zeros((pad, LANES), F32)
        for r0, rc in chunks:
            vp_ref[pl.ds(pad + r0, rc), :] = v_ref[pl.ds(r0, rc), :].astype(F32)
            gp_ref[pl.ds(pad + r0, rc), :] = g_ref[pl.ds(r0, rc), :].astype(F32)
        for r0, rc in chunks:
            vc = jnp.zeros((rc, LANES), F32) + bv_ref[...]
            gc = jnp.zeros((rc, LANES), F32) + bg_ref[...]
            for k in range(kw):
                off = pad + r0 - (kw - 1) + k
                vc = vc + wv_ref[pl.ds(k, 1), :] * vp_ref[pl.ds(off, rc), :]
                gc = gc + wg_ref[pl.ds(k, 1), :] * gp_ref[pl.ds(off, rc), :]
            o_ref[pl.ds(r0, rc), :] = (gc * _sigmoid(gc) * vc).astype(BF16)

    col = lambda off: pl.BlockSpec((s, LANES), lambda c: (0, off + c))
    tap = lambda off: pl.BlockSpec((kw, LANES), lambda c: (0, off + c))
    vec = lambda off: pl.BlockSpec((1, LANES), lambda c: (0, off + c))
    return pl.pallas_call(
        body, name=name, out_shape=jax.ShapeDtypeStruct((s, ff), BF16), grid=(nfb,),
        in_specs=[col(0), col(nfb), tap(0), tap(nfb), vec(0), vec(nfb)], out_specs=col(0),
        scratch_shapes=[pltpu.VMEM((s + pad, LANES), F32), pltpu.VMEM((s + pad, LANES), F32)],
        compiler_params=_params(("parallel",)),
    )(up, up, w, w, b.reshape(1, two_f), b.reshape(1, two_f))


def _ffn_act_bwd(up, dact, w, b, *, name, exchange=None):
    s, two_f = up.shape
    ff = two_f // 2
    nfb = ff // LANES
    kw = w.shape[0]
    pad = 8
    chunks = _row_chunks(s)

    def body(v_ref, g_ref, d_ref, wv_ref, wg_ref, bv_ref, bg_ref, dv_ref, dg_ref, dwv_ref, dwg_ref, dbv_ref, dbg_ref,
             vp_ref, gp_ref, dvc_ref, dgc_ref):
        vp_ref[pl.ds(0, pad), :] = jnp.zeros((pad, LANES), F32)
        gp_ref[pl.ds(0, pad), :] = jnp.zeros((pad, LANES), F32)
        dvc_ref[pl.ds(s, pad), :] = jnp.zeros((pad, LANES), F32)
        dgc_ref[pl.ds(s, pad), :] = jnp.zeros((pad, LANES), F32)
        for r0, rc in chunks:
            vp_ref[pl.ds(pad + r0, rc), :] = v_ref[pl.ds(r0, rc), :].astype(F32)
            gp_ref[pl.ds(pad + r0, rc), :] = g_ref[pl.ds(r0, rc), :].astype(F32)
        dwv = [jnp.zeros((1, LANES), F32) for _ in range(kw)]
        dwg = [jnp.zeros((1, LANES), F32) for _ in range(kw)]
        dbv = jnp.zeros((1, LANES), F32)
        dbg = jnp.zeros((1, LANES), F32)
        for r0, rc in chunks:
            vc = jnp.zeros((rc, LANES), F32) + bv_ref[...]
            gc = jnp.zeros((rc, LANES), F32) + bg_ref[...]
            for k in range(kw):
                off = pad + r0 - (kw - 1) + k
                vc = vc + wv_ref[pl.ds(k, 1), :] * vp_ref[pl.ds(off, rc), :]
                gc = gc + wg_ref[pl.ds(k, 1), :] * gp_ref[pl.ds(off, rc), :]
            sg = _sigmoid(gc)
            d = d_ref[pl.ds(r0, rc), :].astype(F32)
            dvc = d * (gc * sg)
            dgc = d * vc * (sg * (1.0 + gc * (1.0 - sg)))
            dvc_ref[pl.ds(r0, rc), :] = dvc
            dgc_ref[pl.ds(r0, rc), :] = dgc
            dbv = dbv + jnp.sum(dvc, axis=0, keepdims=True)
            dbg = dbg + jnp.sum(dgc, axis=0, keepdims=True)
            for k in range(kw):
                off = pad + r0 - (kw - 1) + k
                dwv[k] = dwv[k] + jnp.sum(dvc * vp_ref[pl.ds(off, rc), :], axis=0, keepdims=True)
                dwg[k] = dwg[k] + jnp.sum(dgc * gp_ref[pl.ds(off, rc), :], axis=0, keepdims=True)
        for r0, rc in chunks:
            dv = jnp.zeros((rc, LANES), F32)
            dg = jnp.zeros((rc, LANES), F32)
            for k in range(kw):
                off = r0 + (kw - 1) - k
                dv = dv + wv_ref[pl.ds(k, 1), :] * dvc_ref[pl.ds(off, rc), :]
                dg = dg + wg_ref[pl.ds(k, 1), :] * dgc_ref[pl.ds(off, rc), :]
            dv_ref[pl.ds(r0, rc), :] = dv.astype(BF16)
            dg_ref[pl.ds(r0, rc), :] = dg.astype(BF16)
        for k in range(kw):
            dwv_ref[pl.ds(k, 1), :] = dwv[k]
            dwg_ref[pl.ds(k, 1), :] = dwg[k]
        dbv_ref[...] = dbv
        dbg_ref[...] = dbg

    col = lambda off: pl.BlockSpec((s, LANES), lambda c: (0, off + c))
    tap = lambda off: pl.BlockSpec((kw, LANES), lambda c: (0, off + c))
    vec = lambda off: pl.BlockSpec((1, LANES), lambda c: (0, off + c))
    big = lambda: pltpu.VMEM((s + pad, LANES), F32)
    body, ex_in, ex_out, ex_scratch = _hosted(exchange, (nfb,), body)
    if exchange is not None:
        body = functools.partial(body, n_own_in=7, n_own_out=6)
    hbm = pl.BlockSpec(memory_space=pl.ANY)
    outs = pl.pallas_call(
        body, name=name,
        out_shape=(jax.ShapeDtypeStruct((s, ff), BF16), jax.ShapeDtypeStruct((s, ff), BF16),
                   jax.ShapeDtypeStruct((kw, ff), F32), jax.ShapeDtypeStruct((kw, ff), F32),
                   jax.ShapeDtypeStruct((1, ff), F32), jax.ShapeDtypeStruct((1, ff), F32), *ex_out),
        grid=(nfb,),
        in_specs=[col(0), col(nfb), col(0), tap(0), tap(nfb), vec(0), vec(nfb)] + [hbm] * len(ex_in),
        out_specs=(col(0), col(0), tap(0), tap(0), vec(0), vec(0), *([hbm] * len(ex_out))),
        scratch_shapes=[big(), big(), big(), big()] + ex_scratch,
        compiler_params=_params(("arbitrary",) if exchange else ("parallel",)),
    )(up, up, dact, w, w, b.reshape(1, two_f), b.reshape(1, two_f), *ex_in)
    dv, dg, dwv, dwg, dbv, dbg = outs[:6]
    return (jnp.concatenate([dv, dg], axis=1), jnp.concatenate([dwv, dwg], axis=1),
            jnp.concatenate([dbv, dbg], axis=1).reshape(two_f), list(outs[6:]))


def _sum_parts(parts, *, name, tr=256):
    n_parts, rows, cols = parts.shape
    tr = _pick(rows, tr, 16)

    def body(p_ref, o_ref):
        g = p_ref[0].astype(F32)
        for k in range(1, n_parts):
            g = g + p_ref[k].astype(F32)
        o_ref[...] = g

    return pl.pallas_call(
        body, name=name, out_shape=jax.ShapeDtypeStruct((rows, cols), F32), grid=(rows // tr,),
        in_specs=[pl.BlockSpec((n_parts, tr, cols), lambda i: (0, i, 0))],
        out_specs=pl.BlockSpec((tr, cols), lambda i: (i, 0)), compiler_params=_params(("parallel",)),
    )(parts)


def _sum_adamw(parts, w, m, v, *, layer, so_far, name, tr=256):
    n_parts, rows, cols = parts.shape
    depth = w.shape[0]
    tr = _pick(rows, tr, 16)
    c1 = 1.0 / (1.0 - ADAM_B1 ** ADAM_STEP)
    c2 = 1.0 / (1.0 - ADAM_B2 ** ADAM_STEP)

    def body(p_ref, w_ref, m_ref, v_ref, *rest):
        g_ref, d_ref, nm_ref, nv_ref = rest[-4:]
        g = p_ref[0].astype(F32)
        for k in range(1, n_parts):
            g = g + p_ref[k].astype(F32)
        nm = ADAM_B1 * m_ref[...] + (1.0 - ADAM_B1) * g
        nv = ADAM_B2 * v_ref[...] + (1.0 - ADAM_B2) * (g * g)
        g_ref[...] = g
        nm_ref[...] = nm
        nv_ref[...] = nv
        d_ref[...] = -ADAM_LR * ((nm * c1) / (jnp.sqrt(nv * c2) + ADAM_EPS) + ADAM_WD * w_ref[...])

    blk = pl.BlockSpec((None, tr, cols), lambda i: (layer, i, 0))
    out = jax.ShapeDtypeStruct((depth, rows, cols), F32)
    kept = list(so_far) if so_far is not None else []
    return pl.pallas_call(
        body, name=name, out_shape=(out, out, out, out), grid=(rows // tr,),
        in_specs=[pl.BlockSpec((n_parts, tr, cols), lambda i: (0, i, 0)), blk, blk, blk]
        + [pl.BlockSpec(memory_space=pl.ANY)] * len(kept),
        out_specs=(blk, blk, blk, blk),
        input_output_aliases={4 + k: k for k in range(len(kept))},
        compiler_params=_params(("parallel",)),
    )(parts, w, m, v, *kept)


def _mesh_pos():
    return lax.axis_index("x"), lax.axis_index("y"), lax.axis_index("c")


def _flip(pos, k):
    x, y, c = pos
    return (1 - x if k & 4 else x, 1 - y if k & 2 else y, 1 - c if k & 1 else c)


def _dev_index(pos):
    return 4 * pos[0] + 2 * pos[1] + pos[2]


N_PEERS = N_DEV - 1


class _Exchange:
    def __init__(self, inputs, out_shapes, start, finish):
        n = len(inputs)
        self.inputs, self.out_shapes, self.start, self.finish = list(inputs), list(out_shapes), start, finish
        self.scratch_shapes = [pltpu.SemaphoreType.DMA((n * N_PEERS,)), pltpu.SemaphoreType.DMA((n * N_PEERS,)),
                               pltpu.SemaphoreType.DMA((n,))]


def _gather_exchange(xs):
    n = len(xs)

    def plan(x_refs, out_refs, sems):
        send_sems, recv_sems, local_sems = sems
        me = _mesh_pos()
        sibling = _flip(me, 1)
        chips = [_flip(me, 4), _flip(me, 2), _flip(me, 6)]

        def copy(a, k, block, to, from_input=False):
            slot = out_refs[a].at[_dev_index(block)]
            return pltpu.make_async_remote_copy(
                src_ref=x_refs[a] if from_input else slot, dst_ref=slot,
                send_sem=send_sems.at[a * N_PEERS + k], recv_sem=recv_sems.at[a * N_PEERS + k],
                device_id=to, device_id_type=pl.DeviceIdType.MESH)

        mine = [pltpu.make_async_copy(x_refs[a], out_refs[a].at[_dev_index(me)], local_sems.at[a]) for a in range(n)]
        first = [copy(a, 0, me, sibling, True) for a in range(n)]
        first += [copy(a, 1 + j, me, chip, True) for j, chip in enumerate(chips) for a in range(n)]
        return me, sibling, chips, copy, mine, first

    def start(x_refs, out_refs, sems):
        _, _, _, _, mine, first = plan(x_refs, out_refs, sems)
        for cp in mine + first:
            cp.start()

    def finish(x_refs, out_refs, sems):
        me, sibling, chips, copy, mine, first = plan(x_refs, out_refs, sems)
        passed = []
        for j, chip in enumerate(chips):
            for a in range(n):
                copy(a, 1 + j, chip, me).wait_recv()
                passed.append(copy(a, 4 + j, chip, sibling))
                passed[-1].start()
        for a in range(n):
            copy(a, 0, sibling, me).wait_recv()
        for j, chip in enumerate(chips):
            for a in range(n):
                copy(a, 4 + j, _flip(chip, 1), me).wait_recv()
        for cp in first + passed:
            cp.wait_send()
        for cp in mine:
            cp.wait()

    return _Exchange(xs, [jax.ShapeDtypeStruct((N_DEV,) + x.shape, x.dtype) for x in xs], start, finish)


def _scatter_exchange(xs):
    n = len(xs)

    def plan(x_refs, out_refs, sems):
        send_sems, recv_sems, local_sems = sems
        me = _mesh_pos()
        my_slot = _dev_index(me)

        def copy(a, k):
            peer = _flip(me, k)
            return pltpu.make_async_remote_copy(
                src_ref=x_refs[a].at[_dev_index(peer)], dst_ref=out_refs[a].at[my_slot],
                send_sem=send_sems.at[a * N_PEERS + k - 1], recv_sem=recv_sems.at[a * N_PEERS + k - 1],
                device_id=peer, device_id_type=pl.DeviceIdType.MESH)

        mine = [pltpu.make_async_copy(x_refs[a].at[my_slot], out_refs[a].at[my_slot], local_sems.at[a]) for a in range(n)]
        return mine, [copy(a, k) for k in range(1, N_DEV) for a in range(n)]

    def start(x_refs, out_refs, sems):
        mine, copies = plan(x_refs, out_refs, sems)
        for cp in mine + copies:
            cp.start()

    def finish(x_refs, out_refs, sems):
        mine, copies = plan(x_refs, out_refs, sems)
        for cp in copies:
            cp.wait_recv()
        for cp in copies:
            cp.wait_send()
        for cp in mine:
            cp.wait()

    return _Exchange(xs, [jax.ShapeDtypeStruct(x.shape, x.dtype) for x in xs], start, finish)


def _run_exchanges(exchanges, *, name):
    counts = [len(ex.inputs) for ex in exchanges]
    n = sum(counts)

    def body(*refs):
        offsets = [sum(counts[:e]) for e in range(len(exchanges))]
        views = [(refs[o:o + c], refs[n + o:n + o + c], refs[2 * n + 3 * e:2 * n + 3 * e + 3])
                 for e, (o, c) in enumerate(zip(offsets, counts))]
        for ex, view in zip(exchanges, views):
            ex.start(*view)
        for ex, view in zip(exchanges, views):
            ex.finish(*view)

    hbm = pl.BlockSpec(memory_space=pl.ANY)
    outs = pl.pallas_call(
        body, name=name, out_shape=tuple(s for ex in exchanges for s in ex.out_shapes), in_specs=[hbm] * n,
        out_specs=tuple([hbm] * n), scratch_shapes=[s for ex in exchanges for s in ex.scratch_shapes],
    )(*[x for ex in exchanges for x in ex.inputs])
    return [list(outs[sum(counts[:e]):sum(counts[:e + 1])]) for e in range(len(exchanges))]


def _pack(arrays):
    flat = jnp.concatenate([a.reshape(-1) for a in arrays])
    n = flat.shape[0]
    tile = PACK_W * PACK_ROW_ALIGN
    total = -(-n // tile) * tile
    return jnp.pad(flat, (0, total - n)).reshape(total // PACK_W, PACK_W)


def _unpack(buf, shapes):
    lead = buf.shape[:-2]
    flat = buf.reshape(lead + (-1,))
    out, off = [], 0
    for shp in shapes:
        n = 1
        for dim in shp:
            n *= dim
        out.append(flat[..., off:off + n].reshape(lead + tuple(shp)))
        off += n
    return out


def _join_columns(blocks):
    return jnp.moveaxis(blocks, 0, 2).reshape(blocks.shape[1], blocks.shape[2], -1)


def _mm_hosting(a, b, exchange, **kw):
    if exchange is None:
        return _mm(a, b, **kw), []
    return _mm(a, b, exchange=exchange, **kw)


GATHERED_BY_ATTENTION = ['w_out', 'mem_wq', 'mem_wk', 'mem_wv', 'mem_wo', 'ffn_up']
GATHERED_BY_CONV = ['ffn_down']


def _as_matrix(blocks):
    return blocks.reshape(-1, blocks.shape[-1])


def _layer_fwd(x, xb, memb, w, alpha, shards, next_w_in):
    w = dict(w)
    cc = w['conv_w'].shape[1]
    n_pairs = (N_DEV * shards['w_out'].shape[0] - cc) // LANES

    proj = _mm(xb, w['w_in'], tb=True, name="mm_proj")
    u1, got = _glu_conv_fwd(proj, w['conv_w'], w['conv_b'], cc=cc, name="glu_conv_fwd",
                            exchange=_gather_exchange([shards[n] for n in GATHERED_BY_CONV]))
    w.update({n: _as_matrix(f) for n, f in zip(GATHERED_BY_CONV, got)})
    u = _cln_silu_fwd(u1, w['conv_ln_g'], w['conv_ln_b'], name="cln_silu_fwd")
    att, t_sum, n_walked, got = _sb_fwd(proj, col0=2 * cc, n_pairs=n_pairs, name="sb_fwd",
                                        exchange=_gather_exchange([shards[n] for n in GATHERED_BY_ATTENTION]))
    w.update({n: _as_matrix(f) for n, f in zip(GATHERED_BY_ATTENTION, got)})
    ua = jnp.concatenate([u, att], axis=1)
    r1, x1, x1b = _mm_res_ln(ua, w['w_out'], x, w['ln1_g'], w['ln1_b'], alpha=alpha, name="mm_mix_ln")
    qm = _mm(x1b, w['mem_wq'], name="mm_memq")
    km = _mm(memb, w['mem_wk'], name="mm_memkv")
    vm = _mm(memb, w['mem_wv'], name="mm_memkv")
    o = _mem_attn_fwd(qm, km, vm, name="mem_attn_fwd")
    r2, x2, x2b = _mm_res_ln(o, w['mem_wo'], x1, w['ln2_g'], w['ln2_b'], alpha=alpha, name="mm_mix_ln")
    up, got = _mm_hosting(x2b, w['ffn_up'], _gather_exchange([next_w_in]) if next_w_in is not None else None,
                          tb=True, out_dtype=BF16, name="mm_up")
    act = _ffn_act_fwd(up, w['ffn_conv_w'], w['ffn_conv_b'], name="ffn_act_fwd")
    r3, x3, x3b = _mm_res_ln(act, w['ffn_down'], x2, w['ln3_g'], w['ln3_b'], alpha=alpha, name="mm_down_ln")
    saved = dict(xb=xb, proj=proj, u1=u1, t_sum=t_sum, n_walked=n_walked, ua=ua, r1=r1, x1b=x1b, qm=qm, km=km, vm=vm,
                 o=o, r2=r2, x2b=x2b, up=up, act=act, r3=r3)
    return x3, x3b, saved, w, _as_matrix(got[0]) if got else None


def _layer_bwd(top, sv, memb, w, alpha, carried):
    g, received = {}, {}
    cc = w['conv_w'].shape[1]
    n_pairs = (w['w_out'].shape[0] - cc) // LANES

    def sending(sends):
        ex = _scatter_exchange([gm.reshape(N_DEV, -1, gm.shape[-1]) for _, gm in sends]) if sends else None
        return ex, lambda got: received.update({key: blocks for (key, _), blocks in zip(sends, got)})

    if top[1] is None:
        dr3, dr3b, g['ln3_g'], g['ln3_b'] = _ln_bwd(top[0], None, sv['r3'], w['ln3_g'], alpha=alpha, name="ln_bwd")
    else:
        dr3, dr3b, g['ln3_g'], g['ln3_b'] = _mm_ln_bwd(*top, sv['r3'], w['ln3_g'], alpha=alpha, name="mm_dx_in_ln")
    g_down = _mm(sv['act'], dr3b, ta=True, out_dtype=BF16, name="mm_dw_down")
    dact = _mm(dr3b, w['ffn_down'], tb=True, out_dtype=BF16, name="mm_dact")
    ex, file = sending(carried + [('ffn_down', g_down)])
    dup, g['ffn_conv_w'], g['ffn_conv_b'], got = _ffn_act_bwd(sv['up'], dact, w['ffn_conv_w'], w['ffn_conv_b'],
                                                                name="ffn_act_bwd", exchange=ex)
    file(got)
    g_up = _mm(dup, sv['x2b'], ta=True, out_dtype=BF16, name="mm_dw_up")
    dr2, dr2b, g['ln2_g'], g['ln2_b'] = _mm_ln_bwd(dup, w['ffn_up'], dr3, sv['r2'], w['ln2_g'], alpha=alpha, name="mm_dx_up_ln")
    g_wo = _mm(sv['o'], dr2b, ta=True, out_dtype=BF16, name="mm_dw_sq")
    do = _mm(dr2b, w['mem_wo'], tb=True, out_dtype=BF16, name="mm_dx_sq")
    dqm, dkm, dvm = _mem_attn_bwd(sv['qm'], sv['km'], sv['vm'], do, name="mem_attn_bwd")
    g_wq = _mm(sv['x1b'], dqm, ta=True, out_dtype=BF16, name="mm_dw_sq")
    g_wk = _mm(memb, dkm, ta=True, out_dtype=BF16, name="mm_dw_memkv")
    g_wv = _mm(memb, dvm, ta=True, out_dtype=BF16, name="mm_dw_memkv")
    dr1, dr1b, g['ln1_g'], g['ln1_b'] = _mm_ln_bwd(dqm, w['mem_wq'], dr2, sv['r1'], w['ln1_g'], tb=True, alpha=alpha,
                                                    name="mm_dx_sq_ln")
    g_out = _mm(sv['ua'], dr1b, ta=True, out_dtype=BF16, name="mm_dw_sq")
    dua = _mm(dr1b, w['w_out'], tb=True, name="mm_dx_sq")
    du1, g['conv_ln_g'], g['conv_ln_b'] = _cln_silu_bwd(dua, sv['u1'], w['conv_ln_g'], w['conv_ln_b'], name="cln_silu_bwd")
    ex, file = sending([('mem_wo', g_wo), ('mem_wq', g_wq), ('w_out', g_out)])
    dga, dgg, g['conv_w'], g['conv_b'], got = _glu_conv_bwd(du1, sv['proj'], w['conv_w'], cc=cc, name="glu_conv_bwd",
                                                             exchange=ex)
    file(got)
    ex, file = sending([('ffn_up', g_up), ('mem_wk', g_wk), ('mem_wv', g_wv)])
    dq, dk, dv, got = _sb_bwd(sv['proj'], sv['t_sum'], sv['n_walked'], dua, col0=2 * cc, n_pairs=n_pairs, do_col0=cc,
                              name="sb_bwd", exchange=ex)
    file(got)
    dproj = jnp.concatenate([dga, dgg, dq.astype(BF16), dk.astype(BF16), dv.astype(BF16)], axis=1)
    g_in = _mm(dproj, sv['xb'], ta=True, out_dtype=BF16, name="mm_dw_in")
    return (dproj, w['w_in'], dr1), g, received, [('w_in', g_in)]


def kernel(x, mem, w_in, conv_w, conv_b, conv_ln_g, conv_ln_b, w_out, ln1_g, ln1_b, mem_wq, mem_wk, mem_wv, mem_wo, ln2_g, ln2_b, ffn_up, ffn_conv_w, ffn_conv_b, ffn_down, ln3_g, ln3_b, loss_target, m_w_in, m_conv_w, m_conv_b, m_conv_ln_g, m_conv_ln_b, m_w_out, m_ln1_g, m_ln1_b, m_mem_wq, m_mem_wk, m_mem_wv, m_mem_wo, m_ln2_g, m_ln2_b, m_ffn_up, m_ffn_conv_w, m_ffn_conv_b, m_ffn_down, m_ln3_g, m_ln3_b, v_w_in, v_conv_w, v_conv_b, v_conv_ln_g, v_conv_ln_b, v_w_out, v_ln1_g, v_ln1_b, v_mem_wq, v_mem_wk, v_mem_wv, v_mem_wo, v_ln2_g, v_ln2_b, v_ffn_up, v_ffn_conv_w, v_ffn_conv_b, v_ffn_down, v_ln3_g, v_ln3_b):
    wts = dict(zip(WEIGHTS, (w_in, conv_w, conv_b, conv_ln_g, conv_ln_b, w_out, ln1_g, ln1_b, mem_wq, mem_wk, mem_wv,
                             mem_wo, ln2_g, ln2_b, ffn_up, ffn_conv_w, ffn_conv_b, ffn_down, ln3_g, ln3_b)))
    mom = dict(zip(WEIGHTS, (m_w_in, m_conv_w, m_conv_b, m_conv_ln_g, m_conv_ln_b, m_w_out, m_ln1_g, m_ln1_b, m_mem_wq,
                             m_mem_wk, m_mem_wv, m_mem_wo, m_ln2_g, m_ln2_b, m_ffn_up, m_ffn_conv_w, m_ffn_conv_b,
                             m_ffn_down, m_ln3_g, m_ln3_b)))
    var = dict(zip(WEIGHTS, (v_w_in, v_conv_w, v_conv_b, v_conv_ln_g, v_conv_ln_b, v_w_out, v_ln1_g, v_ln1_b, v_mem_wq,
                             v_mem_wk, v_mem_wv, v_mem_wo, v_ln2_g, v_ln2_b, v_ffn_up, v_ffn_conv_w, v_ffn_conv_b,
                             v_ffn_down, v_ln3_g, v_ln3_b)))
    depth = w_in.shape[0]
    alpha = (2.0 * depth) ** 0.25
    my_index = _dev_index(_mesh_pos())

    def row_blocks(src, n, col_sharded):
        return jnp.swapaxes(src[n], 1, 2) if col_sharded else src[n]

    bf16_blocks = {n: row_blocks(wts, n, cs).astype(BF16) for n, cs in MATRICES}
    shards = [{n: bf16_blocks[n][l] for n, _ in MATRICES} for l in range(depth)]
    tap_shapes = [wts[n].shape for n in TAPS]
    (gathered_taps, got), = _run_exchanges([_gather_exchange([_pack([wts[n] for n in TAPS]), shards[0]['w_in']])],
                                           name="gather_first")
    full_taps = {n: _join_columns(t) for n, t in zip(TAPS, _unpack(gathered_taps, tap_shapes))}
    full_w_in = _as_matrix(got)

    xs = x[0]
    memb = mem[0].astype(BF16)
    h, hb = xs, xs.astype(BF16)
    saved, weights = [], []
    for l in range(depth):
        w = {'w_in': full_w_in}
        w.update({n: full_taps[n][l] for n in TAPS})
        w.update({n: wts[n][l] for n in REPLICATED})
        h, hb, sv, w, full_w_in = _layer_fwd(h, hb, memb, w, alpha, shards[l],
                                             shards[l + 1]['w_in'] if l + 1 < depth else None)
        saved.append(sv)
        weights.append(w)

    dy, loss_row = _loss_and_grad(h, loss_target[0], name="loss")
    loss = lax.psum(_row_sum(loss_row, name="loss_sum")[0, 0], ("x", "y", "c"))

    results = {}
    col_sharded = dict(MATRICES)
    state = {n: [row_blocks(src, n, cs) for src in (wts, mom, var)] for n, cs in MATRICES}

    def update(n, l, parts):
        results[n] = _sum_adamw(parts, *state[n], layer=l, so_far=results.get(n), name="adamw_matrix")

    top = (dy, None, None)
    grads = [None] * depth
    carried = []
    for l in reversed(range(depth)):
        top, grads[l], received, left = _layer_bwd(top, saved[l], memb, weights[l], alpha, carried)
        for n, parts in received.items():
            update(n, l + 1 if n in dict(carried) else l, parts)
        carried = left
    small = REPLICATED + TAPS
    part = _pack([jnp.stack([grads[l][n] for l in range(depth)]) for n in small])
    last, (parts,) = _run_exchanges([_scatter_exchange([gm.reshape(N_DEV, -1, gm.shape[-1]) for _, gm in carried]),
                                     _gather_exchange([part])], name="scatter_last_gather_small")
    for (n, _), blocks in zip(carried, last):
        update(n, 0, blocks)
    grad_x = _axpy(_mm(top[0], top[1], name="mm_dx_in"), top[2], alpha=alpha, name="grad_x")[None]
    for n, cs in MATRICES:
        if cs:
            results[n] = [jnp.swapaxes(r, 1, 2) for r in results[n]]
    total = _sum_parts(parts, name="sum_small_grads")
    summed = dict(zip(small, _unpack(total, [wts[n].shape for n in REPLICATED] + [full_taps[n].shape for n in TAPS])))
    for n in TAPS:
        cols = wts[n].shape[-1]
        summed[n] = lax.dynamic_slice_in_dim(summed[n], my_index * cols, cols, axis=2)
    res = _sum_adamw(_pack([summed[n] for n in small])[None], *[_pack([src[n] for n in small])[None] for src in (wts, mom, var)],
                     layer=0, so_far=None, name="adamw_small")
    unpacked = [_unpack(r[0], [wts[n].shape for n in small]) for r in res]
    for i, n in enumerate(small):
        results[n] = [u[i] for u in unpacked]

    outs = [loss, grad_x]
    for kind in range(4):
        outs += [results[n][kind] for n in WEIGHTS]
    return tuple(outs)
```

```python
import functools

import jax
import jax.numpy as jnp
from jax import lax
from jax.experimental import pallas as pl
from jax.experimental.pallas import tpu as pltpu

F32 = jnp.float32
BF16 = jnp.bfloat16

N_DEV = 8
LANES = 128
PACK_W = 1024
PACK_ROW_ALIGN = 16
SB_HEAD_DIM = 64
MEM_HEAD_DIM = 256
LN_EPS = 1e-5
EXP_UNDERFLOW = 104.0
VMEM_LIMIT = 56 * 1024 * 1024

ADAM_LR = 0.001
ADAM_B1 = 0.9
ADAM_B2 = 0.999
ADAM_EPS = 1e-08
ADAM_WD = 0.01
ADAM_STEP = 10

IN_NAMES = ['x', 'mem', 'w_in', 'conv_w', 'conv_b', 'conv_ln_g', 'conv_ln_b', 'w_out', 'ln1_g', 'ln1_b',
            'mem_wq', 'mem_wk', 'mem_wv', 'mem_wo', 'ln2_g', 'ln2_b', 'ffn_up', 'ffn_conv_w', 'ffn_conv_b',
            'ffn_down', 'ln3_g', 'ln3_b']
WEIGHTS = IN_NAMES[2:]
MATRICES = [('w_in', True), ('w_out', False), ('mem_wq', False), ('mem_wk', False), ('mem_wv', False),
            ('mem_wo', False), ('ffn_up', True), ('ffn_down', False)]
TAPS = ['conv_w', 'ffn_conv_w']
REPLICATED = ['conv_b', 'conv_ln_g', 'conv_ln_b', 'ln1_g', 'ln1_b', 'ln2_g', 'ln2_b', 'ffn_conv_b', 'ln3_g', 'ln3_b']


def _pick(dim, pref, align=LANES):
    if dim <= pref:
        return dim
    fits = [t for t in range(align, pref + 1, align) if dim % t == 0]
    return fits[-1] if fits else dim


def _params(sem):
    return pltpu.CompilerParams(dimension_semantics=sem, vmem_limit_bytes=VMEM_LIMIT)


def _mm(a, b, *, ta=False, tb=False, out_dtype=F32, name, exchange=None):
    if ta:
        kdim, m = a.shape
        tm, tn, tk = _pick(m, 1408), _pick(b.shape[0 if tb else 1], 1024), _pick(kdim, 1024)
    else:
        m, kdim = a.shape
        tm, tn, tk = _pick(m, 512), _pick(b.shape[0 if tb else 1], 1536), _pick(kdim, 2816)
    if tb:
        n, kb = b.shape
    else:
        kb, n = b.shape
    assert kdim == kb, (a.shape, b.shape, ta, tb)
    grid = (m // tm, n // tn, kdim // tk)
    nk = grid[2]
    dims = (((0 if ta else 1,), (1 if tb else 0,)), ((), ()))
    n_ex_in = len(exchange.inputs) if exchange else 0
    n_ex_out = len(exchange.out_shapes) if exchange else 0

    def body(*refs):
        a_ref, b_ref = refs[:2]
        ex_in = refs[2:2 + n_ex_in]
        o_ref = refs[2 + n_ex_in]
        ex_out = refs[3 + n_ex_in:3 + n_ex_in + n_ex_out]
        scratch = refs[3 + n_ex_in + n_ex_out:]
        if nk > 1:
            acc_ref, scratch = scratch[0], scratch[1:]
        ids = [pl.program_id(d) for d in range(3)]
        if exchange:
            @pl.when((ids[0] == 0) & (ids[1] == 0) & (ids[2] == 0))
            def _():
                exchange.start(ex_in, ex_out, scratch)

        prod = lax.dot_general(a_ref[...].astype(BF16), b_ref[...].astype(BF16), dims,
                               preferred_element_type=F32)
        if nk == 1:
            o_ref[...] = prod.astype(out_dtype)
        else:
            k = ids[2]

            @pl.when(k == 0)
            def _():
                acc_ref[...] = prod

            @pl.when(k > 0)
            def _():
                acc_ref[...] += prod

            @pl.when(k == nk - 1)
            def _():
                o_ref[...] = acc_ref[...].astype(out_dtype)

        if exchange:
            @pl.when((ids[0] == grid[0] - 1) & (ids[1] == grid[1] - 1) & (ids[2] == grid[2] - 1))
            def _():
                exchange.finish(ex_in, ex_out, scratch)

    a_spec = pl.BlockSpec((tk, tm), lambda i, j, k: (k, i)) if ta else pl.BlockSpec((tm, tk), lambda i, j, k: (i, k))
    b_spec = pl.BlockSpec((tn, tk), lambda i, j, k: (j, k)) if tb else pl.BlockSpec((tk, tn), lambda i, j, k: (k, j))
    hbm = pl.BlockSpec(memory_space=pl.ANY)
    outs = pl.pallas_call(
        body, name=name,
        out_shape=(jax.ShapeDtypeStruct((m, n), out_dtype),) + tuple(exchange.out_shapes if exchange else ()),
        grid=grid,
        in_specs=[a_spec, b_spec] + [hbm] * n_ex_in,
        out_specs=(pl.BlockSpec((tm, tn), lambda i, j, k: (i, j)),) + (hbm,) * n_ex_out,
        scratch_shapes=([] if nk == 1 else [pltpu.VMEM((tm, tn), F32)]) + list(exchange.scratch_shapes if exchange else []),
        compiler_params=_params(("arbitrary",) * 3 if exchange else ("parallel", "parallel", "arbitrary")),
    )(a, b, *(exchange.inputs if exchange else ()))
    return (outs[0], list(outs[1:])) if exchange else outs[0]


def _ln_stats(r):
    mu = jnp.mean(r, axis=-1, keepdims=True)
    xc = r - mu
    var = jnp.mean(xc * xc, axis=-1, keepdims=True)
    rstd = lax.rsqrt(var + LN_EPS)
    return xc * rstd, rstd


def _ln_bwd(da, dres, r, g, *, alpha, name, ts=512):
    s, d = r.shape
    ts = _pick(s, ts)
    has_res = dres is not None

    def body(*refs):
        if has_res:
            da_ref, dres_ref, r_ref, g_ref, dr_ref, drb_ref, dg_ref, db_ref = refs
            dy = da_ref[...] + alpha * dres_ref[...]
        else:
            da_ref, r_ref, g_ref, dr_ref, drb_ref, dg_ref, db_ref = refs
            dy = da_ref[...]
        xhat, rstd = _ln_stats(r_ref[...])
        dxhat = dy * g_ref[...]
        m1 = jnp.mean(dxhat, axis=-1, keepdims=True)
        m2 = jnp.mean(dxhat * xhat, axis=-1, keepdims=True)
        dr = rstd * (dxhat - m1 - xhat * m2)
        dr_ref[...] = dr
        drb_ref[...] = dr.astype(BF16)

        @pl.when(pl.program_id(0) == 0)
        def _():
            dg_ref[...] = jnp.zeros_like(dg_ref)
            db_ref[...] = jnp.zeros_like(db_ref)

        dg_ref[...] += jnp.sum(dy * xhat, axis=0, keepdims=True)
        db_ref[...] += jnp.sum(dy, axis=0, keepdims=True)

    tok = pl.BlockSpec((ts, d), lambda i: (i, 0))
    vec = pl.BlockSpec((1, d), lambda i: (0, 0))
    ins = [da, dres, r, g.reshape(1, d)] if has_res else [da, r, g.reshape(1, d)]
    dr, drb, dg, db = pl.pallas_call(
        body, name=name,
        out_shape=(jax.ShapeDtypeStruct((s, d), F32), jax.ShapeDtypeStruct((s, d), BF16),
                   jax.ShapeDtypeStruct((1, d), F32), jax.ShapeDtypeStruct((1, d), F32)),
        grid=(s // ts,), in_specs=[tok] * (len(ins) - 1) + [vec], out_specs=(tok, tok, vec, vec),
        compiler_params=_params(("arbitrary",)),
    )(*ins)
    return dr, drb, dg.reshape(d), db.reshape(d)


def _mm_fused(a, b, fn, *, rows, vecs, out_dtypes, n_sums, tb=False, name, tm=512):
    m, kdim = a.shape
    n = b.shape[0] if tb else b.shape[1]
    assert kdim == (b.shape[1] if tb else b.shape[0])
    tm, tk = _pick(m, tm), _pick(kdim, 2816)
    nk = kdim // tk
    dims = (((1,), (1 if tb else 0,)), ((), ()))
    n_rows, n_vecs, n_outs = len(rows), len(vecs), len(out_dtypes)

    def body(*refs):
        a_ref, b_ref = refs[:2]
        row_refs = refs[2:2 + n_rows]
        vec_refs = refs[2 + n_rows:2 + n_rows + n_vecs]
        out_refs = refs[2 + n_rows + n_vecs:2 + n_rows + n_vecs + n_outs]
        sum_refs = refs[2 + n_rows + n_vecs + n_outs:2 + n_rows + n_vecs + n_outs + n_sums]
        i, k = pl.program_id(0), pl.program_id(1)

        def finish(product):
            res = fn(product, *[r[...] for r in row_refs], *[v[...] for v in vec_refs])
            for o_ref, o in zip(out_refs, res[:n_outs]):
                o_ref[...] = o.astype(o_ref.dtype)
            if n_sums:
                @pl.when(i == 0)
                def _():
                    for s_ref in sum_refs:
                        s_ref[...] = jnp.zeros_like(s_ref)

                for s_ref, part in zip(sum_refs, res[n_outs:]):
                    s_ref[...] += part

        prod = lax.dot_general(a_ref[...].astype(BF16), b_ref[...].astype(BF16), dims, preferred_element_type=F32)
        if nk == 1:
            finish(prod)
        else:
            acc_ref = refs[-1]

            @pl.when(k == 0)
            def _():
                acc_ref[...] = prod

            @pl.when((k > 0) & (k < nk - 1))
            def _():
                acc_ref[...] += prod

            @pl.when(k == nk - 1)
            def _():
                finish(acc_ref[...] + prod)

    tok = pl.BlockSpec((tm, n), lambda i, k: (i, 0))
    vec = pl.BlockSpec((1, n), lambda i, k: (0, 0))
    b_spec = pl.BlockSpec((n, tk), lambda i, k: (0, k)) if tb else pl.BlockSpec((tk, n), lambda i, k: (k, 0))
    return pl.pallas_call(
        body, name=name,
        out_shape=tuple(jax.ShapeDtypeStruct((m, n), dt) for dt in out_dtypes) + (jax.ShapeDtypeStruct((1, n), F32),) * n_sums,
        grid=(m // tm, nk),
        in_specs=[pl.BlockSpec((tm, tk), lambda i, k: (i, k)), b_spec] + [tok] * n_rows + [vec] * n_vecs,
        out_specs=(tok,) * n_outs + (vec,) * n_sums,
        scratch_shapes=[] if nk == 1 else [pltpu.VMEM((tm, n), F32)],
        compiler_params=_params(("arbitrary", "arbitrary") if n_sums else ("parallel", "arbitrary")),
    )(a, b, *rows, *[v.reshape(1, n) for v in vecs])


def _mm_res_ln(a, b, x, g, beta, *, alpha, name):
    def fn(f, x_t, g_t, b_t):
        r = alpha * x_t + f
        xhat, _ = _ln_stats(r)
        y = xhat * g_t + b_t
        return r, y, y

    return _mm_fused(a, b, fn, rows=[x], vecs=[g, beta], out_dtypes=[F32, F32, BF16], n_sums=0, name=name)


def _mm_ln_bwd(a, b, dres, r, g, *, tb=False, alpha, name):
    def fn(f, dres_t, r_t, g_t):
        dy = f + alpha * dres_t
        xhat, rstd = _ln_stats(r_t)
        dxhat = dy * g_t
        m1 = jnp.mean(dxhat, axis=-1, keepdims=True)
        m2 = jnp.mean(dxhat * xhat, axis=-1, keepdims=True)
        dr = rstd * (dxhat - m1 - xhat * m2)
        return dr, dr, jnp.sum(dy * xhat, axis=0, keepdims=True), jnp.sum(dy, axis=0, keepdims=True)

    dr, drb, dg, db = _mm_fused(a, b, fn, rows=[dres, r], vecs=[g], out_dtypes=[F32, BF16], n_sums=2, tb=tb, name=name)
    return dr, drb, dg.reshape(-1), db.reshape(-1)


def _axpy(a, b, *, alpha, name, ts=512):
    s, d = a.shape
    ts = _pick(s, ts)

    def body(a_ref, b_ref, o_ref):
        o_ref[...] = a_ref[...] + alpha * b_ref[...]

    tok = pl.BlockSpec((ts, d), lambda i: (i, 0))
    return pl.pallas_call(body, name=name, out_shape=jax.ShapeDtypeStruct((s, d), F32), grid=(s // ts,),
                          in_specs=[tok, tok], out_specs=tok, compiler_params=_params(("parallel",)))(a, b)


def _loss_and_grad(y, target, *, name, ts=512):
    s, d = y.shape
    ts = _pick(s, ts)
    inv_d = 1.0 / d

    def body(y_ref, t_ref, dy_ref, loss_ref):
        e = y_ref[...] - t_ref[...]
        dy_ref[...] = e * inv_d

        @pl.when(pl.program_id(0) == 0)
        def _():
            loss_ref[...] = jnp.zeros_like(loss_ref)

        loss_ref[...] += jnp.sum(e * e, axis=0, keepdims=True) * (0.5 * inv_d)

    tok = pl.BlockSpec((ts, d), lambda i: (i, 0))
    vec = pl.BlockSpec((1, d), lambda i: (0, 0))
    dy, part = pl.pallas_call(
        body, name=name,
        out_shape=(jax.ShapeDtypeStruct((s, d), F32), jax.ShapeDtypeStruct((1, d), F32)),
        grid=(s // ts,), in_specs=[tok, tok], out_specs=(tok, vec),
        compiler_params=_params(("arbitrary",)),
    )(y, target)
    return dy, part


def _row_sum(v, *, name):
    def body(v_ref, o_ref):
        o_ref[...] = jnp.sum(v_ref[...], axis=1, keepdims=True)

    return pl.pallas_call(body, name=name, out_shape=jax.ShapeDtypeStruct((1, 1), F32))(v)


def _sigmoid(x):
    return 1.0 / (1.0 + jnp.exp(-x))


def _row_chunks(s, pref=512):
    c = _pick(s, pref, 8)
    return [(i * c, c) for i in range(s // c)]


def _glu_conv_fwd(proj, w, b, *, cc, name, exchange=None):
    s = proj.shape[0]
    kw = w.shape[0]
    pad = 32
    assert kw - 1 <= pad
    ncb = cc // LANES
    chunks = _row_chunks(s)

    def body(a_ref, g_ref, w_ref, b_ref, o_ref, u0_ref):
        u0_ref[pl.ds(0, pad), :] = jnp.zeros((pad, LANES), F32)
        for r0, rc in chunks:
            u0_ref[pl.ds(pad + r0, rc), :] = a_ref[pl.ds(r0, rc), :] * _sigmoid(g_ref[pl.ds(r0, rc), :])
        for r0, rc in chunks:
            acc = jnp.zeros((rc, LANES), F32) + b_ref[...]
            for k in range(kw):
                acc = acc + w_ref[pl.ds(k, 1), :] * u0_ref[pl.ds(pad + r0 - (kw - 1) + k, rc), :]
            o_ref[pl.ds(r0, rc), :] = acc

    body, ex_in, ex_out, ex_scratch = _hosted(exchange, (ncb,), body)
    if exchange is not None:
        body = functools.partial(body, n_own_in=4, n_own_out=1)
    hbm = pl.BlockSpec(memory_space=pl.ANY)
    outs = pl.pallas_call(
        body, name=name,
        out_shape=(jax.ShapeDtypeStruct((s, cc), F32), *ex_out),
        grid=(ncb,),
        in_specs=[pl.BlockSpec((s, LANES), lambda c: (0, c)), pl.BlockSpec((s, LANES), lambda c: (0, ncb + c)),
                  pl.BlockSpec((kw, LANES), lambda c: (0, c)), pl.BlockSpec((1, LANES), lambda c: (0, c))] + [hbm] * len(ex_in),
        out_specs=(pl.BlockSpec((s, LANES), lambda c: (0, c)), *([hbm] * len(ex_out))),
        scratch_shapes=[pltpu.VMEM((s + pad, LANES), F32)] + ex_scratch,
        compiler_params=_params(("arbitrary",) if exchange else ("parallel",)),
    )(proj, proj, w, b.reshape(1, cc), *ex_in)
    return outs[0], list(outs[1:])


def _glu_conv_bwd(du1, proj, w, *, cc, name, exchange=None):
    s = proj.shape[0]
    kw = w.shape[0]
    pad = 32
    ncb = cc // LANES
    chunks = _row_chunks(s)

    def body(d_ref, a_ref, g_ref, w_ref, da_ref, dg_ref, dw_ref, db_ref, u0_ref, dp_ref):
        u0_ref[pl.ds(0, pad), :] = jnp.zeros((pad, LANES), F32)
        dp_ref[pl.ds(s, pad), :] = jnp.zeros((pad, LANES), F32)
        for r0, rc in chunks:
            u0_ref[pl.ds(pad + r0, rc), :] = a_ref[pl.ds(r0, rc), :] * _sigmoid(g_ref[pl.ds(r0, rc), :])
            dp_ref[pl.ds(r0, rc), :] = d_ref[pl.ds(r0, rc), :]
        dws = [jnp.zeros((1, LANES), F32) for _ in range(kw)]
        dbs = jnp.zeros((1, LANES), F32)
        for r0, rc in chunks:
            d = dp_ref[pl.ds(r0, rc), :]
            dbs = dbs + jnp.sum(d, axis=0, keepdims=True)
            du0 = jnp.zeros((rc, LANES), F32)
            for k in range(kw):
                du0 = du0 + w_ref[pl.ds(k, 1), :] * dp_ref[pl.ds(r0 + (kw - 1) - k, rc), :]
                dws[k] = dws[k] + jnp.sum(d * u0_ref[pl.ds(pad + r0 - (kw - 1) + k, rc), :], axis=0, keepdims=True)
            sg = _sigmoid(g_ref[pl.ds(r0, rc), :])
            a = a_ref[pl.ds(r0, rc), :]
            da_ref[pl.ds(r0, rc), :] = (du0 * sg).astype(BF16)
            dg_ref[pl.ds(r0, rc), :] = (du0 * a * sg * (1.0 - sg)).astype(BF16)
        for k in range(kw):
            dw_ref[pl.ds(k, 1), :] = dws[k]
        db_ref[...] = dbs

    col = lambda off: pl.BlockSpec((s, LANES), lambda c: (0, off + c))
    body, ex_in, ex_out, ex_scratch = _hosted(exchange, (ncb,), body)
    if exchange is not None:
        body = functools.partial(body, n_own_in=4, n_own_out=4)
    hbm = pl.BlockSpec(memory_space=pl.ANY)
    outs = pl.pallas_call(
        body, name=name,
        out_shape=(jax.ShapeDtypeStruct((s, cc), BF16), jax.ShapeDtypeStruct((s, cc), BF16),
                   jax.ShapeDtypeStruct((kw, cc), F32), jax.ShapeDtypeStruct((1, cc), F32), *ex_out),
        grid=(ncb,),
        in_specs=[col(0), col(0), col(ncb), pl.BlockSpec((kw, LANES), lambda c: (0, c))] + [hbm] * len(ex_in),
        out_specs=(col(0), col(0), pl.BlockSpec((kw, LANES), lambda c: (0, c)), pl.BlockSpec((1, LANES), lambda c: (0, c)),
                   *([hbm] * len(ex_out))),
        scratch_shapes=[pltpu.VMEM((s + pad, LANES), F32), pltpu.VMEM((s + pad, LANES), F32)] + ex_scratch,
        compiler_params=_params(("arbitrary",) if exchange else ("parallel",)),
    )(du1, proj, proj, w, *ex_in)
    da, dg, dw, db = outs[:4]
    return da, dg, dw, db.reshape(cc), list(outs[4:])


def _cln_silu_fwd(u1, g, b, *, name, ts=512):
    s, cc = u1.shape
    ts = _pick(s, ts)

    def body(u_ref, g_ref, b_ref, o_ref):
        xhat, _ = _ln_stats(u_ref[...])
        y = xhat * g_ref[...] + b_ref[...]
        o_ref[...] = (y * _sigmoid(y)).astype(BF16)

    tok = pl.BlockSpec((ts, cc), lambda i: (i, 0))
    vec = pl.BlockSpec((1, cc), lambda i: (0, 0))
    return pl.pallas_call(body, name=name, out_shape=jax.ShapeDtypeStruct((s, cc), BF16), grid=(s // ts,),
                          in_specs=[tok, vec, vec], out_specs=tok,
                          compiler_params=_params(("parallel",)))(u1, g.reshape(1, cc), b.reshape(1, cc))


def _cln_silu_bwd(dua, u1, g, b, *, name, ts=512):
    s, cc = u1.shape
    ts = _pick(s, ts)

    def body(d_ref, u_ref, g_ref, b_ref, du_ref, dg_ref, db_ref):
        xhat, rstd = _ln_stats(u_ref[...])
        y = xhat * g_ref[...] + b_ref[...]
        sg = _sigmoid(y)
        dy = d_ref[...] * (sg * (1.0 + y * (1.0 - sg)))
        dxhat = dy * g_ref[...]
        m1 = jnp.mean(dxhat, axis=-1, keepdims=True)
        m2 = jnp.mean(dxhat * xhat, axis=-1, keepdims=True)
        du_ref[...] = rstd * (dxhat - m1 - xhat * m2)

        @pl.when(pl.program_id(0) == 0)
        def _():
            dg_ref[...] = jnp.zeros_like(dg_ref)
            db_ref[...] = jnp.zeros_like(db_ref)

        dg_ref[...] += jnp.sum(dy * xhat, axis=0, keepdims=True)
        db_ref[...] += jnp.sum(dy, axis=0, keepdims=True)

    tok = pl.BlockSpec((ts, cc), lambda i: (i, 0))
    vec = pl.BlockSpec((1, cc), lambda i: (0, 0))
    du1, dg, db = pl.pallas_call(
        body, name=name,
        out_shape=(jax.ShapeDtypeStruct((s, cc), F32), jax.ShapeDtypeStruct((1, cc), F32),
                   jax.ShapeDtypeStruct((1, cc), F32)),
        grid=(s // ts,), in_specs=[tok, tok, vec, vec], out_specs=(tok, vec, vec),
        compiler_params=_params(("arbitrary",)),
    )(dua, u1, g.reshape(1, cc), b.reshape(1, cc))
    return du1, dg.reshape(cc), db.reshape(cc)


def _softplus_parts(z):
    lk = jnp.minimum(-z, 0.0) - jnp.log1p(jnp.exp(-jnp.abs(z)))
    return lk, z + lk


def _stack_heads(x, hms):
    return jnp.concatenate([jnp.where(hm, x, 0.0) for hm in hms], axis=0).astype(BF16)


def _heads_side_by_side(x_st, tq):
    return jnp.concatenate([x_st[:tq], x_st[tq:]], axis=1)


def _sb_tile_masks(tq):
    row = lax.broadcasted_iota(jnp.int32, (2 * tq, tq), 0)
    col = lax.broadcasted_iota(jnp.int32, (2 * tq, tq), 1)
    vis = (col < jnp.where(row >= tq, row - tq, row)).astype(F32)
    krow, kcol = row[:tq], col[:tq]
    return vis, (krow > kcol).astype(BF16), (krow < kcol).astype(BF16)


_NT = (((1,), (1,)), ((), ()))
_TN = (((0,), (0,)), ((), ()))


def _head_masks():
    lane = lax.broadcasted_iota(jnp.int32, (1, LANES), 1)
    return [(lane >= SB_HEAD_DIM * h) & (lane < SB_HEAD_DIM * (h + 1)) for h in range(2)]


def _hosted(exchange, grid, body):
    if exchange is None:
        return body, [], [], []
    n_in, n_out, n_sems = len(exchange.inputs), len(exchange.out_shapes), len(exchange.scratch_shapes)

    def wrapped(*refs, n_own_in, n_own_out):
        own_in, ex_in = refs[:n_own_in], refs[n_own_in:n_own_in + n_in]
        rest = refs[n_own_in + n_in:]
        own_out, ex_out = rest[:n_own_out], rest[n_own_out:n_own_out + n_out]
        own_scratch, sems = rest[n_own_out + n_out:len(rest) - n_sems], rest[len(rest) - n_sems:]
        ids = [pl.program_id(d) for d in range(len(grid))]
        first = functools.reduce(lambda x, y: x & y, [i == 0 for i in ids])
        last = functools.reduce(lambda x, y: x & y, [i == g - 1 for i, g in zip(ids, grid)])

        @pl.when(first)
        def _():
            exchange.start(ex_in, ex_out, sems)

        body(*own_in, *own_out, *own_scratch)

        @pl.when(last)
        def _():
            exchange.finish(ex_in, ex_out, sems)

    return wrapped, list(exchange.inputs), list(exchange.out_shapes), list(exchange.scratch_shapes)


def _sb_fwd(proj, *, col0, n_pairs, name, tq=256, exchange=None):
    s = proj.shape[0]
    tq = _pick(s, tq)
    nq = s // tq
    cb0 = col0 // LANES
    scale = SB_HEAD_DIM ** -0.5

    def body(q_ref, k_ref, v_ref, vis_ref, after_ref, o_ref, t_ref, n_ref):
        i = pl.program_id(1)
        hms = _head_masks()
        q_st = _stack_heads(q_ref[...] * scale, hms)

        def tile(j, c, acc, masked):
            start = pl.multiple_of(j * tq, tq)
            kb = k_ref[pl.ds(start, tq), :].astype(BF16)
            v_st = _stack_heads(v_ref[pl.ds(start, tq), :], hms)
            z = lax.dot_general(q_st, kb, _NT, preferred_element_type=F32)
            lk, lb = _softplus_parts(z)
            if masked:
                lk = lk * vis_ref[...]
            later = jnp.dot(lk.astype(BF16), after_ref[...], preferred_element_type=F32)
            a = jnp.exp(lb + later + c)
            if masked:
                a = a * vis_ref[...]
            acc = acc + jnp.dot(_heads_side_by_side(a.astype(BF16), tq), v_st, preferred_element_type=F32)
            return c + jnp.sum(lk, axis=1, keepdims=True), acc

        def more(st):
            return jnp.logical_and(st[0] < i, jnp.max(st[1]) > -EXP_UNDERFLOW)

        def step(st):
            c, acc = tile(i - 1 - st[0], st[1], st[2], False)
            return st[0] + 1, c, acc

        c, acc = tile(i, jnp.zeros((2 * tq, 1), F32), jnp.zeros((tq, LANES), F32), True)
        n, c, acc = lax.while_loop(more, step, (jnp.int32(0), c, acc))
        o_ref[...] = acc.astype(BF16)
        t_ref[...] = jnp.where(hms[0], c[:tq], c[tq:])
        n_ref[...] = jnp.zeros((8, LANES), F32) + n.astype(F32)

    grid = (n_pairs, nq)
    body, ex_in, ex_out, ex_scratch = _hosted(exchange, grid, body)
    if exchange is not None:
        body = functools.partial(body, n_own_in=5, n_own_out=3)
    hbm = pl.BlockSpec(memory_space=pl.ANY)
    seq = lambda off: pl.BlockSpec((s, LANES), lambda p, i: (0, cb0 + off + p))
    whole = lambda rows: pl.BlockSpec((rows, tq), lambda p, i: (0, 0))
    vis, m_after, _ = _sb_tile_masks(tq)
    outs = pl.pallas_call(
        body, name=name,
        out_shape=(jax.ShapeDtypeStruct((s, n_pairs * LANES), BF16), jax.ShapeDtypeStruct((n_pairs, s, LANES), F32),
                   jax.ShapeDtypeStruct((n_pairs, nq * 8, LANES), F32), *ex_out),
        grid=grid,
        in_specs=[pl.BlockSpec((tq, LANES), lambda p, i: (i, cb0 + p)), seq(n_pairs), seq(2 * n_pairs),
                  whole(2 * tq), whole(tq)] + [hbm] * len(ex_in),
        out_specs=(pl.BlockSpec((tq, LANES), lambda p, i: (i, p)), pl.BlockSpec((None, tq, LANES), lambda p, i: (p, i, 0)),
                   pl.BlockSpec((None, 8, LANES), lambda p, i: (p, i, 0)), *([hbm] * len(ex_out))),
        scratch_shapes=ex_scratch,
        compiler_params=_params(("arbitrary", "arbitrary") if exchange else ("parallel", "arbitrary")),
    )(proj, proj, proj, vis, m_after, *ex_in)
    return outs[0], outs[1], outs[2], list(outs[3:])


def _sb_bwd(proj, t_sum, n_walked, dua, *, col0, n_pairs, do_col0, name, tq=256, exchange=None):
    s = proj.shape[0]
    tq = _pick(s, tq)
    cb0 = col0 // LANES
    dcb0 = do_col0 // LANES
    scale = SB_HEAD_DIM ** -0.5

    def body(q_ref, k_ref, v_ref, t_ref, n_ref, do_ref, vis_ref, after_ref, before_ref, dq_ref, dk_ref, dv_ref):
        i = pl.program_id(1)

        @pl.when(i == 0)
        def _():
            dk_ref[...] = jnp.zeros_like(dk_ref)
            dv_ref[...] = jnp.zeros_like(dv_ref)

        hms = _head_masks()
        q_st = _stack_heads(q_ref[...] * scale, hms)
        do_st = _stack_heads(do_ref[...], hms)
        t_st = jnp.concatenate([t_ref[:, SB_HEAD_DIM * h:SB_HEAD_DIM * h + 1] for h in range(2)], axis=0)
        first = i - jnp.max(n_ref[...]).astype(jnp.int32)

        def tile(j, p_sum, g_sum, dq, masked):
            start = pl.multiple_of(j * tq, tq)
            k = k_ref[pl.ds(start, tq), :]
            z = lax.dot_general(q_st, k.astype(BF16), _NT, preferred_element_type=F32)
            lk_raw, lb = _softplus_parts(z)
            lk = lk_raw * vis_ref[...] if masked else lk_raw
            p_next = p_sum + jnp.sum(lk, axis=1, keepdims=True)
            later = jnp.dot(lk.astype(BF16), after_ref[...], preferred_element_type=F32)
            a = jnp.exp(lb + (t_st - p_next) + later)
            if masked:
                a = a * vis_ref[...]
            da = lax.dot_general(do_st, v_ref[pl.ds(start, tq), :].astype(BF16), _NT, preferred_element_type=F32)
            g = a * da
            g_before = g_sum + jnp.dot(g.astype(BF16), before_ref[...], preferred_element_type=F32)
            dz = g * jnp.exp(lk_raw) - g_before * jnp.exp(lb)
            if masked:
                dz = dz * vis_ref[...]
            dzb = dz.astype(BF16)
            dv_ref[pl.ds(start, tq), :] += lax.dot_general(a.astype(BF16), do_st, _TN, preferred_element_type=F32)
            dk_ref[pl.ds(start, tq), :] += lax.dot_general(dzb, q_st, _TN, preferred_element_type=F32)
            dq = dq + jnp.dot(_heads_side_by_side(dzb, tq), _stack_heads(k, hms), preferred_element_type=F32)
            return p_next, g_sum + jnp.sum(g, axis=1, keepdims=True), dq

        zero = jnp.zeros((2 * tq, 1), F32)
        st = lax.fori_loop(first, i, lambda j, st: tile(j, *st, False), (zero, zero, jnp.zeros((tq, LANES), F32)))
        dq_ref[...] = tile(i, *st, True)[2] * scale

    grid = (n_pairs, s // tq)
    body, ex_in, ex_out, ex_scratch = _hosted(exchange, grid, body)
    if exchange is not None:
        body = functools.partial(body, n_own_in=9, n_own_out=3)
    hbm = pl.BlockSpec(memory_space=pl.ANY)
    seq = lambda off: pl.BlockSpec((s, LANES), lambda p, i: (0, cb0 + off + p))
    whole = lambda rows: pl.BlockSpec((rows, tq), lambda p, i: (0, 0))
    out = jax.ShapeDtypeStruct((s, n_pairs * LANES), F32)
    res = pl.BlockSpec((s, LANES), lambda p, i: (0, p))
    outs = pl.pallas_call(
        body, name=name,
        out_shape=(out, out, out, *ex_out),
        grid=grid,
        in_specs=[pl.BlockSpec((tq, LANES), lambda p, i: (i, cb0 + p)), seq(n_pairs), seq(2 * n_pairs),
                  pl.BlockSpec((None, tq, LANES), lambda p, i: (p, i, 0)),
                  pl.BlockSpec((None, 8, LANES), lambda p, i: (p, i, 0)),
                  pl.BlockSpec((tq, LANES), lambda p, i: (i, dcb0 + p)),
                  whole(2 * tq), whole(tq), whole(tq)] + [hbm] * len(ex_in),
        out_specs=(pl.BlockSpec((tq, LANES), lambda p, i: (i, p)), res, res, *([hbm] * len(ex_out))),
        scratch_shapes=ex_scratch,
        compiler_params=_params(("arbitrary", "arbitrary")),
    )(proj, proj, proj, t_sum, n_walked, dua, *_sb_tile_masks(tq), *ex_in)
    return outs[0], outs[1], outs[2], list(outs[3:])


def _mem_attn_fwd(qm, km, vm, *, name, tq=512):
    s, d = qm.shape
    heads = d // MEM_HEAD_DIM
    mlen = km.shape[0]
    tq = _pick(s, tq)
    scale = MEM_HEAD_DIM ** -0.5

    def body(q_ref, k_ref, v_ref, o_ref):
        for h in range(heads):
            sl = slice(h * MEM_HEAD_DIM, (h + 1) * MEM_HEAD_DIM)
            q = (q_ref[:, sl] * scale).astype(BF16)
            sc = lax.dot_general(q, k_ref[:, sl].astype(BF16), _NT, preferred_element_type=F32)
            e = jnp.exp(sc - jnp.max(sc, axis=1, keepdims=True))
            p = e / jnp.sum(e, axis=1, keepdims=True)
            o_ref[:, sl] = jnp.dot(p.astype(BF16), v_ref[:, sl].astype(BF16), preferred_element_type=F32).astype(BF16)

    tok = pl.BlockSpec((tq, d), lambda i: (i, 0))
    kv = pl.BlockSpec((mlen, d), lambda i: (0, 0))
    return pl.pallas_call(body, name=name, out_shape=jax.ShapeDtypeStruct((s, d), BF16), grid=(s // tq,),
                          in_specs=[tok, kv, kv], out_specs=tok, compiler_params=_params(("parallel",)))(qm, km, vm)


def _mem_attn_bwd(qm, km, vm, do, *, name, tq=512):
    s, d = qm.shape
    heads = d // MEM_HEAD_DIM
    mlen = km.shape[0]
    tq = _pick(s, tq)
    scale = MEM_HEAD_DIM ** -0.5

    def body(q_ref, k_ref, v_ref, do_ref, dq_ref, dk_ref, dv_ref):
        @pl.when(pl.program_id(0) == 0)
        def _():
            dk_ref[...] = jnp.zeros_like(dk_ref)
            dv_ref[...] = jnp.zeros_like(dv_ref)

        for h in range(heads):
            sl = slice(h * MEM_HEAD_DIM, (h + 1) * MEM_HEAD_DIM)
            q = (q_ref[:, sl] * scale).astype(BF16)
            k = k_ref[:, sl].astype(BF16)
            v = v_ref[:, sl].astype(BF16)
            sc = lax.dot_general(q, k, _NT, preferred_element_type=F32)
            e = jnp.exp(sc - jnp.max(sc, axis=1, keepdims=True))
            p = e / jnp.sum(e, axis=1, keepdims=True)
            dob = do_ref[:, sl].astype(BF16)
            dv_ref[:, sl] += lax.dot_general(p.astype(BF16), dob, _TN, preferred_element_type=F32)
            dp = lax.dot_general(dob, v, _NT, preferred_element_type=F32)
            ds = (p * (dp - jnp.sum(dp * p, axis=1, keepdims=True))).astype(BF16)
            dq_ref[:, sl] = (jnp.dot(ds, k, preferred_element_type=F32) * scale).astype(BF16)
            dk_ref[:, sl] += lax.dot_general(ds, q, _TN, preferred_element_type=F32)

    tok = pl.BlockSpec((tq, d), lambda i: (i, 0))
    kv = pl.BlockSpec((mlen, d), lambda i: (0, 0))
    return pl.pallas_call(
        body, name=name,
        out_shape=(jax.ShapeDtypeStruct((s, d), BF16), jax.ShapeDtypeStruct((mlen, d), F32),
                   jax.ShapeDtypeStruct((mlen, d), F32)),
        grid=(s // tq,), in_specs=[tok, kv, kv, tok], out_specs=(tok, kv, kv),
        compiler_params=_params(("arbitrary",)),
    )(qm, km, vm, do)


def _ffn_act_fwd(up, w, b, *, name, exchange=None):
    s, two_f = up.shape
    ff = two_f // 2
    nfb = ff // LANES
    kw = w.shape[0]
    pad = 8
    chunks = _row_chunks(s)

    def body(v_ref, g_ref, wv_ref, wg_ref, bv_ref, bg_ref, o_ref, vp_ref, gp_ref):
        vp_ref[pl.ds(0, pad), :] = jnp.zeros((pad, LANES), F32)
        gp_ref[pl.ds(0, pad), :] = jnp.zeros((pad, LANES), F32)
        for r0, rc in chunks:
            vp_ref[pl.ds(pad + r0, rc), :] = v_ref[pl.ds(r0, rc), :].astype(F32)
            gp_ref[pl.ds(pad + r0, rc), :] = g_ref[pl.ds(r0, rc), :].astype(F32)
        for r0, rc in chunks:
            vc = jnp.zeros((rc, LANES), F32) + bv_ref[...]
            gc = jnp.zeros((rc, LANES), F32) + bg_ref[...]
            for k in range(kw):
                off = pad + r0 - (kw - 1) + k
                vc = vc + wv_ref[pl.ds(k, 1), :] * vp_ref[pl.ds(off, rc), :]
                gc = gc + wg_ref[pl.ds(k, 1), :] * gp_ref[pl.ds(off, rc), :]
            o_ref[pl.ds(r0, rc), :] = (gc * _sigmoid(gc) * vc).astype(BF16)

    col = lambda off: pl.BlockSpec((s, LANES), lambda c: (0, off + c))
    tap = lambda off: pl.BlockSpec((kw, LANES), lambda c: (0, off + c))
    vec = lambda off: pl.BlockSpec((1, LANES), lambda c: (0, off + c))
    body, ex_in, ex_out, ex_scratch = _hosted(exchange, (nfb,), body)
    if exchange is not None:
        body = functools.partial(body, n_own_in=6, n_own_out=1)
    hbm = pl.BlockSpec(memory_space=pl.ANY)
    outs = pl.pallas_call(
        body, name=name, out_shape=(jax.ShapeDtypeStruct((s, ff), BF16), *ex_out), grid=(nfb,),
        in_specs=[col(0), col(nfb), tap(0), tap(nfb), vec(0), vec(nfb)] + [hbm] * len(ex_in),
        out_specs=(col(0), *([hbm] * len(ex_out))),
        scratch_shapes=[pltpu.VMEM((s + pad, LANES), F32), pltpu.VMEM((s + pad, LANES), F32)] + ex_scratch,
        compiler_params=_params(("arbitrary",) if exchange else ("parallel",)),
    )(up, up, w, w, b.reshape(1, two_f), b.reshape(1, two_f), *ex_in)
    return outs[0], list(outs[1:])


def _ffn_act_bwd(up, dact, w, b, *, name, exchange=None):
    s, two_f = up.shape
    ff = two_f // 2
    nfb = ff // LANES
    kw = w.shape[0]
    pad = 8
    chunks = _row_chunks(s)

    def body(v_ref, g_ref, d_ref, wv_ref, wg_ref, bv_ref, bg_ref, dv_ref, dg_ref, dwv_ref, dwg_ref, dbv_ref, dbg_ref,
             vp_ref, gp_ref, dvc_ref, dgc_ref):
        vp_ref[pl.ds(0, pad), :] = jnp.zeros((pad, LANES), F32)
        gp_ref[pl.ds(0, pad), :] = jnp.zeros((pad, LANES), F32)
        dvc_ref[pl.ds(s, pad), :] = jnp.zeros((pad, LANES), F32)
        dgc_ref[pl.ds(s, pad), :] = jnp.zeros((pad, LANES), F32)
        for r0, rc in chunks:
            vp_ref[pl.ds(pad + r0, rc), :] = v_ref[pl.ds(r0, rc), :].astype(F32)
            gp_ref[pl.ds(pad + r0, rc), :] = g_ref[pl.ds(r0, rc), :].astype(F32)
        dwv = [jnp.zeros((1, LANES), F32) for _ in range(kw)]
        dwg = [jnp.zeros((1, LANES), F32) for _ in range(kw)]
        dbv = jnp.zeros((1, LANES), F32)
        dbg = jnp.zeros((1, LANES), F32)
        for r0, rc in chunks:
            vc = jnp.zeros((rc, LANES), F32) + bv_ref[...]
            gc = jnp.zeros((rc, LANES), F32) + bg_ref[...]
            for k in range(kw):
                off = pad + r0 - (kw - 1) + k
                vc = vc + wv_ref[pl.ds(k, 1), :] * vp_ref[pl.ds(off, rc), :]
                gc = gc + wg_ref[pl.ds(k, 1), :] * gp_ref[pl.ds(off, rc), :]
            sg = _sigmoid(gc)
            d = d_ref[pl.ds(r0, rc), :].astype(F32)
            dvc = d * (gc * sg)
            dgc = d * vc * (sg * (1.0 + gc * (1.0 - sg)))
            dvc_ref[pl.ds(r0, rc), :] = dvc
            dgc_ref[pl.ds(r0, rc), :] = dgc
            dbv = dbv + jnp.sum(dvc, axis=0, keepdims=True)
            dbg = dbg + jnp.sum(dgc, axis=0, keepdims=True)
            for k in range(kw):
                off = pad + r0 - (kw - 1) + k
                dwv[k] = dwv[k] + jnp.sum(dvc * vp_ref[pl.ds(off, rc), :], axis=0, keepdims=True)
                dwg[k] = dwg[k] + jnp.sum(dgc * gp_ref[pl.ds(off, rc), :], axis=0, keepdims=True)
        for r0, rc in chunks:
            dv = jnp.zeros((rc, LANES), F32)
            dg = jnp.zeros((rc, LANES), F32)
            for k in range(kw):
                off = r0 + (kw - 1) - k
                dv = dv + wv_ref[pl.ds(k, 1), :] * dvc_ref[pl.ds(off, rc), :]
                dg = dg + wg_ref[pl.ds(k, 1), :] * dgc_ref[pl.ds(off, rc), :]
            dv_ref[pl.ds(r0, rc), :] = dv.astype(BF16)
            dg_ref[pl.ds(r0, rc), :] = dg.astype(BF16)
        for k in range(kw):
            dwv_ref[pl.ds(k, 1), :] = dwv[k]
            dwg_ref[pl.ds(k, 1), :] = dwg[k]
        dbv_ref[...] = dbv
        dbg_ref[...] = dbg

    col = lambda off: pl.BlockSpec((s, LANES), lambda c: (0, off + c))
    tap = lambda off: pl.BlockSpec((kw, LANES), lambda c: (0, off + c))
    vec = lambda off: pl.BlockSpec((1, LANES), lambda c: (0, off + c))
    big = lambda: pltpu.VMEM((s + pad, LANES), F32)
    body, ex_in, ex_out, ex_scratch = _hosted(exchange, (nfb,), body)
    if exchange is not None:
        body = functools.partial(body, n_own_in=7, n_own_out=6)
    hbm = pl.BlockSpec(memory_space=pl.ANY)
    outs = pl.pallas_call(
        body, name=name,
        out_shape=(jax.ShapeDtypeStruct((s, ff), BF16), jax.ShapeDtypeStruct((s, ff), BF16),
                   jax.ShapeDtypeStruct((kw, ff), F32), jax.ShapeDtypeStruct((kw, ff), F32),
                   jax.ShapeDtypeStruct((1, ff), F32), jax.ShapeDtypeStruct((1, ff), F32), *ex_out),
        grid=(nfb,),
        in_specs=[col(0), col(nfb), col(0), tap(0), tap(nfb), vec(0), vec(nfb)] + [hbm] * len(ex_in),
        out_specs=(col(0), col(0), tap(0), tap(0), vec(0), vec(0), *([hbm] * len(ex_out))),
        scratch_shapes=[big(), big(), big(), big()] + ex_scratch,
        compiler_params=_params(("arbitrary",) if exchange else ("parallel",)),
    )(up, up, dact, w, w, b.reshape(1, two_f), b.reshape(1, two_f), *ex_in)
    dv, dg, dwv, dwg, dbv, dbg = outs[:6]
    return (jnp.concatenate([dv, dg], axis=1), jnp.concatenate([dwv, dwg], axis=1),
            jnp.concatenate([dbv, dbg], axis=1).reshape(two_f), list(outs[6:]))


def _sum_parts(parts, *, name, tr=256):
    n_parts, rows, cols = parts.shape
    tr = _pick(rows, tr, 16)

    def body(p_ref, o_ref):
        g = p_ref[0].astype(F32)
        for k in range(1, n_parts):
            g = g + p_ref[k].astype(F32)
        o_ref[...] = g

    return pl.pallas_call(
        body, name=name, out_shape=jax.ShapeDtypeStruct((rows, cols), F32), grid=(rows // tr,),
        in_specs=[pl.BlockSpec((n_parts, tr, cols), lambda i: (0, i, 0))],
        out_specs=pl.BlockSpec((tr, cols), lambda i: (i, 0)), compiler_params=_params(("parallel",)),
    )(parts)


def _sum_adamw(parts, w, m, v, *, layer, so_far, name, tr=256):
    n_parts, rows, cols = parts.shape
    depth = w.shape[0]
    tr = _pick(rows, tr, 16)
    c1 = 1.0 / (1.0 - ADAM_B1 ** ADAM_STEP)
    c2 = 1.0 / (1.0 - ADAM_B2 ** ADAM_STEP)

    def body(p_ref, w_ref, m_ref, v_ref, *rest):
        g_ref, d_ref, nm_ref, nv_ref = rest[-4:]
        g = p_ref[0].astype(F32)
        for k in range(1, n_parts):
            g = g + p_ref[k].astype(F32)
        nm = ADAM_B1 * m_ref[...] + (1.0 - ADAM_B1) * g
        nv = ADAM_B2 * v_ref[...] + (1.0 - ADAM_B2) * (g * g)
        g_ref[...] = g
        nm_ref[...] = nm
        nv_ref[...] = nv
        d_ref[...] = -ADAM_LR * ((nm * c1) / (jnp.sqrt(nv * c2) + ADAM_EPS) + ADAM_WD * w_ref[...])

    blk = pl.BlockSpec((None, tr, cols), lambda i: (layer, i, 0))
    out = jax.ShapeDtypeStruct((depth, rows, cols), F32)
    kept = list(so_far) if so_far is not None else []
    return pl.pallas_call(
        body, name=name, out_shape=(out, out, out, out), grid=(rows // tr,),
        in_specs=[pl.BlockSpec((n_parts, tr, cols), lambda i: (0, i, 0)), blk, blk, blk]
        + [pl.BlockSpec(memory_space=pl.ANY)] * len(kept),
        out_specs=(blk, blk, blk, blk),
        input_output_aliases={4 + k: k for k in range(len(kept))},
        compiler_params=_params(("parallel",)),
    )(parts, w, m, v, *kept)


def _mesh_pos():
    return lax.axis_index("x"), lax.axis_index("y"), lax.axis_index("c")


def _flip(pos, k):
    x, y, c = pos
    return (1 - x if k & 4 else x, 1 - y if k & 2 else y, 1 - c if k & 1 else c)


def _dev_index(pos):
    return 4 * pos[0] + 2 * pos[1] + pos[2]


N_PEERS = N_DEV - 1


class _Exchange:
    def __init__(self, inputs, out_shapes, start, finish, scratch_shapes=None):
        n = len(inputs)
        self.inputs, self.out_shapes, self.start, self.finish = list(inputs), list(out_shapes), start, finish
        self.scratch_shapes = scratch_shapes or [pltpu.SemaphoreType.DMA((n * N_PEERS,)),
                                                 pltpu.SemaphoreType.DMA((n * N_PEERS,)), pltpu.SemaphoreType.DMA((n,))]


def _both(ex1, ex2):
    if ex1 is None or ex2 is None:
        return ex1 or ex2
    n_in, n_out, n_sems = len(ex1.inputs), len(ex1.out_shapes), len(ex1.scratch_shapes)

    def halves(ins, outs, sems):
        return (ins[:n_in], outs[:n_out], sems[:n_sems]), (ins[n_in:], outs[n_out:], sems[n_sems:])

    def start(ins, outs, sems):
        a, b = halves(ins, outs, sems)
        ex1.start(*a)
        ex2.start(*b)

    def finish(ins, outs, sems):
        a, b = halves(ins, outs, sems)
        ex1.finish(*a)
        ex2.finish(*b)

    return _Exchange(ex1.inputs + ex2.inputs, ex1.out_shapes + ex2.out_shapes, start, finish,
                     ex1.scratch_shapes + ex2.scratch_shapes)


def _gather_exchange(xs):
    n = len(xs)

    def plan(x_refs, out_refs, sems):
        send_sems, recv_sems, local_sems = sems
        me = _mesh_pos()
        sibling = _flip(me, 1)
        chips = [_flip(me, 4), _flip(me, 2), _flip(me, 6)]

        def copy(a, k, block, to, from_input=False):
            slot = out_refs[a].at[_dev_index(block)]
            return pltpu.make_async_remote_copy(
                src_ref=x_refs[a] if from_input else slot, dst_ref=slot,
                send_sem=send_sems.at[a * N_PEERS + k], recv_sem=recv_sems.at[a * N_PEERS + k],
                device_id=to, device_id_type=pl.DeviceIdType.MESH)

        mine = [pltpu.make_async_copy(x_refs[a], out_refs[a].at[_dev_index(me)], local_sems.at[a]) for a in range(n)]
        first = [copy(a, 0, me, sibling, True) for a in range(n)]
        first += [copy(a, 1 + j, me, chip, True) for j, chip in enumerate(chips) for a in range(n)]
        return me, sibling, chips, copy, mine, first

    def start(x_refs, out_refs, sems):
        _, _, _, _, mine, first = plan(x_refs, out_refs, sems)
        for cp in mine + first:
            cp.start()

    def finish(x_refs, out_refs, sems):
        me, sibling, chips, copy, mine, first = plan(x_refs, out_refs, sems)
        passed = []
        for j, chip in enumerate(chips):
            for a in range(n):
                copy(a, 1 + j, chip, me).wait_recv()
                passed.append(copy(a, 4 + j, chip, sibling))
                passed[-1].start()
        for a in range(n):
            copy(a, 0, sibling, me).wait_recv()
        for j, chip in enumerate(chips):
            for a in range(n):
                copy(a, 4 + j, _flip(chip, 1), me).wait_recv()
        for cp in first + passed:
            cp.wait_send()
        for cp in mine:
            cp.wait()

    return _Exchange(xs, [jax.ShapeDtypeStruct((N_DEV,) + x.shape, x.dtype) for x in xs], start, finish)


def _scatter_exchange(xs):
    n = len(xs)

    def plan(x_refs, out_refs, sems):
        send_sems, recv_sems, local_sems = sems
        me = _mesh_pos()
        my_slot = _dev_index(me)

        def copy(a, k):
            peer = _flip(me, k)
            return pltpu.make_async_remote_copy(
                src_ref=x_refs[a].at[_dev_index(peer)], dst_ref=out_refs[a].at[my_slot],
                send_sem=send_sems.at[a * N_PEERS + k - 1], recv_sem=recv_sems.at[a * N_PEERS + k - 1],
                device_id=peer, device_id_type=pl.DeviceIdType.MESH)

        mine = [pltpu.make_async_copy(x_refs[a].at[my_slot], out_refs[a].at[my_slot], local_sems.at[a]) for a in range(n)]
        return mine, [copy(a, k) for k in range(1, N_DEV) for a in range(n)]

    def start(x_refs, out_refs, sems):
        mine, copies = plan(x_refs, out_refs, sems)
        for cp in mine + copies:
            cp.start()

    def finish(x_refs, out_refs, sems):
        mine, copies = plan(x_refs, out_refs, sems)
        for cp in copies:
            cp.wait_recv()
        for cp in copies:
            cp.wait_send()
        for cp in mine:
            cp.wait()

    return _Exchange(xs, [jax.ShapeDtypeStruct(x.shape, x.dtype) for x in xs], start, finish)


def _run_exchanges(exchanges, *, name):
    counts = [len(ex.inputs) for ex in exchanges]
    n = sum(counts)

    def body(*refs):
        offsets = [sum(counts[:e]) for e in range(len(exchanges))]
        views = [(refs[o:o + c], refs[n + o:n + o + c], refs[2 * n + 3 * e:2 * n + 3 * e + 3])
                 for e, (o, c) in enumerate(zip(offsets, counts))]
        for ex, view in zip(exchanges, views):
            ex.start(*view)
        for ex, view in zip(exchanges, views):
            ex.finish(*view)

    hbm = pl.BlockSpec(memory_space=pl.ANY)
    outs = pl.pallas_call(
        body, name=name, out_shape=tuple(s for ex in exchanges for s in ex.out_shapes), in_specs=[hbm] * n,
        out_specs=tuple([hbm] * n), scratch_shapes=[s for ex in exchanges for s in ex.scratch_shapes],
    )(*[x for ex in exchanges for x in ex.inputs])
    return [list(outs[sum(counts[:e]):sum(counts[:e + 1])]) for e in range(len(exchanges))]


def _pack(arrays):
    flat = jnp.concatenate([a.reshape(-1) for a in arrays])
    n = flat.shape[0]
    tile = PACK_W * PACK_ROW_ALIGN
    total = -(-n // tile) * tile
    return jnp.pad(flat, (0, total - n)).reshape(total // PACK_W, PACK_W)


def _unpack(buf, shapes):
    lead = buf.shape[:-2]
    flat = buf.reshape(lead + (-1,))
    out, off = [], 0
    for shp in shapes:
        n = 1
        for dim in shp:
            n *= dim
        out.append(flat[..., off:off + n].reshape(lead + tuple(shp)))
        off += n
    return out


def _join_columns(blocks):
    return jnp.moveaxis(blocks, 0, 2).reshape(blocks.shape[1], blocks.shape[2], -1)


def _mm_hosting(a, b, exchange, **kw):
    if exchange is None:
        return _mm(a, b, **kw), []
    return _mm(a, b, exchange=exchange, **kw)


GATHERED_BY_ATTENTION = ['w_out', 'mem_wq', 'mem_wk', 'mem_wv', 'mem_wo', 'ffn_up']
GATHERED_BY_CONV = ['ffn_down']


def _as_matrix(blocks):
    return blocks.reshape(-1, blocks.shape[-1])


def _layer_fwd(x, xb, memb, w, alpha, shards, next_w_in):
    w = dict(w)
    cc = w['conv_w'].shape[1]
    n_pairs = (N_DEV * shards['w_out'].shape[0] - cc) // LANES

    proj = _mm(xb, w['w_in'], tb=True, name="mm_proj")
    u1, got = _glu_conv_fwd(proj, w['conv_w'], w['conv_b'], cc=cc, name="glu_conv_fwd",
                            exchange=_gather_exchange([shards[n] for n in GATHERED_BY_CONV]))
    w.update({n: _as_matrix(f) for n, f in zip(GATHERED_BY_CONV, got)})
    u = _cln_silu_fwd(u1, w['conv_ln_g'], w['conv_ln_b'], name="cln_silu_fwd")
    att, t_sum, n_walked, got = _sb_fwd(proj, col0=2 * cc, n_pairs=n_pairs, name="sb_fwd",
                                        exchange=_gather_exchange([shards[n] for n in GATHERED_BY_ATTENTION]))
    w.update({n: _as_matrix(f) for n, f in zip(GATHERED_BY_ATTENTION, got)})
    ua = jnp.concatenate([u, att], axis=1)
    r1, x1, x1b = _mm_res_ln(ua, w['w_out'], x, w['ln1_g'], w['ln1_b'], alpha=alpha, name="mm_mix_ln")
    qm = _mm(x1b, w['mem_wq'], name="mm_memq")
    km = _mm(memb, w['mem_wk'], name="mm_memkv")
    vm = _mm(memb, w['mem_wv'], name="mm_memkv")
    o = _mem_attn_fwd(qm, km, vm, name="mem_attn_fwd")
    r2, x2, x2b = _mm_res_ln(o, w['mem_wo'], x1, w['ln2_g'], w['ln2_b'], alpha=alpha, name="mm_mix_ln")
    up = _mm(x2b, w['ffn_up'], tb=True, out_dtype=BF16, name="mm_up")
    act, got = _ffn_act_fwd(up, w['ffn_conv_w'], w['ffn_conv_b'], name="ffn_act_fwd",
                            exchange=_gather_exchange([next_w_in]) if next_w_in is not None else None)
    r3, x3, x3b = _mm_res_ln(act, w['ffn_down'], x2, w['ln3_g'], w['ln3_b'], alpha=alpha, name="mm_down_ln")
    saved = dict(xb=xb, proj=proj, u1=u1, t_sum=t_sum, n_walked=n_walked, ua=ua, r1=r1, x1b=x1b, qm=qm, km=km, vm=vm,
                 o=o, r2=r2, x2b=x2b, up=up, act=act, r3=r3)
    return x3, x3b, saved, w, _as_matrix(got[0]) if got else None


def _layer_bwd(top, sv, memb, w, alpha, carried, also_during_attention=None):
    g, received = {}, {}
    cc = w['conv_w'].shape[1]
    n_pairs = (w['w_out'].shape[0] - cc) // LANES

    def sending(sends):
        ex = _scatter_exchange([gm.reshape(N_DEV, -1, gm.shape[-1]) for _, gm in sends]) if sends else None
        return ex, lambda got: received.update({key: blocks for (key, _), blocks in zip(sends, got)})

    if top[1] is None:
        dr3, dr3b, g['ln3_g'], g['ln3_b'] = _ln_bwd(top[0], None, sv['r3'], w['ln3_g'], alpha=alpha, name="ln_bwd")
    else:
        dr3, dr3b, g['ln3_g'], g['ln3_b'] = _mm_ln_bwd(*top, sv['r3'], w['ln3_g'], alpha=alpha, name="mm_dx_in_ln")
    g_down = _mm(sv['act'], dr3b, ta=True, out_dtype=BF16, name="mm_dw_down")
    dact = _mm(dr3b, w['ffn_down'], tb=True, out_dtype=BF16, name="mm_dact")
    ex, file = sending(carried + [('ffn_down', g_down)])
    dup, g['ffn_conv_w'], g['ffn_conv_b'], got = _ffn_act_bwd(sv['up'], dact, w['ffn_conv_w'], w['ffn_conv_b'],
                                                                name="ffn_act_bwd", exchange=ex)
    file(got)
    g_up = _mm(dup, sv['x2b'], ta=True, out_dtype=BF16, name="mm_dw_up")
    dr2, dr2b, g['ln2_g'], g['ln2_b'] = _mm_ln_bwd(dup, w['ffn_up'], dr3, sv['r2'], w['ln2_g'], alpha=alpha, name="mm_dx_up_ln")
    g_wo = _mm(sv['o'], dr2b, ta=True, out_dtype=BF16, name="mm_dw_sq")
    do = _mm(dr2b, w['mem_wo'], tb=True, out_dtype=BF16, name="mm_dx_sq")
    dqm, dkm, dvm = _mem_attn_bwd(sv['qm'], sv['km'], sv['vm'], do, name="mem_attn_bwd")
    g_wq = _mm(sv['x1b'], dqm, ta=True, out_dtype=BF16, name="mm_dw_sq")
    g_wk = _mm(memb, dkm, ta=True, out_dtype=BF16, name="mm_dw_memkv")
    g_wv = _mm(memb, dvm, ta=True, out_dtype=BF16, name="mm_dw_memkv")
    dr1, dr1b, g['ln1_g'], g['ln1_b'] = _mm_ln_bwd(dqm, w['mem_wq'], dr2, sv['r1'], w['ln1_g'], tb=True, alpha=alpha,
                                                    name="mm_dx_sq_ln")
    g_out = _mm(sv['ua'], dr1b, ta=True, out_dtype=BF16, name="mm_dw_sq")
    dua = _mm(dr1b, w['w_out'], tb=True, name="mm_dx_sq")
    du1, g['conv_ln_g'], g['conv_ln_b'] = _cln_silu_bwd(dua, sv['u1'], w['conv_ln_g'], w['conv_ln_b'], name="cln_silu_bwd")
    ex, file = sending([('mem_wo', g_wo), ('mem_wq', g_wq), ('w_out', g_out)])
    dga, dgg, g['conv_w'], g['conv_b'], got = _glu_conv_bwd(du1, sv['proj'], w['conv_w'], cc=cc, name="glu_conv_bwd",
                                                             exchange=ex)
    file(got)
    ex, file = sending([('ffn_up', g_up), ('mem_wk', g_wk), ('mem_wv', g_wv)])
    extra = also_during_attention(g) if also_during_attention else None
    dq, dk, dv, got = _sb_bwd(sv['proj'], sv['t_sum'], sv['n_walked'], dua, col0=2 * cc, n_pairs=n_pairs, do_col0=cc,
                              name="sb_bwd", exchange=_both(ex, extra))
    file(got[:len(ex.inputs)])
    also_got = got[len(ex.inputs):]
    dproj = jnp.concatenate([dga, dgg, dq.astype(BF16), dk.astype(BF16), dv.astype(BF16)], axis=1)
    g_in = _mm(dproj, sv['xb'], ta=True, out_dtype=BF16, name="mm_dw_in")
    return (dproj, w['w_in'], dr1), g, received, [('w_in', g_in)], also_got


def kernel(x, mem, w_in, conv_w, conv_b, conv_ln_g, conv_ln_b, w_out, ln1_g, ln1_b, mem_wq, mem_wk, mem_wv, mem_wo, ln2_g, ln2_b, ffn_up, ffn_conv_w, ffn_conv_b, ffn_down, ln3_g, ln3_b, loss_target, m_w_in, m_conv_w, m_conv_b, m_conv_ln_g, m_conv_ln_b, m_w_out, m_ln1_g, m_ln1_b, m_mem_wq, m_mem_wk, m_mem_wv, m_mem_wo, m_ln2_g, m_ln2_b, m_ffn_up, m_ffn_conv_w, m_ffn_conv_b, m_ffn_down, m_ln3_g, m_ln3_b, v_w_in, v_conv_w, v_conv_b, v_conv_ln_g, v_conv_ln_b, v_w_out, v_ln1_g, v_ln1_b, v_mem_wq, v_mem_wk, v_mem_wv, v_mem_wo, v_ln2_g, v_ln2_b, v_ffn_up, v_ffn_conv_w, v_ffn_conv_b, v_ffn_down, v_ln3_g, v_ln3_b):
    wts = dict(zip(WEIGHTS, (w_in, conv_w, conv_b, conv_ln_g, conv_ln_b, w_out, ln1_g, ln1_b, mem_wq, mem_wk, mem_wv,
                             mem_wo, ln2_g, ln2_b, ffn_up, ffn_conv_w, ffn_conv_b, ffn_down, ln3_g, ln3_b)))
    mom = dict(zip(WEIGHTS, (m_w_in, m_conv_w, m_conv_b, m_conv_ln_g, m_conv_ln_b, m_w_out, m_ln1_g, m_ln1_b, m_mem_wq,
                             m_mem_wk, m_mem_wv, m_mem_wo, m_ln2_g, m_ln2_b, m_ffn_up, m_ffn_conv_w, m_ffn_conv_b,
                             m_ffn_down, m_ln3_g, m_ln3_b)))
    var = dict(zip(WEIGHTS, (v_w_in, v_conv_w, v_conv_b, v_conv_ln_g, v_conv_ln_b, v_w_out, v_ln1_g, v_ln1_b, v_mem_wq,
                             v_mem_wk, v_mem_wv, v_mem_wo, v_ln2_g, v_ln2_b, v_ffn_up, v_ffn_conv_w, v_ffn_conv_b,
                             v_ffn_down, v_ln3_g, v_ln3_b)))
    depth = w_in.shape[0]
    alpha = (2.0 * depth) ** 0.25
    my_index = _dev_index(_mesh_pos())

    def row_blocks(src, n, col_sharded):
        return jnp.swapaxes(src[n], 1, 2) if col_sharded else src[n]

    bf16_blocks = {n: row_blocks(wts, n, cs).astype(BF16) for n, cs in MATRICES}
    shards = [{n: bf16_blocks[n][l] for n, _ in MATRICES} for l in range(depth)]
    tap_shapes = [wts[n].shape for n in TAPS]
    (gathered_taps, got), = _run_exchanges([_gather_exchange([_pack([wts[n] for n in TAPS]), shards[0]['w_in']])],
                                           name="gather_first")
    full_taps = {n: _join_columns(t) for n, t in zip(TAPS, _unpack(gathered_taps, tap_shapes))}
    full_w_in = _as_matrix(got)

    xs = x[0]
    memb = mem[0].astype(BF16)
    h, hb = xs, xs.astype(BF16)
    saved, weights = [], []
    for l in range(depth):
        w = {'w_in': full_w_in}
        w.update({n: full_taps[n][l] for n in TAPS})
        w.update({n: wts[n][l] for n in REPLICATED})
        h, hb, sv, w, full_w_in = _layer_fwd(h, hb, memb, w, alpha, shards[l],
                                             shards[l + 1]['w_in'] if l + 1 < depth else None)
        saved.append(sv)
        weights.append(w)

    dy, loss_row = _loss_and_grad(h, loss_target[0], name="loss")
    loss = lax.psum(_row_sum(loss_row, name="loss_sum")[0, 0], ("x", "y", "c"))

    results = {}
    col_sharded = dict(MATRICES)
    state = {n: [row_blocks(src, n, cs) for src in (wts, mom, var)] for n, cs in MATRICES}

    def update(n, l, parts):
        results[n] = _sum_adamw(parts, *state[n], layer=l, so_far=results.get(n), name="adamw_matrix")

    small = REPLICATED + TAPS

    def gather_small(g0):
        per_layer = [g0] + grads[1:]
        return _gather_exchange([_pack([jnp.stack([per_layer[l][n] for l in range(depth)]) for n in small])])

    top = (dy, None, None)
    grads = [None] * depth
    carried = []
    for l in reversed(range(depth)):
        top, grads[l], received, left, also_got = _layer_bwd(top, saved[l], memb, weights[l], alpha, carried,
                                                             gather_small if l == 0 else None)
        for n, parts in received.items():
            update(n, l + 1 if n in dict(carried) else l, parts)
        carried = left
    parts, = also_got
    da, last = _mm(top[0], top[1], name="mm_dx_in",
                   exchange=_scatter_exchange([gm.reshape(N_DEV, -1, gm.shape[-1]) for _, gm in carried]))
    for (n, _), blocks in zip(carried, last):
        update(n, 0, blocks)
    grad_x = _axpy(da, top[2], alpha=alpha, name="grad_x")[None]
    for n, cs in MATRICES:
        if cs:
            results[n] = [jnp.swapaxes(r, 1, 2) for r in results[n]]
    total = _sum_parts(parts, name="sum_small_grads")
    summed = dict(zip(small, _unpack(total, [wts[n].shape for n in REPLICATED] + [full_taps[n].shape for n in TAPS])))
    for n in TAPS:
        cols = wts[n].shape[-1]
        summed[n] = lax.dynamic_slice_in_dim(summed[n], my_index * cols, cols, axis=2)
    res = _sum_adamw(_pack([summed[n] for n in small])[None], *[_pack([src[n] for n in small])[None] for src in (wts, mom, var)],
                     layer=0, so_far=None, name="adamw_small")
    unpacked = [_unpack(r[0], [wts[n].shape for n in small]) for r in res]
    for i, n in enumerate(small):
        results[n] = [u[i] for u in unpacked]

    outs = [loss, grad_x]
    for kind in range(4):
        outs += [results[n][kind] for n in WEIGHTS]
    return tuple(outs)
```

```python
import functools

import jax
import jax.numpy as jnp
from jax import lax
from jax.experimental import pallas as pl
from jax.experimental.pallas import tpu as pltpu

F32 = jnp.float32
BF16 = jnp.bfloat16

N_DEV = 8
LANES = 128
PACK_W = 1024
PACK_ROW_ALIGN = 16
SB_HEAD_DIM = 64
MEM_HEAD_DIM = 256
LN_EPS = 1e-5
EXP_UNDERFLOW = 104.0
VMEM_LIMIT = 56 * 1024 * 1024

ADAM_LR = 0.001
ADAM_B1 = 0.9
ADAM_B2 = 0.999
ADAM_EPS = 1e-08
ADAM_WD = 0.01
ADAM_STEP = 10

IN_NAMES = ['x', 'mem', 'w_in', 'conv_w', 'conv_b', 'conv_ln_g', 'conv_ln_b', 'w_out', 'ln1_g', 'ln1_b',
            'mem_wq', 'mem_wk', 'mem_wv', 'mem_wo', 'ln2_g', 'ln2_b', 'ffn_up', 'ffn_conv_w', 'ffn_conv_b',
            'ffn_down', 'ln3_g', 'ln3_b']
WEIGHTS = IN_NAMES[2:]
MATRICES = [('w_in', True), ('w_out', False), ('mem_wq', False), ('mem_wk', False), ('mem_wv', False),
            ('mem_wo', False), ('ffn_up', True), ('ffn_down', False)]
TAPS = ['conv_w', 'ffn_conv_w']
REPLICATED = ['conv_b', 'conv_ln_g', 'conv_ln_b', 'ln1_g', 'ln1_b', 'ln2_g', 'ln2_b', 'ffn_conv_b', 'ln3_g', 'ln3_b']


def _pick(dim, pref, align=LANES):
    if dim <= pref:
        return dim
    fits = [t for t in range(align, pref + 1, align) if dim % t == 0]
    return fits[-1] if fits else dim


def _params(sem):
    return pltpu.CompilerParams(dimension_semantics=sem, vmem_limit_bytes=VMEM_LIMIT)


def _mm(a, b, *, ta=False, tb=False, out_dtype=F32, name, exchange=None):
    if ta:
        kdim, m = a.shape
        tm, tn, tk = _pick(m, 1408), _pick(b.shape[0 if tb else 1], 1024), _pick(kdim, 1024)
    else:
        m, kdim = a.shape
        tm, tn, tk = _pick(m, 512), _pick(b.shape[0 if tb else 1], 1536), _pick(kdim, 2816)
    if tb:
        n, kb = b.shape
    else:
        kb, n = b.shape
    assert kdim == kb, (a.shape, b.shape, ta, tb)
    grid = (m // tm, n // tn, kdim // tk)
    nk = grid[2]
    dims = (((0 if ta else 1,), (1 if tb else 0,)), ((), ()))
    n_ex_in = len(exchange.inputs) if exchange else 0
    n_ex_out = len(exchange.out_shapes) if exchange else 0

    def body(*refs):
        a_ref, b_ref = refs[:2]
        ex_in = refs[2:2 + n_ex_in]
        o_ref = refs[2 + n_ex_in]
        ex_out = refs[3 + n_ex_in:3 + n_ex_in + n_ex_out]
        scratch = refs[3 + n_ex_in + n_ex_out:]
        if nk > 1:
            acc_ref, scratch = scratch[0], scratch[1:]
        ids = [pl.program_id(d) for d in range(3)]
        if exchange:
            @pl.when((ids[0] == 0) & (ids[1] == 0) & (ids[2] == 0))
            def _():
                exchange.start(ex_in, ex_out, scratch)

        prod = lax.dot_general(a_ref[...].astype(BF16), b_ref[...].astype(BF16), dims,
                               preferred_element_type=F32)
        if nk == 1:
            o_ref[...] = prod.astype(out_dtype)
        else:
            k = ids[2]

            @pl.when(k == 0)
            def _():
                acc_ref[...] = prod

            @pl.when(k > 0)
            def _():
                acc_ref[...] += prod

            @pl.when(k == nk - 1)
            def _():
                o_ref[...] = acc_ref[...].astype(out_dtype)

        if exchange:
            @pl.when((ids[0] == grid[0] - 1) & (ids[1] == grid[1] - 1) & (ids[2] == grid[2] - 1))
            def _():
                exchange.finish(ex_in, ex_out, scratch)

    a_spec = pl.BlockSpec((tk, tm), lambda i, j, k: (k, i)) if ta else pl.BlockSpec((tm, tk), lambda i, j, k: (i, k))
    b_spec = pl.BlockSpec((tn, tk), lambda i, j, k: (j, k)) if tb else pl.BlockSpec((tk, tn), lambda i, j, k: (k, j))
    hbm = pl.BlockSpec(memory_space=pl.ANY)
    outs = pl.pallas_call(
        body, name=name,
        out_shape=(jax.ShapeDtypeStruct((m, n), out_dtype),) + tuple(exchange.out_shapes if exchange else ()),
        grid=grid,
        in_specs=[a_spec, b_spec] + [hbm] * n_ex_in,
        out_specs=(pl.BlockSpec((tm, tn), lambda i, j, k: (i, j)),) + (hbm,) * n_ex_out,
        scratch_shapes=([] if nk == 1 else [pltpu.VMEM((tm, tn), F32)]) + list(exchange.scratch_shapes if exchange else []),
        compiler_params=_params(("arbitrary",) * 3 if exchange else ("parallel", "parallel", "arbitrary")),
    )(a, b, *(exchange.inputs if exchange else ()))
    return (outs[0], list(outs[1:])) if exchange else outs[0]


def _ln_stats(r):
    mu = jnp.mean(r, axis=-1, keepdims=True)
    xc = r - mu
    var = jnp.mean(xc * xc, axis=-1, keepdims=True)
    rstd = lax.rsqrt(var + LN_EPS)
    return xc * rstd, rstd


def _ln_bwd(da, dres, r, g, *, alpha, name, ts=512):
    s, d = r.shape
    ts = _pick(s, ts)
    has_res = dres is not None

    def body(*refs):
        if has_res:
            da_ref, dres_ref, r_ref, g_ref, dr_ref, drb_ref, dg_ref, db_ref = refs
            dy = da_ref[...] + alpha * dres_ref[...]
        else:
            da_ref, r_ref, g_ref, dr_ref, drb_ref, dg_ref, db_ref = refs
            dy = da_ref[...]
        xhat, rstd = _ln_stats(r_ref[...])
        dxhat = dy * g_ref[...]
        m1 = jnp.mean(dxhat, axis=-1, keepdims=True)
        m2 = jnp.mean(dxhat * xhat, axis=-1, keepdims=True)
        dr = rstd * (dxhat - m1 - xhat * m2)
        dr_ref[...] = dr
        drb_ref[...] = dr.astype(BF16)

        @pl.when(pl.program_id(0) == 0)
        def _():
            dg_ref[...] = jnp.zeros_like(dg_ref)
            db_ref[...] = jnp.zeros_like(db_ref)

        dg_ref[...] += jnp.sum(dy * xhat, axis=0, keepdims=True)
        db_ref[...] += jnp.sum(dy, axis=0, keepdims=True)

    tok = pl.BlockSpec((ts, d), lambda i: (i, 0))
    vec = pl.BlockSpec((1, d), lambda i: (0, 0))
    ins = [da, dres, r, g.reshape(1, d)] if has_res else [da, r, g.reshape(1, d)]
    dr, drb, dg, db = pl.pallas_call(
        body, name=name,
        out_shape=(jax.ShapeDtypeStruct((s, d), F32), jax.ShapeDtypeStruct((s, d), BF16),
                   jax.ShapeDtypeStruct((1, d), F32), jax.ShapeDtypeStruct((1, d), F32)),
        grid=(s // ts,), in_specs=[tok] * (len(ins) - 1) + [vec], out_specs=(tok, tok, vec, vec),
        compiler_params=_params(("arbitrary",)),
    )(*ins)
    return dr, drb, dg.reshape(d), db.reshape(d)


def _mm_fused(a, b, fn, *, rows, vecs, out_dtypes, n_sums, tb=False, name, tm=512):
    parts = list(a) if isinstance(a, (tuple, list)) else [a]
    m = parts[0].shape[0]
    kdim = sum(p.shape[1] for p in parts)
    n = b.shape[0] if tb else b.shape[1]
    assert kdim == (b.shape[1] if tb else b.shape[0])
    tm = _pick(m, tm)
    tk = parts[0].shape[1] if len(parts) > 1 else _pick(kdim, 2816)
    nk = kdim // tk
    assert len(parts) in (1, nk) and all(p.shape == (m, tk) for p in parts[1:])
    dims = (((1,), (1 if tb else 0,)), ((), ()))
    n_parts, n_rows, n_vecs, n_outs = len(parts), len(rows), len(vecs), len(out_dtypes)

    def body(*refs):
        a_refs, b_ref = refs[:n_parts], refs[n_parts]
        refs = refs[n_parts - 1:]
        row_refs = refs[2:2 + n_rows]
        vec_refs = refs[2 + n_rows:2 + n_rows + n_vecs]
        out_refs = refs[2 + n_rows + n_vecs:2 + n_rows + n_vecs + n_outs]
        sum_refs = refs[2 + n_rows + n_vecs + n_outs:2 + n_rows + n_vecs + n_outs + n_sums]
        i, k = pl.program_id(0), pl.program_id(1)

        def product(j):
            a_ref = a_refs[j if n_parts > 1 else 0]
            return lax.dot_general(a_ref[...].astype(BF16), b_ref[...].astype(BF16), dims, preferred_element_type=F32)

        def finish(product):
            res = fn(product, *[r[...] for r in row_refs], *[v[...] for v in vec_refs])
            for o_ref, o in zip(out_refs, res[:n_outs]):
                o_ref[...] = o.astype(o_ref.dtype)
            if n_sums:
                @pl.when(i == 0)
                def _():
                    for s_ref in sum_refs:
                        s_ref[...] = jnp.zeros_like(s_ref)

                for s_ref, part in zip(sum_refs, res[n_outs:]):
                    s_ref[...] += part

        if nk == 1:
            finish(product(0))
        else:
            acc_ref = refs[-1]
            for j in range(nk):
                @pl.when(k == j)
                def _(j=j):
                    if j == 0:
                        acc_ref[...] = product(j)
                    elif j < nk - 1:
                        acc_ref[...] += product(j)
                    else:
                        finish(acc_ref[...] + product(j))

    tok = pl.BlockSpec((tm, n), lambda i, k: (i, 0))
    vec = pl.BlockSpec((1, n), lambda i, k: (0, 0))
    b_spec = pl.BlockSpec((n, tk), lambda i, k: (0, k)) if tb else pl.BlockSpec((tk, n), lambda i, k: (k, 0))
    a_spec = pl.BlockSpec((tm, tk), (lambda i, k: (i, 0)) if n_parts > 1 else (lambda i, k: (i, k)))
    return pl.pallas_call(
        body, name=name,
        out_shape=tuple(jax.ShapeDtypeStruct((m, n), dt) for dt in out_dtypes) + (jax.ShapeDtypeStruct((1, n), F32),) * n_sums,
        grid=(m // tm, nk),
        in_specs=[a_spec] * n_parts + [b_spec] + [tok] * n_rows + [vec] * n_vecs,
        out_specs=(tok,) * n_outs + (vec,) * n_sums,
        scratch_shapes=[] if nk == 1 else [pltpu.VMEM((tm, n), F32)],
        compiler_params=_params(("arbitrary", "arbitrary") if n_sums else ("parallel", "arbitrary")),
    )(*parts, b, *rows, *[v.reshape(1, n) for v in vecs])


def _mm_res_ln(a, b, x, g, beta, *, alpha, name):
    def fn(f, x_t, g_t, b_t):
        r = alpha * x_t + f
        xhat, _ = _ln_stats(r)
        y = xhat * g_t + b_t
        return r, y, y

    return _mm_fused(a, b, fn, rows=[x], vecs=[g, beta], out_dtypes=[F32, F32, BF16], n_sums=0, name=name)


def _mm_ln_bwd(a, b, dres, r, g, *, tb=False, alpha, name):
    def fn(f, dres_t, r_t, g_t):
        dy = f + alpha * dres_t
        xhat, rstd = _ln_stats(r_t)
        dxhat = dy * g_t
        m1 = jnp.mean(dxhat, axis=-1, keepdims=True)
        m2 = jnp.mean(dxhat * xhat, axis=-1, keepdims=True)
        dr = rstd * (dxhat - m1 - xhat * m2)
        return dr, dr, jnp.sum(dy * xhat, axis=0, keepdims=True), jnp.sum(dy, axis=0, keepdims=True)

    dr, drb, dg, db = _mm_fused(a, b, fn, rows=[dres, r], vecs=[g], out_dtypes=[F32, BF16], n_sums=2, tb=tb, name=name)
    return dr, drb, dg.reshape(-1), db.reshape(-1)


def _axpy(a, b, *, alpha, name, ts=512):
    s, d = a.shape
    ts = _pick(s, ts)

    def body(a_ref, b_ref, o_ref):
        o_ref[...] = a_ref[...] + alpha * b_ref[...]

    tok = pl.BlockSpec((ts, d), lambda i: (i, 0))
    return pl.pallas_call(body, name=name, out_shape=jax.ShapeDtypeStruct((s, d), F32), grid=(s // ts,),
                          in_specs=[tok, tok], out_specs=tok, compiler_params=_params(("parallel",)))(a, b)


def _loss_and_grad(y, target, *, name, ts=512):
    s, d = y.shape
    ts = _pick(s, ts)
    inv_d = 1.0 / d

    def body(y_ref, t_ref, dy_ref, loss_ref):
        e = y_ref[...] - t_ref[...]
        dy_ref[...] = e * inv_d

        @pl.when(pl.program_id(0) == 0)
        def _():
            loss_ref[...] = jnp.zeros_like(loss_ref)

        loss_ref[...] += jnp.sum(e * e, axis=0, keepdims=True) * (0.5 * inv_d)

    tok = pl.BlockSpec((ts, d), lambda i: (i, 0))
    vec = pl.BlockSpec((1, d), lambda i: (0, 0))
    dy, part = pl.pallas_call(
        body, name=name,
        out_shape=(jax.ShapeDtypeStruct((s, d), F32), jax.ShapeDtypeStruct((1, d), F32)),
        grid=(s // ts,), in_specs=[tok, tok], out_specs=(tok, vec),
        compiler_params=_params(("arbitrary",)),
    )(y, target)
    return dy, part


def _row_sum(v, *, name):
    def body(v_ref, o_ref):
        o_ref[...] = jnp.sum(v_ref[...], axis=1, keepdims=True)

    return pl.pallas_call(body, name=name, out_shape=jax.ShapeDtypeStruct((1, 1), F32))(v)


def _sigmoid(x):
    return 1.0 / (1.0 + jnp.exp(-x))


def _row_chunks(s, pref=512):
    c = _pick(s, pref, 8)
    return [(i * c, c) for i in range(s // c)]


def _glu_conv_fwd(proj, w, b, *, cc, name, exchange=None):
    s = proj.shape[0]
    kw = w.shape[0]
    pad = 32
    assert kw - 1 <= pad
    ncb = cc // LANES
    chunks = _row_chunks(s)

    def body(a_ref, g_ref, w_ref, b_ref, o_ref, u0_ref):
        u0_ref[pl.ds(0, pad), :] = jnp.zeros((pad, LANES), F32)
        for r0, rc in chunks:
            u0_ref[pl.ds(pad + r0, rc), :] = a_ref[pl.ds(r0, rc), :] * _sigmoid(g_ref[pl.ds(r0, rc), :])
        for r0, rc in chunks:
            acc = jnp.zeros((rc, LANES), F32) + b_ref[...]
            for k in range(kw):
                acc = acc + w_ref[pl.ds(k, 1), :] * u0_ref[pl.ds(pad + r0 - (kw - 1) + k, rc), :]
            o_ref[pl.ds(r0, rc), :] = acc

    body, ex_in, ex_out, ex_scratch = _hosted(exchange, (ncb,), body)
    if exchange is not None:
        body = functools.partial(body, n_own_in=4, n_own_out=1)
    hbm = pl.BlockSpec(memory_space=pl.ANY)
    outs = pl.pallas_call(
        body, name=name,
        out_shape=(jax.ShapeDtypeStruct((s, cc), F32), *ex_out),
        grid=(ncb,),
        in_specs=[pl.BlockSpec((s, LANES), lambda c: (0, c)), pl.BlockSpec((s, LANES), lambda c: (0, ncb + c)),
                  pl.BlockSpec((kw, LANES), lambda c: (0, c)), pl.BlockSpec((1, LANES), lambda c: (0, c))] + [hbm] * len(ex_in),
        out_specs=(pl.BlockSpec((s, LANES), lambda c: (0, c)), *([hbm] * len(ex_out))),
        scratch_shapes=[pltpu.VMEM((s + pad, LANES), F32)] + ex_scratch,
        compiler_params=_params(("arbitrary",) if exchange else ("parallel",)),
    )(proj, proj, w, b.reshape(1, cc), *ex_in)
    return outs[0], list(outs[1:])


def _glu_conv_bwd(du1, proj, w, *, cc, name, exchange=None):
    s = proj.shape[0]
    kw = w.shape[0]
    pad = 32
    ncb = cc // LANES
    chunks = _row_chunks(s)

    def body(d_ref, a_ref, g_ref, w_ref, da_ref, dg_ref, dw_ref, db_ref, u0_ref, dp_ref):
        u0_ref[pl.ds(0, pad), :] = jnp.zeros((pad, LANES), F32)
        dp_ref[pl.ds(s, pad), :] = jnp.zeros((pad, LANES), F32)
        for r0, rc in chunks:
            u0_ref[pl.ds(pad + r0, rc), :] = a_ref[pl.ds(r0, rc), :] * _sigmoid(g_ref[pl.ds(r0, rc), :])
            dp_ref[pl.ds(r0, rc), :] = d_ref[pl.ds(r0, rc), :]
        dws = [jnp.zeros((1, LANES), F32) for _ in range(kw)]
        dbs = jnp.zeros((1, LANES), F32)
        for r0, rc in chunks:
            d = dp_ref[pl.ds(r0, rc), :]
            dbs = dbs + jnp.sum(d, axis=0, keepdims=True)
            du0 = jnp.zeros((rc, LANES), F32)
            for k in range(kw):
                du0 = du0 + w_ref[pl.ds(k, 1), :] * dp_ref[pl.ds(r0 + (kw - 1) - k, rc), :]
                dws[k] = dws[k] + jnp.sum(d * u0_ref[pl.ds(pad + r0 - (kw - 1) + k, rc), :], axis=0, keepdims=True)
            sg = _sigmoid(g_ref[pl.ds(r0, rc), :])
            a = a_ref[pl.ds(r0, rc), :]
            da_ref[pl.ds(r0, rc), :] = (du0 * sg).astype(BF16)
            dg_ref[pl.ds(r0, rc), :] = (du0 * a * sg * (1.0 - sg)).astype(BF16)
        for k in range(kw):
            dw_ref[pl.ds(k, 1), :] = dws[k]
        db_ref[...] = dbs

    col = lambda off: pl.BlockSpec((s, LANES), lambda c: (0, off + c))
    body, ex_in, ex_out, ex_scratch = _hosted(exchange, (ncb,), body)
    if exchange is not None:
        body = functools.partial(body, n_own_in=4, n_own_out=4)
    hbm = pl.BlockSpec(memory_space=pl.ANY)
    outs = pl.pallas_call(
        body, name=name,
        out_shape=(jax.ShapeDtypeStruct((s, cc), BF16), jax.ShapeDtypeStruct((s, cc), BF16),
                   jax.ShapeDtypeStruct((kw, cc), F32), jax.ShapeDtypeStruct((1, cc), F32), *ex_out),
        grid=(ncb,),
        in_specs=[col(0), col(0), col(ncb), pl.BlockSpec((kw, LANES), lambda c: (0, c))] + [hbm] * len(ex_in),
        out_specs=(col(0), col(0), pl.BlockSpec((kw, LANES), lambda c: (0, c)), pl.BlockSpec((1, LANES), lambda c: (0, c)),
                   *([hbm] * len(ex_out))),
        scratch_shapes=[pltpu.VMEM((s + pad, LANES), F32), pltpu.VMEM((s + pad, LANES), F32)] + ex_scratch,
        compiler_params=_params(("arbitrary",) if exchange else ("parallel",)),
    )(du1, proj, proj, w, *ex_in)
    da, dg, dw, db = outs[:4]
    return da, dg, dw, db.reshape(cc), list(outs[4:])


def _cln_silu_fwd(u1, g, b, *, name, ts=512):
    s, cc = u1.shape
    ts = _pick(s, ts)

    def body(u_ref, g_ref, b_ref, o_ref):
        xhat, _ = _ln_stats(u_ref[...])
        y = xhat * g_ref[...] + b_ref[...]
        o_ref[...] = (y * _sigmoid(y)).astype(BF16)

    tok = pl.BlockSpec((ts, cc), lambda i: (i, 0))
    vec = pl.BlockSpec((1, cc), lambda i: (0, 0))
    return pl.pallas_call(body, name=name, out_shape=jax.ShapeDtypeStruct((s, cc), BF16), grid=(s // ts,),
                          in_specs=[tok, vec, vec], out_specs=tok,
                          compiler_params=_params(("parallel",)))(u1, g.reshape(1, cc), b.reshape(1, cc))


def _cln_silu_bwd(dua, u1, g, b, *, name, ts=512):
    s, cc = u1.shape
    ts = _pick(s, ts)

    def body(d_ref, u_ref, g_ref, b_ref, du_ref, dg_ref, db_ref):
        xhat, rstd = _ln_stats(u_ref[...])
        y = xhat * g_ref[...] + b_ref[...]
        sg = _sigmoid(y)
        dy = d_ref[...] * (sg * (1.0 + y * (1.0 - sg)))
        dxhat = dy * g_ref[...]
        m1 = jnp.mean(dxhat, axis=-1, keepdims=True)
        m2 = jnp.mean(dxhat * xhat, axis=-1, keepdims=True)
        du_ref[...] = rstd * (dxhat - m1 - xhat * m2)

        @pl.when(pl.program_id(0) == 0)
        def _():
            dg_ref[...] = jnp.zeros_like(dg_ref)
            db_ref[...] = jnp.zeros_like(db_ref)

        dg_ref[...] += jnp.sum(dy * xhat, axis=0, keepdims=True)
        db_ref[...] += jnp.sum(dy, axis=0, keepdims=True)

    tok = pl.BlockSpec((ts, cc), lambda i: (i, 0))
    vec = pl.BlockSpec((1, cc), lambda i: (0, 0))
    du1, dg, db = pl.pallas_call(
        body, name=name,
        out_shape=(jax.ShapeDtypeStruct((s, cc), F32), jax.ShapeDtypeStruct((1, cc), F32),
                   jax.ShapeDtypeStruct((1, cc), F32)),
        grid=(s // ts,), in_specs=[tok, tok, vec, vec], out_specs=(tok, vec, vec),
        compiler_params=_params(("arbitrary",)),
    )(dua, u1, g.reshape(1, cc), b.reshape(1, cc))
    return du1, dg.reshape(cc), db.reshape(cc)


def _softplus_parts(z):
    lk = jnp.minimum(-z, 0.0) - jnp.log1p(jnp.exp(-jnp.abs(z)))
    return lk, z + lk


def _stack_heads(x, hms):
    return jnp.concatenate([jnp.where(hm, x, 0.0) for hm in hms], axis=0).astype(BF16)


def _heads_side_by_side(x_st, tq):
    return jnp.concatenate([x_st[:tq], x_st[tq:]], axis=1)


def _sb_tile_masks(tq):
    row = lax.broadcasted_iota(jnp.int32, (2 * tq, tq), 0)
    col = lax.broadcasted_iota(jnp.int32, (2 * tq, tq), 1)
    vis = (col < jnp.where(row >= tq, row - tq, row)).astype(F32)
    krow, kcol = row[:tq], col[:tq]
    return vis, (krow > kcol).astype(BF16), (krow < kcol).astype(BF16)


_NT = (((1,), (1,)), ((), ()))
_TN = (((0,), (0,)), ((), ()))


def _head_masks():
    lane = lax.broadcasted_iota(jnp.int32, (1, LANES), 1)
    return [(lane >= SB_HEAD_DIM * h) & (lane < SB_HEAD_DIM * (h + 1)) for h in range(2)]


def _hosted(exchange, grid, body):
    if exchange is None:
        return body, [], [], []
    n_in, n_out, n_sems = len(exchange.inputs), len(exchange.out_shapes), len(exchange.scratch_shapes)

    def wrapped(*refs, n_own_in, n_own_out):
        own_in, ex_in = refs[:n_own_in], refs[n_own_in:n_own_in + n_in]
        rest = refs[n_own_in + n_in:]
        own_out, ex_out = rest[:n_own_out], rest[n_own_out:n_own_out + n_out]
        own_scratch, sems = rest[n_own_out + n_out:len(rest) - n_sems], rest[len(rest) - n_sems:]
        ids = [pl.program_id(d) for d in range(len(grid))]
        first = functools.reduce(lambda x, y: x & y, [i == 0 for i in ids])
        last = functools.reduce(lambda x, y: x & y, [i == g - 1 for i, g in zip(ids, grid)])

        @pl.when(first)
        def _():
            exchange.start(ex_in, ex_out, sems)

        body(*own_in, *own_out, *own_scratch)

        @pl.when(last)
        def _():
            exchange.finish(ex_in, ex_out, sems)

    return wrapped, list(exchange.inputs), list(exchange.out_shapes), list(exchange.scratch_shapes)


def _sb_fwd(proj, *, col0, n_pairs, name, tq=256, exchange=None):
    s = proj.shape[0]
    tq = _pick(s, tq)
    nq = s // tq
    cb0 = col0 // LANES
    scale = SB_HEAD_DIM ** -0.5

    def body(q_ref, k_ref, v_ref, vis_ref, after_ref, o_ref, t_ref, n_ref):
        i = pl.program_id(1)
        hms = _head_masks()
        q_st = _stack_heads(q_ref[...] * scale, hms)

        def tile(j, c, acc, masked):
            start = pl.multiple_of(j * tq, tq)
            kb = k_ref[pl.ds(start, tq), :].astype(BF16)
            v_st = _stack_heads(v_ref[pl.ds(start, tq), :], hms)
            z = lax.dot_general(q_st, kb, _NT, preferred_element_type=F32)
            lk, lb = _softplus_parts(z)
            if masked:
                lk = lk * vis_ref[...]
            later = jnp.dot(lk.astype(BF16), after_ref[...], preferred_element_type=F32)
            a = jnp.exp(lb + later + c)
            if masked:
                a = a * vis_ref[...]
            acc = acc + jnp.dot(_heads_side_by_side(a.astype(BF16), tq), v_st, preferred_element_type=F32)
            return c + jnp.sum(lk, axis=1, keepdims=True), acc

        def more(st):
            return jnp.logical_and(st[0] < i, jnp.max(st[1]) > -EXP_UNDERFLOW)

        def step(st):
            c, acc = tile(i - 1 - st[0], st[1], st[2], False)
            return st[0] + 1, c, acc

        c, acc = tile(i, jnp.zeros((2 * tq, 1), F32), jnp.zeros((tq, LANES), F32), True)
        n, c, acc = lax.while_loop(more, step, (jnp.int32(0), c, acc))
        o_ref[...] = acc.astype(BF16)
        t_ref[...] = jnp.where(hms[0], c[:tq], c[tq:])
        n_ref[...] = jnp.zeros((8, LANES), F32) + n.astype(F32)

    grid = (n_pairs, nq)
    body, ex_in, ex_out, ex_scratch = _hosted(exchange, grid, body)
    if exchange is not None:
        body = functools.partial(body, n_own_in=5, n_own_out=3)
    hbm = pl.BlockSpec(memory_space=pl.ANY)
    seq = lambda off: pl.BlockSpec((s, LANES), lambda p, i: (0, cb0 + off + p))
    whole = lambda rows: pl.BlockSpec((rows, tq), lambda p, i: (0, 0))
    vis, m_after, _ = _sb_tile_masks(tq)
    outs = pl.pallas_call(
        body, name=name,
        out_shape=(jax.ShapeDtypeStruct((s, n_pairs * LANES), BF16), jax.ShapeDtypeStruct((n_pairs, s, LANES), F32),
                   jax.ShapeDtypeStruct((n_pairs, nq * 8, LANES), F32), *ex_out),
        grid=grid,
        in_specs=[pl.BlockSpec((tq, LANES), lambda p, i: (i, cb0 + p)), seq(n_pairs), seq(2 * n_pairs),
                  whole(2 * tq), whole(tq)] + [hbm] * len(ex_in),
        out_specs=(pl.BlockSpec((tq, LANES), lambda p, i: (i, p)), pl.BlockSpec((None, tq, LANES), lambda p, i: (p, i, 0)),
                   pl.BlockSpec((None, 8, LANES), lambda p, i: (p, i, 0)), *([hbm] * len(ex_out))),
        scratch_shapes=ex_scratch,
        compiler_params=_params(("arbitrary", "arbitrary") if exchange else ("parallel", "arbitrary")),
    )(proj, proj, proj, vis, m_after, *ex_in)
    return outs[0], outs[1], outs[2], list(outs[3:])


def _sb_bwd(proj, t_sum, n_walked, dua, *, col0, n_pairs, do_col0, name, tq=256, exchange=None):
    s = proj.shape[0]
    tq = _pick(s, tq)
    cb0 = col0 // LANES
    dcb0 = do_col0 // LANES
    scale = SB_HEAD_DIM ** -0.5

    def body(q_ref, k_ref, v_ref, t_ref, n_ref, do_ref, vis_ref, after_ref, before_ref, dq_ref, dk_ref, dv_ref):
        i = pl.program_id(1)

        @pl.when(i == 0)
        def _():
            dk_ref[...] = jnp.zeros_like(dk_ref)
            dv_ref[...] = jnp.zeros_like(dv_ref)

        hms = _head_masks()
        q_st = _stack_heads(q_ref[...] * scale, hms)
        do_st = _stack_heads(do_ref[...], hms)
        t_st = jnp.concatenate([t_ref[:, SB_HEAD_DIM * h:SB_HEAD_DIM * h + 1] for h in range(2)], axis=0)
        first = i - jnp.max(n_ref[...]).astype(jnp.int32)

        def tile(j, p_sum, g_sum, dq, masked):
            start = pl.multiple_of(j * tq, tq)
            k = k_ref[pl.ds(start, tq), :]
            z = lax.dot_general(q_st, k.astype(BF16), _NT, preferred_element_type=F32)
            lk_raw, lb = _softplus_parts(z)
            lk = lk_raw * vis_ref[...] if masked else lk_raw
            p_next = p_sum + jnp.sum(lk, axis=1, keepdims=True)
            later = jnp.dot(lk.astype(BF16), after_ref[...], preferred_element_type=F32)
            a = jnp.exp(lb + (t_st - p_next) + later)
            if masked:
                a = a * vis_ref[...]
            da = lax.dot_general(do_st, v_ref[pl.ds(start, tq), :].astype(BF16), _NT, preferred_element_type=F32)
            g = a * da
            g_before = g_sum + jnp.dot(g.astype(BF16), before_ref[...], preferred_element_type=F32)
            dz = g * jnp.exp(lk_raw) - g_before * jnp.exp(lb)
            if masked:
                dz = dz * vis_ref[...]
            dzb = dz.astype(BF16)
            dv_ref[pl.ds(start, tq), :] += lax.dot_general(a.astype(BF16), do_st, _TN, preferred_element_type=F32)
            dk_ref[pl.ds(start, tq), :] += lax.dot_general(dzb, q_st, _TN, preferred_element_type=F32)
            dq = dq + jnp.dot(_heads_side_by_side(dzb, tq), _stack_heads(k, hms), preferred_element_type=F32)
            return p_next, g_sum + jnp.sum(g, axis=1, keepdims=True), dq

        zero = jnp.zeros((2 * tq, 1), F32)
        st = lax.fori_loop(first, i, lambda j, st: tile(j, *st, False), (zero, zero, jnp.zeros((tq, LANES), F32)))
        dq_ref[...] = tile(i, *st, True)[2] * scale

    grid = (n_pairs, s // tq)
    body, ex_in, ex_out, ex_scratch = _hosted(exchange, grid, body)
    if exchange is not None:
        body = functools.partial(body, n_own_in=9, n_own_out=3)
    hbm = pl.BlockSpec(memory_space=pl.ANY)
    seq = lambda off: pl.BlockSpec((s, LANES), lambda p, i: (0, cb0 + off + p))
    whole = lambda rows: pl.BlockSpec((rows, tq), lambda p, i: (0, 0))
    out = jax.ShapeDtypeStruct((s, n_pairs * LANES), F32)
    res = pl.BlockSpec((s, LANES), lambda p, i: (0, p))
    outs = pl.pallas_call(
        body, name=name,
        out_shape=(out, out, out, *ex_out),
        grid=grid,
        in_specs=[pl.BlockSpec((tq, LANES), lambda p, i: (i, cb0 + p)), seq(n_pairs), seq(2 * n_pairs),
                  pl.BlockSpec((None, tq, LANES), lambda p, i: (p, i, 0)),
                  pl.BlockSpec((None, 8, LANES), lambda p, i: (p, i, 0)),
                  pl.BlockSpec((tq, LANES), lambda p, i: (i, dcb0 + p)),
                  whole(2 * tq), whole(tq), whole(tq)] + [hbm] * len(ex_in),
        out_specs=(pl.BlockSpec((tq, LANES), lambda p, i: (i, p)), res, res, *([hbm] * len(ex_out))),
        scratch_shapes=ex_scratch,
        compiler_params=_params(("arbitrary", "arbitrary")),
    )(proj, proj, proj, t_sum, n_walked, dua, *_sb_tile_masks(tq), *ex_in)
    return outs[0], outs[1], outs[2], list(outs[3:])


def _mem_attn_fwd(qm, km, vm, *, name, tq=512):
    s, d = qm.shape
    heads = d // MEM_HEAD_DIM
    mlen = km.shape[0]
    tq = _pick(s, tq)
    scale = MEM_HEAD_DIM ** -0.5

    def body(q_ref, k_ref, v_ref, o_ref):
        for h in range(heads):
            sl = slice(h * MEM_HEAD_DIM, (h + 1) * MEM_HEAD_DIM)
            q = (q_ref[:, sl] * scale).astype(BF16)
            sc = lax.dot_general(q, k_ref[:, sl].astype(BF16), _NT, preferred_element_type=F32)
            e = jnp.exp(sc - jnp.max(sc, axis=1, keepdims=True))
            p = e / jnp.sum(e, axis=1, keepdims=True)
            o_ref[:, sl] = jnp.dot(p.astype(BF16), v_ref[:, sl].astype(BF16), preferred_element_type=F32).astype(BF16)

    tok = pl.BlockSpec((tq, d), lambda i: (i, 0))
    kv = pl.BlockSpec((mlen, d), lambda i: (0, 0))
    return pl.pallas_call(body, name=name, out_shape=jax.ShapeDtypeStruct((s, d), BF16), grid=(s // tq,),
                          in_specs=[tok, kv, kv], out_specs=tok, compiler_params=_params(("parallel",)))(qm, km, vm)


def _mem_attn_bwd(qm, km, vm, do, *, name, tq=512):
    s, d = qm.shape
    heads = d // MEM_HEAD_DIM
    mlen = km.shape[0]
    tq = _pick(s, tq)
    scale = MEM_HEAD_DIM ** -0.5

    def body(q_ref, k_ref, v_ref, do_ref, dq_ref, dk_ref, dv_ref):
        @pl.when(pl.program_id(0) == 0)
        def _():
            dk_ref[...] = jnp.zeros_like(dk_ref)
            dv_ref[...] = jnp.zeros_like(dv_ref)

        for h in range(heads):
            sl = slice(h * MEM_HEAD_DIM, (h + 1) * MEM_HEAD_DIM)
            q = (q_ref[:, sl] * scale).astype(BF16)
            k = k_ref[:, sl].astype(BF16)
            v = v_ref[:, sl].astype(BF16)
            sc = lax.dot_general(q, k, _NT, preferred_element_type=F32)
            e = jnp.exp(sc - jnp.max(sc, axis=1, keepdims=True))
            p = e / jnp.sum(e, axis=1, keepdims=True)
            dob = do_ref[:, sl].astype(BF16)
            dv_ref[:, sl] += lax.dot_general(p.astype(BF16), dob, _TN, preferred_element_type=F32)
            dp = lax.dot_general(dob, v, _NT, preferred_element_type=F32)
            ds = (p * (dp - jnp.sum(dp * p, axis=1, keepdims=True))).astype(BF16)
            dq_ref[:, sl] = (jnp.dot(ds, k, preferred_element_type=F32) * scale).astype(BF16)
            dk_ref[:, sl] += lax.dot_general(ds, q, _TN, preferred_element_type=F32)

    tok = pl.BlockSpec((tq, d), lambda i: (i, 0))
    kv = pl.BlockSpec((mlen, d), lambda i: (0, 0))
    return pl.pallas_call(
        body, name=name,
        out_shape=(jax.ShapeDtypeStruct((s, d), BF16), jax.ShapeDtypeStruct((mlen, d), F32),
                   jax.ShapeDtypeStruct((mlen, d), F32)),
        grid=(s // tq,), in_specs=[tok, kv, kv, tok], out_specs=(tok, kv, kv),
        compiler_params=_params(("arbitrary",)),
    )(qm, km, vm, do)


def _ffn_act_fwd(up, w, b, *, name, exchange=None):
    s, two_f = up.shape
    ff = two_f // 2
    nfb = ff // LANES
    kw = w.shape[0]
    pad = 8
    chunks = _row_chunks(s)

    def body(v_ref, g_ref, wv_ref, wg_ref, bv_ref, bg_ref, o_ref, vp_ref, gp_ref):
        vp_ref[pl.ds(0, pad), :] = jnp.zeros((pad, LANES), F32)
        gp_ref[pl.ds(0, pad), :] = jnp.zeros((pad, LANES), F32)
        for r0, rc in chunks:
            vp_ref[pl.ds(pad + r0, rc), :] = v_ref[pl.ds(r0, rc), :].astype(F32)
            gp_ref[pl.ds(pad + r0, rc), :] = g_ref[pl.ds(r0, rc), :].astype(F32)
        for r0, rc in chunks:
            vc = jnp.zeros((rc, LANES), F32) + bv_ref[...]
            gc = jnp.zeros((rc, LANES), F32) + bg_ref[...]
            for k in range(kw):
                off = pad + r0 - (kw - 1) + k
                vc = vc + wv_ref[pl.ds(k, 1), :] * vp_ref[pl.ds(off, rc), :]
                gc = gc + wg_ref[pl.ds(k, 1), :] * gp_ref[pl.ds(off, rc), :]
            o_ref[pl.ds(r0, rc), :] = (gc * _sigmoid(gc) * vc).astype(BF16)

    col = lambda off: pl.BlockSpec((s, LANES), lambda c: (0, off + c))
    tap = lambda off: pl.BlockSpec((kw, LANES), lambda c: (0, off + c))
    vec = lambda off: pl.BlockSpec((1, LANES), lambda c: (0, off + c))
    body, ex_in, ex_out, ex_scratch = _hosted(exchange, (nfb,), body)
    if exchange is not None:
        body = functools.partial(body, n_own_in=6, n_own_out=1)
    hbm = pl.BlockSpec(memory_space=pl.ANY)
    outs = pl.pallas_call(
        body, name=name, out_shape=(jax.ShapeDtypeStruct((s, ff), BF16), *ex_out), grid=(nfb,),
        in_specs=[col(0), col(nfb), tap(0), tap(nfb), vec(0), vec(nfb)] + [hbm] * len(ex_in),
        out_specs=(col(0), *([hbm] * len(ex_out))),
        scratch_shapes=[pltpu.VMEM((s + pad, LANES), F32), pltpu.VMEM((s + pad, LANES), F32)] + ex_scratch,
        compiler_params=_params(("arbitrary",) if exchange else ("parallel",)),
    )(up, up, w, w, b.reshape(1, two_f), b.reshape(1, two_f), *ex_in)
    return outs[0], list(outs[1:])


def _ffn_act_bwd(up, dact, w, b, *, name, exchange=None):
    s, two_f = up.shape
    ff = two_f // 2
    nfb = ff // LANES
    kw = w.shape[0]
    pad = 8
    chunks = _row_chunks(s)

    def body(v_ref, g_ref, d_ref, wv_ref, wg_ref, bv_ref, bg_ref, dv_ref, dg_ref, dwv_ref, dwg_ref, dbv_ref, dbg_ref,
             vp_ref, gp_ref, dvc_ref, dgc_ref):
        vp_ref[pl.ds(0, pad), :] = jnp.zeros((pad, LANES), F32)
        gp_ref[pl.ds(0, pad), :] = jnp.zeros((pad, LANES), F32)
        dvc_ref[pl.ds(s, pad), :] = jnp.zeros((pad, LANES), F32)
        dgc_ref[pl.ds(s, pad), :] = jnp.zeros((pad, LANES), F32)
        for r0, rc in chunks:
            vp_ref[pl.ds(pad + r0, rc), :] = v_ref[pl.ds(r0, rc), :].astype(F32)
            gp_ref[pl.ds(pad + r0, rc), :] = g_ref[pl.ds(r0, rc), :].astype(F32)
        dwv = [jnp.zeros((1, LANES), F32) for _ in range(kw)]
        dwg = [jnp.zeros((1, LANES), F32) for _ in range(kw)]
        dbv = jnp.zeros((1, LANES), F32)
        dbg = jnp.zeros((1, LANES), F32)
        for r0, rc in chunks:
            vc = jnp.zeros((rc, LANES), F32) + bv_ref[...]
            gc = jnp.zeros((rc, LANES), F32) + bg_ref[...]
            for k in range(kw):
                off = pad + r0 - (kw - 1) + k
                vc = vc + wv_ref[pl.ds(k, 1), :] * vp_ref[pl.ds(off, rc), :]
                gc = gc + wg_ref[pl.ds(k, 1), :] * gp_ref[pl.ds(off, rc), :]
            sg = _sigmoid(gc)
            d = d_ref[pl.ds(r0, rc), :].astype(F32)
            dvc = d * (gc * sg)
            dgc = d * vc * (sg * (1.0 + gc * (1.0 - sg)))
            dvc_ref[pl.ds(r0, rc), :] = dvc
            dgc_ref[pl.ds(r0, rc), :] = dgc
            dbv = dbv + jnp.sum(dvc, axis=0, keepdims=True)
            dbg = dbg + jnp.sum(dgc, axis=0, keepdims=True)
            for k in range(kw):
                off = pad + r0 - (kw - 1) + k
                dwv[k] = dwv[k] + jnp.sum(dvc * vp_ref[pl.ds(off, rc), :], axis=0, keepdims=True)
                dwg[k] = dwg[k] + jnp.sum(dgc * gp_ref[pl.ds(off, rc), :], axis=0, keepdims=True)
        for r0, rc in chunks:
            dv = jnp.zeros((rc, LANES), F32)
            dg = jnp.zeros((rc, LANES), F32)
            for k in range(kw):
                off = r0 + (kw - 1) - k
                dv = dv + wv_ref[pl.ds(k, 1), :] * dvc_ref[pl.ds(off, rc), :]
                dg = dg + wg_ref[pl.ds(k, 1), :] * dgc_ref[pl.ds(off, rc), :]
            dv_ref[pl.ds(r0, rc), :] = dv.astype(BF16)
            dg_ref[pl.ds(r0, rc), :] = dg.astype(BF16)
        for k in range(kw):
            dwv_ref[pl.ds(k, 1), :] = dwv[k]
            dwg_ref[pl.ds(k, 1), :] = dwg[k]
        dbv_ref[...] = dbv
        dbg_ref[...] = dbg

    col = lambda off: pl.BlockSpec((s, LANES), lambda c: (0, off + c))
    tap = lambda off: pl.BlockSpec((kw, LANES), lambda c: (0, off + c))
    vec = lambda off: pl.BlockSpec((1, LANES), lambda c: (0, off + c))
    big = lambda: pltpu.VMEM((s + pad, LANES), F32)
    body, ex_in, ex_out, ex_scratch = _hosted(exchange, (nfb,), body)
    if exchange is not None:
        body = functools.partial(body, n_own_in=7, n_own_out=6)
    hbm = pl.BlockSpec(memory_space=pl.ANY)
    outs = pl.pallas_call(
        body, name=name,
        out_shape=(jax.ShapeDtypeStruct((s, ff), BF16), jax.ShapeDtypeStruct((s, ff), BF16),
                   jax.ShapeDtypeStruct((kw, ff), F32), jax.ShapeDtypeStruct((kw, ff), F32),
                   jax.ShapeDtypeStruct((1, ff), F32), jax.ShapeDtypeStruct((1, ff), F32), *ex_out),
        grid=(nfb,),
        in_specs=[col(0), col(nfb), col(0), tap(0), tap(nfb), vec(0), vec(nfb)] + [hbm] * len(ex_in),
        out_specs=(col(0), col(0), tap(0), tap(0), vec(0), vec(0), *([hbm] * len(ex_out))),
        scratch_shapes=[big(), big(), big(), big()] + ex_scratch,
        compiler_params=_params(("arbitrary",) if exchange else ("parallel",)),
    )(up, up, dact, w, w, b.reshape(1, two_f), b.reshape(1, two_f), *ex_in)
    dv, dg, dwv, dwg, dbv, dbg = outs[:6]
    return (dv, dg), jnp.concatenate([dwv, dwg], axis=1), jnp.concatenate([dbv, dbg], axis=1).reshape(two_f), list(outs[6:])


def _sum_parts(parts, *, name, tr=256):
    n_parts, rows, cols = parts.shape
    tr = _pick(rows, tr, 16)

    def body(p_ref, o_ref):
        g = p_ref[0].astype(F32)
        for k in range(1, n_parts):
            g = g + p_ref[k].astype(F32)
        o_ref[...] = g

    return pl.pallas_call(
        body, name=name, out_shape=jax.ShapeDtypeStruct((rows, cols), F32), grid=(rows // tr,),
        in_specs=[pl.BlockSpec((n_parts, tr, cols), lambda i: (0, i, 0))],
        out_specs=pl.BlockSpec((tr, cols), lambda i: (i, 0)), compiler_params=_params(("parallel",)),
    )(parts)


def _sum_adamw(parts, w, m, v, *, layer, so_far, name, tr=256):
    n_parts, rows, cols = parts.shape
    depth = w.shape[0]
    tr = _pick(rows, tr, 16)
    c1 = 1.0 / (1.0 - ADAM_B1 ** ADAM_STEP)
    c2 = 1.0 / (1.0 - ADAM_B2 ** ADAM_STEP)

    def body(p_ref, w_ref, m_ref, v_ref, *rest):
        g_ref, d_ref, nm_ref, nv_ref = rest[-4:]
        g = p_ref[0].astype(F32)
        for k in range(1, n_parts):
            g = g + p_ref[k].astype(F32)
        nm = ADAM_B1 * m_ref[...] + (1.0 - ADAM_B1) * g
        nv = ADAM_B2 * v_ref[...] + (1.0 - ADAM_B2) * (g * g)
        g_ref[...] = g
        nm_ref[...] = nm
        nv_ref[...] = nv
        d_ref[...] = -ADAM_LR * ((nm * c1) / (jnp.sqrt(nv * c2) + ADAM_EPS) + ADAM_WD * w_ref[...])

    blk = pl.BlockSpec((None, tr, cols), lambda i: (layer, i, 0))
    out = jax.ShapeDtypeStruct((depth, rows, cols), F32)
    kept = list(so_far) if so_far is not None else []
    return pl.pallas_call(
        body, name=name, out_shape=(out, out, out, out), grid=(rows // tr,),
        in_specs=[pl.BlockSpec((n_parts, tr, cols), lambda i: (0, i, 0)), blk, blk, blk]
        + [pl.BlockSpec(memory_space=pl.ANY)] * len(kept),
        out_specs=(blk, blk, blk, blk),
        input_output_aliases={4 + k: k for k in range(len(kept))},
        compiler_params=_params(("parallel",)),
    )(parts, w, m, v, *kept)


def _mesh_pos():
    return lax.axis_index("x"), lax.axis_index("y"), lax.axis_index("c")


def _flip(pos, k):
    x, y, c = pos
    return (1 - x if k & 4 else x, 1 - y if k & 2 else y, 1 - c if k & 1 else c)


def _dev_index(pos):
    return 4 * pos[0] + 2 * pos[1] + pos[2]


N_PEERS = N_DEV - 1


class _Exchange:
    def __init__(self, inputs, out_shapes, start, finish, scratch_shapes=None):
        n = len(inputs)
        self.inputs, self.out_shapes, self.start, self.finish = list(inputs), list(out_shapes), start, finish
        self.scratch_shapes = scratch_shapes or [pltpu.SemaphoreType.DMA((n * N_PEERS,)),
                                                 pltpu.SemaphoreType.DMA((n * N_PEERS,)), pltpu.SemaphoreType.DMA((n,))]


def _both(ex1, ex2):
    if ex1 is None or ex2 is None:
        return ex1 or ex2
    n_in, n_out, n_sems = len(ex1.inputs), len(ex1.out_shapes), len(ex1.scratch_shapes)

    def halves(ins, outs, sems):
        return (ins[:n_in], outs[:n_out], sems[:n_sems]), (ins[n_in:], outs[n_out:], sems[n_sems:])

    def start(ins, outs, sems):
        a, b = halves(ins, outs, sems)
        ex1.start(*a)
        ex2.start(*b)

    def finish(ins, outs, sems):
        a, b = halves(ins, outs, sems)
        ex1.finish(*a)
        ex2.finish(*b)

    return _Exchange(ex1.inputs + ex2.inputs, ex1.out_shapes + ex2.out_shapes, start, finish,
                     ex1.scratch_shapes + ex2.scratch_shapes)


def _gather_exchange(xs):
    n = len(xs)

    def plan(x_refs, out_refs, sems):
        send_sems, recv_sems, local_sems = sems
        me = _mesh_pos()
        sibling = _flip(me, 1)
        chips = [_flip(me, 4), _flip(me, 2), _flip(me, 6)]

        def copy(a, k, block, to, from_input=False):
            slot = out_refs[a].at[_dev_index(block)]
            return pltpu.make_async_remote_copy(
                src_ref=x_refs[a] if from_input else slot, dst_ref=slot,
                send_sem=send_sems.at[a * N_PEERS + k], recv_sem=recv_sems.at[a * N_PEERS + k],
                device_id=to, device_id_type=pl.DeviceIdType.MESH)

        mine = [pltpu.make_async_copy(x_refs[a], out_refs[a].at[_dev_index(me)], local_sems.at[a]) for a in range(n)]
        first = [copy(a, 0, me, sibling, True) for a in range(n)]
        first += [copy(a, 1 + j, me, chip, True) for j, chip in enumerate(chips) for a in range(n)]
        return me, sibling, chips, copy, mine, first

    def start(x_refs, out_refs, sems):
        _, _, _, _, mine, first = plan(x_refs, out_refs, sems)
        for cp in mine + first:
            cp.start()

    def finish(x_refs, out_refs, sems):
        me, sibling, chips, copy, mine, first = plan(x_refs, out_refs, sems)
        passed = []
        for j, chip in enumerate(chips):
            for a in range(n):
                copy(a, 1 + j, chip, me).wait_recv()
                passed.append(copy(a, 4 + j, chip, sibling))
                passed[-1].start()
        for a in range(n):
            copy(a, 0, sibling, me).wait_recv()
        for j, chip in enumerate(chips):
            for a in range(n):
                copy(a, 4 + j, _flip(chip, 1), me).wait_recv()
        for cp in first + passed:
            cp.wait_send()
        for cp in mine:
            cp.wait()

    return _Exchange(xs, [jax.ShapeDtypeStruct((N_DEV,) + x.shape, x.dtype) for x in xs], start, finish)


def _scatter_exchange(xs):
    n = len(xs)

    def plan(x_refs, out_refs, sems):
        send_sems, recv_sems, local_sems = sems
        me = _mesh_pos()
        my_slot = _dev_index(me)

        def copy(a, k):
            peer = _flip(me, k)
            return pltpu.make_async_remote_copy(
                src_ref=x_refs[a].at[_dev_index(peer)], dst_ref=out_refs[a].at[my_slot],
                send_sem=send_sems.at[a * N_PEERS + k - 1], recv_sem=recv_sems.at[a * N_PEERS + k - 1],
                device_id=peer, device_id_type=pl.DeviceIdType.MESH)

        mine = [pltpu.make_async_copy(x_refs[a].at[my_slot], out_refs[a].at[my_slot], local_sems.at[a]) for a in range(n)]
        return mine, [copy(a, k) for k in range(1, N_DEV) for a in range(n)]

    def start(x_refs, out_refs, sems):
        mine, copies = plan(x_refs, out_refs, sems)
        for cp in mine + copies:
            cp.start()

    def finish(x_refs, out_refs, sems):
        mine, copies = plan(x_refs, out_refs, sems)
        for cp in copies:
            cp.wait_recv()
        for cp in copies:
            cp.wait_send()
        for cp in mine:
            cp.wait()

    return _Exchange(xs, [jax.ShapeDtypeStruct(x.shape, x.dtype) for x in xs], start, finish)


def _run_exchanges(exchanges, *, name):
    counts = [len(ex.inputs) for ex in exchanges]
    n = sum(counts)

    def body(*refs):
        offsets = [sum(counts[:e]) for e in range(len(exchanges))]
        views = [(refs[o:o + c], refs[n + o:n + o + c], refs[2 * n + 3 * e:2 * n + 3 * e + 3])
                 for e, (o, c) in enumerate(zip(offsets, counts))]
        for ex, view in zip(exchanges, views):
            ex.start(*view)
        for ex, view in zip(exchanges, views):
            ex.finish(*view)

    hbm = pl.BlockSpec(memory_space=pl.ANY)
    outs = pl.pallas_call(
        body, name=name, out_shape=tuple(s for ex in exchanges for s in ex.out_shapes), in_specs=[hbm] * n,
        out_specs=tuple([hbm] * n), scratch_shapes=[s for ex in exchanges for s in ex.scratch_shapes],
    )(*[x for ex in exchanges for x in ex.inputs])
    return [list(outs[sum(counts[:e]):sum(counts[:e + 1])]) for e in range(len(exchanges))]


def _pack(arrays):
    flat = jnp.concatenate([a.reshape(-1) for a in arrays])
    n = flat.shape[0]
    tile = PACK_W * PACK_ROW_ALIGN
    total = -(-n // tile) * tile
    return jnp.pad(flat, (0, total - n)).reshape(total // PACK_W, PACK_W)


def _unpack(buf, shapes):
    lead = buf.shape[:-2]
    flat = buf.reshape(lead + (-1,))
    out, off = [], 0
    for shp in shapes:
        n = 1
        for dim in shp:
            n *= dim
        out.append(flat[..., off:off + n].reshape(lead + tuple(shp)))
        off += n
    return out


def _join_columns(blocks):
    return jnp.moveaxis(blocks, 0, 2).reshape(blocks.shape[1], blocks.shape[2], -1)


def _mm_hosting(a, b, exchange, **kw):
    if exchange is None:
        return _mm(a, b, **kw), []
    return _mm(a, b, exchange=exchange, **kw)


GATHERED_BY_ATTENTION = ['w_out', 'mem_wq', 'mem_wk', 'mem_wv', 'mem_wo', 'ffn_up']
GATHERED_BY_CONV = ['ffn_down']


def _as_matrix(blocks):
    return blocks.reshape(-1, blocks.shape[-1])


def _layer_fwd(x, xb, memb, w, alpha, shards, next_w_in):
    w = dict(w)
    cc = w['conv_w'].shape[1]
    n_pairs = (N_DEV * shards['w_out'].shape[0] - cc) // LANES

    proj = _mm(xb, w['w_in'], tb=True, name="mm_proj")
    u1, got = _glu_conv_fwd(proj, w['conv_w'], w['conv_b'], cc=cc, name="glu_conv_fwd",
                            exchange=_gather_exchange([shards[n] for n in GATHERED_BY_CONV]))
    w.update({n: _as_matrix(f) for n, f in zip(GATHERED_BY_CONV, got)})
    u = _cln_silu_fwd(u1, w['conv_ln_g'], w['conv_ln_b'], name="cln_silu_fwd")
    att, t_sum, n_walked, got = _sb_fwd(proj, col0=2 * cc, n_pairs=n_pairs, name="sb_fwd",
                                        exchange=_gather_exchange([shards[n] for n in GATHERED_BY_ATTENTION]))
    w.update({n: _as_matrix(f) for n, f in zip(GATHERED_BY_ATTENTION, got)})
    ua = jnp.concatenate([u, att], axis=1)
    r1, x1, x1b = _mm_res_ln(ua, w['w_out'], x, w['ln1_g'], w['ln1_b'], alpha=alpha, name="mm_mix_ln")
    qm = _mm(x1b, w['mem_wq'], name="mm_memq")
    km = _mm(memb, w['mem_wk'], name="mm_memkv")
    vm = _mm(memb, w['mem_wv'], name="mm_memkv")
    o = _mem_attn_fwd(qm, km, vm, name="mem_attn_fwd")
    r2, x2, x2b = _mm_res_ln(o, w['mem_wo'], x1, w['ln2_g'], w['ln2_b'], alpha=alpha, name="mm_mix_ln")
    up = _mm(x2b, w['ffn_up'], tb=True, out_dtype=BF16, name="mm_up")
    act, got = _ffn_act_fwd(up, w['ffn_conv_w'], w['ffn_conv_b'], name="ffn_act_fwd",
                            exchange=_gather_exchange([next_w_in]) if next_w_in is not None else None)
    r3, x3, x3b = _mm_res_ln(act, w['ffn_down'], x2, w['ln3_g'], w['ln3_b'], alpha=alpha, name="mm_down_ln")
    saved = dict(xb=xb, proj=proj, u1=u1, t_sum=t_sum, n_walked=n_walked, ua=ua, r1=r1, x1b=x1b, qm=qm, km=km, vm=vm,
                 o=o, r2=r2, x2b=x2b, up=up, act=act, r3=r3)
    return x3, x3b, saved, w, _as_matrix(got[0]) if got else None


def _layer_bwd(top, sv, memb, w, alpha, carried, also_during_attention=None):
    g, received = {}, {}
    cc = w['conv_w'].shape[1]
    n_pairs = (w['w_out'].shape[0] - cc) // LANES

    def sending(sends):
        ex = _scatter_exchange([gm.reshape(N_DEV, -1, gm.shape[-1]) for _, gm in sends]) if sends else None
        return ex, lambda got: received.update({key: blocks for (key, _), blocks in zip(sends, got)})

    if top[1] is None:
        dr3, dr3b, g['ln3_g'], g['ln3_b'] = _ln_bwd(top[0], None, sv['r3'], w['ln3_g'], alpha=alpha, name="ln_bwd")
    else:
        dr3, dr3b, g['ln3_g'], g['ln3_b'] = _mm_ln_bwd(*top, sv['r3'], w['ln3_g'], alpha=alpha, name="mm_dx_in_ln")
    g_down = _mm(sv['act'], dr3b, ta=True, out_dtype=BF16, name="mm_dw_down")
    dact = _mm(dr3b, w['ffn_down'], tb=True, out_dtype=BF16, name="mm_dact")
    ex, file = sending(carried + [('ffn_down', g_down)])
    dup, g['ffn_conv_w'], g['ffn_conv_b'], got = _ffn_act_bwd(sv['up'], dact, w['ffn_conv_w'], w['ffn_conv_b'],
                                                                name="ffn_act_bwd", exchange=ex)
    file(got)
    g_up = jnp.concatenate([_mm(half, sv['x2b'], ta=True, out_dtype=BF16, name="mm_dw_up") for half in dup], axis=0)
    dr2, dr2b, g['ln2_g'], g['ln2_b'] = _mm_ln_bwd(dup, w['ffn_up'], dr3, sv['r2'], w['ln2_g'], alpha=alpha, name="mm_dx_up_ln")
    g_wo = _mm(sv['o'], dr2b, ta=True, out_dtype=BF16, name="mm_dw_sq")
    do = _mm(dr2b, w['mem_wo'], tb=True, out_dtype=BF16, name="mm_dx_sq")
    dqm, dkm, dvm = _mem_attn_bwd(sv['qm'], sv['km'], sv['vm'], do, name="mem_attn_bwd")
    g_wq = _mm(sv['x1b'], dqm, ta=True, out_dtype=BF16, name="mm_dw_sq")
    g_wk = _mm(memb, dkm, ta=True, out_dtype=BF16, name="mm_dw_memkv")
    g_wv = _mm(memb, dvm, ta=True, out_dtype=BF16, name="mm_dw_memkv")
    dr1, dr1b, g['ln1_g'], g['ln1_b'] = _mm_ln_bwd(dqm, w['mem_wq'], dr2, sv['r1'], w['ln1_g'], tb=True, alpha=alpha,
                                                    name="mm_dx_sq_ln")
    g_out = _mm(sv['ua'], dr1b, ta=True, out_dtype=BF16, name="mm_dw_sq")
    dua = _mm(dr1b, w['w_out'], tb=True, name="mm_dx_sq")
    du1, g['conv_ln_g'], g['conv_ln_b'] = _cln_silu_bwd(dua, sv['u1'], w['conv_ln_g'], w['conv_ln_b'], name="cln_silu_bwd")
    ex, file = sending([('mem_wo', g_wo), ('mem_wq', g_wq), ('w_out', g_out)])
    dga, dgg, g['conv_w'], g['conv_b'], got = _glu_conv_bwd(du1, sv['proj'], w['conv_w'], cc=cc, name="glu_conv_bwd",
                                                             exchange=ex)
    file(got)
    ex, file = sending([('ffn_up', g_up), ('mem_wk', g_wk), ('mem_wv', g_wv)])
    extra = also_during_attention(g) if also_during_attention else None
    dq, dk, dv, got = _sb_bwd(sv['proj'], sv['t_sum'], sv['n_walked'], dua, col0=2 * cc, n_pairs=n_pairs, do_col0=cc,
                              name="sb_bwd", exchange=_both(ex, extra))
    file(got[:len(ex.inputs)])
    also_got = got[len(ex.inputs):]
    dproj = jnp.concatenate([dga, dgg, dq.astype(BF16), dk.astype(BF16), dv.astype(BF16)], axis=1)
    g_in = _mm(dproj, sv['xb'], ta=True, out_dtype=BF16, name="mm_dw_in")
    return (dproj, w['w_in'], dr1), g, received, [('w_in', g_in)], also_got


def kernel(x, mem, w_in, conv_w, conv_b, conv_ln_g, conv_ln_b, w_out, ln1_g, ln1_b, mem_wq, mem_wk, mem_wv, mem_wo, ln2_g, ln2_b, ffn_up, ffn_conv_w, ffn_conv_b, ffn_down, ln3_g, ln3_b, loss_target, m_w_in, m_conv_w, m_conv_b, m_conv_ln_g, m_conv_ln_b, m_w_out, m_ln1_g, m_ln1_b, m_mem_wq, m_mem_wk, m_mem_wv, m_mem_wo, m_ln2_g, m_ln2_b, m_ffn_up, m_ffn_conv_w, m_ffn_conv_b, m_ffn_down, m_ln3_g, m_ln3_b, v_w_in, v_conv_w, v_conv_b, v_conv_ln_g, v_conv_ln_b, v_w_out, v_ln1_g, v_ln1_b, v_mem_wq, v_mem_wk, v_mem_wv, v_mem_wo, v_ln2_g, v_ln2_b, v_ffn_up, v_ffn_conv_w, v_ffn_conv_b, v_ffn_down, v_ln3_g, v_ln3_b):
    wts = dict(zip(WEIGHTS, (w_in, conv_w, conv_b, conv_ln_g, conv_ln_b, w_out, ln1_g, ln1_b, mem_wq, mem_wk, mem_wv,
                             mem_wo, ln2_g, ln2_b, ffn_up, ffn_conv_w, ffn_conv_b, ffn_down, ln3_g, ln3_b)))
    mom = dict(zip(WEIGHTS, (m_w_in, m_conv_w, m_conv_b, m_conv_ln_g, m_conv_ln_b, m_w_out, m_ln1_g, m_ln1_b, m_mem_wq,
                             m_mem_wk, m_mem_wv, m_mem_wo, m_ln2_g, m_ln2_b, m_ffn_up, m_ffn_conv_w, m_ffn_conv_b,
                             m_ffn_down, m_ln3_g, m_ln3_b)))
    var = dict(zip(WEIGHTS, (v_w_in, v_conv_w, v_conv_b, v_conv_ln_g, v_conv_ln_b, v_w_out, v_ln1_g, v_ln1_b, v_mem_wq,
                             v_mem_wk, v_mem_wv, v_mem_wo, v_ln2_g, v_ln2_b, v_ffn_up, v_ffn_conv_w, v_ffn_conv_b,
                             v_ffn_down, v_ln3_g, v_ln3_b)))
    depth = w_in.shape[0]
    alpha = (2.0 * depth) ** 0.25
    my_index = _dev_index(_mesh_pos())

    def row_blocks(src, n, col_sharded):
        return jnp.swapaxes(src[n], 1, 2) if col_sharded else src[n]

    bf16_blocks = {n: row_blocks(wts, n, cs).astype(BF16) for n, cs in MATRICES}
    shards = [{n: bf16_blocks[n][l] for n, _ in MATRICES} for l in range(depth)]
    tap_shapes = [wts[n].shape for n in TAPS]
    (gathered_taps, got), = _run_exchanges([_gather_exchange([_pack([wts[n] for n in TAPS]), shards[0]['w_in']])],
                                           name="gather_first")
    full_taps = {n: _join_columns(t) for n, t in zip(TAPS, _unpack(gathered_taps, tap_shapes))}
    full_w_in = _as_matrix(got)

    xs = x[0]
    memb = mem[0].astype(BF16)
    h, hb = xs, xs.astype(BF16)
    saved, weights = [], []
    for l in range(depth):
        w = {'w_in': full_w_in}
        w.update({n: full_taps[n][l] for n in TAPS})
        w.update({n: wts[n][l] for n in REPLICATED})
        h, hb, sv, w, full_w_in = _layer_fwd(h, hb, memb, w, alpha, shards[l],
                                             shards[l + 1]['w_in'] if l + 1 < depth else None)
        saved.append(sv)
        weights.append(w)

    dy, loss_row = _loss_and_grad(h, loss_target[0], name="loss")
    loss = lax.psum(_row_sum(loss_row, name="loss_sum")[0, 0], ("x", "y", "c"))

    results = {}
    col_sharded = dict(MATRICES)
    state = {n: [row_blocks(src, n, cs) for src in (wts, mom, var)] for n, cs in MATRICES}

    def update(n, l, parts):
        results[n] = _sum_adamw(parts, *state[n], layer=l, so_far=results.get(n), name="adamw_matrix")

    small = REPLICATED + TAPS

    def gather_small(g0):
        per_layer = [g0] + grads[1:]
        return _gather_exchange([_pack([jnp.stack([per_layer[l][n] for l in range(depth)]) for n in small])])

    top = (dy, None, None)
    grads = [None] * depth
    carried = []
    for l in reversed(range(depth)):
        top, grads[l], received, left, also_got = _layer_bwd(top, saved[l], memb, weights[l], alpha, carried,
                                                             gather_small if l == 0 else None)
        for n, parts in received.items():
            update(n, l + 1 if n in dict(carried) else l, parts)
        carried = left
    parts, = also_got
    da, last = _mm(top[0], top[1], name="mm_dx_in",
                   exchange=_scatter_exchange([gm.reshape(N_DEV, -1, gm.shape[-1]) for _, gm in carried]))
    for (n, _), blocks in zip(carried, last):
        update(n, 0, blocks)
    grad_x = _axpy(da, top[2], alpha=alpha, name="grad_x")[None]
    for n, cs in MATRICES:
        if cs:
            results[n] = [jnp.swapaxes(r, 1, 2) for r in results[n]]
    total = _sum_parts(parts, name="sum_small_grads")
    summed = dict(zip(small, _unpack(total, [wts[n].shape for n in REPLICATED] + [full_taps[n].shape for n in TAPS])))
    for n in TAPS:
        cols = wts[n].shape[-1]
        summed[n] = lax.dynamic_slice_in_dim(summed[n], my_index * cols, cols, axis=2)
    res = _sum_adamw(_pack([summed[n] for n in small])[None], *[_pack([src[n] for n in small])[None] for src in (wts, mom, var)],
                     layer=0, so_far=None, name="adamw_small")
    unpacked = [_unpack(r[0], [wts[n].shape for n in small]) for r in res]
    for i, n in enumerate(small):
        results[n] = [u[i] for u in unpacked]

    outs = [loss, grad_x]
    for kind in range(4):
        outs += [results[n][kind] for n in WEIGHTS]
    return tuple(outs)
```

```python
import functools

import jax
import jax.numpy as jnp
from jax import lax
from jax.experimental import pallas as pl
from jax.experimental.pallas import tpu as pltpu

F32 = jnp.float32
BF16 = jnp.bfloat16

N_DEV = 8
LANES = 128
PACK_W = 1024
PACK_ROW_ALIGN = 16
SB_HEAD_DIM = 64
MEM_HEAD_DIM = 256
LN_EPS = 1e-5
EXP_UNDERFLOW = 104.0
VMEM_LIMIT = 56 * 1024 * 1024

ADAM_LR = 0.001
ADAM_B1 = 0.9
ADAM_B2 = 0.999
ADAM_EPS = 1e-08
ADAM_WD = 0.01
ADAM_STEP = 10

IN_NAMES = ['x', 'mem', 'w_in', 'conv_w', 'conv_b', 'conv_ln_g', 'conv_ln_b', 'w_out', 'ln1_g', 'ln1_b',
            'mem_wq', 'mem_wk', 'mem_wv', 'mem_wo', 'ln2_g', 'ln2_b', 'ffn_up', 'ffn_conv_w', 'ffn_conv_b',
            'ffn_down', 'ln3_g', 'ln3_b']
WEIGHTS = IN_NAMES[2:]
MATRICES = [('w_in', True), ('w_out', False), ('mem_wq', False), ('mem_wk', False), ('mem_wv', False),
            ('mem_wo', False), ('ffn_up', True), ('ffn_down', False)]
TAPS = ['conv_w', 'ffn_conv_w']
REPLICATED = ['conv_b', 'conv_ln_g', 'conv_ln_b', 'ln1_g', 'ln1_b', 'ln2_g', 'ln2_b', 'ffn_conv_b', 'ln3_g', 'ln3_b']


def _pick(dim, pref, align=LANES):
    if dim <= pref:
        return dim
    fits = [t for t in range(align, pref + 1, align) if dim % t == 0]
    return fits[-1] if fits else dim


def _params(sem):
    return pltpu.CompilerParams(dimension_semantics=sem, vmem_limit_bytes=VMEM_LIMIT)


def _mm(a, b, *, ta=False, tb=False, out_dtype=F32, name, exchange=None):
    if ta:
        kdim, m = a.shape
        tm, tn, tk = _pick(m, 1408), _pick(b.shape[0 if tb else 1], 1024), _pick(kdim, 1024)
    else:
        m, kdim = a.shape
        tm, tn, tk = _pick(m, 512), _pick(b.shape[0 if tb else 1], 1536), _pick(kdim, 2816)
    if tb:
        n, kb = b.shape
    else:
        kb, n = b.shape
    assert kdim == kb, (a.shape, b.shape, ta, tb)
    grid = (m // tm, n // tn, kdim // tk)
    nk = grid[2]
    dims = (((0 if ta else 1,), (1 if tb else 0,)), ((), ()))
    n_ex_in = len(exchange.inputs) if exchange else 0
    n_ex_out = len(exchange.out_shapes) if exchange else 0

    def body(*refs):
        a_ref, b_ref = refs[:2]
        ex_in = refs[2:2 + n_ex_in]
        o_ref = refs[2 + n_ex_in]
        ex_out = refs[3 + n_ex_in:3 + n_ex_in + n_ex_out]
        scratch = refs[3 + n_ex_in + n_ex_out:]
        if nk > 1:
            acc_ref, scratch = scratch[0], scratch[1:]
        ids = [pl.program_id(d) for d in range(3)]
        if exchange:
            @pl.when((ids[0] == 0) & (ids[1] == 0) & (ids[2] == 0))
            def _():
                exchange.start(ex_in, ex_out, scratch)

        prod = lax.dot_general(a_ref[...].astype(BF16), b_ref[...].astype(BF16), dims,
                               preferred_element_type=F32)
        if nk == 1:
            o_ref[...] = prod.astype(out_dtype)
        else:
            k = ids[2]

            @pl.when(k == 0)
            def _():
                acc_ref[...] = prod

            @pl.when(k > 0)
            def _():
                acc_ref[...] += prod

            @pl.when(k == nk - 1)
            def _():
                o_ref[...] = acc_ref[...].astype(out_dtype)

        if exchange:
            @pl.when((ids[0] == grid[0] - 1) & (ids[1] == grid[1] - 1) & (ids[2] == grid[2] - 1))
            def _():
                exchange.finish(ex_in, ex_out, scratch)

    a_spec = pl.BlockSpec((tk, tm), lambda i, j, k: (k, i)) if ta else pl.BlockSpec((tm, tk), lambda i, j, k: (i, k))
    b_spec = pl.BlockSpec((tn, tk), lambda i, j, k: (j, k)) if tb else pl.BlockSpec((tk, tn), lambda i, j, k: (k, j))
    hbm = pl.BlockSpec(memory_space=pl.ANY)
    outs = pl.pallas_call(
        body, name=name,
        out_shape=(jax.ShapeDtypeStruct((m, n), out_dtype),) + tuple(exchange.out_shapes if exchange else ()),
        grid=grid,
        in_specs=[a_spec, b_spec] + [hbm] * n_ex_in,
        out_specs=(pl.BlockSpec((tm, tn), lambda i, j, k: (i, j)),) + (hbm,) * n_ex_out,
        scratch_shapes=([] if nk == 1 else [pltpu.VMEM((tm, tn), F32)]) + list(exchange.scratch_shapes if exchange else []),
        compiler_params=_params(("arbitrary",) * 3 if exchange else ("parallel", "parallel", "arbitrary")),
    )(a, b, *(exchange.inputs if exchange else ()))
    return (outs[0], list(outs[1:])) if exchange else outs[0]


def _ln_stats(r):
    mu = jnp.mean(r, axis=-1, keepdims=True)
    xc = r - mu
    var = jnp.mean(xc * xc, axis=-1, keepdims=True)
    rstd = lax.rsqrt(var + LN_EPS)
    return xc * rstd, rstd


def _ln_bwd(da, dres, r, g, *, alpha, name, ts=512):
    s, d = r.shape
    ts = _pick(s, ts)
    has_res = dres is not None

    def body(*refs):
        if has_res:
            da_ref, dres_ref, r_ref, g_ref, dr_ref, drb_ref, dg_ref, db_ref = refs
            dy = da_ref[...] + alpha * dres_ref[...]
        else:
            da_ref, r_ref, g_ref, dr_ref, drb_ref, dg_ref, db_ref = refs
            dy = da_ref[...]
        xhat, rstd = _ln_stats(r_ref[...])
        dxhat = dy * g_ref[...]
        m1 = jnp.mean(dxhat, axis=-1, keepdims=True)
        m2 = jnp.mean(dxhat * xhat, axis=-1, keepdims=True)
        dr = rstd * (dxhat - m1 - xhat * m2)
        dr_ref[...] = dr
        drb_ref[...] = dr.astype(BF16)

        @pl.when(pl.program_id(0) == 0)
        def _():
            dg_ref[...] = jnp.zeros_like(dg_ref)
            db_ref[...] = jnp.zeros_like(db_ref)

        dg_ref[...] += jnp.sum(dy * xhat, axis=0, keepdims=True)
        db_ref[...] += jnp.sum(dy, axis=0, keepdims=True)

    tok = pl.BlockSpec((ts, d), lambda i: (i, 0))
    vec = pl.BlockSpec((1, d), lambda i: (0, 0))
    ins = [da, dres, r, g.reshape(1, d)] if has_res else [da, r, g.reshape(1, d)]
    dr, drb, dg, db = pl.pallas_call(
        body, name=name,
        out_shape=(jax.ShapeDtypeStruct((s, d), F32), jax.ShapeDtypeStruct((s, d), BF16),
                   jax.ShapeDtypeStruct((1, d), F32), jax.ShapeDtypeStruct((1, d), F32)),
        grid=(s // ts,), in_specs=[tok] * (len(ins) - 1) + [vec], out_specs=(tok, tok, vec, vec),
        compiler_params=_params(("arbitrary",)),
    )(*ins)
    return dr, drb, dg.reshape(d), db.reshape(d)


def _mm_fused(a, b, fn, *, rows, vecs, out_dtypes, n_sums, tb=False, name, tm=512):
    parts = list(a) if isinstance(a, (tuple, list)) else [a]
    m = parts[0].shape[0]
    kdim = sum(p.shape[1] for p in parts)
    n = b.shape[0] if tb else b.shape[1]
    assert kdim == (b.shape[1] if tb else b.shape[0])
    tm = _pick(m, tm)
    tk = parts[0].shape[1] if len(parts) > 1 else _pick(kdim, 2816)
    nk = kdim // tk
    assert len(parts) in (1, nk) and all(p.shape == (m, tk) for p in parts[1:])
    dims = (((1,), (1 if tb else 0,)), ((), ()))
    n_parts, n_rows, n_vecs, n_outs = len(parts), len(rows), len(vecs), len(out_dtypes)

    def body(*refs):
        a_refs, b_ref = refs[:n_parts], refs[n_parts]
        refs = refs[n_parts - 1:]
        row_refs = refs[2:2 + n_rows]
        vec_refs = refs[2 + n_rows:2 + n_rows + n_vecs]
        out_refs = refs[2 + n_rows + n_vecs:2 + n_rows + n_vecs + n_outs]
        sum_refs = refs[2 + n_rows + n_vecs + n_outs:2 + n_rows + n_vecs + n_outs + n_sums]
        i, k = pl.program_id(0), pl.program_id(1)

        def product(j):
            a_ref = a_refs[j if n_parts > 1 else 0]
            return lax.dot_general(a_ref[...].astype(BF16), b_ref[...].astype(BF16), dims, preferred_element_type=F32)

        def finish(product):
            res = fn(product, *[r[...] for r in row_refs], *[v[...] for v in vec_refs])
            for o_ref, o in zip(out_refs, res[:n_outs]):
                o_ref[...] = o.astype(o_ref.dtype)
            if n_sums:
                @pl.when(i == 0)
                def _():
                    for s_ref in sum_refs:
                        s_ref[...] = jnp.zeros_like(s_ref)

                for s_ref, part in zip(sum_refs, res[n_outs:]):
                    s_ref[...] += part

        if nk == 1:
            finish(product(0))
        else:
            acc_ref = refs[-1]
            for j in range(nk):
                @pl.when(k == j)
                def _(j=j):
                    if j == 0:
                        acc_ref[...] = product(j)
                    elif j < nk - 1:
                        acc_ref[...] += product(j)
                    else:
                        finish(acc_ref[...] + product(j))

    tok = pl.BlockSpec((tm, n), lambda i, k: (i, 0))
    vec = pl.BlockSpec((1, n), lambda i, k: (0, 0))
    b_spec = pl.BlockSpec((n, tk), lambda i, k: (0, k)) if tb else pl.BlockSpec((tk, n), lambda i, k: (k, 0))
    a_spec = pl.BlockSpec((tm, tk), (lambda i, k: (i, 0)) if n_parts > 1 else (lambda i, k: (i, k)))
    return pl.pallas_call(
        body, name=name,
        out_shape=tuple(jax.ShapeDtypeStruct((m, n), dt) for dt in out_dtypes) + (jax.ShapeDtypeStruct((1, n), F32),) * n_sums,
        grid=(m // tm, nk),
        in_specs=[a_spec] * n_parts + [b_spec] + [tok] * n_rows + [vec] * n_vecs,
        out_specs=(tok,) * n_outs + (vec,) * n_sums,
        scratch_shapes=[] if nk == 1 else [pltpu.VMEM((tm, n), F32)],
        compiler_params=_params(("arbitrary", "arbitrary") if n_sums else ("parallel", "arbitrary")),
    )(*parts, b, *rows, *[v.reshape(1, n) for v in vecs])


def _mm_res_ln(a, b, x, g, beta, *, alpha, name):
    def fn(f, x_t, g_t, b_t):
        r = alpha * x_t + f
        xhat, _ = _ln_stats(r)
        y = xhat * g_t + b_t
        return r, y, y

    return _mm_fused(a, b, fn, rows=[x], vecs=[g, beta], out_dtypes=[F32, F32, BF16], n_sums=0, name=name)


def _mm_ln_bwd(a, b, dres, r, g, *, tb=False, alpha, name):
    def fn(f, dres_t, r_t, g_t):
        dy = f + alpha * dres_t
        xhat, rstd = _ln_stats(r_t)
        dxhat = dy * g_t
        m1 = jnp.mean(dxhat, axis=-1, keepdims=True)
        m2 = jnp.mean(dxhat * xhat, axis=-1, keepdims=True)
        dr = rstd * (dxhat - m1 - xhat * m2)
        return dr, dr, jnp.sum(dy * xhat, axis=0, keepdims=True), jnp.sum(dy, axis=0, keepdims=True)

    dr, drb, dg, db = _mm_fused(a, b, fn, rows=[dres, r], vecs=[g], out_dtypes=[F32, BF16], n_sums=2, tb=tb, name=name)
    return dr, drb, dg.reshape(-1), db.reshape(-1)


def _axpy(a, b, *, alpha, name, ts=512):
    s, d = a.shape
    ts = _pick(s, ts)

    def body(a_ref, b_ref, o_ref):
        o_ref[...] = a_ref[...] + alpha * b_ref[...]

    tok = pl.BlockSpec((ts, d), lambda i: (i, 0))
    return pl.pallas_call(body, name=name, out_shape=jax.ShapeDtypeStruct((s, d), F32), grid=(s // ts,),
                          in_specs=[tok, tok], out_specs=tok, compiler_params=_params(("parallel",)))(a, b)


def _loss_and_grad(y, target, *, name, ts=512):
    s, d = y.shape
    ts = _pick(s, ts)
    inv_d = 1.0 / d

    def body(y_ref, t_ref, dy_ref, loss_ref):
        e = y_ref[...] - t_ref[...]
        dy_ref[...] = e * inv_d

        @pl.when(pl.program_id(0) == 0)
        def _():
            loss_ref[...] = jnp.zeros_like(loss_ref)

        loss_ref[...] += jnp.sum(e * e, axis=0, keepdims=True) * (0.5 * inv_d)

    tok = pl.BlockSpec((ts, d), lambda i: (i, 0))
    vec = pl.BlockSpec((1, d), lambda i: (0, 0))
    dy, part = pl.pallas_call(
        body, name=name,
        out_shape=(jax.ShapeDtypeStruct((s, d), F32), jax.ShapeDtypeStruct((1, d), F32)),
        grid=(s // ts,), in_specs=[tok, tok], out_specs=(tok, vec),
        compiler_params=_params(("arbitrary",)),
    )(y, target)
    return dy, part


def _row_sum(v, *, name):
    def body(v_ref, o_ref):
        o_ref[...] = jnp.sum(v_ref[...], axis=1, keepdims=True)

    return pl.pallas_call(body, name=name, out_shape=jax.ShapeDtypeStruct((1, 1), F32))(v)


def _sigmoid(x):
    return 1.0 / (1.0 + jnp.exp(-x))


def _row_chunks(s, pref=512):
    c = _pick(s, pref, 8)
    return [(i * c, c) for i in range(s // c)]


def _glu_conv_fwd(proj, w, b, *, cc, name, exchange=None):
    s = proj.shape[0]
    kw = w.shape[0]
    pad = 32
    assert kw - 1 <= pad
    ncb = cc // LANES
    chunks = _row_chunks(s)

    def body(a_ref, g_ref, w_ref, b_ref, o_ref, u0_ref):
        u0_ref[pl.ds(0, pad), :] = jnp.zeros((pad, LANES), F32)
        for r0, rc in chunks:
            u0_ref[pl.ds(pad + r0, rc), :] = a_ref[pl.ds(r0, rc), :] * _sigmoid(g_ref[pl.ds(r0, rc), :])
        for r0, rc in chunks:
            acc = jnp.zeros((rc, LANES), F32) + b_ref[...]
            for k in range(kw):
                acc = acc + w_ref[pl.ds(k, 1), :] * u0_ref[pl.ds(pad + r0 - (kw - 1) + k, rc), :]
            o_ref[pl.ds(r0, rc), :] = acc

    body, ex_in, ex_out, ex_scratch = _hosted(exchange, (ncb,), body)
    if exchange is not None:
        body = functools.partial(body, n_own_in=4, n_own_out=1)
    hbm = pl.BlockSpec(memory_space=pl.ANY)
    outs = pl.pallas_call(
        body, name=name,
        out_shape=(jax.ShapeDtypeStruct((s, cc), F32), *ex_out),
        grid=(ncb,),
        in_specs=[pl.BlockSpec((s, LANES), lambda c: (0, c)), pl.BlockSpec((s, LANES), lambda c: (0, ncb + c)),
                  pl.BlockSpec((kw, LANES), lambda c: (0, c)), pl.BlockSpec((1, LANES), lambda c: (0, c))] + [hbm] * len(ex_in),
        out_specs=(pl.BlockSpec((s, LANES), lambda c: (0, c)), *([hbm] * len(ex_out))),
        scratch_shapes=[pltpu.VMEM((s + pad, LANES), F32)] + ex_scratch,
        compiler_params=_params(("arbitrary",) if exchange else ("parallel",)),
    )(proj, proj, w, b.reshape(1, cc), *ex_in)
    return outs[0], list(outs[1:])


def _glu_conv_bwd(du1, proj, w, *, cc, name, exchange=None):
    s = proj.shape[0]
    kw = w.shape[0]
    pad = 32
    ncb = cc // LANES
    chunks = _row_chunks(s)

    def body(d_ref, a_ref, g_ref, w_ref, da_ref, dg_ref, dw_ref, db_ref, u0_ref, dp_ref):
        u0_ref[pl.ds(0, pad), :] = jnp.zeros((pad, LANES), F32)
        dp_ref[pl.ds(s, pad), :] = jnp.zeros((pad, LANES), F32)
        for r0, rc in chunks:
            u0_ref[pl.ds(pad + r0, rc), :] = a_ref[pl.ds(r0, rc), :] * _sigmoid(g_ref[pl.ds(r0, rc), :])
            dp_ref[pl.ds(r0, rc), :] = d_ref[pl.ds(r0, rc), :]
        dws = [jnp.zeros((1, LANES), F32) for _ in range(kw)]
        dbs = jnp.zeros((1, LANES), F32)
        for r0, rc in chunks:
            d = dp_ref[pl.ds(r0, rc), :]
            dbs = dbs + jnp.sum(d, axis=0, keepdims=True)
            du0 = jnp.zeros((rc, LANES), F32)
            for k in range(kw):
                du0 = du0 + w_ref[pl.ds(k, 1), :] * dp_ref[pl.ds(r0 + (kw - 1) - k, rc), :]
                dws[k] = dws[k] + jnp.sum(d * u0_ref[pl.ds(pad + r0 - (kw - 1) + k, rc), :], axis=0, keepdims=True)
            sg = _sigmoid(g_ref[pl.ds(r0, rc), :])
            a = a_ref[pl.ds(r0, rc), :]
            da_ref[pl.ds(r0, rc), :] = (du0 * sg).astype(BF16)
            dg_ref[pl.ds(r0, rc), :] = (du0 * a * sg * (1.0 - sg)).astype(BF16)
        for k in range(kw):
            dw_ref[pl.ds(k, 1), :] = dws[k]
        db_ref[...] = dbs

    col = lambda off: pl.BlockSpec((s, LANES), lambda c: (0, off + c))
    body, ex_in, ex_out, ex_scratch = _hosted(exchange, (ncb,), body)
    if exchange is not None:
        body = functools.partial(body, n_own_in=4, n_own_out=4)
    hbm = pl.BlockSpec(memory_space=pl.ANY)
    outs = pl.pallas_call(
        body, name=name,
        out_shape=(jax.ShapeDtypeStruct((s, cc), BF16), jax.ShapeDtypeStruct((s, cc), BF16),
                   jax.ShapeDtypeStruct((kw, cc), F32), jax.ShapeDtypeStruct((1, cc), F32), *ex_out),
        grid=(ncb,),
        in_specs=[col(0), col(0), col(ncb), pl.BlockSpec((kw, LANES), lambda c: (0, c))] + [hbm] * len(ex_in),
        out_specs=(col(0), col(0), pl.BlockSpec((kw, LANES), lambda c: (0, c)), pl.BlockSpec((1, LANES), lambda c: (0, c)),
                   *([hbm] * len(ex_out))),
        scratch_shapes=[pltpu.VMEM((s + pad, LANES), F32), pltpu.VMEM((s + pad, LANES), F32)] + ex_scratch,
        compiler_params=_params(("arbitrary",) if exchange else ("parallel",)),
    )(du1, proj, proj, w, *ex_in)
    da, dg, dw, db = outs[:4]
    return da, dg, dw, db.reshape(cc), list(outs[4:])


def _cln_silu_fwd(u1, g, b, *, name, ts=512):
    s, cc = u1.shape
    ts = _pick(s, ts)

    def body(u_ref, g_ref, b_ref, o_ref):
        xhat, _ = _ln_stats(u_ref[...])
        y = xhat * g_ref[...] + b_ref[...]
        o_ref[...] = (y * _sigmoid(y)).astype(BF16)

    tok = pl.BlockSpec((ts, cc), lambda i: (i, 0))
    vec = pl.BlockSpec((1, cc), lambda i: (0, 0))
    return pl.pallas_call(body, name=name, out_shape=jax.ShapeDtypeStruct((s, cc), BF16), grid=(s // ts,),
                          in_specs=[tok, vec, vec], out_specs=tok,
                          compiler_params=_params(("parallel",)))(u1, g.reshape(1, cc), b.reshape(1, cc))


def _cln_silu_bwd(dua, u1, g, b, *, name, ts=512):
    s, cc = u1.shape
    ts = _pick(s, ts)

    def body(d_ref, u_ref, g_ref, b_ref, du_ref, dg_ref, db_ref):
        xhat, rstd = _ln_stats(u_ref[...])
        y = xhat * g_ref[...] + b_ref[...]
        sg = _sigmoid(y)
        dy = d_ref[...] * (sg * (1.0 + y * (1.0 - sg)))
        dxhat = dy * g_ref[...]
        m1 = jnp.mean(dxhat, axis=-1, keepdims=True)
        m2 = jnp.mean(dxhat * xhat, axis=-1, keepdims=True)
        du_ref[...] = rstd * (dxhat - m1 - xhat * m2)

        @pl.when(pl.program_id(0) == 0)
        def _():
            dg_ref[...] = jnp.zeros_like(dg_ref)
            db_ref[...] = jnp.zeros_like(db_ref)

        dg_ref[...] += jnp.sum(dy * xhat, axis=0, keepdims=True)
        db_ref[...] += jnp.sum(dy, axis=0, keepdims=True)

    tok = pl.BlockSpec((ts, cc), lambda i: (i, 0))
    vec = pl.BlockSpec((1, cc), lambda i: (0, 0))
    du1, dg, db = pl.pallas_call(
        body, name=name,
        out_shape=(jax.ShapeDtypeStruct((s, cc), F32), jax.ShapeDtypeStruct((1, cc), F32),
                   jax.ShapeDtypeStruct((1, cc), F32)),
        grid=(s // ts,), in_specs=[tok, tok, vec, vec], out_specs=(tok, vec, vec),
        compiler_params=_params(("arbitrary",)),
    )(dua, u1, g.reshape(1, cc), b.reshape(1, cc))
    return du1, dg.reshape(cc), db.reshape(cc)


def _softplus_parts(z):
    lk = jnp.minimum(-z, 0.0) - jnp.log1p(jnp.exp(-jnp.abs(z)))
    return lk, z + lk


def _stack_heads(x, hms):
    return jnp.concatenate([jnp.where(hm, x, 0.0) for hm in hms], axis=0).astype(BF16)


def _heads_side_by_side(x_st, tq):
    return jnp.concatenate([x_st[:tq], x_st[tq:]], axis=1)


def _sb_tile_masks(tq):
    row = lax.broadcasted_iota(jnp.int32, (2 * tq, tq), 0)
    col = lax.broadcasted_iota(jnp.int32, (2 * tq, tq), 1)
    vis = (col < jnp.where(row >= tq, row - tq, row)).astype(F32)
    krow, kcol = row[:tq], col[:tq]
    return vis, (krow > kcol).astype(BF16), (krow < kcol).astype(BF16)


_NT = (((1,), (1,)), ((), ()))
_TN = (((0,), (0,)), ((), ()))


def _head_masks():
    lane = lax.broadcasted_iota(jnp.int32, (1, LANES), 1)
    return [(lane >= SB_HEAD_DIM * h) & (lane < SB_HEAD_DIM * (h + 1)) for h in range(2)]


def _hosted(exchange, grid, body):
    if exchange is None:
        return body, [], [], []
    n_in, n_out, n_sems = len(exchange.inputs), len(exchange.out_shapes), len(exchange.scratch_shapes)

    def wrapped(*refs, n_own_in, n_own_out):
        own_in, ex_in = refs[:n_own_in], refs[n_own_in:n_own_in + n_in]
        rest = refs[n_own_in + n_in:]
        own_out, ex_out = rest[:n_own_out], rest[n_own_out:n_own_out + n_out]
        own_scratch, sems = rest[n_own_out + n_out:len(rest) - n_sems], rest[len(rest) - n_sems:]
        ids = [pl.program_id(d) for d in range(len(grid))]
        first = functools.reduce(lambda x, y: x & y, [i == 0 for i in ids])
        last = functools.reduce(lambda x, y: x & y, [i == g - 1 for i, g in zip(ids, grid)])

        @pl.when(first)
        def _():
            exchange.start(ex_in, ex_out, sems)

        body(*own_in, *own_out, *own_scratch)

        @pl.when(last)
        def _():
            exchange.finish(ex_in, ex_out, sems)

    return wrapped, list(exchange.inputs), list(exchange.out_shapes), list(exchange.scratch_shapes)


def _sb_fwd(proj, *, col0, n_pairs, name, tq=256, exchange=None):
    s = proj.shape[0]
    tq = _pick(s, tq)
    nq = s // tq
    cb0 = col0 // LANES
    scale = SB_HEAD_DIM ** -0.5

    def body(q_ref, k_ref, v_ref, vis_ref, after_ref, o_ref, t_ref, n_ref):
        i = pl.program_id(1)
        hms = _head_masks()
        q_st = _stack_heads(q_ref[...] * scale, hms)

        def tile(j, c, acc, masked):
            start = pl.multiple_of(j * tq, tq)
            kb = k_ref[pl.ds(start, tq), :].astype(BF16)
            v_st = _stack_heads(v_ref[pl.ds(start, tq), :], hms)
            z = lax.dot_general(q_st, kb, _NT, preferred_element_type=F32)
            lk, lb = _softplus_parts(z)
            if masked:
                lk = lk * vis_ref[...]
            later = jnp.dot(lk.astype(BF16), after_ref[...], preferred_element_type=F32)
            a = jnp.exp(lb + later + c)
            if masked:
                a = a * vis_ref[...]
            acc = acc + jnp.dot(_heads_side_by_side(a.astype(BF16), tq), v_st, preferred_element_type=F32)
            return c + jnp.sum(lk, axis=1, keepdims=True), acc

        def more(st):
            return jnp.logical_and(st[0] < i, jnp.max(st[1]) > -EXP_UNDERFLOW)

        def step(st):
            c, acc = tile(i - 1 - st[0], st[1], st[2], False)
            return st[0] + 1, c, acc

        c, acc = tile(i, jnp.zeros((2 * tq, 1), F32), jnp.zeros((tq, LANES), F32), True)
        n, c, acc = lax.while_loop(more, step, (jnp.int32(0), c, acc))
        o_ref[...] = acc.astype(BF16)
        t_ref[...] = jnp.where(hms[0], c[:tq], c[tq:])
        n_ref[...] = jnp.zeros((8, LANES), F32) + n.astype(F32)

    grid = (n_pairs, nq)
    body, ex_in, ex_out, ex_scratch = _hosted(exchange, grid, body)
    if exchange is not None:
        body = functools.partial(body, n_own_in=5, n_own_out=3)
    hbm = pl.BlockSpec(memory_space=pl.ANY)
    seq = lambda off: pl.BlockSpec((s, LANES), lambda p, i: (0, cb0 + off + p))
    whole = lambda rows: pl.BlockSpec((rows, tq), lambda p, i: (0, 0))
    vis, m_after, _ = _sb_tile_masks(tq)
    outs = pl.pallas_call(
        body, name=name,
        out_shape=(jax.ShapeDtypeStruct((s, n_pairs * LANES), BF16), jax.ShapeDtypeStruct((n_pairs, s, LANES), F32),
                   jax.ShapeDtypeStruct((n_pairs, nq * 8, LANES), F32), *ex_out),
        grid=grid,
        in_specs=[pl.BlockSpec((tq, LANES), lambda p, i: (i, cb0 + p)), seq(n_pairs), seq(2 * n_pairs),
                  whole(2 * tq), whole(tq)] + [hbm] * len(ex_in),
        out_specs=(pl.BlockSpec((tq, LANES), lambda p, i: (i, p)), pl.BlockSpec((None, tq, LANES), lambda p, i: (p, i, 0)),
                   pl.BlockSpec((None, 8, LANES), lambda p, i: (p, i, 0)), *([hbm] * len(ex_out))),
        scratch_shapes=ex_scratch,
        compiler_params=_params(("arbitrary", "arbitrary") if exchange else ("parallel", "arbitrary")),
    )(proj, proj, proj, vis, m_after, *ex_in)
    return outs[0], outs[1], outs[2], list(outs[3:])


def _sb_bwd(proj, t_sum, n_walked, dua, *, col0, n_pairs, do_col0, name, tq=256, exchange=None):
    s = proj.shape[0]
    tq = _pick(s, tq)
    cb0 = col0 // LANES
    dcb0 = do_col0 // LANES
    scale = SB_HEAD_DIM ** -0.5

    def body(q_ref, k_ref, v_ref, t_ref, n_ref, do_ref, vis_ref, after_ref, before_ref, dq_ref, dk_ref, dv_ref):
        i = pl.program_id(1)

        @pl.when(i == 0)
        def _():
            dk_ref[...] = jnp.zeros_like(dk_ref)
            dv_ref[...] = jnp.zeros_like(dv_ref)

        hms = _head_masks()
        q_st = _stack_heads(q_ref[...] * scale, hms)
        do_st = _stack_heads(do_ref[...], hms)
        t_st = jnp.concatenate([t_ref[:, SB_HEAD_DIM * h:SB_HEAD_DIM * h + 1] for h in range(2)], axis=0)
        first = i - jnp.max(n_ref[...]).astype(jnp.int32)

        def tile(j, p_sum, g_sum, dq, masked):
            start = pl.multiple_of(j * tq, tq)
            k = k_ref[pl.ds(start, tq), :]
            z = lax.dot_general(q_st, k.astype(BF16), _NT, preferred_element_type=F32)
            lk_raw, lb = _softplus_parts(z)
            lk = lk_raw * vis_ref[...] if masked else lk_raw
            p_next = p_sum + jnp.sum(lk, axis=1, keepdims=True)
            later = jnp.dot(lk.astype(BF16), after_ref[...], preferred_element_type=F32)
            a = jnp.exp(lb + (t_st - p_next) + later)
            if masked:
                a = a * vis_ref[...]
            da = lax.dot_general(do_st, v_ref[pl.ds(start, tq), :].astype(BF16), _NT, preferred_element_type=F32)
            g = a * da
            g_before = g_sum + jnp.dot(g.astype(BF16), before_ref[...], preferred_element_type=F32)
            dz = g * jnp.exp(lk_raw) - g_before * jnp.exp(lb)
            if masked:
                dz = dz * vis_ref[...]
            dzb = dz.astype(BF16)
            dv_ref[pl.ds(start, tq), :] += lax.dot_general(a.astype(BF16), do_st, _TN, preferred_element_type=F32)
            dk_ref[pl.ds(start, tq), :] += lax.dot_general(dzb, q_st, _TN, preferred_element_type=F32)
            dq = dq + jnp.dot(_heads_side_by_side(dzb, tq), _stack_heads(k, hms), preferred_element_type=F32)
            return p_next, g_sum + jnp.sum(g, axis=1, keepdims=True), dq

        zero = jnp.zeros((2 * tq, 1), F32)
        st = lax.fori_loop(first, i, lambda j, st: tile(j, *st, False), (zero, zero, jnp.zeros((tq, LANES), F32)))
        dq_ref[...] = tile(i, *st, True)[2] * scale

    grid = (n_pairs, s // tq)
    body, ex_in, ex_out, ex_scratch = _hosted(exchange, grid, body)
    if exchange is not None:
        body = functools.partial(body, n_own_in=9, n_own_out=3)
    hbm = pl.BlockSpec(memory_space=pl.ANY)
    seq = lambda off: pl.BlockSpec((s, LANES), lambda p, i: (0, cb0 + off + p))
    whole = lambda rows: pl.BlockSpec((rows, tq), lambda p, i: (0, 0))
    out = jax.ShapeDtypeStruct((s, n_pairs * LANES), F32)
    res = pl.BlockSpec((s, LANES), lambda p, i: (0, p))
    outs = pl.pallas_call(
        body, name=name,
        out_shape=(out, out, out, *ex_out),
        grid=grid,
        in_specs=[pl.BlockSpec((tq, LANES), lambda p, i: (i, cb0 + p)), seq(n_pairs), seq(2 * n_pairs),
                  pl.BlockSpec((None, tq, LANES), lambda p, i: (p, i, 0)),
                  pl.BlockSpec((None, 8, LANES), lambda p, i: (p, i, 0)),
                  pl.BlockSpec((tq, LANES), lambda p, i: (i, dcb0 + p)),
                  whole(2 * tq), whole(tq), whole(tq)] + [hbm] * len(ex_in),
        out_specs=(pl.BlockSpec((tq, LANES), lambda p, i: (i, p)), res, res, *([hbm] * len(ex_out))),
        scratch_shapes=ex_scratch,
        compiler_params=_params(("arbitrary", "arbitrary")),
    )(proj, proj, proj, t_sum, n_walked, dua, *_sb_tile_masks(tq), *ex_in)
    return outs[0], outs[1], outs[2], list(outs[3:])


def _mem_attn_fwd(qm, km, vm, *, name, tq=512):
    s, d = qm.shape
    heads = d // MEM_HEAD_DIM
    mlen = km.shape[0]
    tq = _pick(s, tq)
    scale = MEM_HEAD_DIM ** -0.5

    def body(q_ref, k_ref, v_ref, o_ref):
        for h in range(heads):
            sl = slice(h * MEM_HEAD_DIM, (h + 1) * MEM_HEAD_DIM)
            q = (q_ref[:, sl] * scale).astype(BF16)
            sc = lax.dot_general(q, k_ref[:, sl].astype(BF16), _NT, preferred_element_type=F32)
            e = jnp.exp(sc - jnp.max(sc, axis=1, keepdims=True))
            p = e / jnp.sum(e, axis=1, keepdims=True)
            o_ref[:, sl] = jnp.dot(p.astype(BF16), v_ref[:, sl].astype(BF16), preferred_element_type=F32).astype(BF16)

    tok = pl.BlockSpec((tq, d), lambda i: (i, 0))
    kv = pl.BlockSpec((mlen, d), lambda i: (0, 0))
    return pl.pallas_call(body, name=name, out_shape=jax.ShapeDtypeStruct((s, d), BF16), grid=(s // tq,),
                          in_specs=[tok, kv, kv], out_specs=tok, compiler_params=_params(("parallel",)))(qm, km, vm)


def _mem_attn_bwd(qm, km, vm, do, *, name, tq=512):
    s, d = qm.shape
    heads = d // MEM_HEAD_DIM
    mlen = km.shape[0]
    tq = _pick(s, tq)
    scale = MEM_HEAD_DIM ** -0.5

    def body(q_ref, k_ref, v_ref, do_ref, dq_ref, dk_ref, dv_ref):
        @pl.when(pl.program_id(0) == 0)
        def _():
            dk_ref[...] = jnp.zeros_like(dk_ref)
            dv_ref[...] = jnp.zeros_like(dv_ref)

        for h in range(heads):
            sl = slice(h * MEM_HEAD_DIM, (h + 1) * MEM_HEAD_DIM)
            q = (q_ref[:, sl] * scale).astype(BF16)
            k = k_ref[:, sl].astype(BF16)
            v = v_ref[:, sl].astype(BF16)
            sc = lax.dot_general(q, k, _NT, preferred_element_type=F32)
            e = jnp.exp(sc - jnp.max(sc, axis=1, keepdims=True))
            p = e / jnp.sum(e, axis=1, keepdims=True)
            dob = do_ref[:, sl].astype(BF16)
            dv_ref[:, sl] += lax.dot_general(p.astype(BF16), dob, _TN, preferred_element_type=F32)
            dp = lax.dot_general(dob, v, _NT, preferred_element_type=F32)
            ds = (p * (dp - jnp.sum(dp * p, axis=1, keepdims=True))).astype(BF16)
            dq_ref[:, sl] = (jnp.dot(ds, k, preferred_element_type=F32) * scale).astype(BF16)
            dk_ref[:, sl] += lax.dot_general(ds, q, _TN, preferred_element_type=F32)

    tok = pl.BlockSpec((tq, d), lambda i: (i, 0))
    kv = pl.BlockSpec((mlen, d), lambda i: (0, 0))
    return pl.pallas_call(
        body, name=name,
        out_shape=(jax.ShapeDtypeStruct((s, d), BF16), jax.ShapeDtypeStruct((mlen, d), F32),
                   jax.ShapeDtypeStruct((mlen, d), F32)),
        grid=(s // tq,), in_specs=[tok, kv, kv, tok], out_specs=(tok, kv, kv),
        compiler_params=_params(("arbitrary",)),
    )(qm, km, vm, do)


def _ffn_act_fwd(up, w, b, *, name, exchange=None):
    s, two_f = up.shape
    ff = two_f // 2
    nfb = ff // LANES
    kw = w.shape[0]
    pad = 8
    chunks = _row_chunks(s)

    def body(v_ref, g_ref, wv_ref, wg_ref, bv_ref, bg_ref, o_ref, vp_ref, gp_ref):
        vp_ref[pl.ds(0, pad), :] = jnp.zeros((pad, LANES), F32)
        gp_ref[pl.ds(0, pad), :] = jnp.zeros((pad, LANES), F32)
        for r0, rc in chunks:
            vp_ref[pl.ds(pad + r0, rc), :] = v_ref[pl.ds(r0, rc), :].astype(F32)
            gp_ref[pl.ds(pad + r0, rc), :] = g_ref[pl.ds(r0, rc), :].astype(F32)
        for r0, rc in chunks:
            vc = jnp.zeros((rc, LANES), F32) + bv_ref[...]
            gc = jnp.zeros((rc, LANES), F32) + bg_ref[...]
            for k in range(kw):
                off = pad + r0 - (kw - 1) + k
                vc = vc + wv_ref[pl.ds(k, 1), :] * vp_ref[pl.ds(off, rc), :]
                gc = gc + wg_ref[pl.ds(k, 1), :] * gp_ref[pl.ds(off, rc), :]
            o_ref[pl.ds(r0, rc), :] = (gc * _sigmoid(gc) * vc).astype(BF16)

    col = lambda off: pl.BlockSpec((s, LANES), lambda c: (0, off + c))
    tap = lambda off: pl.BlockSpec((kw, LANES), lambda c: (0, off + c))
    vec = lambda off: pl.BlockSpec((1, LANES), lambda c: (0, off + c))
    body, ex_in, ex_out, ex_scratch = _hosted(exchange, (nfb,), body)
    if exchange is not None:
        body = functools.partial(body, n_own_in=6, n_own_out=1)
    hbm = pl.BlockSpec(memory_space=pl.ANY)
    outs = pl.pallas_call(
        body, name=name, out_shape=(jax.ShapeDtypeStruct((s, ff), BF16), *ex_out), grid=(nfb,),
        in_specs=[col(0), col(nfb), tap(0), tap(nfb), vec(0), vec(nfb)] + [hbm] * len(ex_in),
        out_specs=(col(0), *([hbm] * len(ex_out))),
        scratch_shapes=[pltpu.VMEM((s + pad, LANES), F32), pltpu.VMEM((s + pad, LANES), F32)] + ex_scratch,
        compiler_params=_params(("arbitrary",) if exchange else ("parallel",)),
    )(up, up, w, w, b.reshape(1, two_f), b.reshape(1, two_f), *ex_in)
    return outs[0], list(outs[1:])


def _ffn_act_bwd(up, dact, w, b, *, name, exchange=None):
    s, two_f = up.shape
    ff = two_f // 2
    nfb = ff // LANES
    kw = w.shape[0]
    pad = 8
    chunks = _row_chunks(s)

    def body(v_ref, g_ref, d_ref, wv_ref, wg_ref, bv_ref, bg_ref, dv_ref, dg_ref, dwv_ref, dwg_ref, dbv_ref, dbg_ref,
             vp_ref, gp_ref, dvc_ref, dgc_ref):
        vp_ref[pl.ds(0, pad), :] = jnp.zeros((pad, LANES), F32)
        gp_ref[pl.ds(0, pad), :] = jnp.zeros((pad, LANES), F32)
        dvc_ref[pl.ds(s, pad), :] = jnp.zeros((pad, LANES), F32)
        dgc_ref[pl.ds(s, pad), :] = jnp.zeros((pad, LANES), F32)
        for r0, rc in chunks:
            vp_ref[pl.ds(pad + r0, rc), :] = v_ref[pl.ds(r0, rc), :].astype(F32)
            gp_ref[pl.ds(pad + r0, rc), :] = g_ref[pl.ds(r0, rc), :].astype(F32)
        dwv = [jnp.zeros((1, LANES), F32) for _ in range(kw)]
        dwg = [jnp.zeros((1, LANES), F32) for _ in range(kw)]
        dbv = jnp.zeros((1, LANES), F32)
        dbg = jnp.zeros((1, LANES), F32)
        for r0, rc in chunks:
            vc = jnp.zeros((rc, LANES), F32) + bv_ref[...]
            gc = jnp.zeros((rc, LANES), F32) + bg_ref[...]
            for k in range(kw):
                off = pad + r0 - (kw - 1) + k
                vc = vc + wv_ref[pl.ds(k, 1), :] * vp_ref[pl.ds(off, rc), :]
                gc = gc + wg_ref[pl.ds(k, 1), :] * gp_ref[pl.ds(off, rc), :]
            sg = _sigmoid(gc)
            d = d_ref[pl.ds(r0, rc), :].astype(F32)
            dvc = d * (gc * sg)
            dgc = d * vc * (sg * (1.0 + gc * (1.0 - sg)))
            dvc_ref[pl.ds(r0, rc), :] = dvc
            dgc_ref[pl.ds(r0, rc), :] = dgc
            dbv = dbv + jnp.sum(dvc, axis=0, keepdims=True)
            dbg = dbg + jnp.sum(dgc, axis=0, keepdims=True)
            for k in range(kw):
                off = pad + r0 - (kw - 1) + k
                dwv[k] = dwv[k] + jnp.sum(dvc * vp_ref[pl.ds(off, rc), :], axis=0, keepdims=True)
                dwg[k] = dwg[k] + jnp.sum(dgc * gp_ref[pl.ds(off, rc), :], axis=0, keepdims=True)
        for r0, rc in chunks:
            dv = jnp.zeros((rc, LANES), F32)
            dg = jnp.zeros((rc, LANES), F32)
            for k in range(kw):
                off = r0 + (kw - 1) - k
                dv = dv + wv_ref[pl.ds(k, 1), :] * dvc_ref[pl.ds(off, rc), :]
                dg = dg + wg_ref[pl.ds(k, 1), :] * dgc_ref[pl.ds(off, rc), :]
            dv_ref[pl.ds(r0, rc), :] = dv.astype(BF16)
            dg_ref[pl.ds(r0, rc), :] = dg.astype(BF16)
        for k in range(kw):
            dwv_ref[pl.ds(k, 1), :] = dwv[k]
            dwg_ref[pl.ds(k, 1), :] = dwg[k]
        dbv_ref[...] = dbv
        dbg_ref[...] = dbg

    col = lambda off: pl.BlockSpec((s, LANES), lambda c: (0, off + c))
    tap = lambda off: pl.BlockSpec((kw, LANES), lambda c: (0, off + c))
    vec = lambda off: pl.BlockSpec((1, LANES), lambda c: (0, off + c))
    big = lambda: pltpu.VMEM((s + pad, LANES), F32)
    body, ex_in, ex_out, ex_scratch = _hosted(exchange, (nfb,), body)
    if exchange is not None:
        body = functools.partial(body, n_own_in=7, n_own_out=6)
    hbm = pl.BlockSpec(memory_space=pl.ANY)
    outs = pl.pallas_call(
        body, name=name,
        out_shape=(jax.ShapeDtypeStruct((s, ff), BF16), jax.ShapeDtypeStruct((s, ff), BF16),
                   jax.ShapeDtypeStruct((kw, ff), F32), jax.ShapeDtypeStruct((kw, ff), F32),
                   jax.ShapeDtypeStruct((1, ff), F32), jax.ShapeDtypeStruct((1, ff), F32), *ex_out),
        grid=(nfb,),
        in_specs=[col(0), col(nfb), col(0), tap(0), tap(nfb), vec(0), vec(nfb)] + [hbm] * len(ex_in),
        out_specs=(col(0), col(0), tap(0), tap(0), vec(0), vec(0), *([hbm] * len(ex_out))),
        scratch_shapes=[big(), big(), big(), big()] + ex_scratch,
        compiler_params=_params(("arbitrary",) if exchange else ("parallel",)),
    )(up, up, dact, w, w, b.reshape(1, two_f), b.reshape(1, two_f), *ex_in)
    dv, dg, dwv, dwg, dbv, dbg = outs[:6]
    return (dv, dg), jnp.concatenate([dwv, dwg], axis=1), jnp.concatenate([dbv, dbg], axis=1).reshape(two_f), list(outs[6:])


def _sum_parts(parts, *, name, tr=256):
    n_parts, rows, cols = parts.shape
    tr = _pick(rows, tr, 16)

    def body(p_ref, o_ref):
        g = p_ref[0].astype(F32)
        for k in range(1, n_parts):
            g = g + p_ref[k].astype(F32)
        o_ref[...] = g

    return pl.pallas_call(
        body, name=name, out_shape=jax.ShapeDtypeStruct((rows, cols), F32), grid=(rows // tr,),
        in_specs=[pl.BlockSpec((n_parts, tr, cols), lambda i: (0, i, 0))],
        out_specs=pl.BlockSpec((tr, cols), lambda i: (i, 0)), compiler_params=_params(("parallel",)),
    )(parts)


def _sum_adamw(parts, w, m, v, *, layer, so_far, name, tr=256):
    n_parts, rows, cols = parts.shape
    depth = w.shape[0]
    tr = _pick(rows, tr, 16)
    c1 = 1.0 / (1.0 - ADAM_B1 ** ADAM_STEP)
    c2 = 1.0 / (1.0 - ADAM_B2 ** ADAM_STEP)

    def body(p_ref, w_ref, m_ref, v_ref, *rest):
        g_ref, d_ref, nm_ref, nv_ref = rest[-4:]
        g = p_ref[0].astype(F32)
        for k in range(1, n_parts):
            g = g + p_ref[k].astype(F32)
        nm = ADAM_B1 * m_ref[...] + (1.0 - ADAM_B1) * g
        nv = ADAM_B2 * v_ref[...] + (1.0 - ADAM_B2) * (g * g)
        g_ref[...] = g
        nm_ref[...] = nm
        nv_ref[...] = nv
        d_ref[...] = -ADAM_LR * ((nm * c1) / (jnp.sqrt(nv * c2) + ADAM_EPS) + ADAM_WD * w_ref[...])

    blk = pl.BlockSpec((None, tr, cols), lambda i: (layer, i, 0))
    out = jax.ShapeDtypeStruct((depth, rows, cols), F32)
    kept = list(so_far) if so_far is not None else []
    return pl.pallas_call(
        body, name=name, out_shape=(out, out, out, out), grid=(rows // tr,),
        in_specs=[pl.BlockSpec((n_parts, tr, cols), lambda i: (0, i, 0)), blk, blk, blk]
        + [pl.BlockSpec(memory_space=pl.ANY)] * len(kept),
        out_specs=(blk, blk, blk, blk),
        input_output_aliases={4 + k: k for k in range(len(kept))},
        compiler_params=_params(("parallel",)),
    )(parts, w, m, v, *kept)


def _mesh_pos():
    return lax.axis_index("x"), lax.axis_index("y"), lax.axis_index("c")


def _flip(pos, k):
    x, y, c = pos
    return (1 - x if k & 4 else x, 1 - y if k & 2 else y, 1 - c if k & 1 else c)


def _dev_index(pos):
    return 4 * pos[0] + 2 * pos[1] + pos[2]


N_PEERS = N_DEV - 1


class _Exchange:
    def __init__(self, inputs, out_shapes, start, finish, scratch_shapes=None):
        n = len(inputs)
        self.inputs, self.out_shapes, self.start, self.finish = list(inputs), list(out_shapes), start, finish
        self.scratch_shapes = scratch_shapes or [pltpu.SemaphoreType.DMA((n * N_PEERS,)),
                                                 pltpu.SemaphoreType.DMA((n * N_PEERS,)), pltpu.SemaphoreType.DMA((n,))]


def _both(ex1, ex2):
    if ex1 is None or ex2 is None:
        return ex1 or ex2
    n_in, n_out, n_sems = len(ex1.inputs), len(ex1.out_shapes), len(ex1.scratch_shapes)

    def halves(ins, outs, sems):
        return (ins[:n_in], outs[:n_out], sems[:n_sems]), (ins[n_in:], outs[n_out:], sems[n_sems:])

    def start(ins, outs, sems):
        a, b = halves(ins, outs, sems)
        ex1.start(*a)
        ex2.start(*b)

    def finish(ins, outs, sems):
        a, b = halves(ins, outs, sems)
        ex1.finish(*a)
        ex2.finish(*b)

    return _Exchange(ex1.inputs + ex2.inputs, ex1.out_shapes + ex2.out_shapes, start, finish,
                     ex1.scratch_shapes + ex2.scratch_shapes)


def _gather_exchange(xs):
    n = len(xs)

    def plan(x_refs, out_refs, sems):
        send_sems, recv_sems, local_sems = sems
        me = _mesh_pos()
        sibling = _flip(me, 1)
        chips = [_flip(me, 4), _flip(me, 2), _flip(me, 6)]

        def copy(a, k, block, to, from_input=False):
            slot = out_refs[a].at[_dev_index(block)]
            return pltpu.make_async_remote_copy(
                src_ref=x_refs[a] if from_input else slot, dst_ref=slot,
                send_sem=send_sems.at[a * N_PEERS + k], recv_sem=recv_sems.at[a * N_PEERS + k],
                device_id=to, device_id_type=pl.DeviceIdType.MESH)

        mine = [pltpu.make_async_copy(x_refs[a], out_refs[a].at[_dev_index(me)], local_sems.at[a]) for a in range(n)]
        first = [copy(a, 0, me, sibling, True) for a in range(n)]
        first += [copy(a, 1 + j, me, chip, True) for j, chip in enumerate(chips) for a in range(n)]
        return me, sibling, chips, copy, mine, first

    def start(x_refs, out_refs, sems):
        _, _, _, _, mine, first = plan(x_refs, out_refs, sems)
        for cp in mine + first:
            cp.start()

    def finish(x_refs, out_refs, sems):
        me, sibling, chips, copy, mine, first = plan(x_refs, out_refs, sems)
        passed = []
        for j, chip in enumerate(chips):
            for a in range(n):
                copy(a, 1 + j, chip, me).wait_recv()
                passed.append(copy(a, 4 + j, chip, sibling))
                passed[-1].start()
        for a in range(n):
            copy(a, 0, sibling, me).wait_recv()
        for j, chip in enumerate(chips):
            for a in range(n):
                copy(a, 4 + j, _flip(chip, 1), me).wait_recv()
        for cp in first + passed:
            cp.wait_send()
        for cp in mine:
            cp.wait()

    return _Exchange(xs, [jax.ShapeDtypeStruct((N_DEV,) + x.shape, x.dtype) for x in xs], start, finish)


def _scatter_exchange(xs):
    n = len(xs)

    def plan(x_refs, out_refs, sems):
        send_sems, recv_sems, local_sems = sems
        me = _mesh_pos()
        my_slot = _dev_index(me)

        def copy(a, k):
            peer = _flip(me, k)
            return pltpu.make_async_remote_copy(
                src_ref=x_refs[a].at[_dev_index(peer)], dst_ref=out_refs[a].at[my_slot],
                send_sem=send_sems.at[a * N_PEERS + k - 1], recv_sem=recv_sems.at[a * N_PEERS + k - 1],
                device_id=peer, device_id_type=pl.DeviceIdType.MESH)

        mine = [pltpu.make_async_copy(x_refs[a].at[my_slot], out_refs[a].at[my_slot], local_sems.at[a]) for a in range(n)]
        return mine, [copy(a, k) for k in range(1, N_DEV) for a in range(n)]

    def start(x_refs, out_refs, sems):
        mine, copies = plan(x_refs, out_refs, sems)
        for cp in mine + copies:
            cp.start()

    def finish(x_refs, out_refs, sems):
        mine, copies = plan(x_refs, out_refs, sems)
        for cp in copies:
            cp.wait_recv()
        for cp in copies:
            cp.wait_send()
        for cp in mine:
            cp.wait()

    return _Exchange(xs, [jax.ShapeDtypeStruct(x.shape, x.dtype) for x in xs], start, finish)


def _run_exchanges(exchanges, *, name):
    counts = [len(ex.inputs) for ex in exchanges]
    n = sum(counts)

    def body(*refs):
        offsets = [sum(counts[:e]) for e in range(len(exchanges))]
        views = [(refs[o:o + c], refs[n + o:n + o + c], refs[2 * n + 3 * e:2 * n + 3 * e + 3])
                 for e, (o, c) in enumerate(zip(offsets, counts))]
        for ex, view in zip(exchanges, views):
            ex.start(*view)
        for ex, view in zip(exchanges, views):
            ex.finish(*view)

    hbm = pl.BlockSpec(memory_space=pl.ANY)
    outs = pl.pallas_call(
        body, name=name, out_shape=tuple(s for ex in exchanges for s in ex.out_shapes), in_specs=[hbm] * n,
        out_specs=tuple([hbm] * n), scratch_shapes=[s for ex in exchanges for s in ex.scratch_shapes],
    )(*[x for ex in exchanges for x in ex.inputs])
    return [list(outs[sum(counts[:e]):sum(counts[:e + 1])]) for e in range(len(exchanges))]


def _pack(arrays):
    flat = jnp.concatenate([a.reshape(-1) for a in arrays])
    n = flat.shape[0]
    tile = PACK_W * PACK_ROW_ALIGN
    total = -(-n // tile) * tile
    return jnp.pad(flat, (0, total - n)).reshape(total // PACK_W, PACK_W)


def _unpack(buf, shapes):
    lead = buf.shape[:-2]
    flat = buf.reshape(lead + (-1,))
    out, off = [], 0
    for shp in shapes:
        n = 1
        for dim in shp:
            n *= dim
        out.append(flat[..., off:off + n].reshape(lead + tuple(shp)))
        off += n
    return out


def _join_columns(blocks):
    return jnp.moveaxis(blocks, 0, 2).reshape(blocks.shape[1], blocks.shape[2], -1)


def _mm_hosting(a, b, exchange, **kw):
    if exchange is None:
        return _mm(a, b, **kw), []
    return _mm(a, b, exchange=exchange, **kw)


GATHERED_BY_ATTENTION = ['w_out', 'mem_wq', 'mem_wk', 'mem_wv', 'mem_wo', 'ffn_up']
GATHERED_BY_CONV = ['ffn_down']


def _as_matrix(blocks):
    return blocks.reshape(-1, blocks.shape[-1])


def _layer_fwd(x, xb, memb, w, alpha, shards, next_w_in):
    w = dict(w)
    cc = w['conv_w'].shape[1]
    n_pairs = (N_DEV * shards['w_out'].shape[0] - cc) // LANES

    proj = _mm(xb, w['w_in'], tb=True, name="mm_proj")
    u1, got = _glu_conv_fwd(proj, w['conv_w'], w['conv_b'], cc=cc, name="glu_conv_fwd",
                            exchange=_gather_exchange([shards[n] for n in GATHERED_BY_CONV]))
    w.update({n: _as_matrix(f) for n, f in zip(GATHERED_BY_CONV, got)})
    u = _cln_silu_fwd(u1, w['conv_ln_g'], w['conv_ln_b'], name="cln_silu_fwd")
    att, t_sum, n_walked, got = _sb_fwd(proj, col0=2 * cc, n_pairs=n_pairs, name="sb_fwd",
                                        exchange=_gather_exchange([shards[n] for n in GATHERED_BY_ATTENTION]))
    w.update({n: _as_matrix(f) for n, f in zip(GATHERED_BY_ATTENTION, got)})
    ua = jnp.concatenate([u, att], axis=1)
    r1, x1, x1b = _mm_res_ln(ua, w['w_out'], x, w['ln1_g'], w['ln1_b'], alpha=alpha, name="mm_mix_ln")
    qm = _mm(x1b, w['mem_wq'], name="mm_memq")
    km = _mm(memb, w['mem_wk'], name="mm_memkv")
    vm = _mm(memb, w['mem_wv'], name="mm_memkv")
    o = _mem_attn_fwd(qm, km, vm, name="mem_attn_fwd")
    r2, x2, x2b = _mm_res_ln(o, w['mem_wo'], x1, w['ln2_g'], w['ln2_b'], alpha=alpha, name="mm_mix_ln")
    up = _mm(x2b, w['ffn_up'], tb=True, out_dtype=BF16, name="mm_up")
    act, got = _ffn_act_fwd(up, w['ffn_conv_w'], w['ffn_conv_b'], name="ffn_act_fwd",
                            exchange=_gather_exchange([next_w_in]) if next_w_in is not None else None)
    r3, x3, x3b = _mm_res_ln(act, w['ffn_down'], x2, w['ln3_g'], w['ln3_b'], alpha=alpha, name="mm_down_ln")
    saved = dict(xb=xb, proj=proj, u1=u1, t_sum=t_sum, n_walked=n_walked, ua=ua, r1=r1, x1b=x1b, qm=qm, km=km, vm=vm,
                 o=o, r2=r2, x2b=x2b, up=up, act=act, r3=r3)
    return x3, x3b, saved, w, _as_matrix(got[0]) if got else None


def _layer_bwd(top, sv, memb, w, alpha, carried, also_during_attention=None):
    g, received = {}, {}
    cc = w['conv_w'].shape[1]
    n_pairs = (w['w_out'].shape[0] - cc) // LANES

    def sending(sends):
        ex = _scatter_exchange([gm.reshape(N_DEV, -1, gm.shape[-1]) for _, gm in sends]) if sends else None
        return ex, lambda got: received.update({key: blocks for (key, _), blocks in zip(sends, got)})

    if top[1] is None:
        dr3, dr3b, g['ln3_g'], g['ln3_b'] = _ln_bwd(top[0], None, sv['r3'], w['ln3_g'], alpha=alpha, name="ln_bwd")
    else:
        dr3, dr3b, g['ln3_g'], g['ln3_b'] = _mm_ln_bwd(*top, sv['r3'], w['ln3_g'], alpha=alpha, name="mm_dx_in_ln")
    g_down = _mm(sv['act'], dr3b, ta=True, out_dtype=BF16, name="mm_dw_down")
    dact = _mm(dr3b, w['ffn_down'], tb=True, out_dtype=BF16, name="mm_dact")
    ex, file = sending(carried + [('ffn_down', g_down)])
    dup, g['ffn_conv_w'], g['ffn_conv_b'], got = _ffn_act_bwd(sv['up'], dact, w['ffn_conv_w'], w['ffn_conv_b'],
                                                                name="ffn_act_bwd", exchange=ex)
    file(got)
    g_up = jnp.concatenate([_mm(half, sv['x2b'], ta=True, out_dtype=BF16, name="mm_dw_up") for half in dup], axis=0)
    dr2, dr2b, g['ln2_g'], g['ln2_b'] = _mm_ln_bwd(dup, w['ffn_up'], dr3, sv['r2'], w['ln2_g'], alpha=alpha, name="mm_dx_up_ln")
    g_wo = _mm(sv['o'], dr2b, ta=True, out_dtype=BF16, name="mm_dw_sq")
    do = _mm(dr2b, w['mem_wo'], tb=True, out_dtype=BF16, name="mm_dx_sq")
    dqm, dkm, dvm = _mem_attn_bwd(sv['qm'], sv['km'], sv['vm'], do, name="mem_attn_bwd")
    g_wq = _mm(sv['x1b'], dqm, ta=True, out_dtype=BF16, name="mm_dw_sq")
    g_wk = _mm(memb, dkm, ta=True, out_dtype=BF16, name="mm_dw_memkv")
    g_wv = _mm(memb, dvm, ta=True, out_dtype=BF16, name="mm_dw_memkv")
    dr1, dr1b, g['ln1_g'], g['ln1_b'] = _mm_ln_bwd(dqm, w['mem_wq'], dr2, sv['r1'], w['ln1_g'], tb=True, alpha=alpha,
                                                    name="mm_dx_sq_ln")
    g_out = _mm(sv['ua'], dr1b, ta=True, out_dtype=BF16, name="mm_dw_sq")
    dua = _mm(dr1b, w['w_out'], tb=True, name="mm_dx_sq")
    du1, g['conv_ln_g'], g['conv_ln_b'] = _cln_silu_bwd(dua, sv['u1'], w['conv_ln_g'], w['conv_ln_b'], name="cln_silu_bwd")
    ex, file = sending([('mem_wo', g_wo), ('mem_wq', g_wq), ('w_out', g_out)])
    dga, dgg, g['conv_w'], g['conv_b'], got = _glu_conv_bwd(du1, sv['proj'], w['conv_w'], cc=cc, name="glu_conv_bwd",
                                                             exchange=ex)
    file(got)
    ex, file = sending([('ffn_up', g_up), ('mem_wk', g_wk), ('mem_wv', g_wv)])
    extra = also_during_attention(g) if also_during_attention else None
    dq, dk, dv, got = _sb_bwd(sv['proj'], sv['t_sum'], sv['n_walked'], dua, col0=2 * cc, n_pairs=n_pairs, do_col0=cc,
                              name="sb_bwd", exchange=_both(ex, extra))
    file(got[:len(ex.inputs)])
    also_got = got[len(ex.inputs):]
    dproj = (dga, dgg, dq, dk, dv)
    g_in = jnp.concatenate([_mm(part, sv['xb'], ta=True, out_dtype=BF16, name="mm_dw_in") for part in dproj], axis=0)
    return (dproj, w['w_in'], dr1), g, received, [('w_in', g_in)], also_got


def kernel(x, mem, w_in, conv_w, conv_b, conv_ln_g, conv_ln_b, w_out, ln1_g, ln1_b, mem_wq, mem_wk, mem_wv, mem_wo, ln2_g, ln2_b, ffn_up, ffn_conv_w, ffn_conv_b, ffn_down, ln3_g, ln3_b, loss_target, m_w_in, m_conv_w, m_conv_b, m_conv_ln_g, m_conv_ln_b, m_w_out, m_ln1_g, m_ln1_b, m_mem_wq, m_mem_wk, m_mem_wv, m_mem_wo, m_ln2_g, m_ln2_b, m_ffn_up, m_ffn_conv_w, m_ffn_conv_b, m_ffn_down, m_ln3_g, m_ln3_b, v_w_in, v_conv_w, v_conv_b, v_conv_ln_g, v_conv_ln_b, v_w_out, v_ln1_g, v_ln1_b, v_mem_wq, v_mem_wk, v_mem_wv, v_mem_wo, v_ln2_g, v_ln2_b, v_ffn_up, v_ffn_conv_w, v_ffn_conv_b, v_ffn_down, v_ln3_g, v_ln3_b):
    wts = dict(zip(WEIGHTS, (w_in, conv_w, conv_b, conv_ln_g, conv_ln_b, w_out, ln1_g, ln1_b, mem_wq, mem_wk, mem_wv,
                             mem_wo, ln2_g, ln2_b, ffn_up, ffn_conv_w, ffn_conv_b, ffn_down, ln3_g, ln3_b)))
    mom = dict(zip(WEIGHTS, (m_w_in, m_conv_w, m_conv_b, m_conv_ln_g, m_conv_ln_b, m_w_out, m_ln1_g, m_ln1_b, m_mem_wq,
                             m_mem_wk, m_mem_wv, m_mem_wo, m_ln2_g, m_ln2_b, m_ffn_up, m_ffn_conv_w, m_ffn_conv_b,
                             m_ffn_down, m_ln3_g, m_ln3_b)))
    var = dict(zip(WEIGHTS, (v_w_in, v_conv_w, v_conv_b, v_conv_ln_g, v_conv_ln_b, v_w_out, v_ln1_g, v_ln1_b, v_mem_wq,
                             v_mem_wk, v_mem_wv, v_mem_wo, v_ln2_g, v_ln2_b, v_ffn_up, v_ffn_conv_w, v_ffn_conv_b,
                             v_ffn_down, v_ln3_g, v_ln3_b)))
    depth = w_in.shape[0]
    alpha = (2.0 * depth) ** 0.25
    my_index = _dev_index(_mesh_pos())

    def row_blocks(src, n, col_sharded):
        return jnp.swapaxes(src[n], 1, 2) if col_sharded else src[n]

    bf16_blocks = {n: row_blocks(wts, n, cs).astype(BF16) for n, cs in MATRICES}
    shards = [{n: bf16_blocks[n][l] for n, _ in MATRICES} for l in range(depth)]
    tap_shapes = [wts[n].shape for n in TAPS]
    (gathered_taps, got), = _run_exchanges([_gather_exchange([_pack([wts[n] for n in TAPS]), shards[0]['w_in']])],
                                           name="gather_first")
    full_taps = {n: _join_columns(t) for n, t in zip(TAPS, _unpack(gathered_taps, tap_shapes))}
    full_w_in = _as_matrix(got)

    xs = x[0]
    memb = mem[0].astype(BF16)
    h, hb = xs, xs.astype(BF16)
    saved, weights = [], []
    for l in range(depth):
        w = {'w_in': full_w_in}
        w.update({n: full_taps[n][l] for n in TAPS})
        w.update({n: wts[n][l] for n in REPLICATED})
        h, hb, sv, w, full_w_in = _layer_fwd(h, hb, memb, w, alpha, shards[l],
                                             shards[l + 1]['w_in'] if l + 1 < depth else None)
        saved.append(sv)
        weights.append(w)

    dy, loss_row = _loss_and_grad(h, loss_target[0], name="loss")
    loss = lax.psum(_row_sum(loss_row, name="loss_sum")[0, 0], ("x", "y", "c"))

    results = {}
    col_sharded = dict(MATRICES)
    state = {n: [row_blocks(src, n, cs) for src in (wts, mom, var)] for n, cs in MATRICES}

    def update(n, l, parts):
        results[n] = _sum_adamw(parts, *state[n], layer=l, so_far=results.get(n), name="adamw_matrix")

    small = REPLICATED + TAPS

    def gather_small(g0):
        per_layer = [g0] + grads[1:]
        return _gather_exchange([_pack([jnp.stack([per_layer[l][n] for l in range(depth)]) for n in small])])

    top = (dy, None, None)
    grads = [None] * depth
    carried = []
    for l in reversed(range(depth)):
        top, grads[l], received, left, also_got = _layer_bwd(top, saved[l], memb, weights[l], alpha, carried,
                                                             gather_small if l == 0 else None)
        for n, parts in received.items():
            update(n, l + 1 if n in dict(carried) else l, parts)
        carried = left
    parts, = also_got
    da, last = _mm(jnp.concatenate([part.astype(BF16) for part in top[0]], axis=1), top[1], name="mm_dx_in",
                   exchange=_scatter_exchange([gm.reshape(N_DEV, -1, gm.shape[-1]) for _, gm in carried]))
    for (n, _), blocks in zip(carried, last):
        update(n, 0, blocks)
    grad_x = _axpy(da, top[2], alpha=alpha, name="grad_x")[None]
    for n, cs in MATRICES:
        if cs:
            results[n] = [jnp.swapaxes(r, 1, 2) for r in results[n]]
    total = _sum_parts(parts, name="sum_small_grads")
    summed = dict(zip(small, _unpack(total, [wts[n].shape for n in REPLICATED] + [full_taps[n].shape for n in TAPS])))
    for n in TAPS:
        cols = wts[n].shape[-1]
        summed[n] = lax.dynamic_slice_in_dim(summed[n], my_index * cols, cols, axis=2)
    res = _sum_adamw(_pack([summed[n] for n in small])[None], *[_pack([src[n] for n in small])[None] for src in (wts, mom, var)],
                     layer=0, so_far=None, name="adamw_small")
    unpacked = [_unpack(r[0], [wts[n].shape for n in small]) for r in res]
    for i, n in enumerate(small):
        results[n] = [u[i] for u in unpacked]

    outs = [loss, grad_x]
    for kind in range(4):
        outs += [results[n][kind] for n in WEIGHTS]
    return tuple(outs)
```

```python
import functools

import jax
import jax.numpy as jnp
from jax import lax
from jax.experimental import pallas as pl
from jax.experimental.pallas import tpu as pltpu

F32 = jnp.float32
BF16 = jnp.bfloat16

N_DEV = 8
LANES = 128
PACK_W = 1024
PACK_ROW_ALIGN = 16
SB_HEAD_DIM = 64
MEM_HEAD_DIM = 256
LN_EPS = 1e-5
EXP_UNDERFLOW = 104.0
VMEM_LIMIT = 56 * 1024 * 1024

ADAM_LR = 0.001
ADAM_B1 = 0.9
ADAM_B2 = 0.999
ADAM_EPS = 1e-08
ADAM_WD = 0.01
ADAM_STEP = 10

IN_NAMES = ['x', 'mem', 'w_in', 'conv_w', 'conv_b', 'conv_ln_g', 'conv_ln_b', 'w_out', 'ln1_g', 'ln1_b',
            'mem_wq', 'mem_wk', 'mem_wv', 'mem_wo', 'ln2_g', 'ln2_b', 'ffn_up', 'ffn_conv_w', 'ffn_conv_b',
            'ffn_down', 'ln3_g', 'ln3_b']
WEIGHTS = IN_NAMES[2:]
MATRICES = [('w_in', True), ('w_out', False), ('mem_wq', False), ('mem_wk', False), ('mem_wv', False),
            ('mem_wo', False), ('ffn_up', True), ('ffn_down', False)]
TAPS = ['conv_w', 'ffn_conv_w']
REPLICATED = ['conv_b', 'conv_ln_g', 'conv_ln_b', 'ln1_g', 'ln1_b', 'ln2_g', 'ln2_b', 'ffn_conv_b', 'ln3_g', 'ln3_b']


def _pick(dim, pref, align=LANES):
    if dim <= pref:
        return dim
    fits = [t for t in range(align, pref + 1, align) if dim % t == 0]
    return fits[-1] if fits else dim


def _params(sem):
    return pltpu.CompilerParams(dimension_semantics=sem, vmem_limit_bytes=VMEM_LIMIT)


def _mm(a, b, *, ta=False, tb=False, out_dtype=F32, name, exchange=None):
    if ta:
        kdim, m = a.shape
        tm, tn, tk = _pick(m, 1408), _pick(b.shape[0 if tb else 1], 1024), _pick(kdim, 1024)
    else:
        m, kdim = a.shape
        tm, tn, tk = _pick(m, 1024), _pick(b.shape[0 if tb else 1], 1536), _pick(kdim, 2816)
    if tb:
        n, kb = b.shape
    else:
        kb, n = b.shape
    assert kdim == kb, (a.shape, b.shape, ta, tb)
    grid = (m // tm, n // tn, kdim // tk)
    nk = grid[2]
    dims = (((0 if ta else 1,), (1 if tb else 0,)), ((), ()))
    n_ex_in = len(exchange.inputs) if exchange else 0
    n_ex_out = len(exchange.out_shapes) if exchange else 0

    def body(*refs):
        a_ref, b_ref = refs[:2]
        ex_in = refs[2:2 + n_ex_in]
        o_ref = refs[2 + n_ex_in]
        ex_out = refs[3 + n_ex_in:3 + n_ex_in + n_ex_out]
        scratch = refs[3 + n_ex_in + n_ex_out:]
        if nk > 1:
            acc_ref, scratch = scratch[0], scratch[1:]
        ids = [pl.program_id(d) for d in range(3)]
        if exchange:
            @pl.when((ids[0] == 0) & (ids[1] == 0) & (ids[2] == 0))
            def _():
                exchange.start(ex_in, ex_out, scratch)

        prod = lax.dot_general(a_ref[...].astype(BF16), b_ref[...].astype(BF16), dims,
                               preferred_element_type=F32)
        if nk == 1:
            o_ref[...] = prod.astype(out_dtype)
        else:
            k = ids[2]

            @pl.when(k == 0)
            def _():
                acc_ref[...] = prod

            @pl.when(k > 0)
            def _():
                acc_ref[...] += prod

            @pl.when(k == nk - 1)
            def _():
                o_ref[...] = acc_ref[...].astype(out_dtype)

        if exchange:
            @pl.when((ids[0] == grid[0] - 1) & (ids[1] == grid[1] - 1) & (ids[2] == grid[2] - 1))
            def _():
                exchange.finish(ex_in, ex_out, scratch)

    a_spec = pl.BlockSpec((tk, tm), lambda i, j, k: (k, i)) if ta else pl.BlockSpec((tm, tk), lambda i, j, k: (i, k))
    b_spec = pl.BlockSpec((tn, tk), lambda i, j, k: (j, k)) if tb else pl.BlockSpec((tk, tn), lambda i, j, k: (k, j))
    hbm = pl.BlockSpec(memory_space=pl.ANY)
    outs = pl.pallas_call(
        body, name=name,
        out_shape=(jax.ShapeDtypeStruct((m, n), out_dtype),) + tuple(exchange.out_shapes if exchange else ()),
        grid=grid,
        in_specs=[a_spec, b_spec] + [hbm] * n_ex_in,
        out_specs=(pl.BlockSpec((tm, tn), lambda i, j, k: (i, j)),) + (hbm,) * n_ex_out,
        scratch_shapes=([] if nk == 1 else [pltpu.VMEM((tm, tn), F32)]) + list(exchange.scratch_shapes if exchange else []),
        compiler_params=_params(("arbitrary",) * 3 if exchange else ("parallel", "parallel", "arbitrary")),
    )(a, b, *(exchange.inputs if exchange else ()))
    return (outs[0], list(outs[1:])) if exchange else outs[0]


def _ln_stats(r):
    mu = jnp.mean(r, axis=-1, keepdims=True)
    xc = r - mu
    var = jnp.mean(xc * xc, axis=-1, keepdims=True)
    rstd = lax.rsqrt(var + LN_EPS)
    return xc * rstd, rstd


def _ln_bwd(da, dres, r, g, *, alpha, name, ts=512):
    s, d = r.shape
    ts = _pick(s, ts)
    has_res = dres is not None

    def body(*refs):
        if has_res:
            da_ref, dres_ref, r_ref, g_ref, dr_ref, drb_ref, dg_ref, db_ref = refs
            dy = da_ref[...] + alpha * dres_ref[...]
        else:
            da_ref, r_ref, g_ref, dr_ref, drb_ref, dg_ref, db_ref = refs
            dy = da_ref[...]
        xhat, rstd = _ln_stats(r_ref[...])
        dxhat = dy * g_ref[...]
        m1 = jnp.mean(dxhat, axis=-1, keepdims=True)
        m2 = jnp.mean(dxhat * xhat, axis=-1, keepdims=True)
        dr = rstd * (dxhat - m1 - xhat * m2)
        dr_ref[...] = dr
        drb_ref[...] = dr.astype(BF16)

        @pl.when(pl.program_id(0) == 0)
        def _():
            dg_ref[...] = jnp.zeros_like(dg_ref)
            db_ref[...] = jnp.zeros_like(db_ref)

        dg_ref[...] += jnp.sum(dy * xhat, axis=0, keepdims=True)
        db_ref[...] += jnp.sum(dy, axis=0, keepdims=True)

    tok = pl.BlockSpec((ts, d), lambda i: (i, 0))
    vec = pl.BlockSpec((1, d), lambda i: (0, 0))
    ins = [da, dres, r, g.reshape(1, d)] if has_res else [da, r, g.reshape(1, d)]
    dr, drb, dg, db = pl.pallas_call(
        body, name=name,
        out_shape=(jax.ShapeDtypeStruct((s, d), F32), jax.ShapeDtypeStruct((s, d), BF16),
                   jax.ShapeDtypeStruct((1, d), F32), jax.ShapeDtypeStruct((1, d), F32)),
        grid=(s // ts,), in_specs=[tok] * (len(ins) - 1) + [vec], out_specs=(tok, tok, vec, vec),
        compiler_params=_params(("arbitrary",)),
    )(*ins)
    return dr, drb, dg.reshape(d), db.reshape(d)


def _mm_fused(a, b, fn, *, rows, vecs, out_dtypes, n_sums, tb=False, name, tm=512):
    parts = list(a) if isinstance(a, (tuple, list)) else [a]
    m = parts[0].shape[0]
    kdim = sum(p.shape[1] for p in parts)
    n = b.shape[0] if tb else b.shape[1]
    assert kdim == (b.shape[1] if tb else b.shape[0])
    tm = _pick(m, tm)
    tk = parts[0].shape[1] if len(parts) > 1 else _pick(kdim, 2816)
    nk = kdim // tk
    assert len(parts) in (1, nk) and all(p.shape == (m, tk) for p in parts[1:])
    dims = (((1,), (1 if tb else 0,)), ((), ()))
    n_parts, n_rows, n_vecs, n_outs = len(parts), len(rows), len(vecs), len(out_dtypes)

    def body(*refs):
        a_refs, b_ref = refs[:n_parts], refs[n_parts]
        refs = refs[n_parts - 1:]
        row_refs = refs[2:2 + n_rows]
        vec_refs = refs[2 + n_rows:2 + n_rows + n_vecs]
        out_refs = refs[2 + n_rows + n_vecs:2 + n_rows + n_vecs + n_outs]
        sum_refs = refs[2 + n_rows + n_vecs + n_outs:2 + n_rows + n_vecs + n_outs + n_sums]
        i, k = pl.program_id(0), pl.program_id(1)

        def product(j):
            a_ref = a_refs[j if n_parts > 1 else 0]
            return lax.dot_general(a_ref[...].astype(BF16), b_ref[...].astype(BF16), dims, preferred_element_type=F32)

        def finish(product):
            res = fn(product, *[r[...] for r in row_refs], *[v[...] for v in vec_refs])
            for o_ref, o in zip(out_refs, res[:n_outs]):
                o_ref[...] = o.astype(o_ref.dtype)
            if n_sums:
                @pl.when(i == 0)
                def _():
                    for s_ref in sum_refs:
                        s_ref[...] = jnp.zeros_like(s_ref)

                for s_ref, part in zip(sum_refs, res[n_outs:]):
                    s_ref[...] += part

        if nk == 1:
            finish(product(0))
        else:
            acc_ref = refs[-1]
            for j in range(nk):
                @pl.when(k == j)
                def _(j=j):
                    if j == 0:
                        acc_ref[...] = product(j)
                    elif j < nk - 1:
                        acc_ref[...] += product(j)
                    else:
                        finish(acc_ref[...] + product(j))

    tok = pl.BlockSpec((tm, n), lambda i, k: (i, 0))
    vec = pl.BlockSpec((1, n), lambda i, k: (0, 0))
    b_spec = pl.BlockSpec((n, tk), lambda i, k: (0, k)) if tb else pl.BlockSpec((tk, n), lambda i, k: (k, 0))
    a_spec = pl.BlockSpec((tm, tk), (lambda i, k: (i, 0)) if n_parts > 1 else (lambda i, k: (i, k)))
    return pl.pallas_call(
        body, name=name,
        out_shape=tuple(jax.ShapeDtypeStruct((m, n), dt) for dt in out_dtypes) + (jax.ShapeDtypeStruct((1, n), F32),) * n_sums,
        grid=(m // tm, nk),
        in_specs=[a_spec] * n_parts + [b_spec] + [tok] * n_rows + [vec] * n_vecs,
        out_specs=(tok,) * n_outs + (vec,) * n_sums,
        scratch_shapes=[] if nk == 1 else [pltpu.VMEM((tm, n), F32)],
        compiler_params=_params(("arbitrary", "arbitrary") if n_sums else ("parallel", "arbitrary")),
    )(*parts, b, *rows, *[v.reshape(1, n) for v in vecs])


def _mm_res_ln(a, b, x, g, beta, *, alpha, name):
    def fn(f, x_t, g_t, b_t):
        r = alpha * x_t + f
        xhat, _ = _ln_stats(r)
        y = xhat * g_t + b_t
        return r, y, y

    return _mm_fused(a, b, fn, rows=[x], vecs=[g, beta], out_dtypes=[F32, F32, BF16], n_sums=0, name=name)


def _mm_ln_bwd(a, b, dres, r, g, *, tb=False, alpha, name):
    def fn(f, dres_t, r_t, g_t):
        dy = f + alpha * dres_t
        xhat, rstd = _ln_stats(r_t)
        dxhat = dy * g_t
        m1 = jnp.mean(dxhat, axis=-1, keepdims=True)
        m2 = jnp.mean(dxhat * xhat, axis=-1, keepdims=True)
        dr = rstd * (dxhat - m1 - xhat * m2)
        return dr, dr, jnp.sum(dy * xhat, axis=0, keepdims=True), jnp.sum(dy, axis=0, keepdims=True)

    dr, drb, dg, db = _mm_fused(a, b, fn, rows=[dres, r], vecs=[g], out_dtypes=[F32, BF16], n_sums=2, tb=tb, name=name)
    return dr, drb, dg.reshape(-1), db.reshape(-1)


def _axpy(a, b, *, alpha, name, ts=512):
    s, d = a.shape
    ts = _pick(s, ts)

    def body(a_ref, b_ref, o_ref):
        o_ref[...] = a_ref[...] + alpha * b_ref[...]

    tok = pl.BlockSpec((ts, d), lambda i: (i, 0))
    return pl.pallas_call(body, name=name, out_shape=jax.ShapeDtypeStruct((s, d), F32), grid=(s // ts,),
                          in_specs=[tok, tok], out_specs=tok, compiler_params=_params(("parallel",)))(a, b)


def _loss_and_grad(y, target, *, name, ts=512):
    s, d = y.shape
    ts = _pick(s, ts)
    inv_d = 1.0 / d

    def body(y_ref, t_ref, dy_ref, loss_ref):
        e = y_ref[...] - t_ref[...]
        dy_ref[...] = e * inv_d

        @pl.when(pl.program_id(0) == 0)
        def _():
            loss_ref[...] = jnp.zeros_like(loss_ref)

        loss_ref[...] += jnp.sum(e * e, axis=0, keepdims=True) * (0.5 * inv_d)

    tok = pl.BlockSpec((ts, d), lambda i: (i, 0))
    vec = pl.BlockSpec((1, d), lambda i: (0, 0))
    dy, part = pl.pallas_call(
        body, name=name,
        out_shape=(jax.ShapeDtypeStruct((s, d), F32), jax.ShapeDtypeStruct((1, d), F32)),
        grid=(s // ts,), in_specs=[tok, tok], out_specs=(tok, vec),
        compiler_params=_params(("arbitrary",)),
    )(y, target)
    return dy, part


def _row_sum(v, *, name):
    def body(v_ref, o_ref):
        o_ref[...] = jnp.sum(v_ref[...], axis=1, keepdims=True)

    return pl.pallas_call(body, name=name, out_shape=jax.ShapeDtypeStruct((1, 1), F32))(v)


def _sigmoid(x):
    return 1.0 / (1.0 + jnp.exp(-x))


def _row_chunks(s, pref=512):
    c = _pick(s, pref, 8)
    return [(i * c, c) for i in range(s // c)]


def _glu_conv_fwd(proj, w, b, *, cc, name, exchange=None):
    s = proj.shape[0]
    kw = w.shape[0]
    pad = 32
    assert kw - 1 <= pad
    ncb = cc // LANES
    chunks = _row_chunks(s)

    def body(a_ref, g_ref, w_ref, b_ref, o_ref, u0_ref):
        u0_ref[pl.ds(0, pad), :] = jnp.zeros((pad, LANES), F32)
        for r0, rc in chunks:
            u0_ref[pl.ds(pad + r0, rc), :] = a_ref[pl.ds(r0, rc), :] * _sigmoid(g_ref[pl.ds(r0, rc), :])
        for r0, rc in chunks:
            acc = jnp.zeros((rc, LANES), F32) + b_ref[...]
            for k in range(kw):
                acc = acc + w_ref[pl.ds(k, 1), :] * u0_ref[pl.ds(pad + r0 - (kw - 1) + k, rc), :]
            o_ref[pl.ds(r0, rc), :] = acc

    body, ex_in, ex_out, ex_scratch = _hosted(exchange, (ncb,), body)
    if exchange is not None:
        body = functools.partial(body, n_own_in=4, n_own_out=1)
    hbm = pl.BlockSpec(memory_space=pl.ANY)
    outs = pl.pallas_call(
        body, name=name,
        out_shape=(jax.ShapeDtypeStruct((s, cc), F32), *ex_out),
        grid=(ncb,),
        in_specs=[pl.BlockSpec((s, LANES), lambda c: (0, c)), pl.BlockSpec((s, LANES), lambda c: (0, ncb + c)),
                  pl.BlockSpec((kw, LANES), lambda c: (0, c)), pl.BlockSpec((1, LANES), lambda c: (0, c))] + [hbm] * len(ex_in),
        out_specs=(pl.BlockSpec((s, LANES), lambda c: (0, c)), *([hbm] * len(ex_out))),
        scratch_shapes=[pltpu.VMEM((s + pad, LANES), F32)] + ex_scratch,
        compiler_params=_params(("arbitrary",) if exchange else ("parallel",)),
    )(proj, proj, w, b.reshape(1, cc), *ex_in)
    return outs[0], list(outs[1:])


def _glu_conv_bwd(du1, proj, w, *, cc, name, exchange=None):
    s = proj.shape[0]
    kw = w.shape[0]
    pad = 32
    ncb = cc // LANES
    chunks = _row_chunks(s)

    def body(d_ref, a_ref, g_ref, w_ref, da_ref, dg_ref, dw_ref, db_ref, u0_ref, dp_ref):
        u0_ref[pl.ds(0, pad), :] = jnp.zeros((pad, LANES), F32)
        dp_ref[pl.ds(s, pad), :] = jnp.zeros((pad, LANES), F32)
        for r0, rc in chunks:
            u0_ref[pl.ds(pad + r0, rc), :] = a_ref[pl.ds(r0, rc), :] * _sigmoid(g_ref[pl.ds(r0, rc), :])
            dp_ref[pl.ds(r0, rc), :] = d_ref[pl.ds(r0, rc), :]
        dws = [jnp.zeros((1, LANES), F32) for _ in range(kw)]
        dbs = jnp.zeros((1, LANES), F32)
        for r0, rc in chunks:
            d = dp_ref[pl.ds(r0, rc), :]
            dbs = dbs + jnp.sum(d, axis=0, keepdims=True)
            du0 = jnp.zeros((rc, LANES), F32)
            for k in range(kw):
                du0 = du0 + w_ref[pl.ds(k, 1), :] * dp_ref[pl.ds(r0 + (kw - 1) - k, rc), :]
                dws[k] = dws[k] + jnp.sum(d * u0_ref[pl.ds(pad + r0 - (kw - 1) + k, rc), :], axis=0, keepdims=True)
            sg = _sigmoid(g_ref[pl.ds(r0, rc), :])
            a = a_ref[pl.ds(r0, rc), :]
            da_ref[pl.ds(r0, rc), :] = (du0 * sg).astype(BF16)
            dg_ref[pl.ds(r0, rc), :] = (du0 * a * sg * (1.0 - sg)).astype(BF16)
        for k in range(kw):
            dw_ref[pl.ds(k, 1), :] = dws[k]
        db_ref[...] = dbs

    col = lambda off: pl.BlockSpec((s, LANES), lambda c: (0, off + c))
    body, ex_in, ex_out, ex_scratch = _hosted(exchange, (ncb,), body)
    if exchange is not None:
        body = functools.partial(body, n_own_in=4, n_own_out=4)
    hbm = pl.BlockSpec(memory_space=pl.ANY)
    outs = pl.pallas_call(
        body, name=name,
        out_shape=(jax.ShapeDtypeStruct((s, cc), BF16), jax.ShapeDtypeStruct((s, cc), BF16),
                   jax.ShapeDtypeStruct((kw, cc), F32), jax.ShapeDtypeStruct((1, cc), F32), *ex_out),
        grid=(ncb,),
        in_specs=[col(0), col(0), col(ncb), pl.BlockSpec((kw, LANES), lambda c: (0, c))] + [hbm] * len(ex_in),
        out_specs=(col(0), col(0), pl.BlockSpec((kw, LANES), lambda c: (0, c)), pl.BlockSpec((1, LANES), lambda c: (0, c)),
                   *([hbm] * len(ex_out))),
        scratch_shapes=[pltpu.VMEM((s + pad, LANES), F32), pltpu.VMEM((s + pad, LANES), F32)] + ex_scratch,
        compiler_params=_params(("arbitrary",) if exchange else ("parallel",)),
    )(du1, proj, proj, w, *ex_in)
    da, dg, dw, db = outs[:4]
    return da, dg, dw, db.reshape(cc), list(outs[4:])


def _cln_silu_fwd(u1, g, b, *, name, ts=512):
    s, cc = u1.shape
    ts = _pick(s, ts)

    def body(u_ref, g_ref, b_ref, o_ref):
        xhat, _ = _ln_stats(u_ref[...])
        y = xhat * g_ref[...] + b_ref[...]
        o_ref[...] = (y * _sigmoid(y)).astype(BF16)

    tok = pl.BlockSpec((ts, cc), lambda i: (i, 0))
    vec = pl.BlockSpec((1, cc), lambda i: (0, 0))
    return pl.pallas_call(body, name=name, out_shape=jax.ShapeDtypeStruct((s, cc), BF16), grid=(s // ts,),
                          in_specs=[tok, vec, vec], out_specs=tok,
                          compiler_params=_params(("parallel",)))(u1, g.reshape(1, cc), b.reshape(1, cc))


def _cln_silu_bwd(dua, u1, g, b, *, name, ts=512):
    s, cc = u1.shape
    ts = _pick(s, ts)

    def body(d_ref, u_ref, g_ref, b_ref, du_ref, dg_ref, db_ref):
        xhat, rstd = _ln_stats(u_ref[...])
        y = xhat * g_ref[...] + b_ref[...]
        sg = _sigmoid(y)
        dy = d_ref[...] * (sg * (1.0 + y * (1.0 - sg)))
        dxhat = dy * g_ref[...]
        m1 = jnp.mean(dxhat, axis=-1, keepdims=True)
        m2 = jnp.mean(dxhat * xhat, axis=-1, keepdims=True)
        du_ref[...] = rstd * (dxhat - m1 - xhat * m2)

        @pl.when(pl.program_id(0) == 0)
        def _():
            dg_ref[...] = jnp.zeros_like(dg_ref)
            db_ref[...] = jnp.zeros_like(db_ref)

        dg_ref[...] += jnp.sum(dy * xhat, axis=0, keepdims=True)
        db_ref[...] += jnp.sum(dy, axis=0, keepdims=True)

    tok = pl.BlockSpec((ts, cc), lambda i: (i, 0))
    vec = pl.BlockSpec((1, cc), lambda i: (0, 0))
    du1, dg, db = pl.pallas_call(
        body, name=name,
        out_shape=(jax.ShapeDtypeStruct((s, cc), F32), jax.ShapeDtypeStruct((1, cc), F32),
                   jax.ShapeDtypeStruct((1, cc), F32)),
        grid=(s // ts,), in_specs=[tok, tok, vec, vec], out_specs=(tok, vec, vec),
        compiler_params=_params(("arbitrary",)),
    )(dua, u1, g.reshape(1, cc), b.reshape(1, cc))
    return du1, dg.reshape(cc), db.reshape(cc)


def _softplus_parts(z):
    lk = jnp.minimum(-z, 0.0) - jnp.log1p(jnp.exp(-jnp.abs(z)))
    return lk, z + lk


def _stack_heads(x, hms):
    return jnp.concatenate([jnp.where(hm, x, 0.0) for hm in hms], axis=0).astype(BF16)


def _heads_side_by_side(x_st, tq):
    return jnp.concatenate([x_st[:tq], x_st[tq:]], axis=1)


def _sb_tile_masks(tq):
    row = lax.broadcasted_iota(jnp.int32, (2 * tq, tq), 0)
    col = lax.broadcasted_iota(jnp.int32, (2 * tq, tq), 1)
    vis = (col < jnp.where(row >= tq, row - tq, row)).astype(F32)
    krow, kcol = row[:tq], col[:tq]
    return vis, (krow > kcol).astype(BF16), (krow < kcol).astype(BF16)


_NT = (((1,), (1,)), ((), ()))
_TN = (((0,), (0,)), ((), ()))


def _head_masks():
    lane = lax.broadcasted_iota(jnp.int32, (1, LANES), 1)
    return [(lane >= SB_HEAD_DIM * h) & (lane < SB_HEAD_DIM * (h + 1)) for h in range(2)]


def _hosted(exchange, grid, body):
    if exchange is None:
        return body, [], [], []
    n_in, n_out, n_sems = len(exchange.inputs), len(exchange.out_shapes), len(exchange.scratch_shapes)

    def wrapped(*refs, n_own_in, n_own_out):
        own_in, ex_in = refs[:n_own_in], refs[n_own_in:n_own_in + n_in]
        rest = refs[n_own_in + n_in:]
        own_out, ex_out = rest[:n_own_out], rest[n_own_out:n_own_out + n_out]
        own_scratch, sems = rest[n_own_out + n_out:len(rest) - n_sems], rest[len(rest) - n_sems:]
        ids = [pl.program_id(d) for d in range(len(grid))]
        first = functools.reduce(lambda x, y: x & y, [i == 0 for i in ids])
        last = functools.reduce(lambda x, y: x & y, [i == g - 1 for i, g in zip(ids, grid)])

        @pl.when(first)
        def _():
            exchange.start(ex_in, ex_out, sems)

        body(*own_in, *own_out, *own_scratch)

        @pl.when(last)
        def _():
            exchange.finish(ex_in, ex_out, sems)

    return wrapped, list(exchange.inputs), list(exchange.out_shapes), list(exchange.scratch_shapes)


def _sb_fwd(proj, *, col0, n_pairs, name, tq=256, exchange=None):
    s = proj.shape[0]
    tq = _pick(s, tq)
    nq = s // tq
    cb0 = col0 // LANES
    scale = SB_HEAD_DIM ** -0.5

    def body(q_ref, k_ref, v_ref, vis_ref, after_ref, o_ref, t_ref, n_ref):
        i = pl.program_id(1)
        hms = _head_masks()
        q_st = _stack_heads(q_ref[...] * scale, hms)

        def tile(j, c, acc, masked):
            start = pl.multiple_of(j * tq, tq)
            kb = k_ref[pl.ds(start, tq), :].astype(BF16)
            v_st = _stack_heads(v_ref[pl.ds(start, tq), :], hms)
            z = lax.dot_general(q_st, kb, _NT, preferred_element_type=F32)
            lk, lb = _softplus_parts(z)
            if masked:
                lk = lk * vis_ref[...]
            later = jnp.dot(lk.astype(BF16), after_ref[...], preferred_element_type=F32)
            a = jnp.exp(lb + later + c)
            if masked:
                a = a * vis_ref[...]
            acc = acc + jnp.dot(_heads_side_by_side(a.astype(BF16), tq), v_st, preferred_element_type=F32)
            return c + jnp.sum(lk, axis=1, keepdims=True), acc

        def more(st):
            return jnp.logical_and(st[0] < i, jnp.max(st[1]) > -EXP_UNDERFLOW)

        def step(st):
            c, acc = tile(i - 1 - st[0], st[1], st[2], False)
            return st[0] + 1, c, acc

        c, acc = tile(i, jnp.zeros((2 * tq, 1), F32), jnp.zeros((tq, LANES), F32), True)
        n, c, acc = lax.while_loop(more, step, (jnp.int32(0), c, acc))
        o_ref[...] = acc.astype(BF16)
        t_ref[...] = jnp.where(hms[0], c[:tq], c[tq:])
        n_ref[...] = jnp.zeros((8, LANES), F32) + n.astype(F32)

    grid = (n_pairs, nq)
    body, ex_in, ex_out, ex_scratch = _hosted(exchange, grid, body)
    if exchange is not None:
        body = functools.partial(body, n_own_in=5, n_own_out=3)
    hbm = pl.BlockSpec(memory_space=pl.ANY)
    seq = lambda off: pl.BlockSpec((s, LANES), lambda p, i: (0, cb0 + off + p))
    whole = lambda rows: pl.BlockSpec((rows, tq), lambda p, i: (0, 0))
    vis, m_after, _ = _sb_tile_masks(tq)
    outs = pl.pallas_call(
        body, name=name,
        out_shape=(jax.ShapeDtypeStruct((s, n_pairs * LANES), BF16), jax.ShapeDtypeStruct((n_pairs, s, LANES), F32),
                   jax.ShapeDtypeStruct((n_pairs, nq * 8, LANES), F32), *ex_out),
        grid=grid,
        in_specs=[pl.BlockSpec((tq, LANES), lambda p, i: (i, cb0 + p)), seq(n_pairs), seq(2 * n_pairs),
                  whole(2 * tq), whole(tq)] + [hbm] * len(ex_in),
        out_specs=(pl.BlockSpec((tq, LANES), lambda p, i: (i, p)), pl.BlockSpec((None, tq, LANES), lambda p, i: (p, i, 0)),
                   pl.BlockSpec((None, 8, LANES), lambda p, i: (p, i, 0)), *([hbm] * len(ex_out))),
        scratch_shapes=ex_scratch,
        compiler_params=_params(("arbitrary", "arbitrary") if exchange else ("parallel", "arbitrary")),
    )(proj, proj, proj, vis, m_after, *ex_in)
    return outs[0], outs[1], outs[2], list(outs[3:])


def _sb_bwd(proj, t_sum, n_walked, dua, *, col0, n_pairs, do_col0, name, tq=256, exchange=None):
    s = proj.shape[0]
    tq = _pick(s, tq)
    cb0 = col0 // LANES
    dcb0 = do_col0 // LANES
    scale = SB_HEAD_DIM ** -0.5

    def body(q_ref, k_ref, v_ref, t_ref, n_ref, do_ref, vis_ref, after_ref, before_ref, dq_ref, dk_ref, dv_ref):
        i = pl.program_id(1)

        @pl.when(i == 0)
        def _():
            dk_ref[...] = jnp.zeros_like(dk_ref)
            dv_ref[...] = jnp.zeros_like(dv_ref)

        hms = _head_masks()
        q_st = _stack_heads(q_ref[...] * scale, hms)
        do_st = _stack_heads(do_ref[...], hms)
        t_st = jnp.concatenate([t_ref[:, SB_HEAD_DIM * h:SB_HEAD_DIM * h + 1] for h in range(2)], axis=0)
        first = i - jnp.max(n_ref[...]).astype(jnp.int32)

        def tile(j, p_sum, g_sum, dq, masked):
            start = pl.multiple_of(j * tq, tq)
            k = k_ref[pl.ds(start, tq), :]
            z = lax.dot_general(q_st, k.astype(BF16), _NT, preferred_element_type=F32)
            lk_raw, lb = _softplus_parts(z)
            lk = lk_raw * vis_ref[...] if masked else lk_raw
            p_next = p_sum + jnp.sum(lk, axis=1, keepdims=True)
            later = jnp.dot(lk.astype(BF16), after_ref[...], preferred_element_type=F32)
            a = jnp.exp(lb + (t_st - p_next) + later)
            if masked:
                a = a * vis_ref[...]
            da = lax.dot_general(do_st, v_ref[pl.ds(start, tq), :].astype(BF16), _NT, preferred_element_type=F32)
            g = a * da
            g_before = g_sum + jnp.dot(g.astype(BF16), before_ref[...], preferred_element_type=F32)
            dz = g * jnp.exp(lk_raw) - g_before * jnp.exp(lb)
            if masked:
                dz = dz * vis_ref[...]
            dzb = dz.astype(BF16)
            dv_ref[pl.ds(start, tq), :] += lax.dot_general(a.astype(BF16), do_st, _TN, preferred_element_type=F32)
            dk_ref[pl.ds(start, tq), :] += lax.dot_general(dzb, q_st, _TN, preferred_element_type=F32)
            dq = dq + jnp.dot(_heads_side_by_side(dzb, tq), _stack_heads(k, hms), preferred_element_type=F32)
            return p_next, g_sum + jnp.sum(g, axis=1, keepdims=True), dq

        zero = jnp.zeros((2 * tq, 1), F32)
        st = lax.fori_loop(first, i, lambda j, st: tile(j, *st, False), (zero, zero, jnp.zeros((tq, LANES), F32)))
        dq_ref[...] = tile(i, *st, True)[2] * scale

    grid = (n_pairs, s // tq)
    body, ex_in, ex_out, ex_scratch = _hosted(exchange, grid, body)
    if exchange is not None:
        body = functools.partial(body, n_own_in=9, n_own_out=3)
    hbm = pl.BlockSpec(memory_space=pl.ANY)
    seq = lambda off: pl.BlockSpec((s, LANES), lambda p, i: (0, cb0 + off + p))
    whole = lambda rows: pl.BlockSpec((rows, tq), lambda p, i: (0, 0))
    out = jax.ShapeDtypeStruct((s, n_pairs * LANES), F32)
    res = pl.BlockSpec((s, LANES), lambda p, i: (0, p))
    outs = pl.pallas_call(
        body, name=name,
        out_shape=(out, out, out, *ex_out),
        grid=grid,
        in_specs=[pl.BlockSpec((tq, LANES), lambda p, i: (i, cb0 + p)), seq(n_pairs), seq(2 * n_pairs),
                  pl.BlockSpec((None, tq, LANES), lambda p, i: (p, i, 0)),
                  pl.BlockSpec((None, 8, LANES), lambda p, i: (p, i, 0)),
                  pl.BlockSpec((tq, LANES), lambda p, i: (i, dcb0 + p)),
                  whole(2 * tq), whole(tq), whole(tq)] + [hbm] * len(ex_in),
        out_specs=(pl.BlockSpec((tq, LANES), lambda p, i: (i, p)), res, res, *([hbm] * len(ex_out))),
        scratch_shapes=ex_scratch,
        compiler_params=_params(("arbitrary", "arbitrary")),
    )(proj, proj, proj, t_sum, n_walked, dua, *_sb_tile_masks(tq), *ex_in)
    return outs[0], outs[1], outs[2], list(outs[3:])


def _mem_attn_fwd(qm, km, vm, *, name, tq=512):
    s, d = qm.shape
    heads = d // MEM_HEAD_DIM
    mlen = km.shape[0]
    tq = _pick(s, tq)
    scale = MEM_HEAD_DIM ** -0.5

    def body(q_ref, k_ref, v_ref, o_ref):
        for h in range(heads):
            sl = slice(h * MEM_HEAD_DIM, (h + 1) * MEM_HEAD_DIM)
            q = (q_ref[:, sl] * scale).astype(BF16)
            sc = lax.dot_general(q, k_ref[:, sl].astype(BF16), _NT, preferred_element_type=F32)
            e = jnp.exp(sc - jnp.max(sc, axis=1, keepdims=True))
            p = e / jnp.sum(e, axis=1, keepdims=True)
            o_ref[:, sl] = jnp.dot(p.astype(BF16), v_ref[:, sl].astype(BF16), preferred_element_type=F32).astype(BF16)

    tok = pl.BlockSpec((tq, d), lambda i: (i, 0))
    kv = pl.BlockSpec((mlen, d), lambda i: (0, 0))
    return pl.pallas_call(body, name=name, out_shape=jax.ShapeDtypeStruct((s, d), BF16), grid=(s // tq,),
                          in_specs=[tok, kv, kv], out_specs=tok, compiler_params=_params(("parallel",)))(qm, km, vm)


def _mem_attn_bwd(qm, km, vm, do, *, name, tq=512):
    s, d = qm.shape
    heads = d // MEM_HEAD_DIM
    mlen = km.shape[0]
    tq = _pick(s, tq)
    scale = MEM_HEAD_DIM ** -0.5

    def body(q_ref, k_ref, v_ref, do_ref, dq_ref, dk_ref, dv_ref):
        @pl.when(pl.program_id(0) == 0)
        def _():
            dk_ref[...] = jnp.zeros_like(dk_ref)
            dv_ref[...] = jnp.zeros_like(dv_ref)

        for h in range(heads):
            sl = slice(h * MEM_HEAD_DIM, (h + 1) * MEM_HEAD_DIM)
            q = (q_ref[:, sl] * scale).astype(BF16)
            k = k_ref[:, sl].astype(BF16)
            v = v_ref[:, sl].astype(BF16)
            sc = lax.dot_general(q, k, _NT, preferred_element_type=F32)
            e = jnp.exp(sc - jnp.max(sc, axis=1, keepdims=True))
            p = e / jnp.sum(e, axis=1, keepdims=True)
            dob = do_ref[:, sl].astype(BF16)
            dv_ref[:, sl] += lax.dot_general(p.astype(BF16), dob, _TN, preferred_element_type=F32)
            dp = lax.dot_general(dob, v, _NT, preferred_element_type=F32)
            ds = (p * (dp - jnp.sum(dp * p, axis=1, keepdims=True))).astype(BF16)
            dq_ref[:, sl] = (jnp.dot(ds, k, preferred_element_type=F32) * scale).astype(BF16)
            dk_ref[:, sl] += lax.dot_general(ds, q, _TN, preferred_element_type=F32)

    tok = pl.BlockSpec((tq, d), lambda i: (i, 0))
    kv = pl.BlockSpec((mlen, d), lambda i: (0, 0))
    return pl.pallas_call(
        body, name=name,
        out_shape=(jax.ShapeDtypeStruct((s, d), BF16), jax.ShapeDtypeStruct((mlen, d), F32),
                   jax.ShapeDtypeStruct((mlen, d), F32)),
        grid=(s // tq,), in_specs=[tok, kv, kv, tok], out_specs=(tok, kv, kv),
        compiler_params=_params(("arbitrary",)),
    )(qm, km, vm, do)


def _ffn_act_fwd(up, w, b, *, name, exchange=None):
    s, two_f = up.shape
    ff = two_f // 2
    nfb = ff // LANES
    kw = w.shape[0]
    pad = 8
    chunks = _row_chunks(s)

    def body(v_ref, g_ref, wv_ref, wg_ref, bv_ref, bg_ref, o_ref, vp_ref, gp_ref):
        vp_ref[pl.ds(0, pad), :] = jnp.zeros((pad, LANES), F32)
        gp_ref[pl.ds(0, pad), :] = jnp.zeros((pad, LANES), F32)
        for r0, rc in chunks:
            vp_ref[pl.ds(pad + r0, rc), :] = v_ref[pl.ds(r0, rc), :].astype(F32)
            gp_ref[pl.ds(pad + r0, rc), :] = g_ref[pl.ds(r0, rc), :].astype(F32)
        for r0, rc in chunks:
            vc = jnp.zeros((rc, LANES), F32) + bv_ref[...]
            gc = jnp.zeros((rc, LANES), F32) + bg_ref[...]
            for k in range(kw):
                off = pad + r0 - (kw - 1) + k
                vc = vc + wv_ref[pl.ds(k, 1), :] * vp_ref[pl.ds(off, rc), :]
                gc = gc + wg_ref[pl.ds(k, 1), :] * gp_ref[pl.ds(off, rc), :]
            o_ref[pl.ds(r0, rc), :] = (gc * _sigmoid(gc) * vc).astype(BF16)

    col = lambda off: pl.BlockSpec((s, LANES), lambda c: (0, off + c))
    tap = lambda off: pl.BlockSpec((kw, LANES), lambda c: (0, off + c))
    vec = lambda off: pl.BlockSpec((1, LANES), lambda c: (0, off + c))
    body, ex_in, ex_out, ex_scratch = _hosted(exchange, (nfb,), body)
    if exchange is not None:
        body = functools.partial(body, n_own_in=6, n_own_out=1)
    hbm = pl.BlockSpec(memory_space=pl.ANY)
    outs = pl.pallas_call(
        body, name=name, out_shape=(jax.ShapeDtypeStruct((s, ff), BF16), *ex_out), grid=(nfb,),
        in_specs=[col(0), col(nfb), tap(0), tap(nfb), vec(0), vec(nfb)] + [hbm] * len(ex_in),
        out_specs=(col(0), *([hbm] * len(ex_out))),
        scratch_shapes=[pltpu.VMEM((s + pad, LANES), F32), pltpu.VMEM((s + pad, LANES), F32)] + ex_scratch,
        compiler_params=_params(("arbitrary",) if exchange else ("parallel",)),
    )(up, up, w, w, b.reshape(1, two_f), b.reshape(1, two_f), *ex_in)
    return outs[0], list(outs[1:])


def _ffn_act_bwd(up, dact, w, b, *, name, exchange=None):
    s, two_f = up.shape
    ff = two_f // 2
    nfb = ff // LANES
    kw = w.shape[0]
    pad = 8
    chunks = _row_chunks(s)

    def body(v_ref, g_ref, d_ref, wv_ref, wg_ref, bv_ref, bg_ref, dv_ref, dg_ref, dwv_ref, dwg_ref, dbv_ref, dbg_ref,
             vp_ref, gp_ref, dvc_ref, dgc_ref):
        vp_ref[pl.ds(0, pad), :] = jnp.zeros((pad, LANES), F32)
        gp_ref[pl.ds(0, pad), :] = jnp.zeros((pad, LANES), F32)
        dvc_ref[pl.ds(s, pad), :] = jnp.zeros((pad, LANES), F32)
        dgc_ref[pl.ds(s, pad), :] = jnp.zeros((pad, LANES), F32)
        for r0, rc in chunks:
            vp_ref[pl.ds(pad + r0, rc), :] = v_ref[pl.ds(r0, rc), :].astype(F32)
            gp_ref[pl.ds(pad + r0, rc), :] = g_ref[pl.ds(r0, rc), :].astype(F32)
        dwv = [jnp.zeros((1, LANES), F32) for _ in range(kw)]
        dwg = [jnp.zeros((1, LANES), F32) for _ in range(kw)]
        dbv = jnp.zeros((1, LANES), F32)
        dbg = jnp.zeros((1, LANES), F32)
        for r0, rc in chunks:
            vc = jnp.zeros((rc, LANES), F32) + bv_ref[...]
            gc = jnp.zeros((rc, LANES), F32) + bg_ref[...]
            for k in range(kw):
                off = pad + r0 - (kw - 1) + k
                vc = vc + wv_ref[pl.ds(k, 1), :] * vp_ref[pl.ds(off, rc), :]
                gc = gc + wg_ref[pl.ds(k, 1), :] * gp_ref[pl.ds(off, rc), :]
            sg = _sigmoid(gc)
            d = d_ref[pl.ds(r0, rc), :].astype(F32)
            dvc = d * (gc * sg)
            dgc = d * vc * (sg * (1.0 + gc * (1.0 - sg)))
            dvc_ref[pl.ds(r0, rc), :] = dvc
            dgc_ref[pl.ds(r0, rc), :] = dgc
            dbv = dbv + jnp.sum(dvc, axis=0, keepdims=True)
            dbg = dbg + jnp.sum(dgc, axis=0, keepdims=True)
            for k in range(kw):
                off = pad + r0 - (kw - 1) + k
                dwv[k] = dwv[k] + jnp.sum(dvc * vp_ref[pl.ds(off, rc), :], axis=0, keepdims=True)
                dwg[k] = dwg[k] + jnp.sum(dgc * gp_ref[pl.ds(off, rc), :], axis=0, keepdims=True)
        for r0, rc in chunks:
            dv = jnp.zeros((rc, LANES), F32)
            dg = jnp.zeros((rc, LANES), F32)
            for k in range(kw):
                off = r0 + (kw - 1) - k
                dv = dv + wv_ref[pl.ds(k, 1), :] * dvc_ref[pl.ds(off, rc), :]
                dg = dg + wg_ref[pl.ds(k, 1), :] * dgc_ref[pl.ds(off, rc), :]
            dv_ref[pl.ds(r0, rc), :] = dv.astype(BF16)
            dg_ref[pl.ds(r0, rc), :] = dg.astype(BF16)
        for k in range(kw):
            dwv_ref[pl.ds(k, 1), :] = dwv[k]
            dwg_ref[pl.ds(k, 1), :] = dwg[k]
        dbv_ref[...] = dbv
        dbg_ref[...] = dbg

    col = lambda off: pl.BlockSpec((s, LANES), lambda c: (0, off + c))
    tap = lambda off: pl.BlockSpec((kw, LANES), lambda c: (0, off + c))
    vec = lambda off: pl.BlockSpec((1, LANES), lambda c: (0, off + c))
    big = lambda: pltpu.VMEM((s + pad, LANES), F32)
    body, ex_in, ex_out, ex_scratch = _hosted(exchange, (nfb,), body)
    if exchange is not None:
        body = functools.partial(body, n_own_in=7, n_own_out=6)
    hbm = pl.BlockSpec(memory_space=pl.ANY)
    outs = pl.pallas_call(
        body, name=name,
        out_shape=(jax.ShapeDtypeStruct((s, ff), BF16), jax.ShapeDtypeStruct((s, ff), BF16),
                   jax.ShapeDtypeStruct((kw, ff), F32), jax.ShapeDtypeStruct((kw, ff), F32),
                   jax.ShapeDtypeStruct((1, ff), F32), jax.ShapeDtypeStruct((1, ff), F32), *ex_out),
        grid=(nfb,),
        in_specs=[col(0), col(nfb), col(0), tap(0), tap(nfb), vec(0), vec(nfb)] + [hbm] * len(ex_in),
        out_specs=(col(0), col(0), tap(0), tap(0), vec(0), vec(0), *([hbm] * len(ex_out))),
        scratch_shapes=[big(), big(), big(), big()] + ex_scratch,
        compiler_params=_params(("arbitrary",) if exchange else ("parallel",)),
    )(up, up, dact, w, w, b.reshape(1, two_f), b.reshape(1, two_f), *ex_in)
    dv, dg, dwv, dwg, dbv, dbg = outs[:6]
    return (dv, dg), jnp.concatenate([dwv, dwg], axis=1), jnp.concatenate([dbv, dbg], axis=1).reshape(two_f), list(outs[6:])


def _sum_parts(parts, *, name, tr=256):
    n_parts, rows, cols = parts.shape
    tr = _pick(rows, tr, 16)

    def body(p_ref, o_ref):
        g = p_ref[0].astype(F32)
        for k in range(1, n_parts):
            g = g + p_ref[k].astype(F32)
        o_ref[...] = g

    return pl.pallas_call(
        body, name=name, out_shape=jax.ShapeDtypeStruct((rows, cols), F32), grid=(rows // tr,),
        in_specs=[pl.BlockSpec((n_parts, tr, cols), lambda i: (0, i, 0))],
        out_specs=pl.BlockSpec((tr, cols), lambda i: (i, 0)), compiler_params=_params(("parallel",)),
    )(parts)


def _sum_adamw(parts, w, m, v, *, layer, so_far, name, tr=256):
    n_parts, rows, cols = parts.shape
    depth = w.shape[0]
    tr = _pick(rows, tr, 16)
    c1 = 1.0 / (1.0 - ADAM_B1 ** ADAM_STEP)
    c2 = 1.0 / (1.0 - ADAM_B2 ** ADAM_STEP)

    def body(p_ref, w_ref, m_ref, v_ref, *rest):
        g_ref, d_ref, nm_ref, nv_ref = rest[-4:]
        g = p_ref[0].astype(F32)
        for k in range(1, n_parts):
            g = g + p_ref[k].astype(F32)
        nm = ADAM_B1 * m_ref[...] + (1.0 - ADAM_B1) * g
        nv = ADAM_B2 * v_ref[...] + (1.0 - ADAM_B2) * (g * g)
        g_ref[...] = g
        nm_ref[...] = nm
        nv_ref[...] = nv
        d_ref[...] = -ADAM_LR * ((nm * c1) / (jnp.sqrt(nv * c2) + ADAM_EPS) + ADAM_WD * w_ref[...])

    blk = pl.BlockSpec((None, tr, cols), lambda i: (layer, i, 0))
    out = jax.ShapeDtypeStruct((depth, rows, cols), F32)
    kept = list(so_far) if so_far is not None else []
    return pl.pallas_call(
        body, name=name, out_shape=(out, out, out, out), grid=(rows // tr,),
        in_specs=[pl.BlockSpec((n_parts, tr, cols), lambda i: (0, i, 0)), blk, blk, blk]
        + [pl.BlockSpec(memory_space=pl.ANY)] * len(kept),
        out_specs=(blk, blk, blk, blk),
        input_output_aliases={4 + k: k for k in range(len(kept))},
        compiler_params=_params(("parallel",)),
    )(parts, w, m, v, *kept)


def _mesh_pos():
    return lax.axis_index("x"), lax.axis_index("y"), lax.axis_index("c")


def _flip(pos, k):
    x, y, c = pos
    return (1 - x if k & 4 else x, 1 - y if k & 2 else y, 1 - c if k & 1 else c)


def _dev_index(pos):
    return 4 * pos[0] + 2 * pos[1] + pos[2]


N_PEERS = N_DEV - 1


class _Exchange:
    def __init__(self, inputs, out_shapes, start, finish, scratch_shapes=None):
        n = len(inputs)
        self.inputs, self.out_shapes, self.start, self.finish = list(inputs), list(out_shapes), start, finish
        self.scratch_shapes = scratch_shapes or [pltpu.SemaphoreType.DMA((n * N_PEERS,)),
                                                 pltpu.SemaphoreType.DMA((n * N_PEERS,)), pltpu.SemaphoreType.DMA((n,))]


def _both(ex1, ex2):
    if ex1 is None or ex2 is None:
        return ex1 or ex2
    n_in, n_out, n_sems = len(ex1.inputs), len(ex1.out_shapes), len(ex1.scratch_shapes)

    def halves(ins, outs, sems):
        return (ins[:n_in], outs[:n_out], sems[:n_sems]), (ins[n_in:], outs[n_out:], sems[n_sems:])

    def start(ins, outs, sems):
        a, b = halves(ins, outs, sems)
        ex1.start(*a)
        ex2.start(*b)

    def finish(ins, outs, sems):
        a, b = halves(ins, outs, sems)
        ex1.finish(*a)
        ex2.finish(*b)

    return _Exchange(ex1.inputs + ex2.inputs, ex1.out_shapes + ex2.out_shapes, start, finish,
                     ex1.scratch_shapes + ex2.scratch_shapes)


def _gather_exchange(xs):
    n = len(xs)

    def plan(x_refs, out_refs, sems):
        send_sems, recv_sems, local_sems = sems
        me = _mesh_pos()
        sibling = _flip(me, 1)
        chips = [_flip(me, 4), _flip(me, 2), _flip(me, 6)]

        def copy(a, k, block, to, from_input=False):
            slot = out_refs[a].at[_dev_index(block)]
            return pltpu.make_async_remote_copy(
                src_ref=x_refs[a] if from_input else slot, dst_ref=slot,
                send_sem=send_sems.at[a * N_PEERS + k], recv_sem=recv_sems.at[a * N_PEERS + k],
                device_id=to, device_id_type=pl.DeviceIdType.MESH)

        mine = [pltpu.make_async_copy(x_refs[a], out_refs[a].at[_dev_index(me)], local_sems.at[a]) for a in range(n)]
        first = [copy(a, 0, me, sibling, True) for a in range(n)]
        first += [copy(a, 1 + j, me, chip, True) for j, chip in enumerate(chips) for a in range(n)]
        return me, sibling, chips, copy, mine, first

    def start(x_refs, out_refs, sems):
        _, _, _, _, mine, first = plan(x_refs, out_refs, sems)
        for cp in mine + first:
            cp.start()

    def finish(x_refs, out_refs, sems):
        me, sibling, chips, copy, mine, first = plan(x_refs, out_refs, sems)
        passed = []
        for j, chip in enumerate(chips):
            for a in range(n):
                copy(a, 1 + j, chip, me).wait_recv()
                passed.append(copy(a, 4 + j, chip, sibling))
                passed[-1].start()
        for a in range(n):
            copy(a, 0, sibling, me).wait_recv()
        for j, chip in enumerate(chips):
            for a in range(n):
                copy(a, 4 + j, _flip(chip, 1), me).wait_recv()
        for cp in first + passed:
            cp.wait_send()
        for cp in mine:
            cp.wait()

    return _Exchange(xs, [jax.ShapeDtypeStruct((N_DEV,) + x.shape, x.dtype) for x in xs], start, finish)


def _scatter_exchange(xs):
    n = len(xs)

    def plan(x_refs, out_refs, sems):
        send_sems, recv_sems, local_sems = sems
        me = _mesh_pos()
        my_slot = _dev_index(me)

        def copy(a, k):
            peer = _flip(me, k)
            return pltpu.make_async_remote_copy(
                src_ref=x_refs[a].at[_dev_index(peer)], dst_ref=out_refs[a].at[my_slot],
                send_sem=send_sems.at[a * N_PEERS + k - 1], recv_sem=recv_sems.at[a * N_PEERS + k - 1],
                device_id=peer, device_id_type=pl.DeviceIdType.MESH)

        mine = [pltpu.make_async_copy(x_refs[a].at[my_slot], out_refs[a].at[my_slot], local_sems.at[a]) for a in range(n)]
        return mine, [copy(a, k) for k in range(1, N_DEV) for a in range(n)]

    def start(x_refs, out_refs, sems):
        mine, copies = plan(x_refs, out_refs, sems)
        for cp in mine + copies:
            cp.start()

    def finish(x_refs, out_refs, sems):
        mine, copies = plan(x_refs, out_refs, sems)
        for cp in copies:
            cp.wait_recv()
        for cp in copies:
            cp.wait_send()
        for cp in mine:
            cp.wait()

    return _Exchange(xs, [jax.ShapeDtypeStruct(x.shape, x.dtype) for x in xs], start, finish)


def _run_exchanges(exchanges, *, name):
    counts = [len(ex.inputs) for ex in exchanges]
    n = sum(counts)

    def body(*refs):
        offsets = [sum(counts[:e]) for e in range(len(exchanges))]
        views = [(refs[o:o + c], refs[n + o:n + o + c], refs[2 * n + 3 * e:2 * n + 3 * e + 3])
                 for e, (o, c) in enumerate(zip(offsets, counts))]
        for ex, view in zip(exchanges, views):
            ex.start(*view)
        for ex, view in zip(exchanges, views):
            ex.finish(*view)

    hbm = pl.BlockSpec(memory_space=pl.ANY)
    outs = pl.pallas_call(
        body, name=name, out_shape=tuple(s for ex in exchanges for s in ex.out_shapes), in_specs=[hbm] * n,
        out_specs=tuple([hbm] * n), scratch_shapes=[s for ex in exchanges for s in ex.scratch_shapes],
    )(*[x for ex in exchanges for x in ex.inputs])
    return [list(outs[sum(counts[:e]):sum(counts[:e + 1])]) for e in range(len(exchanges))]


def _pack(arrays):
    flat = jnp.concatenate([a.reshape(-1) for a in arrays])
    n = flat.shape[0]
    tile = PACK_W * PACK_ROW_ALIGN
    total = -(-n // tile) * tile
    return jnp.pad(flat, (0, total - n)).reshape(total // PACK_W, PACK_W)


def _unpack(buf, shapes):
    lead = buf.shape[:-2]
    flat = buf.reshape(lead + (-1,))
    out, off = [], 0
    for shp in shapes:
        n = 1
        for dim in shp:
            n *= dim
        out.append(flat[..., off:off + n].reshape(lead + tuple(shp)))
        off += n
    return out


def _join_columns(blocks):
    return jnp.moveaxis(blocks, 0, 2).reshape(blocks.shape[1], blocks.shape[2], -1)


def _mm_hosting(a, b, exchange, **kw):
    if exchange is None:
        return _mm(a, b, **kw), []
    return _mm(a, b, exchange=exchange, **kw)


GATHERED_BY_ATTENTION = ['w_out', 'mem_wq', 'mem_wk', 'mem_wv', 'mem_wo', 'ffn_up']
GATHERED_BY_CONV = ['ffn_down']


def _as_matrix(blocks):
    return blocks.reshape(-1, blocks.shape[-1])


def _layer_fwd(x, xb, memb, w, alpha, shards, next_w_in):
    w = dict(w)
    cc = w['conv_w'].shape[1]
    n_pairs = (N_DEV * shards['w_out'].shape[0] - cc) // LANES

    proj = _mm(xb, w['w_in'], tb=True, name="mm_proj")
    u1, got = _glu_conv_fwd(proj, w['conv_w'], w['conv_b'], cc=cc, name="glu_conv_fwd",
                            exchange=_gather_exchange([shards[n] for n in GATHERED_BY_CONV]))
    w.update({n: _as_matrix(f) for n, f in zip(GATHERED_BY_CONV, got)})
    u = _cln_silu_fwd(u1, w['conv_ln_g'], w['conv_ln_b'], name="cln_silu_fwd")
    att, t_sum, n_walked, got = _sb_fwd(proj, col0=2 * cc, n_pairs=n_pairs, name="sb_fwd",
                                        exchange=_gather_exchange([shards[n] for n in GATHERED_BY_ATTENTION]))
    w.update({n: _as_matrix(f) for n, f in zip(GATHERED_BY_ATTENTION, got)})
    ua = jnp.concatenate([u, att], axis=1)
    r1, x1, x1b = _mm_res_ln(ua, w['w_out'], x, w['ln1_g'], w['ln1_b'], alpha=alpha, name="mm_mix_ln")
    qm = _mm(x1b, w['mem_wq'], name="mm_memq")
    km = _mm(memb, w['mem_wk'], name="mm_memkv")
    vm = _mm(memb, w['mem_wv'], name="mm_memkv")
    o = _mem_attn_fwd(qm, km, vm, name="mem_attn_fwd")
    r2, x2, x2b = _mm_res_ln(o, w['mem_wo'], x1, w['ln2_g'], w['ln2_b'], alpha=alpha, name="mm_mix_ln")
    up = _mm(x2b, w['ffn_up'], tb=True, out_dtype=BF16, name="mm_up")
    act, got = _ffn_act_fwd(up, w['ffn_conv_w'], w['ffn_conv_b'], name="ffn_act_fwd",
                            exchange=_gather_exchange([next_w_in]) if next_w_in is not None else None)
    r3, x3, x3b = _mm_res_ln(act, w['ffn_down'], x2, w['ln3_g'], w['ln3_b'], alpha=alpha, name="mm_down_ln")
    saved = dict(xb=xb, proj=proj, u1=u1, t_sum=t_sum, n_walked=n_walked, ua=ua, r1=r1, x1b=x1b, qm=qm, km=km, vm=vm,
                 o=o, r2=r2, x2b=x2b, up=up, act=act, r3=r3)
    return x3, x3b, saved, w, _as_matrix(got[0]) if got else None


def _layer_bwd(top, sv, memb, w, alpha, carried, also_during_attention=None):
    g, received = {}, {}
    cc = w['conv_w'].shape[1]
    n_pairs = (w['w_out'].shape[0] - cc) // LANES

    def sending(sends):
        ex = _scatter_exchange([gm.reshape(N_DEV, -1, gm.shape[-1]) for _, gm in sends]) if sends else None
        return ex, lambda got: received.update({key: blocks for (key, _), blocks in zip(sends, got)})

    if top[1] is None:
        dr3, dr3b, g['ln3_g'], g['ln3_b'] = _ln_bwd(top[0], None, sv['r3'], w['ln3_g'], alpha=alpha, name="ln_bwd")
    else:
        dr3, dr3b, g['ln3_g'], g['ln3_b'] = _mm_ln_bwd(*top, sv['r3'], w['ln3_g'], alpha=alpha, name="mm_dx_in_ln")
    g_down = _mm(sv['act'], dr3b, ta=True, out_dtype=BF16, name="mm_dw_down")
    dact = _mm(dr3b, w['ffn_down'], tb=True, out_dtype=BF16, name="mm_dact")
    ex, file = sending(carried + [('ffn_down', g_down)])
    dup, g['ffn_conv_w'], g['ffn_conv_b'], got = _ffn_act_bwd(sv['up'], dact, w['ffn_conv_w'], w['ffn_conv_b'],
                                                                name="ffn_act_bwd", exchange=ex)
    file(got)
    g_up = jnp.concatenate([_mm(half, sv['x2b'], ta=True, out_dtype=BF16, name="mm_dw_up") for half in dup], axis=0)
    dr2, dr2b, g['ln2_g'], g['ln2_b'] = _mm_ln_bwd(dup, w['ffn_up'], dr3, sv['r2'], w['ln2_g'], alpha=alpha, name="mm_dx_up_ln")
    g_wo = _mm(sv['o'], dr2b, ta=True, out_dtype=BF16, name="mm_dw_sq")
    do = _mm(dr2b, w['mem_wo'], tb=True, out_dtype=BF16, name="mm_dx_sq")
    dqm, dkm, dvm = _mem_attn_bwd(sv['qm'], sv['km'], sv['vm'], do, name="mem_attn_bwd")
    g_wq = _mm(sv['x1b'], dqm, ta=True, out_dtype=BF16, name="mm_dw_sq")
    g_wk = _mm(memb, dkm, ta=True, out_dtype=BF16, name="mm_dw_memkv")
    g_wv = _mm(memb, dvm, ta=True, out_dtype=BF16, name="mm_dw_memkv")
    dr1, dr1b, g['ln1_g'], g['ln1_b'] = _mm_ln_bwd(dqm, w['mem_wq'], dr2, sv['r1'], w['ln1_g'], tb=True, alpha=alpha,
                                                    name="mm_dx_sq_ln")
    g_out = _mm(sv['ua'], dr1b, ta=True, out_dtype=BF16, name="mm_dw_sq")
    dua = _mm(dr1b, w['w_out'], tb=True, name="mm_dx_sq")
    du1, g['conv_ln_g'], g['conv_ln_b'] = _cln_silu_bwd(dua, sv['u1'], w['conv_ln_g'], w['conv_ln_b'], name="cln_silu_bwd")
    ex, file = sending([('mem_wo', g_wo), ('mem_wq', g_wq), ('w_out', g_out)])
    dga, dgg, g['conv_w'], g['conv_b'], got = _glu_conv_bwd(du1, sv['proj'], w['conv_w'], cc=cc, name="glu_conv_bwd",
                                                             exchange=ex)
    file(got)
    ex, file = sending([('ffn_up', g_up), ('mem_wk', g_wk), ('mem_wv', g_wv)])
    extra = also_during_attention(g) if also_during_attention else None
    dq, dk, dv, got = _sb_bwd(sv['proj'], sv['t_sum'], sv['n_walked'], dua, col0=2 * cc, n_pairs=n_pairs, do_col0=cc,
                              name="sb_bwd", exchange=_both(ex, extra))
    file(got[:len(ex.inputs)])
    also_got = got[len(ex.inputs):]
    dproj = jnp.concatenate([dga, dgg, dq.astype(BF16), dk.astype(BF16), dv.astype(BF16)], axis=1)
    g_in = _mm(dproj, sv['xb'], ta=True, out_dtype=BF16, name="mm_dw_in")
    return (dproj, w['w_in'], dr1), g, received, [('w_in', g_in)], also_got


def kernel(x, mem, w_in, conv_w, conv_b, conv_ln_g, conv_ln_b, w_out, ln1_g, ln1_b, mem_wq, mem_wk, mem_wv, mem_wo, ln2_g, ln2_b, ffn_up, ffn_conv_w, ffn_conv_b, ffn_down, ln3_g, ln3_b, loss_target, m_w_in, m_conv_w, m_conv_b, m_conv_ln_g, m_conv_ln_b, m_w_out, m_ln1_g, m_ln1_b, m_mem_wq, m_mem_wk, m_mem_wv, m_mem_wo, m_ln2_g, m_ln2_b, m_ffn_up, m_ffn_conv_w, m_ffn_conv_b, m_ffn_down, m_ln3_g, m_ln3_b, v_w_in, v_conv_w, v_conv_b, v_conv_ln_g, v_conv_ln_b, v_w_out, v_ln1_g, v_ln1_b, v_mem_wq, v_mem_wk, v_mem_wv, v_mem_wo, v_ln2_g, v_ln2_b, v_ffn_up, v_ffn_conv_w, v_ffn_conv_b, v_ffn_down, v_ln3_g, v_ln3_b):
    wts = dict(zip(WEIGHTS, (w_in, conv_w, conv_b, conv_ln_g, conv_ln_b, w_out, ln1_g, ln1_b, mem_wq, mem_wk, mem_wv,
                             mem_wo, ln2_g, ln2_b, ffn_up, ffn_conv_w, ffn_conv_b, ffn_down, ln3_g, ln3_b)))
    mom = dict(zip(WEIGHTS, (m_w_in, m_conv_w, m_conv_b, m_conv_ln_g, m_conv_ln_b, m_w_out, m_ln1_g, m_ln1_b, m_mem_wq,
                             m_mem_wk, m_mem_wv, m_mem_wo, m_ln2_g, m_ln2_b, m_ffn_up, m_ffn_conv_w, m_ffn_conv_b,
                             m_ffn_down, m_ln3_g, m_ln3_b)))
    var = dict(zip(WEIGHTS, (v_w_in, v_conv_w, v_conv_b, v_conv_ln_g, v_conv_ln_b, v_w_out, v_ln1_g, v_ln1_b, v_mem_wq,
                             v_mem_wk, v_mem_wv, v_mem_wo, v_ln2_g, v_ln2_b, v_ffn_up, v_ffn_conv_w, v_ffn_conv_b,
                             v_ffn_down, v_ln3_g, v_ln3_b)))
    depth = w_in.shape[0]
    alpha = (2.0 * depth) ** 0.25
    my_index = _dev_index(_mesh_pos())

    def row_blocks(src, n, col_sharded):
        return jnp.swapaxes(src[n], 1, 2) if col_sharded else src[n]

    bf16_blocks = {n: row_blocks(wts, n, cs).astype(BF16) for n, cs in MATRICES}
    shards = [{n: bf16_blocks[n][l] for n, _ in MATRICES} for l in range(depth)]
    tap_shapes = [wts[n].shape for n in TAPS]
    (gathered_taps, got), = _run_exchanges([_gather_exchange([_pack([wts[n] for n in TAPS]), shards[0]['w_in']])],
                                           name="gather_first")
    full_taps = {n: _join_columns(t) for n, t in zip(TAPS, _unpack(gathered_taps, tap_shapes))}
    full_w_in = _as_matrix(got)

    xs = x[0]
    memb = mem[0].astype(BF16)
    h, hb = xs, xs.astype(BF16)
    saved, weights = [], []
    for l in range(depth):
        w = {'w_in': full_w_in}
        w.update({n: full_taps[n][l] for n in TAPS})
        w.update({n: wts[n][l] for n in REPLICATED})
        h, hb, sv, w, full_w_in = _layer_fwd(h, hb, memb, w, alpha, shards[l],
                                             shards[l + 1]['w_in'] if l + 1 < depth else None)
        saved.append(sv)
        weights.append(w)

    dy, loss_row = _loss_and_grad(h, loss_target[0], name="loss")
    loss = lax.psum(_row_sum(loss_row, name="loss_sum")[0, 0], ("x", "y", "c"))

    results = {}
    col_sharded = dict(MATRICES)
    state = {n: [row_blocks(src, n, cs) for src in (wts, mom, var)] for n, cs in MATRICES}

    def update(n, l, parts):
        results[n] = _sum_adamw(parts, *state[n], layer=l, so_far=results.get(n), name="adamw_matrix")

    small = REPLICATED + TAPS

    def gather_small(g0):
        per_layer = [g0] + grads[1:]
        return _gather_exchange([_pack([jnp.stack([per_layer[l][n] for l in range(depth)]) for n in small])])

    top = (dy, None, None)
    grads = [None] * depth
    carried = []
    for l in reversed(range(depth)):
        top, grads[l], received, left, also_got = _layer_bwd(top, saved[l], memb, weights[l], alpha, carried,
                                                             gather_small if l == 0 else None)
        for n, parts in received.items():
            update(n, l + 1 if n in dict(carried) else l, parts)
        carried = left
    parts, = also_got
    da, last = _mm(top[0], top[1], name="mm_dx_in",
                   exchange=_scatter_exchange([gm.reshape(N_DEV, -1, gm.shape[-1]) for _, gm in carried]))
    for (n, _), blocks in zip(carried, last):
        update(n, 0, blocks)
    grad_x = _axpy(da, top[2], alpha=alpha, name="grad_x")[None]
    for n, cs in MATRICES:
        if cs:
            results[n] = [jnp.swapaxes(r, 1, 2) for r in results[n]]
    total = _sum_parts(parts, name="sum_small_grads")
    summed = dict(zip(small, _unpack(total, [wts[n].shape for n in REPLICATED] + [full_taps[n].shape for n in TAPS])))
    for n in TAPS:
        cols = wts[n].shape[-1]
        summed[n] = lax.dynamic_slice_in_dim(summed[n], my_index * cols, cols, axis=2)
    res = _sum_adamw(_pack([summed[n] for n in small])[None], *[_pack([src[n] for n in small])[None] for src in (wts, mom, var)],
                     layer=0, so_far=None, name="adamw_small")
    unpacked = [_unpack(r[0], [wts[n].shape for n in small]) for r in res]
    for i, n in enumerate(small):
        results[n] = [u[i] for u in unpacked]

    outs = [loss, grad_x]
    for kind in range(4):
        outs += [results[n][kind] for n in WEIGHTS]
    return tuple(outs)
```

```python
import functools

import jax
import jax.numpy as jnp
from jax import lax
from jax.experimental import pallas as pl
from jax.experimental.pallas import tpu as pltpu

F32 = jnp.float32
BF16 = jnp.bfloat16

N_DEV = 8
LANES = 128
PACK_W = 1024
PACK_ROW_ALIGN = 16
SB_HEAD_DIM = 64
MEM_HEAD_DIM = 256
LN_EPS = 1e-5
EXP_UNDERFLOW = 104.0
VMEM_LIMIT = 56 * 1024 * 1024

ADAM_LR = 0.001
ADAM_B1 = 0.9
ADAM_B2 = 0.999
ADAM_EPS = 1e-08
ADAM_WD = 0.01
ADAM_STEP = 10

IN_NAMES = ['x', 'mem', 'w_in', 'conv_w', 'conv_b', 'conv_ln_g', 'conv_ln_b', 'w_out', 'ln1_g', 'ln1_b',
            'mem_wq', 'mem_wk', 'mem_wv', 'mem_wo', 'ln2_g', 'ln2_b', 'ffn_up', 'ffn_conv_w', 'ffn_conv_b',
            'ffn_down', 'ln3_g', 'ln3_b']
WEIGHTS = IN_NAMES[2:]
MATRICES = [('w_in', True), ('w_out', False), ('mem_wq', False), ('mem_wk', False), ('mem_wv', False),
            ('mem_wo', False), ('ffn_up', True), ('ffn_down', False)]
TAPS = ['conv_w', 'ffn_conv_w']
REPLICATED = ['conv_b', 'conv_ln_g', 'conv_ln_b', 'ln1_g', 'ln1_b', 'ln2_g', 'ln2_b', 'ffn_conv_b', 'ln3_g', 'ln3_b']


def _pick(dim, pref, align=LANES):
    if dim <= pref:
        return dim
    fits = [t for t in range(align, pref + 1, align) if dim % t == 0]
    return fits[-1] if fits else dim


def _params(sem):
    return pltpu.CompilerParams(dimension_semantics=sem, vmem_limit_bytes=VMEM_LIMIT)


def _mm(a, b, *, ta=False, tb=False, out_dtype=F32, name, exchange=None):
    if ta:
        kdim, m = a.shape
        tm, tn, tk = _pick(m, 1408), _pick(b.shape[0 if tb else 1], 1024), _pick(kdim, 2048)
    else:
        m, kdim = a.shape
        tm, tn, tk = _pick(m, 1024), _pick(b.shape[0 if tb else 1], 1536), _pick(kdim, 2816)
    if tb:
        n, kb = b.shape
    else:
        kb, n = b.shape
    assert kdim == kb, (a.shape, b.shape, ta, tb)
    grid = (m // tm, n // tn, kdim // tk)
    nk = grid[2]
    dims = (((0 if ta else 1,), (1 if tb else 0,)), ((), ()))
    n_ex_in = len(exchange.inputs) if exchange else 0
    n_ex_out = len(exchange.out_shapes) if exchange else 0

    def body(*refs):
        a_ref, b_ref = refs[:2]
        ex_in = refs[2:2 + n_ex_in]
        o_ref = refs[2 + n_ex_in]
        ex_out = refs[3 + n_ex_in:3 + n_ex_in + n_ex_out]
        scratch = refs[3 + n_ex_in + n_ex_out:]
        if nk > 1:
            acc_ref, scratch = scratch[0], scratch[1:]
        ids = [pl.program_id(d) for d in range(3)]
        if exchange:
            @pl.when((ids[0] == 0) & (ids[1] == 0) & (ids[2] == 0))
            def _():
                exchange.start(ex_in, ex_out, scratch)

        prod = lax.dot_general(a_ref[...].astype(BF16), b_ref[...].astype(BF16), dims,
                               preferred_element_type=F32)
        if nk == 1:
            o_ref[...] = prod.astype(out_dtype)
        else:
            k = ids[2]

            @pl.when(k == 0)
            def _():
                acc_ref[...] = prod

            @pl.when(k > 0)
            def _():
                acc_ref[...] += prod

            @pl.when(k == nk - 1)
            def _():
                o_ref[...] = acc_ref[...].astype(out_dtype)

        if exchange:
            @pl.when((ids[0] == grid[0] - 1) & (ids[1] == grid[1] - 1) & (ids[2] == grid[2] - 1))
            def _():
                exchange.finish(ex_in, ex_out, scratch)

    a_spec = pl.BlockSpec((tk, tm), lambda i, j, k: (k, i)) if ta else pl.BlockSpec((tm, tk), lambda i, j, k: (i, k))
    b_spec = pl.BlockSpec((tn, tk), lambda i, j, k: (j, k)) if tb else pl.BlockSpec((tk, tn), lambda i, j, k: (k, j))
    hbm = pl.BlockSpec(memory_space=pl.ANY)
    outs = pl.pallas_call(
        body, name=name,
        out_shape=(jax.ShapeDtypeStruct((m, n), out_dtype),) + tuple(exchange.out_shapes if exchange else ()),
        grid=grid,
        in_specs=[a_spec, b_spec] + [hbm] * n_ex_in,
        out_specs=(pl.BlockSpec((tm, tn), lambda i, j, k: (i, j)),) + (hbm,) * n_ex_out,
        scratch_shapes=([] if nk == 1 else [pltpu.VMEM((tm, tn), F32)]) + list(exchange.scratch_shapes if exchange else []),
        compiler_params=_params(("arbitrary",) * 3 if exchange else ("parallel", "parallel", "arbitrary")),
    )(a, b, *(exchange.inputs if exchange else ()))
    return (outs[0], list(outs[1:])) if exchange else outs[0]


def _ln_stats(r):
    mu = jnp.mean(r, axis=-1, keepdims=True)
    xc = r - mu
    var = jnp.mean(xc * xc, axis=-1, keepdims=True)
    rstd = lax.rsqrt(var + LN_EPS)
    return xc * rstd, rstd


def _ln_bwd(da, dres, r, g, *, alpha, name, ts=512):
    s, d = r.shape
    ts = _pick(s, ts)
    has_res = dres is not None

    def body(*refs):
        if has_res:
            da_ref, dres_ref, r_ref, g_ref, dr_ref, drb_ref, dg_ref, db_ref = refs
            dy = da_ref[...] + alpha * dres_ref[...]
        else:
            da_ref, r_ref, g_ref, dr_ref, drb_ref, dg_ref, db_ref = refs
            dy = da_ref[...]
        xhat, rstd = _ln_stats(r_ref[...])
        dxhat = dy * g_ref[...]
        m1 = jnp.mean(dxhat, axis=-1, keepdims=True)
        m2 = jnp.mean(dxhat * xhat, axis=-1, keepdims=True)
        dr = rstd * (dxhat - m1 - xhat * m2)
        dr_ref[...] = dr
        drb_ref[...] = dr.astype(BF16)

        @pl.when(pl.program_id(0) == 0)
        def _():
            dg_ref[...] = jnp.zeros_like(dg_ref)
            db_ref[...] = jnp.zeros_like(db_ref)

        dg_ref[...] += jnp.sum(dy * xhat, axis=0, keepdims=True)
        db_ref[...] += jnp.sum(dy, axis=0, keepdims=True)

    tok = pl.BlockSpec((ts, d), lambda i: (i, 0))
    vec = pl.BlockSpec((1, d), lambda i: (0, 0))
    ins = [da, dres, r, g.reshape(1, d)] if has_res else [da, r, g.reshape(1, d)]
    dr, drb, dg, db = pl.pallas_call(
        body, name=name,
        out_shape=(jax.ShapeDtypeStruct((s, d), F32), jax.ShapeDtypeStruct((s, d), BF16),
                   jax.ShapeDtypeStruct((1, d), F32), jax.ShapeDtypeStruct((1, d), F32)),
        grid=(s // ts,), in_specs=[tok] * (len(ins) - 1) + [vec], out_specs=(tok, tok, vec, vec),
        compiler_params=_params(("arbitrary",)),
    )(*ins)
    return dr, drb, dg.reshape(d), db.reshape(d)


def _mm_fused(a, b, fn, *, rows, vecs, out_dtypes, n_sums, tb=False, name, tm=512):
    parts = list(a) if isinstance(a, (tuple, list)) else [a]
    m = parts[0].shape[0]
    kdim = sum(p.shape[1] for p in parts)
    n = b.shape[0] if tb else b.shape[1]
    assert kdim == (b.shape[1] if tb else b.shape[0])
    tm = _pick(m, tm)
    tk = parts[0].shape[1] if len(parts) > 1 else _pick(kdim, 2816)
    nk = kdim // tk
    assert len(parts) in (1, nk) and all(p.shape == (m, tk) for p in parts[1:])
    dims = (((1,), (1 if tb else 0,)), ((), ()))
    n_parts, n_rows, n_vecs, n_outs = len(parts), len(rows), len(vecs), len(out_dtypes)

    def body(*refs):
        a_refs, b_ref = refs[:n_parts], refs[n_parts]
        refs = refs[n_parts - 1:]
        row_refs = refs[2:2 + n_rows]
        vec_refs = refs[2 + n_rows:2 + n_rows + n_vecs]
        out_refs = refs[2 + n_rows + n_vecs:2 + n_rows + n_vecs + n_outs]
        sum_refs = refs[2 + n_rows + n_vecs + n_outs:2 + n_rows + n_vecs + n_outs + n_sums]
        i, k = pl.program_id(0), pl.program_id(1)

        def product(j):
            a_ref = a_refs[j if n_parts > 1 else 0]
            return lax.dot_general(a_ref[...].astype(BF16), b_ref[...].astype(BF16), dims, preferred_element_type=F32)

        def finish(product):
            res = fn(product, *[r[...] for r in row_refs], *[v[...] for v in vec_refs])
            for o_ref, o in zip(out_refs, res[:n_outs]):
                o_ref[...] = o.astype(o_ref.dtype)
            if n_sums:
                @pl.when(i == 0)
                def _():
                    for s_ref in sum_refs:
                        s_ref[...] = jnp.zeros_like(s_ref)

                for s_ref, part in zip(sum_refs, res[n_outs:]):
                    s_ref[...] += part

        if nk == 1:
            finish(product(0))
        else:
            acc_ref = refs[-1]
            for j in range(nk):
                @pl.when(k == j)
                def _(j=j):
                    if j == 0:
                        acc_ref[...] = product(j)
                    elif j < nk - 1:
                        acc_ref[...] += product(j)
                    else:
                        finish(acc_ref[...] + product(j))

    tok = pl.BlockSpec((tm, n), lambda i, k: (i, 0))
    vec = pl.BlockSpec((1, n), lambda i, k: (0, 0))
    b_spec = pl.BlockSpec((n, tk), lambda i, k: (0, k)) if tb else pl.BlockSpec((tk, n), lambda i, k: (k, 0))
    a_spec = pl.BlockSpec((tm, tk), (lambda i, k: (i, 0)) if n_parts > 1 else (lambda i, k: (i, k)))
    return pl.pallas_call(
        body, name=name,
        out_shape=tuple(jax.ShapeDtypeStruct((m, n), dt) for dt in out_dtypes) + (jax.ShapeDtypeStruct((1, n), F32),) * n_sums,
        grid=(m // tm, nk),
        in_specs=[a_spec] * n_parts + [b_spec] + [tok] * n_rows + [vec] * n_vecs,
        out_specs=(tok,) * n_outs + (vec,) * n_sums,
        scratch_shapes=[] if nk == 1 else [pltpu.VMEM((tm, n), F32)],
        compiler_params=_params(("arbitrary", "arbitrary") if n_sums else ("parallel", "arbitrary")),
    )(*parts, b, *rows, *[v.reshape(1, n) for v in vecs])


def _mm_res_ln(a, b, x, g, beta, *, alpha, name):
    def fn(f, x_t, g_t, b_t):
        r = alpha * x_t + f
        xhat, _ = _ln_stats(r)
        y = xhat * g_t + b_t
        return r, y, y

    return _mm_fused(a, b, fn, rows=[x], vecs=[g, beta], out_dtypes=[F32, F32, BF16], n_sums=0, name=name)


def _mm_ln_bwd(a, b, dres, r, g, *, tb=False, alpha, name):
    def fn(f, dres_t, r_t, g_t):
        dy = f + alpha * dres_t
        xhat, rstd = _ln_stats(r_t)
        dxhat = dy * g_t
        m1 = jnp.mean(dxhat, axis=-1, keepdims=True)
        m2 = jnp.mean(dxhat * xhat, axis=-1, keepdims=True)
        dr = rstd * (dxhat - m1 - xhat * m2)
        return dr, dr, jnp.sum(dy * xhat, axis=0, keepdims=True), jnp.sum(dy, axis=0, keepdims=True)

    dr, drb, dg, db = _mm_fused(a, b, fn, rows=[dres, r], vecs=[g], out_dtypes=[F32, BF16], n_sums=2, tb=tb, name=name)
    return dr, drb, dg.reshape(-1), db.reshape(-1)


def _axpy(a, b, *, alpha, name, ts=512):
    s, d = a.shape
    ts = _pick(s, ts)

    def body(a_ref, b_ref, o_ref):
        o_ref[...] = a_ref[...] + alpha * b_ref[...]

    tok = pl.BlockSpec((ts, d), lambda i: (i, 0))
    return pl.pallas_call(body, name=name, out_shape=jax.ShapeDtypeStruct((s, d), F32), grid=(s // ts,),
                          in_specs=[tok, tok], out_specs=tok, compiler_params=_params(("parallel",)))(a, b)


def _loss_and_grad(y, target, *, name, ts=512):
    s, d = y.shape
    ts = _pick(s, ts)
    inv_d = 1.0 / d

    def body(y_ref, t_ref, dy_ref, loss_ref):
        e = y_ref[...] - t_ref[...]
        dy_ref[...] = e * inv_d

        @pl.when(pl.program_id(0) == 0)
        def _():
            loss_ref[...] = jnp.zeros_like(loss_ref)

        loss_ref[...] += jnp.sum(e * e, axis=0, keepdims=True) * (0.5 * inv_d)

    tok = pl.BlockSpec((ts, d), lambda i: (i, 0))
    vec = pl.BlockSpec((1, d), lambda i: (0, 0))
    dy, part = pl.pallas_call(
        body, name=name,
        out_shape=(jax.ShapeDtypeStruct((s, d), F32), jax.ShapeDtypeStruct((1, d), F32)),
        grid=(s // ts,), in_specs=[tok, tok], out_specs=(tok, vec),
        compiler_params=_params(("arbitrary",)),
    )(y, target)
    return dy, part


def _row_sum(v, *, name):
    def body(v_ref, o_ref):
        o_ref[...] = jnp.sum(v_ref[...], axis=1, keepdims=True)

    return pl.pallas_call(body, name=name, out_shape=jax.ShapeDtypeStruct((1, 1), F32))(v)


def _sigmoid(x):
    return 1.0 / (1.0 + jnp.exp(-x))


def _row_chunks(s, pref=512):
    c = _pick(s, pref, 8)
    return [(i * c, c) for i in range(s // c)]


def _glu_conv_fwd(proj, w, b, *, cc, name, exchange=None):
    s = proj.shape[0]
    kw = w.shape[0]
    pad = 32
    assert kw - 1 <= pad
    ncb = cc // LANES
    chunks = _row_chunks(s)

    def body(a_ref, g_ref, w_ref, b_ref, o_ref, u0_ref):
        u0_ref[pl.ds(0, pad), :] = jnp.zeros((pad, LANES), F32)
        for r0, rc in chunks:
            u0_ref[pl.ds(pad + r0, rc), :] = a_ref[pl.ds(r0, rc), :] * _sigmoid(g_ref[pl.ds(r0, rc), :])
        for r0, rc in chunks:
            acc = jnp.zeros((rc, LANES), F32) + b_ref[...]
            for k in range(kw):
                acc = acc + w_ref[pl.ds(k, 1), :] * u0_ref[pl.ds(pad + r0 - (kw - 1) + k, rc), :]
            o_ref[pl.ds(r0, rc), :] = acc

    body, ex_in, ex_out, ex_scratch = _hosted(exchange, (ncb,), body)
    if exchange is not None:
        body = functools.partial(body, n_own_in=4, n_own_out=1)
    hbm = pl.BlockSpec(memory_space=pl.ANY)
    outs = pl.pallas_call(
        body, name=name,
        out_shape=(jax.ShapeDtypeStruct((s, cc), F32), *ex_out),
        grid=(ncb,),
        in_specs=[pl.BlockSpec((s, LANES), lambda c: (0, c)), pl.BlockSpec((s, LANES), lambda c: (0, ncb + c)),
                  pl.BlockSpec((kw, LANES), lambda c: (0, c)), pl.BlockSpec((1, LANES), lambda c: (0, c))] + [hbm] * len(ex_in),
        out_specs=(pl.BlockSpec((s, LANES), lambda c: (0, c)), *([hbm] * len(ex_out))),
        scratch_shapes=[pltpu.VMEM((s + pad, LANES), F32)] + ex_scratch,
        compiler_params=_params(("arbitrary",) if exchange else ("parallel",)),
    )(proj, proj, w, b.reshape(1, cc), *ex_in)
    return outs[0], list(outs[1:])


def _glu_conv_bwd(du1, proj, w, *, cc, name, exchange=None):
    s = proj.shape[0]
    kw = w.shape[0]
    pad = 32
    ncb = cc // LANES
    chunks = _row_chunks(s)

    def body(d_ref, a_ref, g_ref, w_ref, da_ref, dg_ref, dw_ref, db_ref, u0_ref, dp_ref):
        u0_ref[pl.ds(0, pad), :] = jnp.zeros((pad, LANES), F32)
        dp_ref[pl.ds(s, pad), :] = jnp.zeros((pad, LANES), F32)
        for r0, rc in chunks:
            u0_ref[pl.ds(pad + r0, rc), :] = a_ref[pl.ds(r0, rc), :] * _sigmoid(g_ref[pl.ds(r0, rc), :])
            dp_ref[pl.ds(r0, rc), :] = d_ref[pl.ds(r0, rc), :]
        dws = [jnp.zeros((1, LANES), F32) for _ in range(kw)]
        dbs = jnp.zeros((1, LANES), F32)
        for r0, rc in chunks:
            d = dp_ref[pl.ds(r0, rc), :]
            dbs = dbs + jnp.sum(d, axis=0, keepdims=True)
            du0 = jnp.zeros((rc, LANES), F32)
            for k in range(kw):
                du0 = du0 + w_ref[pl.ds(k, 1), :] * dp_ref[pl.ds(r0 + (kw - 1) - k, rc), :]
                dws[k] = dws[k] + jnp.sum(d * u0_ref[pl.ds(pad + r0 - (kw - 1) + k, rc), :], axis=0, keepdims=True)
            sg = _sigmoid(g_ref[pl.ds(r0, rc), :])
            a = a_ref[pl.ds(r0, rc), :]
            da_ref[pl.ds(r0, rc), :] = (du0 * sg).astype(BF16)
            dg_ref[pl.ds(r0, rc), :] = (du0 * a * sg * (1.0 - sg)).astype(BF16)
        for k in range(kw):
            dw_ref[pl.ds(k, 1), :] = dws[k]
        db_ref[...] = dbs

    col = lambda off: pl.BlockSpec((s, LANES), lambda c: (0, off + c))
    body, ex_in, ex_out, ex_scratch = _hosted(exchange, (ncb,), body)
    if exchange is not None:
        body = functools.partial(body, n_own_in=4, n_own_out=4)
    hbm = pl.BlockSpec(memory_space=pl.ANY)
    outs = pl.pallas_call(
        body, name=name,
        out_shape=(jax.ShapeDtypeStruct((s, cc), BF16), jax.ShapeDtypeStruct((s, cc), BF16),
                   jax.ShapeDtypeStruct((kw, cc), F32), jax.ShapeDtypeStruct((1, cc), F32), *ex_out),
        grid=(ncb,),
        in_specs=[col(0), col(0), col(ncb), pl.BlockSpec((kw, LANES), lambda c: (0, c))] + [hbm] * len(ex_in),
        out_specs=(col(0), col(0), pl.BlockSpec((kw, LANES), lambda c: (0, c)), pl.BlockSpec((1, LANES), lambda c: (0, c)),
                   *([hbm] * len(ex_out))),
        scratch_shapes=[pltpu.VMEM((s + pad, LANES), F32), pltpu.VMEM((s + pad, LANES), F32)] + ex_scratch,
        compiler_params=_params(("arbitrary",) if exchange else ("parallel",)),
    )(du1, proj, proj, w, *ex_in)
    da, dg, dw, db = outs[:4]
    return da, dg, dw, db.reshape(cc), list(outs[4:])


def _cln_silu_fwd(u1, g, b, *, name, ts=512):
    s, cc = u1.shape
    ts = _pick(s, ts)

    def body(u_ref, g_ref, b_ref, o_ref):
        xhat, _ = _ln_stats(u_ref[...])
        y = xhat * g_ref[...] + b_ref[...]
        o_ref[...] = (y * _sigmoid(y)).astype(BF16)

    tok = pl.BlockSpec((ts, cc), lambda i: (i, 0))
    vec = pl.BlockSpec((1, cc), lambda i: (0, 0))
    return pl.pallas_call(body, name=name, out_shape=jax.ShapeDtypeStruct((s, cc), BF16), grid=(s // ts,),
                          in_specs=[tok, vec, vec], out_specs=tok,
                          compiler_params=_params(("parallel",)))(u1, g.reshape(1, cc), b.reshape(1, cc))


def _cln_silu_bwd(dua, u1, g, b, *, name, ts=512):
    s, cc = u1.shape
    ts = _pick(s, ts)

    def body(d_ref, u_ref, g_ref, b_ref, du_ref, dg_ref, db_ref):
        xhat, rstd = _ln_stats(u_ref[...])
        y = xhat * g_ref[...] + b_ref[...]
        sg = _sigmoid(y)
        dy = d_ref[...] * (sg * (1.0 + y * (1.0 - sg)))
        dxhat = dy * g_ref[...]
        m1 = jnp.mean(dxhat, axis=-1, keepdims=True)
        m2 = jnp.mean(dxhat * xhat, axis=-1, keepdims=True)
        du_ref[...] = rstd * (dxhat - m1 - xhat * m2)

        @pl.when(pl.program_id(0) == 0)
        def _():
            dg_ref[...] = jnp.zeros_like(dg_ref)
            db_ref[...] = jnp.zeros_like(db_ref)

        dg_ref[...] += jnp.sum(dy * xhat, axis=0, keepdims=True)
        db_ref[...] += jnp.sum(dy, axis=0, keepdims=True)

    tok = pl.BlockSpec((ts, cc), lambda i: (i, 0))
    vec = pl.BlockSpec((1, cc), lambda i: (0, 0))
    du1, dg, db = pl.pallas_call(
        body, name=name,
        out_shape=(jax.ShapeDtypeStruct((s, cc), F32), jax.ShapeDtypeStruct((1, cc), F32),
                   jax.ShapeDtypeStruct((1, cc), F32)),
        grid=(s // ts,), in_specs=[tok, tok, vec, vec], out_specs=(tok, vec, vec),
        compiler_params=_params(("arbitrary",)),
    )(dua, u1, g.reshape(1, cc), b.reshape(1, cc))
    return du1, dg.reshape(cc), db.reshape(cc)


def _softplus_parts(z):
    lk = jnp.minimum(-z, 0.0) - jnp.log1p(jnp.exp(-jnp.abs(z)))
    return lk, z + lk


def _stack_heads(x, hms):
    return jnp.concatenate([jnp.where(hm, x, 0.0) for hm in hms], axis=0).astype(BF16)


def _heads_side_by_side(x_st, tq):
    return jnp.concatenate([x_st[:tq], x_st[tq:]], axis=1)


def _sb_tile_masks(tq):
    row = lax.broadcasted_iota(jnp.int32, (2 * tq, tq), 0)
    col = lax.broadcasted_iota(jnp.int32, (2 * tq, tq), 1)
    vis = (col < jnp.where(row >= tq, row - tq, row)).astype(F32)
    krow, kcol = row[:tq], col[:tq]
    return vis, (krow > kcol).astype(BF16), (krow < kcol).astype(BF16)


_NT = (((1,), (1,)), ((), ()))
_TN = (((0,), (0,)), ((), ()))


def _head_masks():
    lane = lax.broadcasted_iota(jnp.int32, (1, LANES), 1)
    return [(lane >= SB_HEAD_DIM * h) & (lane < SB_HEAD_DIM * (h + 1)) for h in range(2)]


def _hosted(exchange, grid, body):
    if exchange is None:
        return body, [], [], []
    n_in, n_out, n_sems = len(exchange.inputs), len(exchange.out_shapes), len(exchange.scratch_shapes)

    def wrapped(*refs, n_own_in, n_own_out):
        own_in, ex_in = refs[:n_own_in], refs[n_own_in:n_own_in + n_in]
        rest = refs[n_own_in + n_in:]
        own_out, ex_out = rest[:n_own_out], rest[n_own_out:n_own_out + n_out]
        own_scratch, sems = rest[n_own_out + n_out:len(rest) - n_sems], rest[len(rest) - n_sems:]
        ids = [pl.program_id(d) for d in range(len(grid))]
        first = functools.reduce(lambda x, y: x & y, [i == 0 for i in ids])
        last = functools.reduce(lambda x, y: x & y, [i == g - 1 for i, g in zip(ids, grid)])

        @pl.when(first)
        def _():
            exchange.start(ex_in, ex_out, sems)

        body(*own_in, *own_out, *own_scratch)

        @pl.when(last)
        def _():
            exchange.finish(ex_in, ex_out, sems)

    return wrapped, list(exchange.inputs), list(exchange.out_shapes), list(exchange.scratch_shapes)


def _sb_fwd(proj, *, col0, n_pairs, name, tq=256, exchange=None):
    s = proj.shape[0]
    tq = _pick(s, tq)
    nq = s // tq
    cb0 = col0 // LANES
    scale = SB_HEAD_DIM ** -0.5

    def body(q_ref, k_ref, v_ref, vis_ref, after_ref, o_ref, t_ref, n_ref):
        i = pl.program_id(1)
        hms = _head_masks()
        q_st = _stack_heads(q_ref[...] * scale, hms)

        def tile(j, c, acc, masked):
            start = pl.multiple_of(j * tq, tq)
            kb = k_ref[pl.ds(start, tq), :].astype(BF16)
            v_st = _stack_heads(v_ref[pl.ds(start, tq), :], hms)
            z = lax.dot_general(q_st, kb, _NT, preferred_element_type=F32)
            lk, lb = _softplus_parts(z)
            if masked:
                lk = lk * vis_ref[...]
            later = jnp.dot(lk.astype(BF16), after_ref[...], preferred_element_type=F32)
            a = jnp.exp(lb + later + c)
            if masked:
                a = a * vis_ref[...]
            acc = acc + jnp.dot(_heads_side_by_side(a.astype(BF16), tq), v_st, preferred_element_type=F32)
            return c + jnp.sum(lk, axis=1, keepdims=True), acc

        def more(st):
            return jnp.logical_and(st[0] < i, jnp.max(st[1]) > -EXP_UNDERFLOW)

        def step(st):
            c, acc = tile(i - 1 - st[0], st[1], st[2], False)
            return st[0] + 1, c, acc

        c, acc = tile(i, jnp.zeros((2 * tq, 1), F32), jnp.zeros((tq, LANES), F32), True)
        n, c, acc = lax.while_loop(more, step, (jnp.int32(0), c, acc))
        o_ref[...] = acc.astype(BF16)
        t_ref[...] = jnp.where(hms[0], c[:tq], c[tq:])
        n_ref[...] = jnp.zeros((8, LANES), F32) + n.astype(F32)

    grid = (n_pairs, nq)
    body, ex_in, ex_out, ex_scratch = _hosted(exchange, grid, body)
    if exchange is not None:
        body = functools.partial(body, n_own_in=5, n_own_out=3)
    hbm = pl.BlockSpec(memory_space=pl.ANY)
    seq = lambda off: pl.BlockSpec((s, LANES), lambda p, i: (0, cb0 + off + p))
    whole = lambda rows: pl.BlockSpec((rows, tq), lambda p, i: (0, 0))
    vis, m_after, _ = _sb_tile_masks(tq)
    outs = pl.pallas_call(
        body, name=name,
        out_shape=(jax.ShapeDtypeStruct((s, n_pairs * LANES), BF16), jax.ShapeDtypeStruct((n_pairs, s, LANES), F32),
                   jax.ShapeDtypeStruct((n_pairs, nq * 8, LANES), F32), *ex_out),
        grid=grid,
        in_specs=[pl.BlockSpec((tq, LANES), lambda p, i: (i, cb0 + p)), seq(n_pairs), seq(2 * n_pairs),
                  whole(2 * tq), whole(tq)] + [hbm] * len(ex_in),
        out_specs=(pl.BlockSpec((tq, LANES), lambda p, i: (i, p)), pl.BlockSpec((None, tq, LANES), lambda p, i: (p, i, 0)),
                   pl.BlockSpec((None, 8, LANES), lambda p, i: (p, i, 0)), *([hbm] * len(ex_out))),
        scratch_shapes=ex_scratch,
        compiler_params=_params(("arbitrary", "arbitrary") if exchange else ("parallel", "arbitrary")),
    )(proj, proj, proj, vis, m_after, *ex_in)
    return outs[0], outs[1], outs[2], list(outs[3:])


def _sb_bwd(proj, t_sum, n_walked, dua, *, col0, n_pairs, do_col0, name, tq=256, exchange=None):
    s = proj.shape[0]
    tq = _pick(s, tq)
    cb0 = col0 // LANES
    dcb0 = do_col0 // LANES
    scale = SB_HEAD_DIM ** -0.5

    def body(q_ref, k_ref, v_ref, t_ref, n_ref, do_ref, vis_ref, after_ref, before_ref, dq_ref, dk_ref, dv_ref):
        i = pl.program_id(1)

        @pl.when(i == 0)
        def _():
            dk_ref[...] = jnp.zeros_like(dk_ref)
            dv_ref[...] = jnp.zeros_like(dv_ref)

        hms = _head_masks()
        q_st = _stack_heads(q_ref[...] * scale, hms)
        do_st = _stack_heads(do_ref[...], hms)
        t_st = jnp.concatenate([t_ref[:, SB_HEAD_DIM * h:SB_HEAD_DIM * h + 1] for h in range(2)], axis=0)
        first = i - jnp.max(n_ref[...]).astype(jnp.int32)

        def tile(j, p_sum, g_sum, dq, masked):
            start = pl.multiple_of(j * tq, tq)
            k = k_ref[pl.ds(start, tq), :]
            z = lax.dot_general(q_st, k.astype(BF16), _NT, preferred_element_type=F32)
            lk_raw, lb = _softplus_parts(z)
            lk = lk_raw * vis_ref[...] if masked else lk_raw
            p_next = p_sum + jnp.sum(lk, axis=1, keepdims=True)
            later = jnp.dot(lk.astype(BF16), after_ref[...], preferred_element_type=F32)
            a = jnp.exp(lb + (t_st - p_next) + later)
            if masked:
                a = a * vis_ref[...]
            da = lax.dot_general(do_st, v_ref[pl.ds(start, tq), :].astype(BF16), _NT, preferred_element_type=F32)
            g = a * da
            g_before = g_sum + jnp.dot(g.astype(BF16), before_ref[...], preferred_element_type=F32)
            dz = g * jnp.exp(lk_raw) - g_before * jnp.exp(lb)
            if masked:
                dz = dz * vis_ref[...]
            dzb = dz.astype(BF16)
            dv_ref[pl.ds(start, tq), :] += lax.dot_general(a.astype(BF16), do_st, _TN, preferred_element_type=F32)
            dk_ref[pl.ds(start, tq), :] += lax.dot_general(dzb, q_st, _TN, preferred_element_type=F32)
            dq = dq + jnp.dot(_heads_side_by_side(dzb, tq), _stack_heads(k, hms), preferred_element_type=F32)
            return p_next, g_sum + jnp.sum(g, axis=1, keepdims=True), dq

        zero = jnp.zeros((2 * tq, 1), F32)
        st = lax.fori_loop(first, i, lambda j, st: tile(j, *st, False), (zero, zero, jnp.zeros((tq, LANES), F32)))
        dq_ref[...] = tile(i, *st, True)[2] * scale

    grid = (n_pairs, s // tq)
    body, ex_in, ex_out, ex_scratch = _hosted(exchange, grid, body)
    if exchange is not None:
        body = functools.partial(body, n_own_in=9, n_own_out=3)
    hbm = pl.BlockSpec(memory_space=pl.ANY)
    seq = lambda off: pl.BlockSpec((s, LANES), lambda p, i: (0, cb0 + off + p))
    whole = lambda rows: pl.BlockSpec((rows, tq), lambda p, i: (0, 0))
    out = jax.ShapeDtypeStruct((s, n_pairs * LANES), F32)
    res = pl.BlockSpec((s, LANES), lambda p, i: (0, p))
    outs = pl.pallas_call(
        body, name=name,
        out_shape=(out, out, out, *ex_out),
        grid=grid,
        in_specs=[pl.BlockSpec((tq, LANES), lambda p, i: (i, cb0 + p)), seq(n_pairs), seq(2 * n_pairs),
                  pl.BlockSpec((None, tq, LANES), lambda p, i: (p, i, 0)),
                  pl.BlockSpec((None, 8, LANES), lambda p, i: (p, i, 0)),
                  pl.BlockSpec((tq, LANES), lambda p, i: (i, dcb0 + p)),
                  whole(2 * tq), whole(tq), whole(tq)] + [hbm] * len(ex_in),
        out_specs=(pl.BlockSpec((tq, LANES), lambda p, i: (i, p)), res, res, *([hbm] * len(ex_out))),
        scratch_shapes=ex_scratch,
        compiler_params=_params(("arbitrary", "arbitrary")),
    )(proj, proj, proj, t_sum, n_walked, dua, *_sb_tile_masks(tq), *ex_in)
    return outs[0], outs[1], outs[2], list(outs[3:])


def _mem_attn_fwd(qm, km, vm, *, name, tq=512):
    s, d = qm.shape
    heads = d // MEM_HEAD_DIM
    mlen = km.shape[0]
    tq = _pick(s, tq)
    scale = MEM_HEAD_DIM ** -0.5

    def body(q_ref, k_ref, v_ref, o_ref):
        for h in range(heads):
            sl = slice(h * MEM_HEAD_DIM, (h + 1) * MEM_HEAD_DIM)
            q = (q_ref[:, sl] * scale).astype(BF16)
            sc = lax.dot_general(q, k_ref[:, sl].astype(BF16), _NT, preferred_element_type=F32)
            e = jnp.exp(sc - jnp.max(sc, axis=1, keepdims=True))
            p = e / jnp.sum(e, axis=1, keepdims=True)
            o_ref[:, sl] = jnp.dot(p.astype(BF16), v_ref[:, sl].astype(BF16), preferred_element_type=F32).astype(BF16)

    tok = pl.BlockSpec((tq, d), lambda i: (i, 0))
    kv = pl.BlockSpec((mlen, d), lambda i: (0, 0))
    return pl.pallas_call(body, name=name, out_shape=jax.ShapeDtypeStruct((s, d), BF16), grid=(s // tq,),
                          in_specs=[tok, kv, kv], out_specs=tok, compiler_params=_params(("parallel",)))(qm, km, vm)


def _mem_attn_bwd(qm, km, vm, do, *, name, tq=512):
    s, d = qm.shape
    heads = d // MEM_HEAD_DIM
    mlen = km.shape[0]
    tq = _pick(s, tq)
    scale = MEM_HEAD_DIM ** -0.5

    def body(q_ref, k_ref, v_ref, do_ref, dq_ref, dk_ref, dv_ref):
        @pl.when(pl.program_id(0) == 0)
        def _():
            dk_ref[...] = jnp.zeros_like(dk_ref)
            dv_ref[...] = jnp.zeros_like(dv_ref)

        for h in range(heads):
            sl = slice(h * MEM_HEAD_DIM, (h + 1) * MEM_HEAD_DIM)
            q = (q_ref[:, sl] * scale).astype(BF16)
            k = k_ref[:, sl].astype(BF16)
            v = v_ref[:, sl].astype(BF16)
            sc = lax.dot_general(q, k, _NT, preferred_element_type=F32)
            e = jnp.exp(sc - jnp.max(sc, axis=1, keepdims=True))
            p = e / jnp.sum(e, axis=1, keepdims=True)
            dob = do_ref[:, sl].astype(BF16)
            dv_ref[:, sl] += lax.dot_general(p.astype(BF16), dob, _TN, preferred_element_type=F32)
            dp = lax.dot_general(dob, v, _NT, preferred_element_type=F32)
            ds = (p * (dp - jnp.sum(dp * p, axis=1, keepdims=True))).astype(BF16)
            dq_ref[:, sl] = (jnp.dot(ds, k, preferred_element_type=F32) * scale).astype(BF16)
            dk_ref[:, sl] += lax.dot_general(ds, q, _TN, preferred_element_type=F32)

    tok = pl.BlockSpec((tq, d), lambda i: (i, 0))
    kv = pl.BlockSpec((mlen, d), lambda i: (0, 0))
    return pl.pallas_call(
        body, name=name,
        out_shape=(jax.ShapeDtypeStruct((s, d), BF16), jax.ShapeDtypeStruct((mlen, d), F32),
                   jax.ShapeDtypeStruct((mlen, d), F32)),
        grid=(s // tq,), in_specs=[tok, kv, kv, tok], out_specs=(tok, kv, kv),
        compiler_params=_params(("arbitrary",)),
    )(qm, km, vm, do)


def _ffn_act_fwd(up, w, b, *, name, exchange=None):
    s, two_f = up.shape
    ff = two_f // 2
    nfb = ff // LANES
    kw = w.shape[0]
    pad = 8
    chunks = _row_chunks(s)

    def body(v_ref, g_ref, wv_ref, wg_ref, bv_ref, bg_ref, o_ref, vp_ref, gp_ref):
        vp_ref[pl.ds(0, pad), :] = jnp.zeros((pad, LANES), F32)
        gp_ref[pl.ds(0, pad), :] = jnp.zeros((pad, LANES), F32)
        for r0, rc in chunks:
            vp_ref[pl.ds(pad + r0, rc), :] = v_ref[pl.ds(r0, rc), :].astype(F32)
            gp_ref[pl.ds(pad + r0, rc), :] = g_ref[pl.ds(r0, rc), :].astype(F32)
        for r0, rc in chunks:
            vc = jnp.zeros((rc, LANES), F32) + bv_ref[...]
            gc = jnp.zeros((rc, LANES), F32) + bg_ref[...]
            for k in range(kw):
                off = pad + r0 - (kw - 1) + k
                vc = vc + wv_ref[pl.ds(k, 1), :] * vp_ref[pl.ds(off, rc), :]
                gc = gc + wg_ref[pl.ds(k, 1), :] * gp_ref[pl.ds(off, rc), :]
            o_ref[pl.ds(r0, rc), :] = (gc * _sigmoid(gc) * vc).astype(BF16)

    col = lambda off: pl.BlockSpec((s, LANES), lambda c: (0, off + c))
    tap = lambda off: pl.BlockSpec((kw, LANES), lambda c: (0, off + c))
    vec = lambda off: pl.BlockSpec((1, LANES), lambda c: (0, off + c))
    body, ex_in, ex_out, ex_scratch = _hosted(exchange, (nfb,), body)
    if exchange is not None:
        body = functools.partial(body, n_own_in=6, n_own_out=1)
    hbm = pl.BlockSpec(memory_space=pl.ANY)
    outs = pl.pallas_call(
        body, name=name, out_shape=(jax.ShapeDtypeStruct((s, ff), BF16), *ex_out), grid=(nfb,),
        in_specs=[col(0), col(nfb), tap(0), tap(nfb), vec(0), vec(nfb)] + [hbm] * len(ex_in),
        out_specs=(col(0), *([hbm] * len(ex_out))),
        scratch_shapes=[pltpu.VMEM((s + pad, LANES), F32), pltpu.VMEM((s + pad, LANES), F32)] + ex_scratch,
        compiler_params=_params(("arbitrary",) if exchange else ("parallel",)),
    )(up, up, w, w, b.reshape(1, two_f), b.reshape(1, two_f), *ex_in)
    return outs[0], list(outs[1:])


def _ffn_act_bwd(up, dact, w, b, *, name, exchange=None):
    s, two_f = up.shape
    ff = two_f // 2
    nfb = ff // LANES
    kw = w.shape[0]
    pad = 8
    chunks = _row_chunks(s)

    def body(v_ref, g_ref, d_ref, wv_ref, wg_ref, bv_ref, bg_ref, dv_ref, dg_ref, dwv_ref, dwg_ref, dbv_ref, dbg_ref,
             vp_ref, gp_ref, dvc_ref, dgc_ref):
        vp_ref[pl.ds(0, pad), :] = jnp.zeros((pad, LANES), F32)
        gp_ref[pl.ds(0, pad), :] = jnp.zeros((pad, LANES), F32)
        dvc_ref[pl.ds(s, pad), :] = jnp.zeros((pad, LANES), F32)
        dgc_ref[pl.ds(s, pad), :] = jnp.zeros((pad, LANES), F32)
        for r0, rc in chunks:
            vp_ref[pl.ds(pad + r0, rc), :] = v_ref[pl.ds(r0, rc), :].astype(F32)
            gp_ref[pl.ds(pad + r0, rc), :] = g_ref[pl.ds(r0, rc), :].astype(F32)
        dwv = [jnp.zeros((1, LANES), F32) for _ in range(kw)]
        dwg = [jnp.zeros((1, LANES), F32) for _ in range(kw)]
        dbv = jnp.zeros((1, LANES), F32)
        dbg = jnp.zeros((1, LANES), F32)
        for r0, rc in chunks:
            vc = jnp.zeros((rc, LANES), F32) + bv_ref[...]
            gc = jnp.zeros((rc, LANES), F32) + bg_ref[...]
            for k in range(kw):
                off = pad + r0 - (kw - 1) + k
                vc = vc + wv_ref[pl.ds(k, 1), :] * vp_ref[pl.ds(off, rc), :]
                gc = gc + wg_ref[pl.ds(k, 1), :] * gp_ref[pl.ds(off, rc), :]
            sg = _sigmoid(gc)
            d = d_ref[pl.ds(r0, rc), :].astype(F32)
            dvc = d * (gc * sg)
            dgc = d * vc * (sg * (1.0 + gc * (1.0 - sg)))
            dvc_ref[pl.ds(r0, rc), :] = dvc
            dgc_ref[pl.ds(r0, rc), :] = dgc
            dbv = dbv + jnp.sum(dvc, axis=0, keepdims=True)
            dbg = dbg + jnp.sum(dgc, axis=0, keepdims=True)
            for k in range(kw):
                off = pad + r0 - (kw - 1) + k
                dwv[k] = dwv[k] + jnp.sum(dvc * vp_ref[pl.ds(off, rc), :], axis=0, keepdims=True)
                dwg[k] = dwg[k] + jnp.sum(dgc * gp_ref[pl.ds(off, rc), :], axis=0, keepdims=True)
        for r0, rc in chunks:
            dv = jnp.zeros((rc, LANES), F32)
            dg = jnp.zeros((rc, LANES), F32)
            for k in range(kw):
                off = r0 + (kw - 1) - k
                dv = dv + wv_ref[pl.ds(k, 1), :] * dvc_ref[pl.ds(off, rc), :]
                dg = dg + wg_ref[pl.ds(k, 1), :] * dgc_ref[pl.ds(off, rc), :]
            dv_ref[pl.ds(r0, rc), :] = dv.astype(BF16)
            dg_ref[pl.ds(r0, rc), :] = dg.astype(BF16)
        for k in range(kw):
            dwv_ref[pl.ds(k, 1), :] = dwv[k]
            dwg_ref[pl.ds(k, 1), :] = dwg[k]
        dbv_ref[...] = dbv
        dbg_ref[...] = dbg

    col = lambda off: pl.BlockSpec((s, LANES), lambda c: (0, off + c))
    tap = lambda off: pl.BlockSpec((kw, LANES), lambda c: (0, off + c))
    vec = lambda off: pl.BlockSpec((1, LANES), lambda c: (0, off + c))
    big = lambda: pltpu.VMEM((s + pad, LANES), F32)
    body, ex_in, ex_out, ex_scratch = _hosted(exchange, (nfb,), body)
    if exchange is not None:
        body = functools.partial(body, n_own_in=7, n_own_out=6)
    hbm = pl.BlockSpec(memory_space=pl.ANY)
    outs = pl.pallas_call(
        body, name=name,
        out_shape=(jax.ShapeDtypeStruct((s, ff), BF16), jax.ShapeDtypeStruct((s, ff), BF16),
                   jax.ShapeDtypeStruct((kw, ff), F32), jax.ShapeDtypeStruct((kw, ff), F32),
                   jax.ShapeDtypeStruct((1, ff), F32), jax.ShapeDtypeStruct((1, ff), F32), *ex_out),
        grid=(nfb,),
        in_specs=[col(0), col(nfb), col(0), tap(0), tap(nfb), vec(0), vec(nfb)] + [hbm] * len(ex_in),
        out_specs=(col(0), col(0), tap(0), tap(0), vec(0), vec(0), *([hbm] * len(ex_out))),
        scratch_shapes=[big(), big(), big(), big()] + ex_scratch,
        compiler_params=_params(("arbitrary",) if exchange else ("parallel",)),
    )(up, up, dact, w, w, b.reshape(1, two_f), b.reshape(1, two_f), *ex_in)
    dv, dg, dwv, dwg, dbv, dbg = outs[:6]
    return (dv, dg), jnp.concatenate([dwv, dwg], axis=1), jnp.concatenate([dbv, dbg], axis=1).reshape(two_f), list(outs[6:])


def _sum_parts(parts, *, name, tr=256):
    n_parts, rows, cols = parts.shape
    tr = _pick(rows, tr, 16)

    def body(p_ref, o_ref):
        g = p_ref[0].astype(F32)
        for k in range(1, n_parts):
            g = g + p_ref[k].astype(F32)
        o_ref[...] = g

    return pl.pallas_call(
        body, name=name, out_shape=jax.ShapeDtypeStruct((rows, cols), F32), grid=(rows // tr,),
        in_specs=[pl.BlockSpec((n_parts, tr, cols), lambda i: (0, i, 0))],
        out_specs=pl.BlockSpec((tr, cols), lambda i: (i, 0)), compiler_params=_params(("parallel",)),
    )(parts)


def _sum_adamw(parts, w, m, v, *, layer, so_far, name, tr=256):
    n_parts, rows, cols = parts.shape
    depth = w.shape[0]
    tr = _pick(rows, tr, 16)
    c1 = 1.0 / (1.0 - ADAM_B1 ** ADAM_STEP)
    c2 = 1.0 / (1.0 - ADAM_B2 ** ADAM_STEP)

    def body(p_ref, w_ref, m_ref, v_ref, *rest):
        g_ref, d_ref, nm_ref, nv_ref = rest[-4:]
        g = p_ref[0].astype(F32)
        for k in range(1, n_parts):
            g = g + p_ref[k].astype(F32)
        nm = ADAM_B1 * m_ref[...] + (1.0 - ADAM_B1) * g
        nv = ADAM_B2 * v_ref[...] + (1.0 - ADAM_B2) * (g * g)
        g_ref[...] = g
        nm_ref[...] = nm
        nv_ref[...] = nv
        d_ref[...] = -ADAM_LR * ((nm * c1) / (jnp.sqrt(nv * c2) + ADAM_EPS) + ADAM_WD * w_ref[...])

    blk = pl.BlockSpec((None, tr, cols), lambda i: (layer, i, 0))
    out = jax.ShapeDtypeStruct((depth, rows, cols), F32)
    kept = list(so_far) if so_far is not None else []
    return pl.pallas_call(
        body, name=name, out_shape=(out, out, out, out), grid=(rows // tr,),
        in_specs=[pl.BlockSpec((n_parts, tr, cols), lambda i: (0, i, 0)), blk, blk, blk]
        + [pl.BlockSpec(memory_space=pl.ANY)] * len(kept),
        out_specs=(blk, blk, blk, blk),
        input_output_aliases={4 + k: k for k in range(len(kept))},
        compiler_params=_params(("parallel",)),
    )(parts, w, m, v, *kept)


def _mesh_pos():
    return lax.axis_index("x"), lax.axis_index("y"), lax.axis_index("c")


def _flip(pos, k):
    x, y, c = pos
    return (1 - x if k & 4 else x, 1 - y if k & 2 else y, 1 - c if k & 1 else c)


def _dev_index(pos):
    return 4 * pos[0] + 2 * pos[1] + pos[2]


N_PEERS = N_DEV - 1


class _Exchange:
    def __init__(self, inputs, out_shapes, start, finish, scratch_shapes=None):
        n = len(inputs)
        self.inputs, self.out_shapes, self.start, self.finish = list(inputs), list(out_shapes), start, finish
        self.scratch_shapes = scratch_shapes or [pltpu.SemaphoreType.DMA((n * N_PEERS,)),
                                                 pltpu.SemaphoreType.DMA((n * N_PEERS,)), pltpu.SemaphoreType.DMA((n,))]


def _both(ex1, ex2):
    if ex1 is None or ex2 is None:
        return ex1 or ex2
    n_in, n_out, n_sems = len(ex1.inputs), len(ex1.out_shapes), len(ex1.scratch_shapes)

    def halves(ins, outs, sems):
        return (ins[:n_in], outs[:n_out], sems[:n_sems]), (ins[n_in:], outs[n_out:], sems[n_sems:])

    def start(ins, outs, sems):
        a, b = halves(ins, outs, sems)
        ex1.start(*a)
        ex2.start(*b)

    def finish(ins, outs, sems):
        a, b = halves(ins, outs, sems)
        ex1.finish(*a)
        ex2.finish(*b)

    return _Exchange(ex1.inputs + ex2.inputs, ex1.out_shapes + ex2.out_shapes, start, finish,
                     ex1.scratch_shapes + ex2.scratch_shapes)


def _gather_exchange(xs):
    n = len(xs)

    def plan(x_refs, out_refs, sems):
        send_sems, recv_sems, local_sems = sems
        me = _mesh_pos()
        sibling = _flip(me, 1)
        chips = [_flip(me, 4), _flip(me, 2), _flip(me, 6)]

        def copy(a, k, block, to, from_input=False):
            slot = out_refs[a].at[_dev_index(block)]
            return pltpu.make_async_remote_copy(
                src_ref=x_refs[a] if from_input else slot, dst_ref=slot,
                send_sem=send_sems.at[a * N_PEERS + k], recv_sem=recv_sems.at[a * N_PEERS + k],
                device_id=to, device_id_type=pl.DeviceIdType.MESH)

        mine = [pltpu.make_async_copy(x_refs[a], out_refs[a].at[_dev_index(me)], local_sems.at[a]) for a in range(n)]
        first = [copy(a, 0, me, sibling, True) for a in range(n)]
        first += [copy(a, 1 + j, me, chip, True) for j, chip in enumerate(chips) for a in range(n)]
        return me, sibling, chips, copy, mine, first

    def start(x_refs, out_refs, sems):
        _, _, _, _, mine, first = plan(x_refs, out_refs, sems)
        for cp in mine + first:
            cp.start()

    def finish(x_refs, out_refs, sems):
        me, sibling, chips, copy, mine, first = plan(x_refs, out_refs, sems)
        passed = []
        for j, chip in enumerate(chips):
            for a in range(n):
                copy(a, 1 + j, chip, me).wait_recv()
                passed.append(copy(a, 4 + j, chip, sibling))
                passed[-1].start()
        for a in range(n):
            copy(a, 0, sibling, me).wait_recv()
        for j, chip in enumerate(chips):
            for a in range(n):
                copy(a, 4 + j, _flip(chip, 1), me).wait_recv()
        for cp in first + passed:
            cp.wait_send()
        for cp in mine:
            cp.wait()

    return _Exchange(xs, [jax.ShapeDtypeStruct((N_DEV,) + x.shape, x.dtype) for x in xs], start, finish)


def _scatter_exchange(xs):
    n = len(xs)

    def plan(x_refs, out_refs, sems):
        send_sems, recv_sems, local_sems = sems
        me = _mesh_pos()
        my_slot = _dev_index(me)

        def copy(a, k):
            peer = _flip(me, k)
            return pltpu.make_async_remote_copy(
                src_ref=x_refs[a].at[_dev_index(peer)], dst_ref=out_refs[a].at[my_slot],
                send_sem=send_sems.at[a * N_PEERS + k - 1], recv_sem=recv_sems.at[a * N_PEERS + k - 1],
                device_id=peer, device_id_type=pl.DeviceIdType.MESH)

        mine = [pltpu.make_async_copy(x_refs[a].at[my_slot], out_refs[a].at[my_slot], local_sems.at[a]) for a in range(n)]
        return mine, [copy(a, k) for k in range(1, N_DEV) for a in range(n)]

    def start(x_refs, out_refs, sems):
        mine, copies = plan(x_refs, out_refs, sems)
        for cp in mine + copies:
            cp.start()

    def finish(x_refs, out_refs, sems):
        mine, copies = plan(x_refs, out_refs, sems)
        for cp in copies:
            cp.wait_recv()
        for cp in copies:
            cp.wait_send()
        for cp in mine:
            cp.wait()

    return _Exchange(xs, [jax.ShapeDtypeStruct(x.shape, x.dtype) for x in xs], start, finish)


def _run_exchanges(exchanges, *, name):
    counts = [len(ex.inputs) for ex in exchanges]
    n = sum(counts)

    def body(*refs):
        offsets = [sum(counts[:e]) for e in range(len(exchanges))]
        views = [(refs[o:o + c], refs[n + o:n + o + c], refs[2 * n + 3 * e:2 * n + 3 * e + 3])
                 for e, (o, c) in enumerate(zip(offsets, counts))]
        for ex, view in zip(exchanges, views):
            ex.start(*view)
        for ex, view in zip(exchanges, views):
            ex.finish(*view)

    hbm = pl.BlockSpec(memory_space=pl.ANY)
    outs = pl.pallas_call(
        body, name=name, out_shape=tuple(s for ex in exchanges for s in ex.out_shapes), in_specs=[hbm] * n,
        out_specs=tuple([hbm] * n), scratch_shapes=[s for ex in exchanges for s in ex.scratch_shapes],
    )(*[x for ex in exchanges for x in ex.inputs])
    return [list(outs[sum(counts[:e]):sum(counts[:e + 1])]) for e in range(len(exchanges))]


def _pack(arrays):
    flat = jnp.concatenate([a.reshape(-1) for a in arrays])
    n = flat.shape[0]
    tile = PACK_W * PACK_ROW_ALIGN
    total = -(-n // tile) * tile
    return jnp.pad(flat, (0, total - n)).reshape(total // PACK_W, PACK_W)


def _unpack(buf, shapes):
    lead = buf.shape[:-2]
    flat = buf.reshape(lead + (-1,))
    out, off = [], 0
    for shp in shapes:
        n = 1
        for dim in shp:
            n *= dim
        out.append(flat[..., off:off + n].reshape(lead + tuple(shp)))
        off += n
    return out


def _join_columns(blocks):
    return jnp.moveaxis(blocks, 0, 2).reshape(blocks.shape[1], blocks.shape[2], -1)


def _mm_hosting(a, b, exchange, **kw):
    if exchange is None:
        return _mm(a, b, **kw), []
    return _mm(a, b, exchange=exchange, **kw)


GATHERED_BY_ATTENTION = ['w_out', 'mem_wq', 'mem_wk', 'mem_wv', 'mem_wo', 'ffn_up']
GATHERED_BY_CONV = ['ffn_down']


def _as_matrix(blocks):
    return blocks.reshape(-1, blocks.shape[-1])


def _layer_fwd(x, xb, memb, w, alpha, shards, next_w_in):
    w = dict(w)
    cc = w['conv_w'].shape[1]
    n_pairs = (N_DEV * shards['w_out'].shape[0] - cc) // LANES

    proj = _mm(xb, w['w_in'], tb=True, name="mm_proj")
    u1, got = _glu_conv_fwd(proj, w['conv_w'], w['conv_b'], cc=cc, name="glu_conv_fwd",
                            exchange=_gather_exchange([shards[n] for n in GATHERED_BY_CONV]))
    w.update({n: _as_matrix(f) for n, f in zip(GATHERED_BY_CONV, got)})
    u = _cln_silu_fwd(u1, w['conv_ln_g'], w['conv_ln_b'], name="cln_silu_fwd")
    att, t_sum, n_walked, got = _sb_fwd(proj, col0=2 * cc, n_pairs=n_pairs, name="sb_fwd",
                                        exchange=_gather_exchange([shards[n] for n in GATHERED_BY_ATTENTION]))
    w.update({n: _as_matrix(f) for n, f in zip(GATHERED_BY_ATTENTION, got)})
    ua = jnp.concatenate([u, att], axis=1)
    r1, x1, x1b = _mm_res_ln(ua, w['w_out'], x, w['ln1_g'], w['ln1_b'], alpha=alpha, name="mm_mix_ln")
    qm = _mm(x1b, w['mem_wq'], name="mm_memq")
    km = _mm(memb, w['mem_wk'], name="mm_memkv")
    vm = _mm(memb, w['mem_wv'], name="mm_memkv")
    o = _mem_attn_fwd(qm, km, vm, name="mem_attn_fwd")
    r2, x2, x2b = _mm_res_ln(o, w['mem_wo'], x1, w['ln2_g'], w['ln2_b'], alpha=alpha, name="mm_mix_ln")
    up = _mm(x2b, w['ffn_up'], tb=True, out_dtype=BF16, name="mm_up")
    act, got = _ffn_act_fwd(up, w['ffn_conv_w'], w['ffn_conv_b'], name="ffn_act_fwd",
                            exchange=_gather_exchange([next_w_in]) if next_w_in is not None else None)
    r3, x3, x3b = _mm_res_ln(act, w['ffn_down'], x2, w['ln3_g'], w['ln3_b'], alpha=alpha, name="mm_down_ln")
    saved = dict(xb=xb, proj=proj, u1=u1, t_sum=t_sum, n_walked=n_walked, ua=ua, r1=r1, x1b=x1b, qm=qm, km=km, vm=vm,
                 o=o, r2=r2, x2b=x2b, up=up, act=act, r3=r3)
    return x3, x3b, saved, w, _as_matrix(got[0]) if got else None


def _layer_bwd(top, sv, memb, w, alpha, carried, also_during_attention=None):
    g, received = {}, {}
    cc = w['conv_w'].shape[1]
    n_pairs = (w['w_out'].shape[0] - cc) // LANES

    def sending(sends):
        ex = _scatter_exchange([gm.reshape(N_DEV, -1, gm.shape[-1]) for _, gm in sends]) if sends else None
        return ex, lambda got: received.update({key: blocks for (key, _), blocks in zip(sends, got)})

    if top[1] is None:
        dr3, dr3b, g['ln3_g'], g['ln3_b'] = _ln_bwd(top[0], None, sv['r3'], w['ln3_g'], alpha=alpha, name="ln_bwd")
    else:
        dr3, dr3b, g['ln3_g'], g['ln3_b'] = _mm_ln_bwd(*top, sv['r3'], w['ln3_g'], alpha=alpha, name="mm_dx_in_ln")
    g_down = _mm(sv['act'], dr3b, ta=True, out_dtype=BF16, name="mm_dw_down")
    dact = _mm(dr3b, w['ffn_down'], tb=True, out_dtype=BF16, name="mm_dact")
    ex, file = sending(carried + [('ffn_down', g_down)])
    dup, g['ffn_conv_w'], g['ffn_conv_b'], got = _ffn_act_bwd(sv['up'], dact, w['ffn_conv_w'], w['ffn_conv_b'],
                                                                name="ffn_act_bwd", exchange=ex)
    file(got)
    g_up = jnp.concatenate([_mm(half, sv['x2b'], ta=True, out_dtype=BF16, name="mm_dw_up") for half in dup], axis=0)
    dr2, dr2b, g['ln2_g'], g['ln2_b'] = _mm_ln_bwd(dup, w['ffn_up'], dr3, sv['r2'], w['ln2_g'], alpha=alpha, name="mm_dx_up_ln")
    g_wo = _mm(sv['o'], dr2b, ta=True, out_dtype=BF16, name="mm_dw_sq")
    do = _mm(dr2b, w['mem_wo'], tb=True, out_dtype=BF16, name="mm_dx_sq")
    dqm, dkm, dvm = _mem_attn_bwd(sv['qm'], sv['km'], sv['vm'], do, name="mem_attn_bwd")
    g_wq = _mm(sv['x1b'], dqm, ta=True, out_dtype=BF16, name="mm_dw_sq")
    g_wk = _mm(memb, dkm, ta=True, out_dtype=BF16, name="mm_dw_memkv")
    g_wv = _mm(memb, dvm, ta=True, out_dtype=BF16, name="mm_dw_memkv")
    dr1, dr1b, g['ln1_g'], g['ln1_b'] = _mm_ln_bwd(dqm, w['mem_wq'], dr2, sv['r1'], w['ln1_g'], tb=True, alpha=alpha,
                                                    name="mm_dx_sq_ln")
    g_out = _mm(sv['ua'], dr1b, ta=True, out_dtype=BF16, name="mm_dw_sq")
    dua = _mm(dr1b, w['w_out'], tb=True, name="mm_dx_sq")
    du1, g['conv_ln_g'], g['conv_ln_b'] = _cln_silu_bwd(dua, sv['u1'], w['conv_ln_g'], w['conv_ln_b'], name="cln_silu_bwd")
    ex, file = sending([('mem_wo', g_wo), ('mem_wq', g_wq), ('w_out', g_out)])
    dga, dgg, g['conv_w'], g['conv_b'], got = _glu_conv_bwd(du1, sv['proj'], w['conv_w'], cc=cc, name="glu_conv_bwd",
                                                             exchange=ex)
    file(got)
    ex, file = sending([('ffn_up', g_up), ('mem_wk', g_wk), ('mem_wv', g_wv)])
    extra = also_during_attention(g) if also_during_attention else None
    dq, dk, dv, got = _sb_bwd(sv['proj'], sv['t_sum'], sv['n_walked'], dua, col0=2 * cc, n_pairs=n_pairs, do_col0=cc,
                              name="sb_bwd", exchange=_both(ex, extra))
    file(got[:len(ex.inputs)])
    also_got = got[len(ex.inputs):]
    dproj = jnp.concatenate([dga, dgg, dq.astype(BF16), dk.astype(BF16), dv.astype(BF16)], axis=1)
    g_in = _mm(dproj, sv['xb'], ta=True, out_dtype=BF16, name="mm_dw_in")
    return (dproj, w['w_in'], dr1), g, received, [('w_in', g_in)], also_got


def kernel(x, mem, w_in, conv_w, conv_b, conv_ln_g, conv_ln_b, w_out, ln1_g, ln1_b, mem_wq, mem_wk, mem_wv, mem_wo, ln2_g, ln2_b, ffn_up, ffn_conv_w, ffn_conv_b, ffn_down, ln3_g, ln3_b, loss_target, m_w_in, m_conv_w, m_conv_b, m_conv_ln_g, m_conv_ln_b, m_w_out, m_ln1_g, m_ln1_b, m_mem_wq, m_mem_wk, m_mem_wv, m_mem_wo, m_ln2_g, m_ln2_b, m_ffn_up, m_ffn_conv_w, m_ffn_conv_b, m_ffn_down, m_ln3_g, m_ln3_b, v_w_in, v_conv_w, v_conv_b, v_conv_ln_g, v_conv_ln_b, v_w_out, v_ln1_g, v_ln1_b, v_mem_wq, v_mem_wk, v_mem_wv, v_mem_wo, v_ln2_g, v_ln2_b, v_ffn_up, v_ffn_conv_w, v_ffn_conv_b, v_ffn_down, v_ln3_g, v_ln3_b):
    wts = dict(zip(WEIGHTS, (w_in, conv_w, conv_b, conv_ln_g, conv_ln_b, w_out, ln1_g, ln1_b, mem_wq, mem_wk, mem_wv,
                             mem_wo, ln2_g, ln2_b, ffn_up, ffn_conv_w, ffn_conv_b, ffn_down, ln3_g, ln3_b)))
    mom = dict(zip(WEIGHTS, (m_w_in, m_conv_w, m_conv_b, m_conv_ln_g, m_conv_ln_b, m_w_out, m_ln1_g, m_ln1_b, m_mem_wq,
                             m_mem_wk, m_mem_wv, m_mem_wo, m_ln2_g, m_ln2_b, m_ffn_up, m_ffn_conv_w, m_ffn_conv_b,
                             m_ffn_down, m_ln3_g, m_ln3_b)))
    var = dict(zip(WEIGHTS, (v_w_in, v_conv_w, v_conv_b, v_conv_ln_g, v_conv_ln_b, v_w_out, v_ln1_g, v_ln1_b, v_mem_wq,
                             v_mem_wk, v_mem_wv, v_mem_wo, v_ln2_g, v_ln2_b, v_ffn_up, v_ffn_conv_w, v_ffn_conv_b,
                             v_ffn_down, v_ln3_g, v_ln3_b)))
    depth = w_in.shape[0]
    alpha = (2.0 * depth) ** 0.25
    my_index = _dev_index(_mesh_pos())

    def row_blocks(src, n, col_sharded):
        return jnp.swapaxes(src[n], 1, 2) if col_sharded else src[n]

    bf16_blocks = {n: row_blocks(wts, n, cs).astype(BF16) for n, cs in MATRICES}
    shards = [{n: bf16_blocks[n][l] for n, _ in MATRICES} for l in range(depth)]
    tap_shapes = [wts[n].shape for n in TAPS]
    (gathered_taps, got), = _run_exchanges([_gather_exchange([_pack([wts[n] for n in TAPS]), shards[0]['w_in']])],
                                           name="gather_first")
    full_taps = {n: _join_columns(t) for n, t in zip(TAPS, _unpack(gathered_taps, tap_shapes))}
    full_w_in = _as_matrix(got)

    xs = x[0]
    memb = mem[0].astype(BF16)
    h, hb = xs, xs.astype(BF16)
    saved, weights = [], []
    for l in range(depth):
        w = {'w_in': full_w_in}
        w.update({n: full_taps[n][l] for n in TAPS})
        w.update({n: wts[n][l] for n in REPLICATED})
        h, hb, sv, w, full_w_in = _layer_fwd(h, hb, memb, w, alpha, shards[l],
                                             shards[l + 1]['w_in'] if l + 1 < depth else None)
        saved.append(sv)
        weights.append(w)

    dy, loss_row = _loss_and_grad(h, loss_target[0], name="loss")
    loss = lax.psum(_row_sum(loss_row, name="loss_sum")[0, 0], ("x", "y", "c"))

    results = {}
    col_sharded = dict(MATRICES)
    state = {n: [row_blocks(src, n, cs) for src in (wts, mom, var)] for n, cs in MATRICES}

    def update(n, l, parts):
        results[n] = _sum_adamw(parts, *state[n], layer=l, so_far=results.get(n), name="adamw_matrix")

    small = REPLICATED + TAPS

    def gather_small(g0):
        per_layer = [g0] + grads[1:]
        return _gather_exchange([_pack([jnp.stack([per_layer[l][n] for l in range(depth)]) for n in small])])

    top = (dy, None, None)
    grads = [None] * depth
    carried = []
    for l in reversed(range(depth)):
        top, grads[l], received, left, also_got = _layer_bwd(top, saved[l], memb, weights[l], alpha, carried,
                                                             gather_small if l == 0 else None)
        for n, parts in received.items():
            update(n, l + 1 if n in dict(carried) else l, parts)
        carried = left
    parts, = also_got
    da, last = _mm(top[0], top[1], name="mm_dx_in",
                   exchange=_scatter_exchange([gm.reshape(N_DEV, -1, gm.shape[-1]) for _, gm in carried]))
    for (n, _), blocks in zip(carried, last):
        update(n, 0, blocks)
    grad_x = _axpy(da, top[2], alpha=alpha, name="grad_x")[None]
    for n, cs in MATRICES:
        if cs:
            results[n] = [jnp.swapaxes(r, 1, 2) for r in results[n]]
    total = _sum_parts(parts, name="sum_small_grads")
    summed = dict(zip(small, _unpack(total, [wts[n].shape for n in REPLICATED] + [full_taps[n].shape for n in TAPS])))
    for n in TAPS:
        cols = wts[n].shape[-1]
        summed[n] = lax.dynamic_slice_in_dim(summed[n], my_index * cols, cols, axis=2)
    res = _sum_adamw(_pack([summed[n] for n in small])[None], *[_pack([src[n] for n in small])[None] for src in (wts, mom, var)],
                     layer=0, so_far=None, name="adamw_small")
    unpacked = [_unpack(r[0], [wts[n].shape for n in small]) for r in res]
    for i, n in enumerate(small):
        results[n] = [u[i] for u in unpacked]

    outs = [loss, grad_x]
    for kind in range(4):
        outs += [results[n][kind] for n in WEIGHTS]
    return tuple(outs)
```
